```python
import math, functools
import jax, jax.numpy as jnp
from jax import lax
import numpy as np

D_MODEL = 2048
BATCH = 1
SEQ = 8192
DEPTH = 2
DEC_BATCH = 32
DEC_SEQ = 4
PAST_LEN = 8192
PAGE_SIZE = 128

D_MIX = D_MODEL
POOL_DIM = D_MIX // 4
POOL_GROUPS = 4
POOL_GDIM = POOL_DIM // POOL_GROUPS
POOL_WINDOWS = (2, 4, 8, 16)
POOL_HIST = 15
SSM_DIM = D_MIX // 4
SSM_HEAD_DIM = 64
SSM_HEADS = SSM_DIM // SSM_HEAD_DIM
SSM_GROUPS = 2
SSM_STATE = 128
SSM_CONV = 4
SSM_CHUNK = 128
SSM_XBC = SSM_DIM + 2 * SSM_GROUPS * SSM_STATE
ATT_DIM = D_MIX // 4
ATT_HEAD_DIM = 64
ATT_HEADS = ATT_DIM // ATT_HEAD_DIM
IDX_HEADS = ATT_HEADS // 2
IDX_DIM = 64
TOPK_MAX = 256
ATT_BLOCK = 128
REL_BUCKETS = 32
REL_MAX_DIST = 128
ATT_SCALE = ATT_HEAD_DIM ** -0.5
IDX_SCALE = (IDX_HEADS * IDX_DIM) ** -0.5
SGU_DIM = D_MIX // 4
SGU_GROUPS = 4
SGU_GDIM = SGU_DIM // SGU_GROUPS
SGU_CHUNK = 128
D_FF = 4 * D_MODEL
EPS = 1e-6
NEG = -1e30
IN_SPLITS = (POOL_DIM, SSM_DIM, SSM_XBC, SSM_HEADS, ATT_DIM, ATT_DIM, ATT_DIM, IDX_HEADS * IDX_DIM, IDX_DIM, IDX_HEADS, SGU_DIM, SGU_DIM)
D_IN = sum(IN_SPLITS)

kernel_name = 'hybrid_pool_ssd_dsa_sgu_step'


def rmsnorm(x, g):
    x32 = x.astype(jnp.float32)
    y = x32 * lax.rsqrt(jnp.mean(jnp.square(x32), axis=-1, keepdims=True) + EPS)
    return (y * g.astype(jnp.float32)).astype(x.dtype)


def group_rmsnorm(y, g, groups):
    shp = y.shape
    y32 = y.astype(jnp.float32).reshape(*shp[:-1], groups, shp[-1] // groups)
    y32 = y32 * lax.rsqrt(jnp.mean(jnp.square(y32), axis=-1, keepdims=True) + EPS)
    return y32.reshape(shp) * g.astype(jnp.float32)


def pool_mixer(p, hist, pos0, pool_w, pool_scale):
    b, t, _ = p.shape
    ext = jnp.concatenate([hist.astype(p.dtype), p], axis=1)
    cs = jnp.cumsum(ext.astype(jnp.float32), axis=1)
    cs = jnp.concatenate([jnp.zeros((b, 1, POOL_DIM), jnp.float32), cs], axis=1)
    pos = pos0 + jnp.arange(t, dtype=jnp.int32)
    means = []
    for g, w in enumerate(POOL_WINDOWS):
        ch = slice(g * POOL_GDIM, (g + 1) * POOL_GDIM)
        s = cs[:, POOL_HIST + 1:, ch] - cs[:, POOL_HIST + 1 - w:POOL_HIST + 1 - w + t, ch]
        cnt = jnp.minimum(pos + 1, w).astype(jnp.float32)
        means.append(s / cnt[None, :, None])
    mean = jnp.stack(means, axis=2)
    diff = mean - p.astype(jnp.float32).reshape(b, t, POOL_GROUPS, POOL_GDIM)
    y = jnp.einsum('btgc,gcd->btgd', diff, pool_w.astype(jnp.float32)).reshape(b, t, POOL_DIM)
    return y * pool_scale.astype(jnp.float32), ext[:, -POOL_HIST:]


def causal_conv(xbc, hist, conv_w, conv_b):
    t = xbc.shape[1]
    ext = jnp.concatenate([hist.astype(xbc.dtype), xbc], axis=1)
    out = conv_b + sum(ext[:, j:j + t] * conv_w[j] for j in range(SSM_CONV))
    return jax.nn.silu(out), ext[:, -(SSM_CONV - 1):]


def ssd_scan(x, dt, a_log, b_in, c_in, h0):
    bsz, t, nh, hp = x.shape
    ng, ns = b_in.shape[2], b_in.shape[3]
    hg = nh // ng
    q = SSM_CHUNK if t % SSM_CHUNK == 0 else t
    nc = t // q
    f32 = jnp.float32
    a = (dt * (-jnp.exp(a_log.astype(f32)))).reshape(bsz, nc, q, ng, hg)
    xdt = (x.astype(f32) * dt[..., None]).reshape(bsz, nc, q, ng, hg, hp)
    bc = b_in.astype(f32).reshape(bsz, nc, q, ng, ns)
    cc = c_in.astype(f32).reshape(bsz, nc, q, ng, ns)
    a_cum = jnp.cumsum(a, axis=2)
    seg = a_cum[:, :, :, None] - a_cum[:, :, None, :]
    causal = jnp.tril(jnp.ones((q, q), bool))[None, None, :, :, None, None]
    decay = jnp.exp(jnp.where(causal, seg, -jnp.inf))
    cb = jnp.einsum('bclgn,bcsgn->bclsg', cc, bc)
    y_diag = jnp.einsum('bclsgh,bcsghp->bclghp', cb[..., None] * decay, xdt)
    decay_end = jnp.exp(a_cum[:, :, -1:] - a_cum)
    chunk_states = jnp.einsum('bclgn,bclghp->bcghpn', bc, xdt * decay_end[..., None])
    chunk_decay = jnp.exp(a_cum[:, :, -1])

    def step(h, inp):
        dec, st = inp
        return dec[..., None, None] * h + st, h

    h_init = h0.astype(f32).reshape(bsz, ng, hg, hp, ns)
    h_final, h_prev = lax.scan(step, h_init, (jnp.moveaxis(chunk_decay, 1, 0), jnp.moveaxis(chunk_states, 1, 0)))
    h_prev = jnp.moveaxis(h_prev, 0, 1)
    y_off = jnp.einsum('bclgn,bcghpn->bclghp', cc, h_prev) * jnp.exp(a_cum)[..., None]
    y = (y_diag + y_off).reshape(bsz, t, nh, hp)
    return y, h_final.reshape(bsz, nh, hp, ns)


def t5_bucket(n):
    max_exact = REL_BUCKETS // 2
    nf = jnp.maximum(n, 1).astype(jnp.float32)
    large = max_exact + (jnp.log(nf / max_exact) / math.log(REL_MAX_DIST / max_exact) * (REL_BUCKETS - max_exact)).astype(jnp.int32)
    large = jnp.minimum(large, REL_BUCKETS - 1)
    return jnp.where(n < max_exact, n, large)


def sparse_block(q, q_idx, w_idx, pos_q, kidx_all, n_sel, fetch, rel_bias):
    f32 = jnp.float32
    sc = jnp.einsum('bqhd,bsd->bqhs', q_idx.astype(f32), kidx_all.astype(f32))
    score = jnp.einsum('bqhs,bqh->bqs', jax.nn.relu(sc), w_idx.astype(f32) * IDX_SCALE)
    key_pos = jnp.arange(kidx_all.shape[1], dtype=jnp.int32)
    score = jnp.where(key_pos[None, None, :] <= pos_q[None, :, None], score, -jnp.inf)
    _, idx = lax.top_k(score, n_sel)
    dist = pos_q[None, :, None] - idx
    valid = dist >= 0
    k_sel, v_sel = fetch(idx)
    logits = jnp.einsum('bqhd,bqkhd->bqhk', q.astype(f32), k_sel.astype(f32)) * ATT_SCALE
    bias = rel_bias.astype(f32)[t5_bucket(jnp.maximum(dist, 0))]
    logits = logits + jnp.moveaxis(bias, -1, 2)
    logits = jnp.where(valid[:, :, None, :], logits, NEG)
    p = jax.nn.softmax(logits, axis=-1)
    return jnp.einsum('bqhk,bqkhd->bqhd', p, v_sel.astype(f32))


def prompt_attention(q, k, v, q_idx, k_idx, w_idx, rel_bias):
    b, t = q.shape[:2]
    n_sel = min(TOPK_MAX, t // 4)
    nb = t // ATT_BLOCK
    gather = jax.vmap(lambda rows, i: rows[i])

    def fetch(idx):
        return gather(k, idx), gather(v, idx)

    def blk(args):
        qb, qib, wb, pos = args
        return sparse_block(qb, qib, wb, pos, k_idx, n_sel, fetch, rel_bias)

    def to_blocks(a):
        return jnp.moveaxis(a.reshape(b, nb, ATT_BLOCK, *a.shape[2:]), 1, 0)

    pos = jnp.arange(t, dtype=jnp.int32).reshape(nb, ATT_BLOCK)
    out = lax.map(blk, (to_blocks(q), to_blocks(q_idx), to_blocks(w_idx), pos))
    return jnp.moveaxis(out, 0, 1).reshape(b, t, ATT_DIM)


def sample_attention(q, k, v, q_idx, k_idx, w_idx, rel_bias, layer, cache_k, cache_v, cache_kidx, page_table):
    b, t = q.shape[:2]
    n_pages = page_table.shape[1]
    page = cache_k.shape[2]
    past = n_pages * page
    n_sel = min(TOPK_MAX, (past + t) // 4)
    kidx_past = cache_kidx[layer, page_table].reshape(b, past, IDX_DIM)
    kidx_all = jnp.concatenate([kidx_past.astype(k_idx.dtype), k_idx], axis=1)
    gather = jax.vmap(lambda rows, i: rows[i])

    def fetch(idx):
        from_past = (idx < past)[..., None, None]
        pi = jnp.minimum(idx, past - 1)
        phys = page_table[jnp.arange(b)[:, None, None], pi // page]
        slot = pi % page
        ni = jnp.clip(idx - past, 0, t - 1)
        k_sel = jnp.where(from_past, cache_k[layer, phys, slot].astype(k.dtype), gather(k, ni))
        v_sel = jnp.where(from_past, cache_v[layer, phys, slot].astype(v.dtype), gather(v, ni))
        return k_sel, v_sel

    pos = past + jnp.arange(t, dtype=jnp.int32)
    out = sparse_block(q, q_idx, w_idx, pos, kidx_all, n_sel, fetch, rel_bias)
    return out.reshape(b, t, ATT_DIM)


def chunk_sgu(u, v, sgu_w, sgu_b):
    b, t, _ = u.shape
    tc = min(SGU_CHUNK, t)
    nc = t // tc
    u = jax.nn.gelu(u)
    v = jax.nn.gelu(v)
    vc = v.reshape(b, nc, tc, SGU_GROUPS, SGU_GDIM)
    w = sgu_w[:, :tc, :tc] * jnp.tril(jnp.ones((tc, tc), sgu_w.dtype))
    mixed = jnp.einsum('gts,bcsgd->bctgd', w, vc) + sgu_b[:, :tc].T[None, None, :, :, None]
    return u * mixed.reshape(b, t, SGU_DIM), v


def token_mixers(xn, hist_pool, hist_conv, h0, pos0, attend, w_in, pool_w, pool_scale, conv_w, conv_b, dt_bias, a_log, d_skip, ssm_norm, sgu_w, sgu_b):
    b, t, _ = xn.shape
    h = xn @ w_in
    cuts = np.cumsum(IN_SPLITS)[:-1].tolist()
    p, z, xbc, dt_raw, q, k, v, q_idx, k_idx, w_idx, u, v_sgu = jnp.split(h, cuts, axis=-1)
    y_pool, new_pool = pool_mixer(p, hist_pool, pos0, pool_w, pool_scale)
    xbc, new_conv = causal_conv(xbc, hist_conv, conv_w, conv_b)
    xs, bs, cs = jnp.split(xbc, [SSM_DIM, SSM_DIM + SSM_GROUPS * SSM_STATE], axis=-1)
    dt = jax.nn.softplus(dt_raw.astype(jnp.float32) + dt_bias.astype(jnp.float32))
    xh = xs.reshape(b, t, SSM_HEADS, SSM_HEAD_DIM)
    y_ssm, new_h = ssd_scan(xh, dt, a_log, bs.reshape(b, t, SSM_GROUPS, SSM_STATE), cs.reshape(b, t, SSM_GROUPS, SSM_STATE), h0)
    y_ssm = y_ssm + d_skip.astype(jnp.float32)[:, None] * xh.astype(jnp.float32)
    y_ssm = group_rmsnorm(y_ssm.reshape(b, t, SSM_DIM) * jax.nn.silu(z.astype(jnp.float32)), ssm_norm, SSM_GROUPS)
    kh = k.reshape(b, t, ATT_HEADS, ATT_HEAD_DIM)
    vh = v.reshape(b, t, ATT_HEADS, ATT_HEAD_DIM)
    y_att = attend(q.reshape(b, t, ATT_HEADS, ATT_HEAD_DIM), kh, vh, q_idx.reshape(b, t, IDX_HEADS, IDX_DIM), k_idx, w_idx)
    y_sgu, v_rows = chunk_sgu(u, v_sgu, sgu_w, sgu_b)
    dt_x = xn.dtype
    y = jnp.concatenate([y_pool.astype(dt_x), y_ssm.astype(dt_x), y_att.astype(dt_x), y_sgu.astype(dt_x)], axis=-1)
    return y, new_pool, new_conv, new_h, kh, vh, k_idx, v_rows


def trunk_layer(x, hist_pool, hist_conv, h0, pos0, attend, norm_mix, w_in, pool_w, pool_scale, conv_w, conv_b, dt_bias, a_log, d_skip, ssm_norm, sgu_w, sgu_b, w_out, norm_mlp, mlp_w1, mlp_w2):
    y_mix, new_pool, new_conv, new_h, k_rows, v_rows, kidx_rows, sgu_rows = token_mixers(
        rmsnorm(x, norm_mix), hist_pool, hist_conv, h0, pos0, attend, w_in, pool_w, pool_scale,
        conv_w, conv_b, dt_bias, a_log, d_skip, ssm_norm, sgu_w, sgu_b)
    x = x + y_mix @ w_out
    hid = jnp.square(jax.nn.relu(rmsnorm(x, norm_mlp) @ mlp_w1))
    x = x + hid @ mlp_w2
    return x, (new_pool, new_conv, new_h, k_rows, v_rows, kidx_rows, sgu_rows)


def setup_inputs(seed: int = 0) -> dict:
    key = jax.random.key(seed)
    ks = jax.random.split(key, 32)
    f32 = jnp.float32

    def nrm(k, shape, s):
        return s * jax.random.normal(k, shape, f32)

    n_pages = PAST_LEN // PAGE_SIZE
    n_pool = (DEC_BATCH * n_pages * 5) // 4
    page_table = jax.random.permutation(ks[8], n_pool)[:DEC_BATCH * n_pages].reshape(DEC_BATCH, n_pages).astype(jnp.int32)
    dt0 = jnp.exp(jax.random.uniform(ks[13], (DEPTH, SSM_HEADS), f32, math.log(1e-3), math.log(1e-1)))
    return {
        'x_prompt': nrm(ks[0], (BATCH, SEQ, D_MODEL), 1.0),
        'x_sample': nrm(ks[1], (DEC_BATCH, DEC_SEQ, D_MODEL), 1.0),
        'state_pool': nrm(ks[2], (DEPTH, DEC_BATCH, POOL_HIST, POOL_DIM), 1.0),
        'state_conv': nrm(ks[3], (DEPTH, DEC_BATCH, SSM_CONV - 1, SSM_XBC), 1.0),
        'state_ssm': nrm(ks[4], (DEPTH, DEC_BATCH, SSM_HEADS, SSM_HEAD_DIM, SSM_STATE), 0.5),
        'cache_k': nrm(ks[5], (DEPTH, n_pool, PAGE_SIZE, ATT_HEADS, ATT_HEAD_DIM), 1.0),
        'cache_v': nrm(ks[6], (DEPTH, n_pool, PAGE_SIZE, ATT_HEADS, ATT_HEAD_DIM), 1.0),
        'cache_kidx': nrm(ks[7], (DEPTH, n_pool, PAGE_SIZE, IDX_DIM), 1.0),
        'page_table': page_table,
        'rel_bias': nrm(ks[9], (REL_BUCKETS, ATT_HEADS), 0.5),
        'norm_mix': 1.0 + nrm(ks[10], (DEPTH, D_MODEL), 0.02),
        'w_in': nrm(ks[11], (DEPTH, D_MODEL, D_IN), D_MODEL ** -0.5),
        'pool_w': nrm(ks[12], (DEPTH, POOL_GROUPS, POOL_GDIM, POOL_GDIM), POOL_GDIM ** -0.5),
        'pool_scale': 1.0 + nrm(ks[14], (DEPTH, POOL_DIM), 0.02),
        'conv_w': nrm(ks[15], (DEPTH, SSM_CONV, SSM_XBC), SSM_CONV ** -0.5),
        'conv_b': nrm(ks[16], (DEPTH, SSM_XBC), 0.02),
        'dt_bias': jnp.log(jnp.expm1(dt0)),
        'a_log': jnp.log(jax.random.uniform(ks[17], (DEPTH, SSM_HEADS), f32, 1.0, 16.0)),
        'd_skip': 1.0 + nrm(ks[18], (DEPTH, SSM_HEADS), 0.1),
        'ssm_norm': 1.0 + nrm(ks[19], (DEPTH, SSM_DIM), 0.02),
        'sgu_w': nrm(ks[20], (DEPTH, SGU_GROUPS, SGU_CHUNK, SGU_CHUNK), SGU_CHUNK ** -0.5),
        'sgu_b': 1.0 + nrm(ks[21], (DEPTH, SGU_GROUPS, SGU_CHUNK), 0.02),
        'w_out': nrm(ks[22], (DEPTH, D_MIX, D_MODEL), D_MIX ** -0.5),
        'norm_mlp': 1.0 + nrm(ks[23], (DEPTH, D_MODEL), 0.02),
        'mlp_w1': nrm(ks[24], (DEPTH, D_MODEL, D_FF), D_MODEL ** -0.5),
        'mlp_w2': nrm(ks[25], (DEPTH, D_FF, D_MODEL), D_FF ** -0.5),
        'norm_final': 1.0 + nrm(ks[26], (D_MODEL,), 0.02),
    }


def reference(x_prompt, x_sample, state_pool, state_conv, state_ssm, cache_k, cache_v, cache_kidx, page_table, rel_bias, norm_mix, w_in, pool_w, pool_scale, conv_w, conv_b, dt_bias, a_log, d_skip, ssm_norm, sgu_w, sgu_b, w_out, norm_mlp, mlp_w1, mlp_w2, norm_final):
    bp = x_prompt.shape[0]
    past_len = page_table.shape[1] * cache_k.shape[2]
    xp, xs = x_prompt, x_sample
    attend_p = functools.partial(prompt_attention, rel_bias=rel_bias)
    prompt_states, sample_states = [], []
    for l in range(DEPTH):
        wl = (norm_mix[l], w_in[l], pool_w[l], pool_scale[l], conv_w[l], conv_b[l], dt_bias[l], a_log[l],
              d_skip[l], ssm_norm[l], sgu_w[l], sgu_b[l], w_out[l], norm_mlp[l], mlp_w1[l], mlp_w2[l])
        xp, st_p = trunk_layer(
            xp, jnp.zeros((bp, POOL_HIST, POOL_DIM), xp.dtype), jnp.zeros((bp, SSM_CONV - 1, SSM_XBC), xp.dtype),
            jnp.zeros((bp, SSM_HEADS, SSM_HEAD_DIM, SSM_STATE), jnp.float32), 0, attend_p, *wl)
        attend_s = functools.partial(sample_attention, rel_bias=rel_bias, layer=l, cache_k=cache_k, cache_v=cache_v,
                                     cache_kidx=cache_kidx, page_table=page_table)
        xs, st_s = trunk_layer(xs, state_pool[l], state_conv[l], state_ssm[l], past_len, attend_s, *wl)
        prompt_states.append(st_p)
        sample_states.append(st_s)
    y_prompt = rmsnorm(xp, norm_final)
    y_sample = rmsnorm(xs, norm_final)
    new_pool_p = jnp.stack([s[0] for s in prompt_states])
    new_conv_p = jnp.stack([s[1] for s in prompt_states])
    new_ssm_p = jnp.stack([s[2] for s in prompt_states])
    new_k_p = jnp.stack([s[3] for s in prompt_states])
    new_v_p = jnp.stack([s[4] for s in prompt_states])
    new_kidx_p = jnp.stack([s[5] for s in prompt_states])
    new_pool_s = jnp.stack([s[0] for s in sample_states])
    new_conv_s = jnp.stack([s[1] for s in sample_states])
    new_ssm_s = jnp.stack([s[2] for s in sample_states])
    new_k_s = jnp.stack([s[3] for s in sample_states])
    new_v_s = jnp.stack([s[4] for s in sample_states])
    new_kidx_s = jnp.stack([s[5] for s in sample_states])
    new_sgu_v_s = jnp.stack([s[6] for s in sample_states])
    return (y_prompt, y_sample, new_pool_p, new_conv_p, new_ssm_p, new_k_p, new_v_p, new_kidx_p,
            new_pool_s, new_conv_s, new_ssm_s, new_k_s, new_v_s, new_kidx_s, new_sgu_v_s)
```

```python
import functools
import math

import numpy as np
import jax
import jax.numpy as jnp
from jax import lax
from jax.experimental import pallas as pl
from jax.experimental.pallas import tpu as pltpu

F32 = jnp.float32
BF16 = jnp.bfloat16
I32 = jnp.int32

LANES = 128
D_MODEL = 2048
N_HEADS = 8
HEAD_DIM = 64
MIX_DIM = 512
POOL_WINDOWS = (2, 4, 8, 16)
POOL_HIST = 15
CONV_W = 4
SSM_GROUPS = 2
SSM_STATE = 128
XBC_DIM = MIX_DIM + 2 * SSM_GROUPS * SSM_STATE
IDX_HEADS = 4
IDX_DIM = 64
TOPK_MAX = 256
PAGE = 128
REL_BUCKETS = 32
REL_MAX_DIST = 128
ATT_SCALE = HEAD_DIM ** -0.5
IDX_SCALE = (IDX_HEADS * IDX_DIM) ** -0.5
SGU_GROUPS = 4
D_FF = 4 * D_MODEL
EPS = 1e-6
NEG = -1e30
INT_MIN = -(2 ** 31)
INT_MAX = 2 ** 31 - 1
IN_SPLITS = (512, 512, 1024, 8, 512, 512, 512, 256, 64, 4, 512, 512)

C_P, C_Z, C_XBC, C_Q, C_K, C_V, C_U, C_VS, C_QIDX, C_KW, C_DT = (
    0, 512, 1024, 2048, 2560, 3072, 3584, 4096, 4608, 4864, 4992)
H_PACKED = 5120
VMEM_LIMIT = 56 * 1024 * 1024


def _cparams(sem):
    return pltpu.CompilerParams(dimension_semantics=sem, vmem_limit_bytes=VMEM_LIMIT)


def _dot(a, b):
    return jnp.dot(a, b, preferred_element_type=F32)


def _dot_nt(a, b):
    return lax.dot_general(a, b, (((1,), (1,)), ((), ())), preferred_element_type=F32)


def _dot_tn(a, b):
    return lax.dot_general(a, b, (((0,), (0,)), ((), ())), preferred_element_type=F32)


def _silu(x):
    return x * (1.0 / (1.0 + jnp.exp(-x)))


def _gelu_tanh(x):
    return 0.5 * x * (1.0 + jnp.tanh(math.sqrt(2.0 / math.pi) * (x + 0.044715 * (x * x * x))))


def _softplus(x):
    return jnp.maximum(x, 0.0) + jnp.log1p(jnp.exp(-jnp.abs(x)))


def _rms(x, g):
    ms = jnp.mean(x * x, axis=-1, keepdims=True)
    return x * lax.rsqrt(ms + EPS) * g


def _inproj_kernel(x_ref, g_ref, w_ref, o_ref, xn_ref):
    @pl.when(pl.program_id(1) == 0)
    def _():
        xn_ref[...] = _rms(x_ref[...], g_ref[...]).astype(BF16)

    o_ref[...] = _dot(xn_ref[...], w_ref[...])


def _inproj(x, g, w, tm, tn=512):
    m = x.shape[0]
    n = w.shape[1]
    return pl.pallas_call(
        _inproj_kernel,
        grid=(m // tm, n // tn),
        in_specs=[pl.BlockSpec((tm, D_MODEL), lambda i, j: (i, 0)),
                  pl.BlockSpec((1, D_MODEL), lambda i, j: (0, 0)),
                  pl.BlockSpec((D_MODEL, tn), lambda i, j: (0, j))],
        out_specs=pl.BlockSpec((tm, tn), lambda i, j: (i, j)),
        out_shape=jax.ShapeDtypeStruct((m, n), F32),
        scratch_shapes=[pltpu.VMEM((tm, D_MODEL), BF16)],
        compiler_params=_cparams(("parallel", "arbitrary")),
        name="inproj",
    )(x, g, w)


def _outproj_kernel(x_ref, y0_ref, y1_ref, y2_ref, y3_ref, w_ref, g_ref, x1_ref, xn_ref):
    y = jnp.concatenate([y0_ref[...], y1_ref[...], y2_ref[...], y3_ref[...]], axis=1).astype(BF16)
    x1 = x_ref[...] + _dot(y, w_ref[...])
    x1_ref[...] = x1
    xn_ref[...] = _rms(x1, g_ref[...]).astype(BF16)


def _outproj(x, ys, w, g, tm):
    m = x.shape[0]
    row = lambda i: (i, 0)
    return pl.pallas_call(
        _outproj_kernel,
        grid=(m // tm,),
        in_specs=[pl.BlockSpec((tm, D_MODEL), row)]
                 + [pl.BlockSpec((tm, MIX_DIM), row)] * 4
                 + [pl.BlockSpec((D_MODEL, D_MODEL), lambda i: (0, 0)),
                    pl.BlockSpec((1, D_MODEL), lambda i: (0, 0))],
        out_specs=[pl.BlockSpec((tm, D_MODEL), row), pl.BlockSpec((tm, D_MODEL), row)],
        out_shape=[jax.ShapeDtypeStruct((m, D_MODEL), F32), jax.ShapeDtypeStruct((m, D_MODEL), BF16)],
        compiler_params=_cparams(("parallel",)),
        name="outproj",
    )(x, *ys, w, g)


def _mlp_kernel(xn_ref, w1_ref, w2_ref, x1_ref, g_ref, o_ref, acc_ref, *, final_norm):
    j = pl.program_id(1)

    @pl.when(j == 0)
    def _():
        acc_ref[...] = jnp.zeros_like(acc_ref)

    hid = _dot(xn_ref[...], w1_ref[...])
    hid = jnp.square(jnp.maximum(hid, 0.0)).astype(BF16)
    acc_ref[...] += _dot(hid, w2_ref[...])

    @pl.when(j == pl.num_programs(1) - 1)
    def _():
        x2 = x1_ref[...] + acc_ref[...]
        o_ref[...] = _rms(x2, g_ref[...]) if final_norm else x2


def _mlp(xn, w1, w2, x1, g, tm, final_norm, tf=512):
    m = xn.shape[0]
    return pl.pallas_call(
        functools.partial(_mlp_kernel, final_norm=final_norm),
        grid=(m // tm, D_FF // tf),
        in_specs=[pl.BlockSpec((tm, D_MODEL), lambda i, j: (i, 0)),
                  pl.BlockSpec((D_MODEL, tf), lambda i, j: (0, j)),
                  pl.BlockSpec((tf, D_MODEL), lambda i, j: (j, 0)),
                  pl.BlockSpec((tm, D_MODEL), lambda i, j: (i, 0)),
                  pl.BlockSpec((1, D_MODEL), lambda i, j: (0, 0))],
        out_specs=pl.BlockSpec((tm, D_MODEL), lambda i, j: (i, 0)),
        out_shape=jax.ShapeDtypeStruct((m, D_MODEL), F32),
        scratch_shapes=[pltpu.VMEM((tm, D_MODEL), F32)],
        compiler_params=_cparams(("parallel", "arbitrary")),
        name="mlp",
    )(xn, w1, w2, x1, g)


def _pool_kernel(p_ref, hist_ref, w_ref, scale_ref, y_ref, newhist_ref, ext_ref, *, tc, treal, pos0):
    c = pl.program_id(1)
    hrows = POOL_HIST + 1

    @pl.when(c == 0)
    def _():
        ext_ref[0:1, :] = jnp.zeros((1, MIX_DIM), F32)
        ext_ref[1:hrows, :] = hist_ref[0]

    if treal < tc:
        ext_ref[hrows:hrows + tc, :] = jnp.zeros((tc, MIX_DIM), F32)
    ext_ref[hrows:hrows + treal, :] = p_ref[0]

    pos = pos0 + c * treal + lax.broadcasted_iota(I32, (tc, 1), 0)
    for g, win in enumerate(POOL_WINDOWS):
        cols = slice(g * LANES, (g + 1) * LANES)
        cur = ext_ref[hrows:hrows + tc, cols]
        s = cur
        for j in range(1, win):
            s = s + ext_ref[hrows - j:hrows - j + tc, cols]
        cnt = jnp.minimum(pos + 1, win).astype(F32)
        diff = s / cnt - cur
        y = _dot(diff, w_ref[g]) * scale_ref[:, cols]
        y_ref[0, :, cols] = y[:treal]

    newhist_ref[0] = ext_ref[treal + 1:treal + hrows, :]
    ext_ref[0:hrows, :] = ext_ref[treal:treal + hrows, :]


def _pool(h3, hist, pool_w, pool_scale, *, tc, treal, pos0):
    b, t, _ = h3.shape
    return pl.pallas_call(
        functools.partial(_pool_kernel, tc=tc, treal=treal, pos0=pos0),
        grid=(b, t // treal),
        in_specs=[pl.BlockSpec((1, treal, MIX_DIM), lambda i, c: (i, c, C_P // MIX_DIM)),
                  pl.BlockSpec((1, POOL_HIST, MIX_DIM), lambda i, c: (i, 0, 0)),
                  pl.BlockSpec((4, LANES, LANES), lambda i, c: (0, 0, 0)),
                  pl.BlockSpec((1, MIX_DIM), lambda i, c: (0, 0))],
        out_specs=[pl.BlockSpec((1, treal, MIX_DIM), lambda i, c: (i, c, 0)),
                   pl.BlockSpec((1, POOL_HIST, MIX_DIM), lambda i, c: (i, 0, 0))],
        out_shape=[jax.ShapeDtypeStruct((b, t, MIX_DIM), F32),
                   jax.ShapeDtypeStruct((b, POOL_HIST, MIX_DIM), F32)],
        scratch_shapes=[pltpu.VMEM((POOL_HIST + 1 + tc, MIX_DIM), F32)],
        compiler_params=_cparams(("parallel", "arbitrary")),
        name="pool",
    )(h3, hist, pool_w, pool_scale)


def _sgu_kernel(u_ref, v_ref, w_ref, b_ref, y_ref, vout_ref, ubuf_ref, vbuf_ref, *, tc, treal):
    if treal < tc:
        ubuf_ref[...] = jnp.zeros((tc, MIX_DIM), F32)
        vbuf_ref[...] = jnp.zeros((tc, MIX_DIM), F32)
    ubuf_ref[0:treal, :] = u_ref[0]
    vbuf_ref[0:treal, :] = v_ref[0]
    u = _gelu_tanh(ubuf_ref[...])
    v = _gelu_tanh(vbuf_ref[...])
    vout_ref[0] = v[:treal]
    r = lax.broadcasted_iota(I32, (tc, tc), 0)
    s = lax.broadcasted_iota(I32, (tc, tc), 1)
    for g in range(SGU_GROUPS):
        cols = slice(g * LANES, (g + 1) * LANES)
        w = jnp.where(s <= r, w_ref[g], 0.0)
        mixed = _dot(w, v[:, cols]) + b_ref[g]
        y_ref[0, :, cols] = (u[:, cols] * mixed)[:treal]


def _sgu(h3, sgu_w, sgu_b_col, *, tc, treal):
    b, t, _ = h3.shape
    return pl.pallas_call(
        functools.partial(_sgu_kernel, tc=tc, treal=treal),
        grid=(b, t // treal),
        in_specs=[pl.BlockSpec((1, treal, MIX_DIM), lambda i, c: (i, c, C_U // MIX_DIM)),
                  pl.BlockSpec((1, treal, MIX_DIM), lambda i, c: (i, c, C_VS // MIX_DIM)),
                  pl.BlockSpec((SGU_GROUPS, tc, tc), lambda i, c: (0, 0, 0)),
                  pl.BlockSpec((SGU_GROUPS, tc, 1), lambda i, c: (0, 0, 0))],
        out_specs=[pl.BlockSpec((1, treal, MIX_DIM), lambda i, c: (i, c, 0)),
                   pl.BlockSpec((1, treal, MIX_DIM), lambda i, c: (i, c, 0))],
        out_shape=[jax.ShapeDtypeStruct((b, t, MIX_DIM), F32),
                   jax.ShapeDtypeStruct((b, t, MIX_DIM), F32)],
        scratch_shapes=[pltpu.VMEM((tc, MIX_DIM), F32), pltpu.VMEM((tc, MIX_DIM), F32)],
        compiler_params=_cparams(("parallel", "parallel")),
        name="sgu",
    )(h3, h3, sgu_w, sgu_b_col)


def _cumsum_rows(a, n):
    row = lax.broadcasted_iota(I32, a.shape, 0)
    sh = 1
    while sh < n:
        a = a + jnp.where(row >= sh, pltpu.roll(a, sh, 0), 0.0)
        sh *= 2
    return a


def _ssd_kernel(z_ref, xbc_ref, dt_ref, hc_ref, h0_ref, convw_ref, convb_ref, dtb_ref, alog_ref, dskip_ref,
                norm_ref, y_ref, newconv_ref, newh_ref, ext_ref, zbuf_ref, dtbuf_ref, state_ref, *, tc, treal):
    c = pl.program_id(1)
    pre = 8
    hist = CONV_W - 1

    @pl.when(c == 0)
    def _():
        ext_ref[0:pre - hist, :] = jnp.zeros((pre - hist, XBC_DIM), F32)
        ext_ref[pre - hist:pre, :] = hc_ref[0]
        state_ref[...] = h0_ref[...]

    if treal < tc:
        ext_ref[pre:pre + tc, :] = jnp.zeros((tc, XBC_DIM), F32)
        zbuf_ref[...] = jnp.zeros((tc, MIX_DIM), F32)
        dtbuf_ref[...] = jnp.zeros((tc, LANES), F32)
    ext_ref[pre:pre + treal, :] = xbc_ref[0]
    zbuf_ref[0:treal, :] = z_ref[0]
    dtbuf_ref[0:treal, :] = dt_ref[0]

    conv = convb_ref[...]
    for j in range(CONV_W):
        conv = conv + ext_ref[pre - hist + j:pre - hist + j + tc, :] * convw_ref[j:j + 1, :]
    conv = _silu(conv)
    xs = conv[:, :MIX_DIM]
    bmat = conv[:, MIX_DIM:MIX_DIM + SSM_GROUPS * SSM_STATE]
    cmat = conv[:, MIX_DIM + SSM_GROUPS * SSM_STATE:]

    rowi = lax.broadcasted_iota(I32, (tc, LANES), 0)
    dt = _softplus(dtbuf_ref[...] + dtb_ref[...])
    if treal < tc:
        dt = jnp.where(rowi < treal, dt, 0.0)
    a = dt * (-jnp.exp(alog_ref[...]))
    acum = _cumsum_rows(a, tc)
    acum_t = jnp.transpose(acum)
    total = acum[tc - 1:tc, :]
    causal = lax.broadcasted_iota(I32, (tc, tc), 1) <= lax.broadcasted_iota(I32, (tc, tc), 0)

    ys = []
    heads_per_group = N_HEADS // SSM_GROUPS
    for g in range(SSM_GROUPS):
        bg = bmat[:, g * SSM_STATE:(g + 1) * SSM_STATE]
        cg = cmat[:, g * SSM_STATE:(g + 1) * SSM_STATE]
        cb = _dot_nt(cg, bg)
        for hh in range(heads_per_group):
            h = g * heads_per_group + hh
            col = acum[:, h:h + 1]
            row = acum_t[h:h + 1, :]
            decay = jnp.exp(jnp.where(causal, col - row, NEG))
            xh = xs[:, h * HEAD_DIM:(h + 1) * HEAD_DIM]
            xdt = xh * dt[:, h:h + 1]
            hprev = state_ref[0, h]
            y = _dot(cb * decay, xdt)
            y = y + _dot_nt(cg, hprev) * jnp.exp(col)
            y = y + dskip_ref[:, h:h + 1] * xh
            ys.append(y)
            tot = total[:, h:h + 1]
            st = _dot_tn(xdt * jnp.exp(tot - col), bg)
            state_ref[0, h] = jnp.exp(tot) * hprev + st
    y = jnp.concatenate(ys, axis=1) * _silu(zbuf_ref[...])
    gw = MIX_DIM // SSM_GROUPS
    outs = []
    for g in range(SSM_GROUPS):
        yg = y[:, g * gw:(g + 1) * gw]
        ms = jnp.mean(yg * yg, axis=-1, keepdims=True)
        outs.append(yg * lax.rsqrt(ms + EPS) * norm_ref[:, g * gw:(g + 1) * gw])
    y_ref[0] = jnp.concatenate(outs, axis=1)[:treal]

    newconv_ref[0] = ext_ref[pre + treal - hist:pre + treal, :]
    ext_ref[0:pre, :] = ext_ref[treal:treal + pre, :]
    newh_ref[...] = state_ref[...]


def _ssd(h3, hist_conv, h0, conv_w, conv_b, dtb, alog, dskip, norm, *, tc, treal):
    b, t, _ = h3.shape
    full2 = lambda i, c: (0, 0)
    return pl.pallas_call(
        functools.partial(_ssd_kernel, tc=tc, treal=treal),
        grid=(b, t // treal),
        in_specs=[pl.BlockSpec((1, treal, MIX_DIM), lambda i, c: (i, c, C_Z // MIX_DIM)),
                  pl.BlockSpec((1, treal, XBC_DIM), lambda i, c: (i, c, C_XBC // XBC_DIM)),
                  pl.BlockSpec((1, treal, LANES), lambda i, c: (i, c, C_DT // LANES)),
                  pl.BlockSpec((1, CONV_W - 1, XBC_DIM), lambda i, c: (i, 0, 0)),
                  pl.BlockSpec((1, N_HEADS, HEAD_DIM, SSM_STATE), lambda i, c: (i, 0, 0, 0)),
                  pl.BlockSpec((CONV_W, XBC_DIM), full2),
                  pl.BlockSpec((1, XBC_DIM), full2),
                  pl.BlockSpec((1, LANES), full2),
                  pl.BlockSpec((1, LANES), full2),
                  pl.BlockSpec((1, LANES), full2),
                  pl.BlockSpec((1, MIX_DIM), full2)],
        out_specs=[pl.BlockSpec((1, treal, MIX_DIM), lambda i, c: (i, c, 0)),
                   pl.BlockSpec((1, CONV_W - 1, XBC_DIM), lambda i, c: (i, 0, 0)),
                   pl.BlockSpec((1, N_HEADS, HEAD_DIM, SSM_STATE), lambda i, c: (i, 0, 0, 0))],
        out_shape=[jax.ShapeDtypeStruct((b, t, MIX_DIM), F32),
                   jax.ShapeDtypeStruct((b, CONV_W - 1, XBC_DIM), F32),
                   jax.ShapeDtypeStruct((b, N_HEADS, HEAD_DIM, SSM_STATE), F32)],
        scratch_shapes=[pltpu.VMEM((8 + tc, XBC_DIM), F32),
                        pltpu.VMEM((tc, MIX_DIM), F32),
                        pltpu.VMEM((tc, LANES), F32),
                        pltpu.VMEM((1, N_HEADS, HEAD_DIM, SSM_STATE), F32)],
        compiler_params=_cparams(("parallel", "arbitrary")),
        name="ssd",
    )(h3, h3, h3, hist_conv, h0, conv_w, conv_b, dtb, alog, dskip, norm)


def _mono_key(s):
    s = jnp.where(s == 0.0, 0.0, s)
    b = lax.bitcast_convert_type(s, I32)
    return b ^ ((b >> 31) & INT_MAX)


def _select(keys_ref, ntiles, rows, nsel, row_ok):
    def count(pred):
        def body(j, acc):
            off = pl.multiple_of(j * LANES, LANES)
            kt = keys_ref[:, pl.ds(off, LANES)]
            col = off + lax.broadcasted_iota(I32, (rows, LANES), 1)
            return acc + jnp.where(pred(kt, col), 1.0, 0.0)
        acc = lax.fori_loop(0, ntiles, body, jnp.zeros((rows, LANES), F32))
        return jnp.sum(acc, axis=1, keepdims=True)

    kf = float(nsel)
    c0 = count(lambda kt, col: kt >= 0)
    prefix = jnp.where(c0 >= kf, 0, INT_MIN).astype(I32)

    def bit_body(it, prefix):
        cand = prefix | jnp.left_shift(jnp.int32(1), 30 - it)
        cnt = count(lambda kt, col: kt >= cand)
        return jnp.where(cnt >= kf, cand, prefix)

    v = lax.fori_loop(0, 31, bit_body, prefix)
    cgt = count(lambda kt, col: kt > v)
    ceq = count(lambda kt, col: kt == v)
    need = (cgt + ceq > kf) & (v != INT_MIN) & row_ok
    want = kf - cgt

    def tie_search():
        def body(it, x):
            cand = x | jnp.left_shift(jnp.int32(1), 13 - it)
            cnt = count(lambda kt, col: (kt == v) & (col < cand))
            return jnp.where(cnt < want, cand, x)
        x = lax.fori_loop(0, 14, body, jnp.zeros((rows, 1), I32))
        return jnp.where(need, x, INT_MAX)

    any_need = jnp.max(jnp.where(need, 1.0, 0.0)) > 0.0
    jlim = lax.cond(any_need, tie_search, lambda: jnp.full((rows, 1), INT_MAX, I32))
    return v, jlim


def _attn_prompt_kernel(far_ref, qidx_ref, kw_ref, q_ref, kidx_ref, k_ref, v_ref, bd_ref, bs_ref, o_ref,
                        keys_ref, m_ref, l_ref, acc_ref, *, tq, nsel):
    i = pl.program_id(0)
    rowid = i * tq + lax.broadcasted_iota(I32, (tq, LANES), 0)
    lane = lax.broadcasted_iota(I32, (tq, LANES), 1)
    wsc = kw_ref[:, IDX_DIM:IDX_DIM + IDX_HEADS] * IDX_SCALE

    def score_body(j, carry):
        off = pl.multiple_of(j * LANES, LANES)
        kt = kidx_ref[pl.ds(off, LANES), :]
        s = jnp.zeros((tq, LANES), F32)
        for hh in range(IDX_HEADS):
            sc = _dot_nt(qidx_ref[:, hh * IDX_DIM:(hh + 1) * IDX_DIM], kt)
            s = s + jnp.maximum(sc, 0.0) * wsc[:, hh:hh + 1]
        key = jnp.where(off + lane <= rowid, _mono_key(s), INT_MIN)
        keys_ref[:, pl.ds(off, LANES)] = key
        return carry

    lax.fori_loop(0, i + 1, score_body, 0)
    thr, jlim = _select(keys_ref, i + 1, tq, nsel, True)

    m_ref[...] = jnp.full(m_ref.shape, NEG, F32)
    l_ref[...] = jnp.zeros(l_ref.shape, F32)
    acc_ref[...] = jnp.zeros(acc_ref.shape, F32)

    def tile(j, bias_of):
        off = pl.multiple_of(j * LANES, LANES)
        kt = keys_ref[:, pl.ds(off, LANES)]
        col = off + lane
        sel = ((kt > thr) | ((kt == thr) & (col <= jlim))) & (col <= rowid)
        kk = k_ref[pl.ds(off, LANES), :]
        vv = v_ref[pl.ds(off, LANES), :]
        for h in range(N_HEADS):
            hs = slice(h * HEAD_DIM, (h + 1) * HEAD_DIM)
            lg = _dot_nt(q_ref[:, hs], kk[:, hs]) * ATT_SCALE + bias_of(h)
            lg = jnp.where(sel, lg, NEG)
            m_old = m_ref[h]
            m_new = jnp.maximum(m_old, jnp.max(lg, axis=1, keepdims=True))
            alpha = jnp.exp(m_old - m_new)
            p = jnp.where(sel, jnp.exp(lg - m_new), 0.0)
            l_ref[h] = alpha * l_ref[h] + jnp.sum(p, axis=1, keepdims=True)
            acc_ref[:, hs] = alpha[:, :HEAD_DIM] * acc_ref[:, hs] + _dot(p.astype(BF16), vv[:, hs])
            m_ref[h] = m_new

    def far_body(j, carry):
        tile(j, lambda h: far_ref[h])
        return carry

    lax.fori_loop(0, i - 1, far_body, 0)

    @pl.when(i >= 1)
    def _():
        tile(i - 1, lambda h: bs_ref[h])

    tile(i, lambda h: bd_ref[h])
    outs = [acc_ref[:, h * HEAD_DIM:(h + 1) * HEAD_DIM] / l_ref[h][:, :HEAD_DIM] for h in range(N_HEADS)]
    o_ref[...] = jnp.concatenate(outs, axis=1)


def _attn_prompt(far, qidx, h2, q, kidx, k, v, bias_diag, bias_sub, nsel):
    t = q.shape[0]
    tq = LANES
    whole = lambda i: (0, 0)
    return pl.pallas_call(
        functools.partial(_attn_prompt_kernel, tq=tq, nsel=nsel),
        grid=(t // tq,),
        in_specs=[pl.BlockSpec(memory_space=pltpu.SMEM),
                  pl.BlockSpec((tq, IDX_HEADS * IDX_DIM), lambda i: (i, 0)),
                  pl.BlockSpec((tq, LANES), lambda i: (i, C_KW // LANES)),
                  pl.BlockSpec((tq, MIX_DIM), lambda i: (i, 0)),
                  pl.BlockSpec((t, IDX_DIM), whole),
                  pl.BlockSpec((t, MIX_DIM), whole),
                  pl.BlockSpec((t, MIX_DIM), whole),
                  pl.BlockSpec((N_HEADS, tq, LANES), lambda i: (0, 0, 0)),
                  pl.BlockSpec((N_HEADS, tq, LANES), lambda i: (0, 0, 0))],
        out_specs=pl.BlockSpec((tq, MIX_DIM), lambda i: (i, 0)),
        out_shape=jax.ShapeDtypeStruct((t, MIX_DIM), F32),
        scratch_shapes=[pltpu.VMEM((tq, t), I32),
                        pltpu.VMEM((N_HEADS, tq, LANES), F32),
                        pltpu.VMEM((N_HEADS, tq, LANES), F32),
                        pltpu.VMEM((tq, MIX_DIM), F32)],
        compiler_params=_cparams(("parallel",)),
        name="attn_prompt",
    )(far, qidx, h2, q, kidx, k, v, bias_diag, bias_sub)


QROWS = 8


def _sel_sample_kernel(pt_ref, qst_ref, wcol_ref, page_ref, knew_ref, mask_ref, keys_ref, *, nsel, treal):
    j = pl.program_id(1)
    npages = pl.num_programs(1)
    ntiles = keys_ref.shape[1] // LANES

    def score(kt):
        r = jnp.maximum(_dot_nt(qst_ref[0], kt), 0.0) * (wcol_ref[0] * IDX_SCALE)
        s = r[0:QROWS]
        for hh in range(1, IDX_HEADS):
            s = s + r[hh * QROWS:(hh + 1) * QROWS]
        return _mono_key(s)

    keys_ref[:, pl.ds(pl.multiple_of(j * LANES, LANES), LANES)] = score(page_ref[0, 0].astype(BF16))

    @pl.when(j == npages - 1)
    def _():
        rowq = lax.broadcasted_iota(I32, (QROWS, LANES), 0)
        lane = lax.broadcasted_iota(I32, (QROWS, LANES), 1)
        knew = jnp.where((lane < treal) & (lane <= rowq), score(knew_ref[0]), INT_MIN)
        keys_ref[:, (ntiles - 1) * LANES:] = knew
        row_ok = lax.broadcasted_iota(I32, (QROWS, 1), 0) < treal
        thr, jlim = _select(keys_ref, ntiles, QROWS, nsel, row_ok)

        def body(jt, carry):
            off = pl.multiple_of(jt * LANES, LANES)
            kt = keys_ref[:, pl.ds(off, LANES)]
            sel = (kt > thr) | ((kt == thr) & (off + lane <= jlim))
            mask_ref[0, :, pl.ds(off, LANES)] = jnp.where(sel, 1.0, 0.0)
            return carry

        lax.fori_loop(0, ntiles, body, 0)


def _sel_sample(page_table, qst, wcol, cache_kidx, knew, layer, nsel, treal):
    b, npages = page_table.shape
    nk = (npages + 1) * PAGE
    grid_spec = pltpu.PrefetchScalarGridSpec(
        num_scalar_prefetch=1,
        grid=(b, npages),
        in_specs=[pl.BlockSpec((1, IDX_HEADS * QROWS, IDX_DIM), lambda i, j, pt: (i, 0, 0)),
                  pl.BlockSpec((1, IDX_HEADS * QROWS, 1), lambda i, j, pt: (i, 0, 0)),
                  pl.BlockSpec((1, 1, PAGE, IDX_DIM), lambda i, j, pt: (layer, pt[i, j], 0, 0)),
                  pl.BlockSpec((1, PAGE, IDX_DIM), lambda i, j, pt: (i, 0, 0))],
        out_specs=pl.BlockSpec((1, QROWS, nk), lambda i, j, pt: (i, 0, 0)),
        scratch_shapes=[pltpu.VMEM((QROWS, nk), I32)],
    )
    return pl.pallas_call(
        functools.partial(_sel_sample_kernel, nsel=nsel, treal=treal),
        grid_spec=grid_spec,
        out_shape=jax.ShapeDtypeStruct((b, QROWS, nk), F32),
        compiler_params=_cparams(("parallel", "arbitrary")),
        name="sel_sample",
    )(page_table, qst, wcol, cache_kidx, knew)


def _attn_sample_kernel(pt_ref, qbd_ref, kp_ref, vp_ref, mask_ref, masknew_ref, bias_ref, biasnew_ref,
                        knew_ref, vnew_ref, o_ref, m_ref, l_ref, acc_ref):
    j = pl.program_id(1)
    rows = N_HEADS * QROWS

    @pl.when(j == 0)
    def _():
        m_ref[...] = jnp.full(m_ref.shape, NEG, F32)
        l_ref[...] = jnp.zeros(l_ref.shape, F32)
        acc_ref[...] = jnp.zeros(acc_ref.shape, F32)

    def tile(kk, vv, mask8, bias):
        sel = jnp.broadcast_to(mask8[None], (N_HEADS, QROWS, LANES)).reshape(rows, LANES) > 0.5
        lg = _dot_nt(qbd_ref[0], kk) * ATT_SCALE + bias
        lg = jnp.where(sel, lg, NEG)
        m_old = m_ref[...]
        m_new = jnp.maximum(m_old, jnp.max(lg, axis=1, keepdims=True))
        alpha = jnp.exp(m_old - m_new)
        p = jnp.where(sel, jnp.exp(lg - m_new), 0.0)
        l_ref[...] = alpha * l_ref[...] + jnp.sum(p, axis=1, keepdims=True)
        acc_ref[...] = alpha[:, 0:1] * acc_ref[...] + _dot(p.astype(BF16), vv)
        m_ref[...] = m_new

    tile(kp_ref[0, 0].astype(BF16), vp_ref[0, 0].astype(BF16), mask_ref[0], bias_ref[0])

    @pl.when(j == pl.num_programs(1) - 1)
    def _():
        tile(knew_ref[0], vnew_ref[0], masknew_ref[0], biasnew_ref[0])
        outs = []
        for h in range(N_HEADS):
            lh = l_ref[h * QROWS:(h + 1) * QROWS, 0:HEAD_DIM]
            lh = jnp.where(lh > 0.0, lh, 1.0)
            outs.append(acc_ref[h * QROWS:(h + 1) * QROWS, h * HEAD_DIM:(h + 1) * HEAD_DIM] / lh)
        o_ref[0] = jnp.concatenate(outs, axis=1)


def _attn_sample(page_table, qbd, cache_k, cache_v, mask, bias3, knew, vnew, layer):
    b, npages = page_table.shape
    rows = N_HEADS * QROWS
    grid_spec = pltpu.PrefetchScalarGridSpec(
        num_scalar_prefetch=1,
        grid=(b, npages),
        in_specs=[pl.BlockSpec((1, rows, MIX_DIM), lambda i, j, pt: (i, 0, 0)),
                  pl.BlockSpec((1, 1, PAGE, MIX_DIM), lambda i, j, pt: (layer, pt[i, j], 0, 0)),
                  pl.BlockSpec((1, 1, PAGE, MIX_DIM), lambda i, j, pt: (layer, pt[i, j], 0, 0)),
                  pl.BlockSpec((1, QROWS, LANES), lambda i, j, pt: (i, 0, j)),
                  pl.BlockSpec((1, QROWS, LANES), lambda i, j, pt: (i, 0, npages)),
                  pl.BlockSpec((1, rows, LANES), lambda i, j, pt: (jnp.where(j == npages - 1, 1, 0), 0, 0)),
                  pl.BlockSpec((1, rows, LANES), lambda i, j, pt: (2, 0, 0)),
                  pl.BlockSpec((1, PAGE, MIX_DIM), lambda i, j, pt: (i, 0, 0)),
                  pl.BlockSpec((1, PAGE, MIX_DIM), lambda i, j, pt: (i, 0, 0))],
        out_specs=pl.BlockSpec((1, QROWS, MIX_DIM), lambda i, j, pt: (i, 0, 0)),
        scratch_shapes=[pltpu.VMEM((rows, LANES), F32),
                        pltpu.VMEM((rows, LANES), F32),
                        pltpu.VMEM((rows, MIX_DIM), F32)],
    )
    return pl.pallas_call(
        _attn_sample_kernel,
        grid_spec=grid_spec,
        out_shape=jax.ShapeDtypeStruct((b, QROWS, MIX_DIM), F32),
        compiler_params=_cparams(("parallel", "arbitrary")),
        name="attn_sample",
    )(page_table, qbd, cache_k, cache_v, mask, mask, bias3, bias3, knew, vnew)


def _bucket_table(n):
    d = np.arange(n)
    nf = np.maximum(d, 1).astype(np.float32)
    half = REL_BUCKETS // 2
    large = half + (np.log(nf / np.float32(half)) / np.float32(math.log(REL_MAX_DIST / half))
                    * np.float32(REL_BUCKETS - half)).astype(np.int32)
    return np.where(d < half, d, np.minimum(large, REL_BUCKETS - 1)).astype(np.int32)


def _pack_w_in(w):
    p, z, xbc, dt, q, k, v, qi, ki, wi, u, vs = jnp.split(w, np.cumsum(IN_SPLITS)[:-1].tolist(), axis=1)
    zeros = lambda n: jnp.zeros((w.shape[0], n), w.dtype)
    packed = jnp.concatenate(
        [p, z, xbc, q, k, v, u, vs, qi, ki, wi, zeros(LANES - IDX_DIM - IDX_HEADS), dt, zeros(LANES - N_HEADS)],
        axis=1)
    return packed.astype(BF16)


def _pad_lanes(x):
    return jnp.pad(x.reshape(1, -1), ((0, 0), (0, LANES - x.shape[-1])))


def _pad_rows(x, rows):
    return jnp.pad(x, ((0, 0), (0, rows - x.shape[1])) + ((0, 0),) * (x.ndim - 2))


def kernel(x_prompt, x_sample, state_pool, state_conv, state_ssm, cache_k, cache_v, cache_kidx, page_table,
           rel_bias, norm_mix, w_in, pool_w, pool_scale, conv_w, conv_b, dt_bias, a_log, d_skip, ssm_norm,
           sgu_w, sgu_b, w_out, norm_mlp, mlp_w1, mlp_w2, norm_final):
    depth = w_in.shape[0]
    bp, t, _ = x_prompt.shape
    assert bp == 1
    bs, ts, _ = x_sample.shape
    npages = page_table.shape[1]
    past = npages * PAGE
    n_pool = cache_k.shape[1]
    tc = LANES
    nsel_p = min(TOPK_MAX, t // 4)
    nsel_s = min(TOPK_MAX, (past + ts) // 4)

    bucket = _bucket_table(2 * LANES)
    r = np.arange(LANES)[:, None]
    c = np.arange(LANES)[None, :]
    bias_diag = jnp.transpose(rel_bias[bucket[np.maximum(r - c, 0)]], (2, 0, 1))
    bias_sub = jnp.transpose(rel_bias[bucket[np.minimum(LANES + r - c, 2 * LANES - 1)]], (2, 0, 1))
    far = rel_bias[REL_BUCKETS - 1]
    qq = np.arange(QROWS)[:, None]
    last_idx = bucket[np.minimum(PAGE + qq - c, 2 * LANES - 1)]
    new_idx = bucket[np.maximum(qq - c, 0)]
    as_rows = lambda b3: jnp.transpose(b3, (2, 0, 1)).reshape(N_HEADS * QROWS, LANES)
    bias3 = jnp.stack([jnp.broadcast_to(far[:, None, None], (N_HEADS, QROWS, LANES)).reshape(-1, LANES),
                       as_rows(rel_bias[last_idx]), as_rows(rel_bias[new_idx])])

    cache_k4 = cache_k.reshape(depth, n_pool, PAGE, MIX_DIM)
    cache_v4 = cache_v.reshape(depth, n_pool, PAGE, MIX_DIM)

    xp = x_prompt.reshape(t, D_MODEL)
    xs = x_sample.reshape(bs * ts, D_MODEL)
    zero_pool = jnp.zeros((1, POOL_HIST, MIX_DIM), F32)
    zero_conv = jnp.zeros((1, CONV_W - 1, XBC_DIM), F32)
    zero_ssm = jnp.zeros((1, N_HEADS, HEAD_DIM, SSM_STATE), F32)
    eye = jnp.eye(N_HEADS, dtype=F32)
    tm_p = 512
    tm_s = bs * ts
    outs_p, outs_s = [], []

    for l in range(depth):
        wl_in = _pack_w_in(w_in[l])
        wl_out = w_out[l].astype(BF16)
        w1 = mlp_w1[l].astype(BF16)
        w2 = mlp_w2[l].astype(BF16)
        g_mix = norm_mix[l].reshape(1, D_MODEL)
        g_mlp = norm_mlp[l].reshape(1, D_MODEL)
        g_out = norm_final.reshape(1, D_MODEL) if l == depth - 1 else g_mlp
        pscale = pool_scale[l].reshape(1, MIX_DIM)
        cb = conv_b[l].reshape(1, XBC_DIM)
        dtb, alog, dsk = _pad_lanes(dt_bias[l]), _pad_lanes(a_log[l]), _pad_lanes(d_skip[l])
        snorm = ssm_norm[l].reshape(1, MIX_DIM)
        sgu_b_col = sgu_b[l][:, :, None]
        final = l == depth - 1

        hp = _inproj(xp, g_mix, wl_in, tm=min(1024, t))
        hp3 = hp.reshape(1, t, H_PACKED)
        y_pool, new_pool = _pool(hp3, zero_pool, pool_w[l], pscale, tc=tc, treal=tc, pos0=0)
        y_ssm, new_conv, new_h = _ssd(hp3, zero_conv, zero_ssm, conv_w[l], cb, dtb, alog, dsk, snorm,
                                      tc=tc, treal=tc)
        y_sgu, _ = _sgu(hp3, sgu_w[l], sgu_b_col, tc=tc, treal=tc)
        kf = hp[:, C_K:C_K + MIX_DIM]
        vf = hp[:, C_V:C_V + MIX_DIM]
        kif = hp[:, C_KW:C_KW + IDX_DIM]
        y_att = _attn_prompt(far, hp[:, C_QIDX:C_QIDX + IDX_HEADS * IDX_DIM].astype(BF16), hp,
                             hp[:, C_Q:C_Q + MIX_DIM].astype(BF16), kif.astype(BF16), kf.astype(BF16),
                             vf.astype(BF16), bias_diag, bias_sub, nsel_p)
        x1, xn = _outproj(xp, [y_pool[0], y_ssm[0], y_att, y_sgu[0]], wl_out, g_mlp, tm=256)
        xp = _mlp(xn, w1, w2, x1, g_out, tm=tm_p, final_norm=final)
        outs_p.append((new_pool, new_conv, new_h, kf.reshape(1, t, N_HEADS, HEAD_DIM),
                       vf.reshape(1, t, N_HEADS, HEAD_DIM), kif.reshape(1, t, IDX_DIM)))

        hs = _inproj(xs, g_mix, wl_in, tm=tm_s)
        hs3 = hs.reshape(bs, ts, H_PACKED)
        y_pool, new_pool = _pool(hs3, state_pool[l], pool_w[l], pscale, tc=tc, treal=ts, pos0=past)
        y_ssm, new_conv, new_h = _ssd(hs3, state_conv[l], state_ssm[l], conv_w[l], cb, dtb, alog, dsk, snorm,
                                      tc=tc, treal=ts)
        y_sgu, v_rows = _sgu(hs3, sgu_w[l], sgu_b_col, tc=tc, treal=ts)
        kf = hs3[:, :, C_K:C_K + MIX_DIM]
        vf = hs3[:, :, C_V:C_V + MIX_DIM]
        kif = hs3[:, :, C_KW:C_KW + IDX_DIM]
        qi = hs3[:, :, C_QIDX:C_QIDX + IDX_HEADS * IDX_DIM].reshape(bs, ts, IDX_HEADS, IDX_DIM)
        qst = _pad_rows(jnp.transpose(qi, (0, 2, 1, 3)).reshape(bs * IDX_HEADS, ts, IDX_DIM), QROWS)
        qst = qst.reshape(bs, IDX_HEADS * QROWS, IDX_DIM).astype(BF16)
        wi = hs3[:, :, C_KW + IDX_DIM:C_KW + IDX_DIM + IDX_HEADS]
        wcol = _pad_rows(jnp.transpose(wi, (0, 2, 1)).reshape(bs * IDX_HEADS, ts), QROWS)
        wcol = wcol.reshape(bs, IDX_HEADS * QROWS, 1)
        mask = _sel_sample(page_table, qst, wcol, cache_kidx, _pad_rows(kif, PAGE).astype(BF16), l, nsel_s, ts)
        qh = hs3[:, :, C_Q:C_Q + MIX_DIM].reshape(bs, ts, N_HEADS, HEAD_DIM)
        qbd = jnp.einsum('bqhd,hg->bhqgd', qh, eye).reshape(bs * N_HEADS, ts, MIX_DIM)
        qbd = _pad_rows(qbd, QROWS).reshape(bs, N_HEADS * QROWS, MIX_DIM).astype(BF16)
        att = _attn_sample(page_table, qbd, cache_k4, cache_v4, mask, bias3,
                           _pad_rows(kf, PAGE).astype(BF16), _pad_rows(vf, PAGE).astype(BF16), l)
        y_att = att[:, :ts].reshape(bs * ts, MIX_DIM)
        flat = lambda y: y.reshape(bs * ts, MIX_DIM)
        x1, xn = _outproj(xs, [flat(y_pool), flat(y_ssm), y_att, flat(y_sgu)], wl_out, g_mlp, tm=tm_s)
        xs = _mlp(xn, w1, w2, x1, g_out, tm=tm_s, final_norm=final)
        outs_s.append((new_pool, new_conv, new_h, kf.reshape(bs, ts, N_HEADS, HEAD_DIM),
                       vf.reshape(bs, ts, N_HEADS, HEAD_DIM), kif, v_rows))

    stack = lambda outs, i: jnp.stack([o[i] for o in outs])
    return (xp.reshape(1, t, D_MODEL), xs.reshape(bs, ts, D_MODEL),
            *[stack(outs_p, i) for i in range(6)],
            *[stack(outs_s, i) for i in range(7)])
```

```python
import functools
import math

import numpy as np
import jax
import jax.numpy as jnp
from jax import lax
from jax.experimental import pallas as pl
from jax.experimental.pallas import tpu as pltpu

F32 = jnp.float32
BF16 = jnp.bfloat16
I32 = jnp.int32

LANES = 128
SUBLANES = 8
D_MODEL = 2048
N_HEADS = 8
HEAD_DIM = 64
MIX_DIM = 512
POOL_WINDOWS = (2, 4, 8, 16)
POOL_HIST = 15
CONV_W = 4
SSM_GROUPS = 2
SSM_STATE = 128
XBC_DIM = MIX_DIM + 2 * SSM_GROUPS * SSM_STATE
IDX_HEADS = 4
IDX_DIM = 64
TOPK_MAX = 256
PAGE = 128
REL_BUCKETS = 32
REL_MAX_DIST = 128
ATT_SCALE = HEAD_DIM ** -0.5
IDX_SCALE = (IDX_HEADS * IDX_DIM) ** -0.5
SGU_GROUPS = 4
D_FF = 4 * D_MODEL
EPS = 1e-6
NEG = -1e30
INT_MIN = -(2 ** 31)
INT_MAX = 2 ** 31 - 1
IN_SPLITS = (512, 512, 1024, 8, 512, 512, 512, 256, 64, 4, 512, 512)

C_P, C_Z, C_XBC, C_Q, C_K, C_V, C_U, C_VS, C_QIDX, C_KW, C_DT = (
    0, 512, 1024, 2048, 2560, 3072, 3584, 4096, 4608, 4864, 4992)
H_PACKED = 5120
VMEM_LIMIT = 56 * 1024 * 1024


def _cparams(sem):
    return pltpu.CompilerParams(dimension_semantics=sem, vmem_limit_bytes=VMEM_LIMIT)


def _dot(a, b):
    return jnp.dot(a, b, preferred_element_type=F32)


def _dot_nt(a, b):
    return lax.dot_general(a, b, (((1,), (1,)), ((), ())), preferred_element_type=F32)


def _dot_tn(a, b):
    return lax.dot_general(a, b, (((0,), (0,)), ((), ())), preferred_element_type=F32)


def _silu(x):
    return x * (1.0 / (1.0 + jnp.exp(-x)))


def _gelu_tanh(x):
    return 0.5 * x * (1.0 + jnp.tanh(math.sqrt(2.0 / math.pi) * (x + 0.044715 * (x * x * x))))


def _softplus(x):
    return jnp.maximum(x, 0.0) + jnp.log1p(jnp.exp(-jnp.abs(x)))


def _rms(x, g):
    ms = jnp.mean(x * x, axis=-1, keepdims=True)
    return x * lax.rsqrt(ms + EPS) * g


def _inproj_kernel(x_ref, g_ref, w_ref, o_ref, xn_ref):
    @pl.when(pl.program_id(1) == 0)
    def _():
        xn_ref[...] = _rms(x_ref[...], g_ref[...]).astype(BF16)

    o_ref[...] = _dot(xn_ref[...], w_ref[...])


def _inproj(x, g, w, tm, tn=512):
    m = x.shape[0]
    n = w.shape[1]
    return pl.pallas_call(
        _inproj_kernel,
        grid=(m // tm, n // tn),
        in_specs=[pl.BlockSpec((tm, D_MODEL), lambda i, j: (i, 0)),
                  pl.BlockSpec((1, D_MODEL), lambda i, j: (0, 0)),
                  pl.BlockSpec((D_MODEL, tn), lambda i, j: (0, j))],
        out_specs=pl.BlockSpec((tm, tn), lambda i, j: (i, j)),
        out_shape=jax.ShapeDtypeStruct((m, n), F32),
        scratch_shapes=[pltpu.VMEM((tm, D_MODEL), BF16)],
        compiler_params=_cparams(("parallel", "arbitrary")),
        name="inproj",
    )(x, g, w)


def _outproj_kernel(x_ref, y0_ref, y1_ref, y2_ref, y3_ref, w_ref, g_ref, x1_ref, xn_ref):
    y = jnp.concatenate([y0_ref[...], y1_ref[...], y2_ref[...], y3_ref[...]], axis=1).astype(BF16)
    x1 = x_ref[...] + _dot(y, w_ref[...])
    x1_ref[...] = x1
    xn_ref[...] = _rms(x1, g_ref[...]).astype(BF16)


def _outproj(x, ys, w, g, tm):
    m = x.shape[0]
    row = lambda i: (i, 0)
    return pl.pallas_call(
        _outproj_kernel,
        grid=(m // tm,),
        in_specs=[pl.BlockSpec((tm, D_MODEL), row)]
                 + [pl.BlockSpec((tm, MIX_DIM), row)] * 4
                 + [pl.BlockSpec((D_MODEL, D_MODEL), lambda i: (0, 0)),
                    pl.BlockSpec((1, D_MODEL), lambda i: (0, 0))],
        out_specs=[pl.BlockSpec((tm, D_MODEL), row), pl.BlockSpec((tm, D_MODEL), row)],
        out_shape=[jax.ShapeDtypeStruct((m, D_MODEL), F32), jax.ShapeDtypeStruct((m, D_MODEL), BF16)],
        compiler_params=_cparams(("parallel",)),
        name="outproj",
    )(x, *ys, w, g)


def _mlp_kernel(xn_ref, w1_ref, w2_ref, x1_ref, g_ref, o_ref, acc_ref, *, final_norm):
    j = pl.program_id(1)

    @pl.when(j == 0)
    def _():
        acc_ref[...] = jnp.zeros_like(acc_ref)

    hid = _dot(xn_ref[...], w1_ref[...])
    hid = jnp.square(jnp.maximum(hid, 0.0)).astype(BF16)
    acc_ref[...] += _dot(hid, w2_ref[...])

    @pl.when(j == pl.num_programs(1) - 1)
    def _():
        x2 = x1_ref[...] + acc_ref[...]
        o_ref[...] = _rms(x2, g_ref[...]) if final_norm else x2


def _mlp(xn, w1, w2, x1, g, tm, final_norm, tf=512):
    m = xn.shape[0]
    return pl.pallas_call(
        functools.partial(_mlp_kernel, final_norm=final_norm),
        grid=(m // tm, D_FF // tf),
        in_specs=[pl.BlockSpec((tm, D_MODEL), lambda i, j: (i, 0)),
                  pl.BlockSpec((D_MODEL, tf), lambda i, j: (0, j)),
                  pl.BlockSpec((tf, D_MODEL), lambda i, j: (j, 0)),
                  pl.BlockSpec((tm, D_MODEL), lambda i, j: (i, 0)),
                  pl.BlockSpec((1, D_MODEL), lambda i, j: (0, 0))],
        out_specs=pl.BlockSpec((tm, D_MODEL), lambda i, j: (i, 0)),
        out_shape=jax.ShapeDtypeStruct((m, D_MODEL), F32),
        scratch_shapes=[pltpu.VMEM((tm, D_MODEL), F32)],
        compiler_params=_cparams(("parallel", "arbitrary")),
        name="mlp",
    )(xn, w1, w2, x1, g)


def _pool_kernel(p_ref, hist_ref, w_ref, scale_ref, y_ref, newhist_ref, ext_ref, *, tc, treal, pos0):
    c = pl.program_id(1)
    hrows = POOL_HIST + 1

    @pl.when(c == 0)
    def _():
        ext_ref[0:1, :] = jnp.zeros((1, MIX_DIM), F32)
        ext_ref[1:hrows, :] = hist_ref[0]

    if treal < tc:
        ext_ref[hrows:hrows + tc, :] = jnp.zeros((tc, MIX_DIM), F32)
    ext_ref[hrows:hrows + treal, :] = p_ref[0]

    pos = pos0 + c * treal + lax.broadcasted_iota(I32, (tc, 1), 0)
    for g, win in enumerate(POOL_WINDOWS):
        cols = slice(g * LANES, (g + 1) * LANES)
        cur = ext_ref[hrows:hrows + tc, cols]
        s = cur
        for j in range(1, win):
            s = s + ext_ref[hrows - j:hrows - j + tc, cols]
        cnt = jnp.minimum(pos + 1, win).astype(F32)
        diff = s / cnt - cur
        y = _dot(diff, w_ref[g]) * scale_ref[:, cols]
        y_ref[0, :, cols] = y[:treal]

    newhist_ref[0] = ext_ref[treal + 1:treal + hrows, :]
    ext_ref[0:hrows, :] = ext_ref[treal:treal + hrows, :]


def _pool(h3, hist, pool_w, pool_scale, *, tc, treal, pos0):
    b, t, _ = h3.shape
    return pl.pallas_call(
        functools.partial(_pool_kernel, tc=tc, treal=treal, pos0=pos0),
        grid=(b, t // treal),
        in_specs=[pl.BlockSpec((1, treal, MIX_DIM), lambda i, c: (i, c, C_P // MIX_DIM)),
                  pl.BlockSpec((1, POOL_HIST, MIX_DIM), lambda i, c: (i, 0, 0)),
                  pl.BlockSpec((4, LANES, LANES), lambda i, c: (0, 0, 0)),
                  pl.BlockSpec((1, MIX_DIM), lambda i, c: (0, 0))],
        out_specs=[pl.BlockSpec((1, treal, MIX_DIM), lambda i, c: (i, c, 0)),
                   pl.BlockSpec((1, POOL_HIST, MIX_DIM), lambda i, c: (i, 0, 0))],
        out_shape=[jax.ShapeDtypeStruct((b, t, MIX_DIM), F32),
                   jax.ShapeDtypeStruct((b, POOL_HIST, MIX_DIM), F32)],
        scratch_shapes=[pltpu.VMEM((POOL_HIST + 1 + tc, MIX_DIM), F32)],
        compiler_params=_cparams(("parallel", "arbitrary")),
        name="pool",
    )(h3, hist, pool_w, pool_scale)


def _sgu_kernel(u_ref, v_ref, w_ref, b_ref, y_ref, vout_ref, ubuf_ref, vbuf_ref, *, tc, treal):
    if treal < tc:
        ubuf_ref[...] = jnp.zeros((tc, MIX_DIM), F32)
        vbuf_ref[...] = jnp.zeros((tc, MIX_DIM), F32)
    ubuf_ref[0:treal, :] = u_ref[0]
    vbuf_ref[0:treal, :] = v_ref[0]
    u = _gelu_tanh(ubuf_ref[...])
    v = _gelu_tanh(vbuf_ref[...])
    vout_ref[0] = v[:treal]
    r = lax.broadcasted_iota(I32, (tc, tc), 0)
    s = lax.broadcasted_iota(I32, (tc, tc), 1)
    for g in range(SGU_GROUPS):
        cols = slice(g * LANES, (g + 1) * LANES)
        w = jnp.where(s <= r, w_ref[g], 0.0)
        mixed = _dot(w, v[:, cols]) + b_ref[g]
        y_ref[0, :, cols] = (u[:, cols] * mixed)[:treal]


def _sgu(h3, sgu_w, sgu_b_col, *, tc, treal):
    b, t, _ = h3.shape
    return pl.pallas_call(
        functools.partial(_sgu_kernel, tc=tc, treal=treal),
        grid=(b, t // treal),
        in_specs=[pl.BlockSpec((1, treal, MIX_DIM), lambda i, c: (i, c, C_U // MIX_DIM)),
                  pl.BlockSpec((1, treal, MIX_DIM), lambda i, c: (i, c, C_VS // MIX_DIM)),
                  pl.BlockSpec((SGU_GROUPS, tc, tc), lambda i, c: (0, 0, 0)),
                  pl.BlockSpec((SGU_GROUPS, tc, 1), lambda i, c: (0, 0, 0))],
        out_specs=[pl.BlockSpec((1, treal, MIX_DIM), lambda i, c: (i, c, 0)),
                   pl.BlockSpec((1, treal, MIX_DIM), lambda i, c: (i, c, 0))],
        out_shape=[jax.ShapeDtypeStruct((b, t, MIX_DIM), F32),
                   jax.ShapeDtypeStruct((b, t, MIX_DIM), F32)],
        scratch_shapes=[pltpu.VMEM((tc, MIX_DIM), F32), pltpu.VMEM((tc, MIX_DIM), F32)],
        compiler_params=_cparams(("parallel", "parallel")),
        name="sgu",
    )(h3, h3, sgu_w, sgu_b_col)


def _cumsum_rows(a, n):
    row = lax.broadcasted_iota(I32, a.shape, 0)
    sh = 1
    while sh < n:
        a = a + jnp.where(row >= sh, pltpu.roll(a, sh, 0), 0.0)
        sh *= 2
    return a


def _ssd_kernel(z_ref, xbc_ref, dt_ref, hc_ref, h0_ref, convw_ref, convb_ref, dtb_ref, alog_ref, dskip_ref,
                norm_ref, y_ref, newconv_ref, newh_ref, ext_ref, zbuf_ref, dtbuf_ref, state_ref, *, tc, treal):
    c = pl.program_id(1)
    pre = SUBLANES
    hist = CONV_W - 1

    @pl.when(c == 0)
    def _():
        ext_ref[0:pre - hist, :] = jnp.zeros((pre - hist, XBC_DIM), F32)
        ext_ref[pre - hist:pre, :] = hc_ref[0]
        state_ref[...] = h0_ref[...]

    if treal < tc:
        ext_ref[pre:pre + tc, :] = jnp.zeros((tc, XBC_DIM), F32)
        zbuf_ref[...] = jnp.zeros((tc, MIX_DIM), F32)
        dtbuf_ref[...] = jnp.zeros((tc, LANES), F32)
    ext_ref[pre:pre + treal, :] = xbc_ref[0]
    zbuf_ref[0:treal, :] = z_ref[0]
    dtbuf_ref[0:treal, :] = dt_ref[0]

    conv = convb_ref[...]
    for j in range(CONV_W):
        conv = conv + ext_ref[pre - hist + j:pre - hist + j + tc, :] * convw_ref[j:j + 1, :]
    conv = _silu(conv)
    xs = conv[:, :MIX_DIM]
    bmat = conv[:, MIX_DIM:MIX_DIM + SSM_GROUPS * SSM_STATE]
    cmat = conv[:, MIX_DIM + SSM_GROUPS * SSM_STATE:]

    rowi = lax.broadcasted_iota(I32, (tc, LANES), 0)
    dt = _softplus(dtbuf_ref[...] + dtb_ref[...])
    if treal < tc:
        dt = jnp.where(rowi < treal, dt, 0.0)
    a = dt * (-jnp.exp(alog_ref[...]))
    acum = _cumsum_rows(a, tc)
    acum_t = jnp.transpose(acum)
    total = acum[tc - 1:tc, :]
    causal = lax.broadcasted_iota(I32, (tc, tc), 1) <= lax.broadcasted_iota(I32, (tc, tc), 0)

    ys = []
    heads_per_group = N_HEADS // SSM_GROUPS
    for g in range(SSM_GROUPS):
        bg = bmat[:, g * SSM_STATE:(g + 1) * SSM_STATE]
        cg = cmat[:, g * SSM_STATE:(g + 1) * SSM_STATE]
        cb = _dot_nt(cg, bg)
        for hh in range(heads_per_group):
            h = g * heads_per_group + hh
            col = acum[:, h:h + 1]
            row = acum_t[h:h + 1, :]
            decay = jnp.exp(jnp.where(causal, col - row, NEG))
            xh = xs[:, h * HEAD_DIM:(h + 1) * HEAD_DIM]
            xdt = xh * dt[:, h:h + 1]
            hprev = state_ref[0, h]
            y = _dot(cb * decay, xdt)
            y = y + _dot_nt(cg, hprev) * jnp.exp(col)
            y = y + dskip_ref[:, h:h + 1] * xh
            ys.append(y)
            tot = total[:, h:h + 1]
            st = _dot_tn(xdt * jnp.exp(tot - col), bg)
            state_ref[0, h] = jnp.exp(tot) * hprev + st
    y = jnp.concatenate(ys, axis=1) * _silu(zbuf_ref[...])
    gw = MIX_DIM // SSM_GROUPS
    outs = []
    for g in range(SSM_GROUPS):
        yg = y[:, g * gw:(g + 1) * gw]
        ms = jnp.mean(yg * yg, axis=-1, keepdims=True)
        outs.append(yg * lax.rsqrt(ms + EPS) * norm_ref[:, g * gw:(g + 1) * gw])
    y_ref[0] = jnp.concatenate(outs, axis=1)[:treal]

    newconv_ref[0] = ext_ref[pre + treal - hist:pre + treal, :]
    ext_ref[0:pre, :] = ext_ref[treal:treal + pre, :]
    newh_ref[...] = state_ref[...]


def _ssd(h3, hist_conv, h0, conv_w, conv_b, dtb, alog, dskip, norm, *, tc, treal):
    b, t, _ = h3.shape
    full2 = lambda i, c: (0, 0)
    return pl.pallas_call(
        functools.partial(_ssd_kernel, tc=tc, treal=treal),
        grid=(b, t // treal),
        in_specs=[pl.BlockSpec((1, treal, MIX_DIM), lambda i, c: (i, c, C_Z // MIX_DIM)),
                  pl.BlockSpec((1, treal, XBC_DIM), lambda i, c: (i, c, C_XBC // XBC_DIM)),
                  pl.BlockSpec((1, treal, LANES), lambda i, c: (i, c, C_DT // LANES)),
                  pl.BlockSpec((1, CONV_W - 1, XBC_DIM), lambda i, c: (i, 0, 0)),
                  pl.BlockSpec((1, N_HEADS, HEAD_DIM, SSM_STATE), lambda i, c: (i, 0, 0, 0)),
                  pl.BlockSpec((CONV_W, XBC_DIM), full2),
                  pl.BlockSpec((1, XBC_DIM), full2),
                  pl.BlockSpec((1, LANES), full2),
                  pl.BlockSpec((1, LANES), full2),
                  pl.BlockSpec((1, LANES), full2),
                  pl.BlockSpec((1, MIX_DIM), full2)],
        out_specs=[pl.BlockSpec((1, treal, MIX_DIM), lambda i, c: (i, c, 0)),
                   pl.BlockSpec((1, CONV_W - 1, XBC_DIM), lambda i, c: (i, 0, 0)),
                   pl.BlockSpec((1, N_HEADS, HEAD_DIM, SSM_STATE), lambda i, c: (i, 0, 0, 0))],
        out_shape=[jax.ShapeDtypeStruct((b, t, MIX_DIM), F32),
                   jax.ShapeDtypeStruct((b, CONV_W - 1, XBC_DIM), F32),
                   jax.ShapeDtypeStruct((b, N_HEADS, HEAD_DIM, SSM_STATE), F32)],
        scratch_shapes=[pltpu.VMEM((SUBLANES + tc, XBC_DIM), F32),
                        pltpu.VMEM((tc, MIX_DIM), F32),
                        pltpu.VMEM((tc, LANES), F32),
                        pltpu.VMEM((1, N_HEADS, HEAD_DIM, SSM_STATE), F32)],
        compiler_params=_cparams(("parallel", "arbitrary")),
        name="ssd",
    )(h3, h3, h3, hist_conv, h0, conv_w, conv_b, dtb, alog, dskip, norm)


M_INIT = 0.5 * NEG
POS_BITS = 14
assert math.frexp(ATT_SCALE)[0] == 0.5, "q is pre-scaled in bf16, exact only for a power-of-two scale"


def _mono_key(s):
    s = jnp.where(s == 0.0, 0.0, s)
    b = lax.bitcast_convert_type(s, I32)
    return b ^ ((b >> 31) & INT_MAX)


def _select(count, stat_shape, nsel, query_ok):
    kf = float(nsel)
    c0 = count(lambda kt, pos: kt >= 0)
    prefix = jnp.where(c0 >= kf, 0, INT_MIN).astype(I32)

    def bit_body(it, prefix):
        cand = prefix | jnp.left_shift(jnp.int32(1), 30 - it)
        cnt = count(lambda kt, pos: kt >= cand)
        return jnp.where(cnt >= kf, cand, prefix)

    v = lax.fori_loop(0, 31, bit_body, prefix)
    cgt = count(lambda kt, pos: kt > v)
    ceq = count(lambda kt, pos: kt == v)
    need = (cgt + ceq > kf) & (v != INT_MIN) & query_ok
    want = kf - cgt

    def tie_search():
        def body(it, x):
            cand = x | jnp.left_shift(jnp.int32(1), POS_BITS - 1 - it)
            cnt = count(lambda kt, pos: (kt == v) & (pos < cand))
            return jnp.where(cnt < want, cand, x)
        x = lax.fori_loop(0, POS_BITS, body, jnp.zeros(stat_shape, I32))
        return jnp.where(need, x, INT_MAX)

    any_need = jnp.max(jnp.where(need, 1.0, 0.0)) > 0.0
    jlim = lax.cond(any_need, tie_search, lambda: jnp.full(stat_shape, INT_MAX, I32))
    return v, jlim


WIDE = 4 * LANES
PAIRS = N_HEADS // 2
COUNT_ROWS = 8 * SUBLANES


def _attn_prompt_kernel(far_ref, qidx_ref, kw_ref, q_ref, kidx_ref, k_ref, vt_ref, bd_ref, bs_ref, o_ref,
                        keys_ref, qit_ref, qz_ref, m_ref, l_ref, acc_ref, *, tq, nsel):
    i = pl.program_id(0)
    qpos = i * tq + lax.broadcasted_iota(I32, (1, tq), 1)
    nwide = i // (WIDE // LANES) + 1

    qit_ref[...] = jnp.transpose(qidx_ref[...].astype(F32)).astype(BF16)
    w_t = jnp.transpose(kw_ref[...])[IDX_DIM:IDX_DIM + IDX_HEADS, :] * IDX_SCALE
    q_t = jnp.transpose(q_ref[...].astype(F32) * ATT_SCALE).astype(BF16)
    upper = lax.broadcasted_iota(I32, (LANES, tq), 0) < HEAD_DIM
    zero = jnp.zeros((LANES, tq), BF16)
    for p in range(PAIRS):
        blk = q_t[p * LANES:(p + 1) * LANES, :]
        qz_ref[p, :, 0:tq] = jnp.where(upper, blk, zero)
        qz_ref[p, :, tq:2 * tq] = jnp.where(upper, zero, blk)

    def score_body(c, carry):
        off = pl.multiple_of(c * WIDE, WIDE)
        kt = kidx_ref[pl.ds(off, WIDE), :]
        s = jnp.zeros((WIDE, tq), F32)
        for hh in range(IDX_HEADS):
            sc = _dot(kt, qit_ref[hh * IDX_DIM:(hh + 1) * IDX_DIM, :])
            s = s + jnp.maximum(sc, 0.0) * w_t[hh:hh + 1, :]
        kpos = off + lax.broadcasted_iota(I32, (WIDE, tq), 0)
        keys_ref[pl.ds(off, WIDE), :] = jnp.where(kpos <= qpos, _mono_key(s), INT_MIN)
        return carry

    lax.fori_loop(0, nwide, score_body, 0)

    def count(pred):
        def body(c, acc):
            off = pl.multiple_of(c * WIDE, WIDE)
            kt = keys_ref[pl.ds(off, WIDE), :]
            kpos = off + lax.broadcasted_iota(I32, (WIDE, tq), 0)
            hit = jnp.where(pred(kt, kpos), 1.0, 0.0)
            return acc + jnp.sum(hit.reshape(WIDE // COUNT_ROWS, COUNT_ROWS, tq), axis=0)
        acc = lax.fori_loop(0, nwide, body, jnp.zeros((COUNT_ROWS, tq), F32))
        return jnp.sum(acc, axis=0, keepdims=True)

    thr, jlim = _select(count, (1, tq), nsel, True)

    m_ref[...] = jnp.full(m_ref.shape, M_INIT, F32)
    l_ref[...] = jnp.zeros(l_ref.shape, F32)
    acc_ref[...] = jnp.zeros(acc_ref.shape, F32)

    def tile(off, height, bias_ref):
        kt = keys_ref[pl.ds(off, height), :]
        kpos = off + lax.broadcasted_iota(I32, (height, tq), 0)
        sel = ((kt > thr) | ((kt == thr) & (kpos <= jlim))) & (kpos <= qpos)
        amask = jnp.where(sel, 0.0, NEG)
        sts = [_dot(k_ref[pl.ds(off, height), p * LANES:(p + 1) * LANES], qz_ref[p]) for p in range(PAIRS)]
        prs, alphas = [], []
        for h in range(N_HEADS):
            lg = sts[h // 2][:, (h % 2) * tq:(h % 2 + 1) * tq] + amask
            if bias_ref is not None:
                lg = lg + bias_ref[h]
            m_old = m_ref[h]
            m_new = jnp.maximum(m_old, jnp.max(lg, axis=0, keepdims=True))
            alpha = jnp.exp(m_old - m_new)
            pr = jnp.exp(lg - m_new[0:1, :])
            l_ref[h] = alpha * l_ref[h] + jnp.sum(pr, axis=0, keepdims=True)
            m_ref[h] = m_new
            prs.append(pr.astype(BF16))
            alphas.append(alpha[0:1, :])
        for h in range(N_HEADS):
            pv = _dot(vt_ref[h * HEAD_DIM:(h + 1) * HEAD_DIM, pl.ds(off, height)], prs[h])
            acc_ref[h] = alphas[h] * acc_ref[h] + pv

    nfar = jnp.maximum(i - 1, 0)
    nfar_wide = nfar // (WIDE // LANES)

    def far_wide(c, carry):
        tile(pl.multiple_of(c * WIDE, WIDE), WIDE, None)
        return carry

    def far_narrow(j, carry):
        tile(pl.multiple_of(j * LANES, LANES), LANES, None)
        return carry

    lax.fori_loop(0, nfar_wide, far_wide, 0)
    lax.fori_loop(nfar_wide * (WIDE // LANES), nfar, far_narrow, 0)
    for h in range(N_HEADS):
        m_ref[h] = m_ref[h] + far_ref[h]

    @pl.when(i >= 1)
    def _():
        tile(pl.multiple_of((i - 1) * LANES, LANES), LANES, bs_ref)

    tile(pl.multiple_of(i * LANES, LANES), LANES, bd_ref)
    out_t = jnp.concatenate([acc_ref[h] / l_ref[h][0:1, :] for h in range(N_HEADS)], axis=0)
    o_ref[...] = jnp.transpose(out_t)


def _attn_prompt(far, qidx, h2, q, kidx, k, vt, bias_diag_t, bias_sub_t, nsel):
    t = q.shape[0]
    tq = LANES
    whole = lambda i: (0, 0)
    return pl.pallas_call(
        functools.partial(_attn_prompt_kernel, tq=tq, nsel=nsel),
        grid=(t // tq,),
        in_specs=[pl.BlockSpec(memory_space=pltpu.SMEM),
                  pl.BlockSpec((tq, IDX_HEADS * IDX_DIM), lambda i: (i, 0)),
                  pl.BlockSpec((tq, LANES), lambda i: (i, C_KW // LANES)),
                  pl.BlockSpec((tq, MIX_DIM), lambda i: (i, 0)),
                  pl.BlockSpec((t, IDX_DIM), whole),
                  pl.BlockSpec((t, MIX_DIM), whole),
                  pl.BlockSpec((MIX_DIM, t), whole),
                  pl.BlockSpec((N_HEADS, LANES, tq), lambda i: (0, 0, 0)),
                  pl.BlockSpec((N_HEADS, LANES, tq), lambda i: (0, 0, 0))],
        out_specs=pl.BlockSpec((tq, MIX_DIM), lambda i: (i, 0)),
        out_shape=jax.ShapeDtypeStruct((t, MIX_DIM), F32),
        scratch_shapes=[pltpu.VMEM((t, tq), I32),
                        pltpu.VMEM((IDX_HEADS * IDX_DIM, tq), BF16),
                        pltpu.VMEM((PAIRS, LANES, 2 * tq), BF16),
                        pltpu.VMEM((N_HEADS, SUBLANES, tq), F32),
                        pltpu.VMEM((N_HEADS, SUBLANES, tq), F32),
                        pltpu.VMEM((N_HEADS, HEAD_DIM, tq), F32)],
        compiler_params=_cparams(("parallel",)),
        name="attn_prompt",
    )(far, qidx, h2, q, kidx, k, vt, bias_diag_t, bias_sub_t)


QROWS = 8
SEL_PAGES = 8
ATT_PAGES = 4


def _sel_sample_kernel(pt_ref, qst_ref, wcol_ref, *refs, nsel, treal):
    pages = refs[:SEL_PAGES]
    knew_ref, mask_ref, keys_ref = refs[SEL_PAGES:]
    j = pl.program_id(1)
    ntiles = keys_ref.shape[1] // LANES

    def score(kt):
        r = jnp.maximum(_dot_nt(qst_ref[0], kt), 0.0) * (wcol_ref[0] * IDX_SCALE)
        s = r[0:QROWS]
        for hh in range(1, IDX_HEADS):
            s = s + r[hh * QROWS:(hh + 1) * QROWS]
        return _mono_key(s)

    for g in range(SEL_PAGES):
        off = pl.multiple_of((j * SEL_PAGES + g) * LANES, LANES)
        keys_ref[:, pl.ds(off, LANES)] = score(pages[g][0, 0].astype(BF16))

    @pl.when(j == pl.num_programs(1) - 1)
    def _():
        rowq = lax.broadcasted_iota(I32, (QROWS, LANES), 0)
        lane = lax.broadcasted_iota(I32, (QROWS, LANES), 1)
        knew = jnp.where((lane < treal) & (lane <= rowq), score(knew_ref[0]), INT_MIN)
        keys_ref[:, (ntiles - 1) * LANES:] = knew

        def count(pred):
            def body(jt, acc):
                off = pl.multiple_of(jt * LANES, LANES)
                return acc + jnp.where(pred(keys_ref[:, pl.ds(off, LANES)], off + lane), 1.0, 0.0)
            acc = lax.fori_loop(0, ntiles, body, jnp.zeros((QROWS, LANES), F32))
            return jnp.sum(acc, axis=1, keepdims=True)

        row_ok = lax.broadcasted_iota(I32, (QROWS, 1), 0) < treal
        thr, jlim = _select(count, (QROWS, 1), nsel, row_ok)

        def body(jt, carry):
            off = pl.multiple_of(jt * LANES, LANES)
            kt = keys_ref[:, pl.ds(off, LANES)]
            sel = (kt > thr) | ((kt == thr) & (off + lane <= jlim))
            mask_ref[0, :, pl.ds(off, LANES)] = jnp.where(sel, 1.0, 0.0)
            return carry

        lax.fori_loop(0, ntiles, body, 0)


def _sel_sample(page_table, qst, wcol, cache_kidx, knew, layer, nsel, treal):
    b, npages = page_table.shape
    nk = (npages + 1) * PAGE
    page_spec = lambda g: pl.BlockSpec(
        (1, 1, PAGE, IDX_DIM), lambda i, j, pt: (layer, pt[i, j * SEL_PAGES + g], 0, 0))
    grid_spec = pltpu.PrefetchScalarGridSpec(
        num_scalar_prefetch=1,
        grid=(b, npages // SEL_PAGES),
        in_specs=[pl.BlockSpec((1, IDX_HEADS * QROWS, IDX_DIM), lambda i, j, pt: (i, 0, 0)),
                  pl.BlockSpec((1, IDX_HEADS * QROWS, 1), lambda i, j, pt: (i, 0, 0))]
                 + [page_spec(g) for g in range(SEL_PAGES)]
                 + [pl.BlockSpec((1, PAGE, IDX_DIM), lambda i, j, pt: (i, 0, 0))],
        out_specs=pl.BlockSpec((1, QROWS, nk), lambda i, j, pt: (i, 0, 0)),
        scratch_shapes=[pltpu.VMEM((QROWS, nk), I32)],
    )
    return pl.pallas_call(
        functools.partial(_sel_sample_kernel, nsel=nsel, treal=treal),
        grid_spec=grid_spec,
        out_shape=jax.ShapeDtypeStruct((b, QROWS, nk), F32),
        compiler_params=_cparams(("parallel", "arbitrary")),
        name="sel_sample",
    )(page_table, qst, wcol, *([cache_kidx] * SEL_PAGES), knew)


def _attn_sample_kernel(pt_ref, far_ref, qh_ref, *refs):
    kp = refs[:ATT_PAGES]
    vp = refs[ATT_PAGES:2 * ATT_PAGES]
    mask_ref, masknew_ref, blast_ref, bnew_ref, knew_ref, vnew_ref, o_ref, m_ref, l_ref, acc_ref = refs[2 * ATT_PAGES:]
    j = pl.program_id(1)
    is_last = j == pl.num_programs(1) - 1

    @pl.when(j == 0)
    def _():
        m_ref[...] = jnp.full(m_ref.shape, M_INIT, F32)
        l_ref[...] = jnp.zeros(l_ref.shape, F32)
        acc_ref[...] = jnp.zeros(acc_ref.shape, F32)

    def update(lgs, vhs):
        prs, alphas = [], []
        for h in range(N_HEADS):
            m_old = m_ref[h]
            m_new = jnp.maximum(m_old, jnp.max(lgs[h], axis=1, keepdims=True))
            alpha = jnp.exp(m_old - m_new)
            pr = jnp.exp(lgs[h] - m_new[:, 0:1])
            l_ref[h] = alpha * l_ref[h] + jnp.sum(pr, axis=1, keepdims=True)
            m_ref[h] = m_new
            prs.append(pr.astype(BF16))
            alphas.append(alpha[:, :HEAD_DIM])
        for h in range(N_HEADS):
            acc_ref[h] = alphas[h] * acc_ref[h] + _dot(prs[h], vhs[h])

    def q_of(h):
        return (qh_ref[0, h].astype(F32) * ATT_SCALE).astype(BF16)

    def page_rows(refs_, h):
        rows = pl.ds(h, PAGE, stride=N_HEADS)
        return jnp.concatenate([refs_[g][0, 0, rows, :] for g in range(ATT_PAGES)], axis=0).astype(BF16)

    amask = jnp.where(mask_ref[0] > 0.5, 0.0, NEG)
    lgs = []
    for h in range(N_HEADS):
        far = jnp.full((QROWS, LANES), far_ref[h], F32)
        bias = jnp.concatenate([far] * (ATT_PAGES - 1) + [jnp.where(is_last, blast_ref[h], far)], axis=1)
        lgs.append(_dot_nt(q_of(h), page_rows(kp, h)) + amask + bias)
    update(lgs, [page_rows(vp, h) for h in range(N_HEADS)])

    @pl.when(is_last)
    def _():
        amask_new = jnp.where(masknew_ref[0] > 0.5, 0.0, NEG)
        lgs_new = [_dot_nt(q_of(h), knew_ref[0, h]) + amask_new + bnew_ref[h] for h in range(N_HEADS)]
        update(lgs_new, [vnew_ref[0, h] for h in range(N_HEADS)])
        outs = []
        for h in range(N_HEADS):
            lh = l_ref[h][:, :HEAD_DIM]
            outs.append(acc_ref[h] / jnp.where(lh > 0.0, lh, 1.0))
        o_ref[0] = jnp.concatenate(outs, axis=1)


def _attn_sample(page_table, far, qh, cache_k2, cache_v2, mask, bias_last, bias_new, knew, vnew, layer):
    b, npages = page_table.shape
    nsteps = npages // ATT_PAGES
    page_spec = lambda g: pl.BlockSpec(
        (1, 1, PAGE * N_HEADS, HEAD_DIM), lambda i, j, pt, far: (layer, pt[i, j * ATT_PAGES + g], 0, 0))
    const3 = lambda i, j, pt, far: (0, 0, 0)
    per_b4 = lambda i, j, pt, far: (i, 0, 0, 0)
    grid_spec = pltpu.PrefetchScalarGridSpec(
        num_scalar_prefetch=2,
        grid=(b, nsteps),
        in_specs=[pl.BlockSpec((1, N_HEADS, QROWS, HEAD_DIM), per_b4)]
                 + [page_spec(g) for g in range(ATT_PAGES)] * 2
                 + [pl.BlockSpec((1, QROWS, ATT_PAGES * PAGE), lambda i, j, pt, far: (i, 0, j)),
                    pl.BlockSpec((1, QROWS, PAGE), lambda i, j, pt, far: (i, 0, npages)),
                    pl.BlockSpec((N_HEADS, QROWS, PAGE), const3),
                    pl.BlockSpec((N_HEADS, QROWS, PAGE), const3),
                    pl.BlockSpec((1, N_HEADS, PAGE, HEAD_DIM), per_b4),
                    pl.BlockSpec((1, N_HEADS, PAGE, HEAD_DIM), per_b4)],
        out_specs=pl.BlockSpec((1, QROWS, MIX_DIM), lambda i, j, pt, far: (i, 0, 0)),
        scratch_shapes=[pltpu.VMEM((N_HEADS, QROWS, LANES), F32),
                        pltpu.VMEM((N_HEADS, QROWS, LANES), F32),
                        pltpu.VMEM((N_HEADS, QROWS, HEAD_DIM), F32)],
    )
    return pl.pallas_call(
        _attn_sample_kernel,
        grid_spec=grid_spec,
        out_shape=jax.ShapeDtypeStruct((b, QROWS, MIX_DIM), F32),
        compiler_params=_cparams(("parallel", "arbitrary")),
        name="attn_sample",
    )(page_table, far, qh, *([cache_k2] * ATT_PAGES), *([cache_v2] * ATT_PAGES), mask, mask,
      bias_last, bias_new, knew, vnew)


def _bucket_table(n):
    d = np.arange(n)
    nf = np.maximum(d, 1).astype(np.float32)
    half = REL_BUCKETS // 2
    large = half + (np.log(nf / np.float32(half)) / np.float32(math.log(REL_MAX_DIST / half))
                    * np.float32(REL_BUCKETS - half)).astype(np.int32)
    return np.where(d < half, d, np.minimum(large, REL_BUCKETS - 1)).astype(np.int32)


def _pack_w_in(w):
    p, z, xbc, dt, q, k, v, qi, ki, wi, u, vs = jnp.split(w, np.cumsum(IN_SPLITS)[:-1].tolist(), axis=1)
    zeros = lambda n: jnp.zeros((w.shape[0], n), w.dtype)
    packed = jnp.concatenate(
        [p, z, xbc, q, k, v, u, vs, qi, ki, wi, zeros(LANES - IDX_DIM - IDX_HEADS), dt, zeros(LANES - N_HEADS)],
        axis=1)
    return packed.astype(BF16)


def _pad_lanes(x):
    return jnp.pad(x.reshape(1, -1), ((0, 0), (0, LANES - x.shape[-1])))


def _pad_axis(x, axis, size):
    pads = [(0, 0)] * x.ndim
    pads[axis] = (0, size - x.shape[axis])
    return jnp.pad(x, pads)


def kernel(x_prompt, x_sample, state_pool, state_conv, state_ssm, cache_k, cache_v, cache_kidx, page_table,
           rel_bias, norm_mix, w_in, pool_w, pool_scale, conv_w, conv_b, dt_bias, a_log, d_skip, ssm_norm,
           sgu_w, sgu_b, w_out, norm_mlp, mlp_w1, mlp_w2, norm_final):
    depth = w_in.shape[0]
    bp, t, _ = x_prompt.shape
    bs, ts, _ = x_sample.shape
    npages = page_table.shape[1]
    past = npages * PAGE
    n_pool = cache_k.shape[1]
    assert bp == 1 and t % WIDE == 0 and npages % SEL_PAGES == 0 and npages % ATT_PAGES == 0
    assert max(t, past + PAGE) <= 2 ** POS_BITS and ts <= QROWS
    tc = LANES
    nsel_p = min(TOPK_MAX, t // 4)
    nsel_s = min(TOPK_MAX, (past + ts) // 4)

    bucket = _bucket_table(2 * LANES)
    far = rel_bias[REL_BUCKETS - 1]
    kk = np.arange(LANES)[:, None]
    qq = np.arange(LANES)[None, :]
    bias_diag_t = jnp.transpose(rel_bias[bucket[np.maximum(qq - kk, 0)]], (2, 0, 1))
    bias_sub_t = jnp.transpose(rel_bias[bucket[np.minimum(LANES + qq - kk, 2 * LANES - 1)]], (2, 0, 1))
    sq = np.arange(QROWS)[:, None]
    sk = np.arange(PAGE)[None, :]
    bias_last = jnp.transpose(rel_bias[bucket[np.minimum(PAGE + sq - sk, 2 * LANES - 1)]], (2, 0, 1))
    bias_new = jnp.transpose(rel_bias[bucket[np.maximum(sq - sk, 0)]], (2, 0, 1))

    cache_k2 = cache_k.reshape(depth, n_pool, PAGE * N_HEADS, HEAD_DIM)
    cache_v2 = cache_v.reshape(depth, n_pool, PAGE * N_HEADS, HEAD_DIM)

    xp = x_prompt.reshape(t, D_MODEL)
    xs = x_sample.reshape(bs * ts, D_MODEL)
    zero_pool = jnp.zeros((1, POOL_HIST, MIX_DIM), F32)
    zero_conv = jnp.zeros((1, CONV_W - 1, XBC_DIM), F32)
    zero_ssm = jnp.zeros((1, N_HEADS, HEAD_DIM, SSM_STATE), F32)
    tm_p = min(512, t)
    tm_s = bs * ts
    outs_p, outs_s = [], []

    for l in range(depth):
        wl_in = _pack_w_in(w_in[l])
        wl_out = w_out[l].astype(BF16)
        w1 = mlp_w1[l].astype(BF16)
        w2 = mlp_w2[l].astype(BF16)
        g_mix = norm_mix[l].reshape(1, D_MODEL)
        g_mlp = norm_mlp[l].reshape(1, D_MODEL)
        g_out = norm_final.reshape(1, D_MODEL) if l == depth - 1 else g_mlp
        pscale = pool_scale[l].reshape(1, MIX_DIM)
        cb = conv_b[l].reshape(1, XBC_DIM)
        dtb, alog, dsk = _pad_lanes(dt_bias[l]), _pad_lanes(a_log[l]), _pad_lanes(d_skip[l])
        snorm = ssm_norm[l].reshape(1, MIX_DIM)
        sgu_b_col = sgu_b[l][:, :, None]
        final = l == depth - 1

        hp = _inproj(xp, g_mix, wl_in, tm=min(1024, t))
        hp3 = hp.reshape(1, t, H_PACKED)
        y_pool, new_pool = _pool(hp3, zero_pool, pool_w[l], pscale, tc=tc, treal=tc, pos0=0)
        y_ssm, new_conv, new_h = _ssd(hp3, zero_conv, zero_ssm, conv_w[l], cb, dtb, alog, dsk, snorm,
                                      tc=tc, treal=tc)
        y_sgu, _ = _sgu(hp3, sgu_w[l], sgu_b_col, tc=tc, treal=tc)
        kf = hp[:, C_K:C_K + MIX_DIM]
        vf = hp[:, C_V:C_V + MIX_DIM]
        kif = hp[:, C_KW:C_KW + IDX_DIM]
        y_att = _attn_prompt(far, hp[:, C_QIDX:C_QIDX + IDX_HEADS * IDX_DIM].astype(BF16), hp,
                             hp[:, C_Q:C_Q + MIX_DIM].astype(BF16), kif.astype(BF16), kf.astype(BF16),
                             jnp.transpose(vf.astype(BF16)), bias_diag_t, bias_sub_t, nsel_p)
        x1, xn = _outproj(xp, [y_pool[0], y_ssm[0], y_att, y_sgu[0]], wl_out, g_mlp, tm=256)
        xp = _mlp(xn, w1, w2, x1, g_out, tm=tm_p, final_norm=final)
        outs_p.append((new_pool, new_conv, new_h, kf.reshape(1, t, N_HEADS, HEAD_DIM),
                       vf.reshape(1, t, N_HEADS, HEAD_DIM), kif.reshape(1, t, IDX_DIM)))

        hs = _inproj(xs, g_mix, wl_in, tm=tm_s)
        hs3 = hs.reshape(bs, ts, H_PACKED)
        y_pool, new_pool = _pool(hs3, state_pool[l], pool_w[l], pscale, tc=tc, treal=ts, pos0=past)
        y_ssm, new_conv, new_h = _ssd(hs3, state_conv[l], state_ssm[l], conv_w[l], cb, dtb, alog, dsk, snorm,
                                      tc=tc, treal=ts)
        y_sgu, v_rows = _sgu(hs3, sgu_w[l], sgu_b_col, tc=tc, treal=ts)
        kf = hs3[:, :, C_K:C_K + MIX_DIM]
        vf = hs3[:, :, C_V:C_V + MIX_DIM]
        kif = hs3[:, :, C_KW:C_KW + IDX_DIM]
        qi = hs3[:, :, C_QIDX:C_QIDX + IDX_HEADS * IDX_DIM].reshape(bs, ts, IDX_HEADS, IDX_DIM)
        qst = _pad_axis(jnp.transpose(qi, (0, 2, 1, 3)), 2, QROWS)
        qst = qst.reshape(bs, IDX_HEADS * QROWS, IDX_DIM).astype(BF16)
        wi = hs3[:, :, C_KW + IDX_DIM:C_KW + IDX_DIM + IDX_HEADS]
        wcol = _pad_axis(jnp.transpose(wi, (0, 2, 1)), 2, QROWS).reshape(bs, IDX_HEADS * QROWS, 1)
        mask = _sel_sample(page_table, qst, wcol, cache_kidx, _pad_axis(kif, 1, PAGE).astype(BF16),
                           l, nsel_s, ts)
        heads = lambda x: jnp.transpose(x.reshape(bs, ts, N_HEADS, HEAD_DIM), (0, 2, 1, 3))
        qh = _pad_axis(heads(hs3[:, :, C_Q:C_Q + MIX_DIM]), 2, QROWS).astype(BF16)
        knew = _pad_axis(heads(kf), 2, PAGE).astype(BF16)
        vnew = _pad_axis(heads(vf), 2, PAGE).astype(BF16)
        att = _attn_sample(page_table, far, qh, cache_k2, cache_v2, mask, bias_last, bias_new, knew, vnew, l)
        y_att = att[:, :ts].reshape(bs * ts, MIX_DIM)
        flat = lambda y: y.reshape(bs * ts, MIX_DIM)
        x1, xn = _outproj(xs, [flat(y_pool), flat(y_ssm), y_att, flat(y_sgu)], wl_out, g_mlp, tm=tm_s)
        xs = _mlp(xn, w1, w2, x1, g_out, tm=tm_s, final_norm=final)
        outs_s.append((new_pool, new_conv, new_h, kf.reshape(bs, ts, N_HEADS, HEAD_DIM),
                       vf.reshape(bs, ts, N_HEADS, HEAD_DIM), kif, v_rows))

    stack = lambda outs, i: jnp.stack([o[i] for o in outs])
    return (xp.reshape(1, t, D_MODEL), xs.reshape(bs, ts, D_MODEL),
            *[stack(outs_p, i) for i in range(6)],
            *[stack(outs_s, i) for i in range(7)])
```

```python
import functools
import math

import numpy as np
import jax
import jax.numpy as jnp
from jax import lax
from jax.experimental import pallas as pl
from jax.experimental.pallas import tpu as pltpu

F32 = jnp.float32
BF16 = jnp.bfloat16
I32 = jnp.int32

LANES = 128
SUBLANES = 8
D_MODEL = 2048
N_HEADS = 8
HEAD_DIM = 64
MIX_DIM = 512
POOL_WINDOWS = (2, 4, 8, 16)
POOL_HIST = 15
CONV_W = 4
SSM_GROUPS = 2
SSM_STATE = 128
XBC_DIM = MIX_DIM + 2 * SSM_GROUPS * SSM_STATE
IDX_HEADS = 4
IDX_DIM = 64
TOPK_MAX = 256
PAGE = 128
REL_BUCKETS = 32
REL_MAX_DIST = 128
ATT_SCALE = HEAD_DIM ** -0.5
IDX_SCALE = (IDX_HEADS * IDX_DIM) ** -0.5
SGU_GROUPS = 4
D_FF = 4 * D_MODEL
EPS = 1e-6
NEG = -1e30
INT_MIN = -(2 ** 31)
INT_MAX = 2 ** 31 - 1
IN_SPLITS = (512, 512, 1024, 8, 512, 512, 512, 256, 64, 4, 512, 512)

C_P, C_Z, C_XBC, C_Q, C_K, C_V, C_U, C_VS, C_QIDX, C_KW, C_DT = (
    0, 512, 1024, 2048, 2560, 3072, 3584, 4096, 4608, 4864, 4992)
H_PACKED = 5120
VMEM_LIMIT = 56 * 1024 * 1024


def _cparams(sem):
    return pltpu.CompilerParams(dimension_semantics=sem, vmem_limit_bytes=VMEM_LIMIT)


def _dot(a, b):
    return jnp.dot(a, b, preferred_element_type=F32)


def _dot_nt(a, b):
    return lax.dot_general(a, b, (((1,), (1,)), ((), ())), preferred_element_type=F32)


def _dot_tn(a, b):
    return lax.dot_general(a, b, (((0,), (0,)), ((), ())), preferred_element_type=F32)


def _silu(x):
    return x * (1.0 / (1.0 + jnp.exp(-x)))


def _gelu_tanh(x):
    return 0.5 * x * (1.0 + jnp.tanh(math.sqrt(2.0 / math.pi) * (x + 0.044715 * (x * x * x))))


def _softplus(x):
    return jnp.maximum(x, 0.0) + jnp.log1p(jnp.exp(-jnp.abs(x)))


def _rms(x, g):
    ms = jnp.mean(x * x, axis=-1, keepdims=True)
    return x * lax.rsqrt(ms + EPS) * g


def _inproj_kernel(x_ref, g_ref, w_ref, o_ref, xn_ref):
    @pl.when(pl.program_id(1) == 0)
    def _():
        xn_ref[...] = _rms(x_ref[...], g_ref[...]).astype(BF16)

    o_ref[...] = _dot_nt(xn_ref[...], w_ref[0])


def _inproj(x, g, w_t, layer, tm, tn=512):
    m = x.shape[0]
    n = w_t.shape[1]
    return pl.pallas_call(
        _inproj_kernel,
        grid=(m // tm, n // tn),
        in_specs=[pl.BlockSpec((tm, D_MODEL), lambda i, j: (i, 0)),
                  pl.BlockSpec((1, D_MODEL), lambda i, j: (0, 0)),
                  pl.BlockSpec((1, tn, D_MODEL), lambda i, j: (layer, j, 0))],
        out_specs=pl.BlockSpec((tm, tn), lambda i, j: (i, j)),
        out_shape=jax.ShapeDtypeStruct((m, n), F32),
        scratch_shapes=[pltpu.VMEM((tm, D_MODEL), BF16)],
        compiler_params=_cparams(("parallel", "arbitrary")),
        name="inproj",
    )(x, g, w_t)


def _outproj_kernel(x_ref, y0_ref, y1_ref, y2_ref, y3_ref, w_ref, g_ref, x1_ref, xn_ref):
    y = jnp.concatenate([y0_ref[...], y1_ref[...], y2_ref[...], y3_ref[...]], axis=1).astype(BF16)
    x1 = x_ref[...] + _dot(y, w_ref[...])
    x1_ref[...] = x1
    xn_ref[...] = _rms(x1, g_ref[...]).astype(BF16)


def _outproj(x, ys, w, g, tm):
    m = x.shape[0]
    row = lambda i: (i, 0)
    return pl.pallas_call(
        _outproj_kernel,
        grid=(m // tm,),
        in_specs=[pl.BlockSpec((tm, D_MODEL), row)]
                 + [pl.BlockSpec((tm, MIX_DIM), row)] * 4
                 + [pl.BlockSpec((D_MODEL, D_MODEL), lambda i: (0, 0)),
                    pl.BlockSpec((1, D_MODEL), lambda i: (0, 0))],
        out_specs=[pl.BlockSpec((tm, D_MODEL), row), pl.BlockSpec((tm, D_MODEL), row)],
        out_shape=[jax.ShapeDtypeStruct((m, D_MODEL), F32), jax.ShapeDtypeStruct((m, D_MODEL), BF16)],
        compiler_params=_cparams(("parallel",)),
        name="outproj",
    )(x, *ys, w, g)


def _mlp_kernel(xn_ref, w1_ref, w2_ref, x1_ref, g_ref, o_ref, acc_ref, *, final_norm):
    j = pl.program_id(1)

    @pl.when(j == 0)
    def _():
        acc_ref[...] = jnp.zeros_like(acc_ref)

    hid = _dot(xn_ref[...], w1_ref[...])
    hid = jnp.square(jnp.maximum(hid, 0.0)).astype(BF16)
    acc_ref[...] += _dot(hid, w2_ref[...])

    @pl.when(j == pl.num_programs(1) - 1)
    def _():
        x2 = x1_ref[...] + acc_ref[...]
        o_ref[...] = _rms(x2, g_ref[...]) if final_norm else x2


def _mlp(xn, w1, w2, x1, g, tm, final_norm, tf=512):
    m = xn.shape[0]
    return pl.pallas_call(
        functools.partial(_mlp_kernel, final_norm=final_norm),
        grid=(m // tm, D_FF // tf),
        in_specs=[pl.BlockSpec((tm, D_MODEL), lambda i, j: (i, 0)),
                  pl.BlockSpec((D_MODEL, tf), lambda i, j: (0, j)),
                  pl.BlockSpec((tf, D_MODEL), lambda i, j: (j, 0)),
                  pl.BlockSpec((tm, D_MODEL), lambda i, j: (i, 0)),
                  pl.BlockSpec((1, D_MODEL), lambda i, j: (0, 0))],
        out_specs=pl.BlockSpec((tm, D_MODEL), lambda i, j: (i, 0)),
        out_shape=jax.ShapeDtypeStruct((m, D_MODEL), F32),
        scratch_shapes=[pltpu.VMEM((tm, D_MODEL), F32)],
        compiler_params=_cparams(("parallel", "arbitrary")),
        name="mlp",
    )(xn, w1, w2, x1, g)


def _pool_kernel(p_ref, hist_ref, w_ref, scale_ref, y_ref, newhist_ref, ext_ref, *, tc, treal, pos0):
    c = pl.program_id(1)
    hrows = POOL_HIST + 1

    @pl.when(c == 0)
    def _():
        ext_ref[0:1, :] = jnp.zeros((1, MIX_DIM), F32)
        ext_ref[1:hrows, :] = hist_ref[0]

    if treal < tc:
        ext_ref[hrows:hrows + tc, :] = jnp.zeros((tc, MIX_DIM), F32)
    ext_ref[hrows:hrows + treal, :] = p_ref[0]

    pos = pos0 + c * treal + lax.broadcasted_iota(I32, (tc, 1), 0)
    for g, win in enumerate(POOL_WINDOWS):
        cols = slice(g * LANES, (g + 1) * LANES)
        cur = ext_ref[hrows:hrows + tc, cols]
        s = cur
        for j in range(1, win):
            s = s + ext_ref[hrows - j:hrows - j + tc, cols]
        cnt = jnp.minimum(pos + 1, win).astype(F32)
        diff = s / cnt - cur
        y = _dot(diff, w_ref[g]) * scale_ref[:, cols]
        y_ref[0, :, cols] = y[:treal]

    newhist_ref[0] = ext_ref[treal + 1:treal + hrows, :]
    ext_ref[0:hrows, :] = ext_ref[treal:treal + hrows, :]


def _pool(h3, hist, pool_w, pool_scale, *, tc, treal, pos0):
    b, t, _ = h3.shape
    return pl.pallas_call(
        functools.partial(_pool_kernel, tc=tc, treal=treal, pos0=pos0),
        grid=(b, t // treal),
        in_specs=[pl.BlockSpec((1, treal, MIX_DIM), lambda i, c: (i, c, C_P // MIX_DIM)),
                  pl.BlockSpec((1, POOL_HIST, MIX_DIM), lambda i, c: (i, 0, 0)),
                  pl.BlockSpec((4, LANES, LANES), lambda i, c: (0, 0, 0)),
                  pl.BlockSpec((1, MIX_DIM), lambda i, c: (0, 0))],
        out_specs=[pl.BlockSpec((1, treal, MIX_DIM), lambda i, c: (i, c, 0)),
                   pl.BlockSpec((1, POOL_HIST, MIX_DIM), lambda i, c: (i, 0, 0))],
        out_shape=[jax.ShapeDtypeStruct((b, t, MIX_DIM), F32),
                   jax.ShapeDtypeStruct((b, POOL_HIST, MIX_DIM), F32)],
        scratch_shapes=[pltpu.VMEM((POOL_HIST + 1 + tc, MIX_DIM), F32)],
        compiler_params=_cparams(("parallel", "arbitrary")),
        name="pool",
    )(h3, hist, pool_w, pool_scale)


def _sgu_kernel(u_ref, v_ref, w_ref, b_ref, y_ref, vout_ref, ubuf_ref, vbuf_ref, *, tc, treal):
    if treal < tc:
        ubuf_ref[...] = jnp.zeros((tc, MIX_DIM), F32)
        vbuf_ref[...] = jnp.zeros((tc, MIX_DIM), F32)
    ubuf_ref[0:treal, :] = u_ref[0]
    vbuf_ref[0:treal, :] = v_ref[0]
    u = _gelu_tanh(ubuf_ref[...])
    v = _gelu_tanh(vbuf_ref[...])
    vout_ref[0] = v[:treal]
    r = lax.broadcasted_iota(I32, (tc, tc), 0)
    s = lax.broadcasted_iota(I32, (tc, tc), 1)
    for g in range(SGU_GROUPS):
        cols = slice(g * LANES, (g + 1) * LANES)
        w = jnp.where(s <= r, w_ref[g], 0.0)
        mixed = _dot(w, v[:, cols]) + b_ref[g]
        y_ref[0, :, cols] = (u[:, cols] * mixed)[:treal]


def _sgu(h3, sgu_w, sgu_b_col, *, tc, treal):
    b, t, _ = h3.shape
    return pl.pallas_call(
        functools.partial(_sgu_kernel, tc=tc, treal=treal),
        grid=(b, t // treal),
        in_specs=[pl.BlockSpec((1, treal, MIX_DIM), lambda i, c: (i, c, C_U // MIX_DIM)),
                  pl.BlockSpec((1, treal, MIX_DIM), lambda i, c: (i, c, C_VS // MIX_DIM)),
                  pl.BlockSpec((SGU_GROUPS, tc, tc), lambda i, c: (0, 0, 0)),
                  pl.BlockSpec((SGU_GROUPS, tc, 1), lambda i, c: (0, 0, 0))],
        out_specs=[pl.BlockSpec((1, treal, MIX_DIM), lambda i, c: (i, c, 0)),
                   pl.BlockSpec((1, treal, MIX_DIM), lambda i, c: (i, c, 0))],
        out_shape=[jax.ShapeDtypeStruct((b, t, MIX_DIM), F32),
                   jax.ShapeDtypeStruct((b, t, MIX_DIM), F32)],
        scratch_shapes=[pltpu.VMEM((tc, MIX_DIM), F32), pltpu.VMEM((tc, MIX_DIM), F32)],
        compiler_params=_cparams(("parallel", "parallel")),
        name="sgu",
    )(h3, h3, sgu_w, sgu_b_col)


def _cumsum_rows(a, n):
    row = lax.broadcasted_iota(I32, a.shape, 0)
    sh = 1
    while sh < n:
        a = a + jnp.where(row >= sh, pltpu.roll(a, sh, 0), 0.0)
        sh *= 2
    return a


def _ssd_kernel(z_ref, xbc_ref, dt_ref, hc_ref, h0_ref, convw_ref, convb_ref, dtb_ref, alog_ref, dskip_ref,
                norm_ref, y_ref, newconv_ref, newh_ref, ext_ref, zbuf_ref, dtbuf_ref, state_ref, *, tc, treal):
    c = pl.program_id(1)
    pre = SUBLANES
    hist = CONV_W - 1

    @pl.when(c == 0)
    def _():
        ext_ref[0:pre - hist, :] = jnp.zeros((pre - hist, XBC_DIM), F32)
        ext_ref[pre - hist:pre, :] = hc_ref[0]
        state_ref[...] = h0_ref[...]

    if treal < tc:
        ext_ref[pre:pre + tc, :] = jnp.zeros((tc, XBC_DIM), F32)
        zbuf_ref[...] = jnp.zeros((tc, MIX_DIM), F32)
        dtbuf_ref[...] = jnp.zeros((tc, LANES), F32)
    ext_ref[pre:pre + treal, :] = xbc_ref[0]
    zbuf_ref[0:treal, :] = z_ref[0]
    dtbuf_ref[0:treal, :] = dt_ref[0]

    conv = convb_ref[...]
    for j in range(CONV_W):
        conv = conv + ext_ref[pre - hist + j:pre - hist + j + tc, :] * convw_ref[j:j + 1, :]
    conv = _silu(conv)
    xs = conv[:, :MIX_DIM]
    bmat = conv[:, MIX_DIM:MIX_DIM + SSM_GROUPS * SSM_STATE]
    cmat = conv[:, MIX_DIM + SSM_GROUPS * SSM_STATE:]

    rowi = lax.broadcasted_iota(I32, (tc, LANES), 0)
    dt = _softplus(dtbuf_ref[...] + dtb_ref[...])
    if treal < tc:
        dt = jnp.where(rowi < treal, dt, 0.0)
    a = dt * (-jnp.exp(alog_ref[...]))
    acum = _cumsum_rows(a, tc)
    acum_t = jnp.transpose(acum)
    total = acum[tc - 1:tc, :]
    causal = lax.broadcasted_iota(I32, (tc, tc), 1) <= lax.broadcasted_iota(I32, (tc, tc), 0)

    ys = []
    heads_per_group = N_HEADS // SSM_GROUPS
    for g in range(SSM_GROUPS):
        bg = bmat[:, g * SSM_STATE:(g + 1) * SSM_STATE]
        cg = cmat[:, g * SSM_STATE:(g + 1) * SSM_STATE]
        cb = _dot_nt(cg, bg)
        for hh in range(heads_per_group):
            h = g * heads_per_group + hh
            col = acum[:, h:h + 1]
            row = acum_t[h:h + 1, :]
            decay = jnp.exp(jnp.where(causal, col - row, NEG))
            xh = xs[:, h * HEAD_DIM:(h + 1) * HEAD_DIM]
            xdt = xh * dt[:, h:h + 1]
            hprev = state_ref[0, h]
            y = _dot(cb * decay, xdt)
            y = y + _dot_nt(cg, hprev) * jnp.exp(col)
            y = y + dskip_ref[:, h:h + 1] * xh
            ys.append(y)
            tot = total[:, h:h + 1]
            st = _dot_tn(xdt * jnp.exp(tot - col), bg)
            state_ref[0, h] = jnp.exp(tot) * hprev + st
    y = jnp.concatenate(ys, axis=1) * _silu(zbuf_ref[...])
    gw = MIX_DIM // SSM_GROUPS
    outs = []
    for g in range(SSM_GROUPS):
        yg = y[:, g * gw:(g + 1) * gw]
        ms = jnp.mean(yg * yg, axis=-1, keepdims=True)
        outs.append(yg * lax.rsqrt(ms + EPS) * norm_ref[:, g * gw:(g + 1) * gw])
    y_ref[0] = jnp.concatenate(outs, axis=1)[:treal]

    newconv_ref[0] = ext_ref[pre + treal - hist:pre + treal, :]
    ext_ref[0:pre, :] = ext_ref[treal:treal + pre, :]
    newh_ref[...] = state_ref[...]


def _ssd(h3, hist_conv, h0, conv_w, conv_b, dtb, alog, dskip, norm, *, tc, treal):
    b, t, _ = h3.shape
    full2 = lambda i, c: (0, 0)
    return pl.pallas_call(
        functools.partial(_ssd_kernel, tc=tc, treal=treal),
        grid=(b, t // treal),
        in_specs=[pl.BlockSpec((1, treal, MIX_DIM), lambda i, c: (i, c, C_Z // MIX_DIM)),
                  pl.BlockSpec((1, treal, XBC_DIM), lambda i, c: (i, c, C_XBC // XBC_DIM)),
                  pl.BlockSpec((1, treal, LANES), lambda i, c: (i, c, C_DT // LANES)),
                  pl.BlockSpec((1, CONV_W - 1, XBC_DIM), lambda i, c: (i, 0, 0)),
                  pl.BlockSpec((1, N_HEADS, HEAD_DIM, SSM_STATE), lambda i, c: (i, 0, 0, 0)),
                  pl.BlockSpec((CONV_W, XBC_DIM), full2),
                  pl.BlockSpec((1, XBC_DIM), full2),
                  pl.BlockSpec((1, LANES), full2),
                  pl.BlockSpec((1, LANES), full2),
                  pl.BlockSpec((1, LANES), full2),
                  pl.BlockSpec((1, MIX_DIM), full2)],
        out_specs=[pl.BlockSpec((1, treal, MIX_DIM), lambda i, c: (i, c, 0)),
                   pl.BlockSpec((1, CONV_W - 1, XBC_DIM), lambda i, c: (i, 0, 0)),
                   pl.BlockSpec((1, N_HEADS, HEAD_DIM, SSM_STATE), lambda i, c: (i, 0, 0, 0))],
        out_shape=[jax.ShapeDtypeStruct((b, t, MIX_DIM), F32),
                   jax.ShapeDtypeStruct((b, CONV_W - 1, XBC_DIM), F32),
                   jax.ShapeDtypeStruct((b, N_HEADS, HEAD_DIM, SSM_STATE), F32)],
        scratch_shapes=[pltpu.VMEM((SUBLANES + tc, XBC_DIM), F32),
                        pltpu.VMEM((tc, MIX_DIM), F32),
                        pltpu.VMEM((tc, LANES), F32),
                        pltpu.VMEM((1, N_HEADS, HEAD_DIM, SSM_STATE), F32)],
        compiler_params=_cparams(("parallel", "arbitrary")),
        name="ssd",
    )(h3, h3, h3, hist_conv, h0, conv_w, conv_b, dtb, alog, dskip, norm)


M_INIT = 0.5 * NEG
POS_BITS = 14
assert math.frexp(ATT_SCALE)[0] == 0.5, "q is pre-scaled in bf16, exact only for a power-of-two scale"


def _mono_key(s):
    s = jnp.where(s == 0.0, 0.0, s)
    b = lax.bitcast_convert_type(s, I32)
    return b ^ ((b >> 31) & INT_MAX)


def _select(count, stat_shape, nsel, query_ok):
    kf = float(nsel)
    c0 = count(lambda kt, pos: kt >= 0)
    prefix = jnp.where(c0 >= kf, 0, INT_MIN).astype(I32)

    def bit_body(it, prefix):
        cand = prefix | jnp.left_shift(jnp.int32(1), 30 - it)
        cnt = count(lambda kt, pos: kt >= cand)
        return jnp.where(cnt >= kf, cand, prefix)

    v = lax.fori_loop(0, 31, bit_body, prefix)
    cgt = count(lambda kt, pos: kt > v)
    ceq = count(lambda kt, pos: kt == v)
    need = (cgt + ceq > kf) & (v != INT_MIN) & query_ok
    want = kf - cgt

    def tie_search():
        def body(it, x):
            cand = x | jnp.left_shift(jnp.int32(1), POS_BITS - 1 - it)
            cnt = count(lambda kt, pos: (kt == v) & (pos < cand))
            return jnp.where(cnt < want, cand, x)
        x = lax.fori_loop(0, POS_BITS, body, jnp.zeros(stat_shape, I32))
        return jnp.where(need, x, INT_MAX)

    any_need = jnp.max(jnp.where(need, 1.0, 0.0)) > 0.0
    jlim = lax.cond(any_need, tie_search, lambda: jnp.full(stat_shape, INT_MAX, I32))
    return v, jlim


WIDE = 4 * LANES
PAIRS = N_HEADS // 2
COUNT_ROWS = 8 * SUBLANES


def _attn_prompt_kernel(far_ref, qidx_ref, kw_ref, q_ref, kidx_ref, k_ref, vt_ref, bd_ref, bs_ref, o_ref,
                        keys_ref, qit_ref, qz_ref, m_ref, l_ref, acc_ref, *, tq, nsel):
    i = pl.program_id(0)
    qpos = i * tq + lax.broadcasted_iota(I32, (1, tq), 1)
    nwide = i // (WIDE // LANES) + 1

    qit_ref[...] = jnp.transpose(qidx_ref[...].astype(F32)).astype(BF16)
    w_t = jnp.transpose(kw_ref[...])[IDX_DIM:IDX_DIM + IDX_HEADS, :] * IDX_SCALE
    q_t = jnp.transpose(q_ref[...].astype(F32) * ATT_SCALE).astype(BF16)
    upper = lax.broadcasted_iota(I32, (LANES, tq), 0) < HEAD_DIM
    zero = jnp.zeros((LANES, tq), BF16)
    for p in range(PAIRS):
        blk = q_t[p * LANES:(p + 1) * LANES, :]
        qz_ref[p, :, 0:tq] = jnp.where(upper, blk, zero)
        qz_ref[p, :, tq:2 * tq] = jnp.where(upper, zero, blk)

    def score_body(c, carry):
        off = pl.multiple_of(c * WIDE, WIDE)
        kt = kidx_ref[pl.ds(off, WIDE), :]
        s = jnp.zeros((WIDE, tq), F32)
        for hh in range(IDX_HEADS):
            sc = _dot(kt, qit_ref[hh * IDX_DIM:(hh + 1) * IDX_DIM, :])
            s = s + jnp.maximum(sc, 0.0) * w_t[hh:hh + 1, :]
        kpos = off + lax.broadcasted_iota(I32, (WIDE, tq), 0)
        keys_ref[pl.ds(off, WIDE), :] = jnp.where(kpos <= qpos, _mono_key(s), INT_MIN)
        return carry

    lax.fori_loop(0, nwide, score_body, 0)

    def count(pred):
        def body(c, acc):
            off = pl.multiple_of(c * WIDE, WIDE)
            kt = keys_ref[pl.ds(off, WIDE), :]
            kpos = off + lax.broadcasted_iota(I32, (WIDE, tq), 0)
            hit = jnp.where(pred(kt, kpos), 1.0, 0.0)
            return acc + jnp.sum(hit.reshape(WIDE // COUNT_ROWS, COUNT_ROWS, tq), axis=0)
        acc = lax.fori_loop(0, nwide, body, jnp.zeros((COUNT_ROWS, tq), F32))
        return jnp.sum(acc, axis=0, keepdims=True)

    thr, jlim = _select(count, (1, tq), nsel, True)

    m_ref[...] = jnp.full(m_ref.shape, M_INIT, F32)
    l_ref[...] = jnp.zeros(l_ref.shape, F32)
    acc_ref[...] = jnp.zeros(acc_ref.shape, F32)

    def tile(off, height, bias_ref):
        kt = keys_ref[pl.ds(off, height), :]
        kpos = off + lax.broadcasted_iota(I32, (height, tq), 0)
        sel = ((kt > thr) | ((kt == thr) & (kpos <= jlim))) & (kpos <= qpos)
        amask = jnp.where(sel, 0.0, NEG)
        sts = [_dot(k_ref[pl.ds(off, height), p * LANES:(p + 1) * LANES], qz_ref[p]) for p in range(PAIRS)]
        prs, alphas = [], []
        for h in range(N_HEADS):
            lg = sts[h // 2][:, (h % 2) * tq:(h % 2 + 1) * tq] + amask
            if bias_ref is not None:
                lg = lg + bias_ref[h]
            m_old = m_ref[h]
            m_new = jnp.maximum(m_old, jnp.max(lg, axis=0, keepdims=True))
            alpha = jnp.exp(m_old - m_new)
            pr = jnp.exp(lg - m_new[0:1, :])
            l_ref[h] = alpha * l_ref[h] + jnp.sum(pr, axis=0, keepdims=True)
            m_ref[h] = m_new
            prs.append(pr.astype(BF16))
            alphas.append(alpha[0:1, :])
        for h in range(N_HEADS):
            pv = _dot(vt_ref[h * HEAD_DIM:(h + 1) * HEAD_DIM, pl.ds(off, height)], prs[h])
            acc_ref[h] = alphas[h] * acc_ref[h] + pv

    nfar = jnp.maximum(i - 1, 0)
    nfar_wide = nfar // (WIDE // LANES)

    def far_wide(c, carry):
        tile(pl.multiple_of(c * WIDE, WIDE), WIDE, None)
        return carry

    def far_narrow(j, carry):
        tile(pl.multiple_of(j * LANES, LANES), LANES, None)
        return carry

    lax.fori_loop(0, nfar_wide, far_wide, 0)
    lax.fori_loop(nfar_wide * (WIDE // LANES), nfar, far_narrow, 0)
    for h in range(N_HEADS):
        m_ref[h] = m_ref[h] + far_ref[h]

    @pl.when(i >= 1)
    def _():
        tile(pl.multiple_of((i - 1) * LANES, LANES), LANES, bs_ref)

    tile(pl.multiple_of(i * LANES, LANES), LANES, bd_ref)
    out_t = jnp.concatenate([acc_ref[h] / l_ref[h][0:1, :] for h in range(N_HEADS)], axis=0)
    o_ref[...] = jnp.transpose(out_t)


def _attn_prompt(far, qidx, h2, q, kidx, k, vt, bias_diag_t, bias_sub_t, nsel):
    t = q.shape[0]
    tq = LANES
    whole = lambda i: (0, 0)
    return pl.pallas_call(
        functools.partial(_attn_prompt_kernel, tq=tq, nsel=nsel),
        grid=(t // tq,),
        in_specs=[pl.BlockSpec(memory_space=pltpu.SMEM),
                  pl.BlockSpec((tq, IDX_HEADS * IDX_DIM), lambda i: (i, 0)),
                  pl.BlockSpec((tq, LANES), lambda i: (i, C_KW // LANES)),
                  pl.BlockSpec((tq, MIX_DIM), lambda i: (i, 0)),
                  pl.BlockSpec((t, IDX_DIM), whole),
                  pl.BlockSpec((t, MIX_DIM), whole),
                  pl.BlockSpec((MIX_DIM, t), whole),
                  pl.BlockSpec((N_HEADS, LANES, tq), lambda i: (0, 0, 0)),
                  pl.BlockSpec((N_HEADS, LANES, tq), lambda i: (0, 0, 0))],
        out_specs=pl.BlockSpec((tq, MIX_DIM), lambda i: (i, 0)),
        out_shape=jax.ShapeDtypeStruct((t, MIX_DIM), F32),
        scratch_shapes=[pltpu.VMEM((t, tq), I32),
                        pltpu.VMEM((IDX_HEADS * IDX_DIM, tq), BF16),
                        pltpu.VMEM((PAIRS, LANES, 2 * tq), BF16),
                        pltpu.VMEM((N_HEADS, SUBLANES, tq), F32),
                        pltpu.VMEM((N_HEADS, SUBLANES, tq), F32),
                        pltpu.VMEM((N_HEADS, HEAD_DIM, tq), F32)],
        compiler_params=_cparams(("parallel",)),
        name="attn_prompt",
    )(far, qidx, h2, q, kidx, k, vt, bias_diag_t, bias_sub_t)


QROWS = 8
SEL_PAGES = 8
ATT_PAGES = 8


def _sel_sample_kernel(pt_ref, qst_ref, wcol_ref, *refs, nsel, treal):
    pages = refs[:SEL_PAGES]
    knew_ref, mask_ref, keys_ref = refs[SEL_PAGES:]
    j = pl.program_id(1)
    ntiles = keys_ref.shape[1] // LANES

    def score(kt_t):
        r = jnp.maximum(_dot(qst_ref[0], kt_t), 0.0) * (wcol_ref[0] * IDX_SCALE)
        s = r[0:QROWS]
        for hh in range(1, IDX_HEADS):
            s = s + r[hh * QROWS:(hh + 1) * QROWS]
        return _mono_key(s)

    for g in range(SEL_PAGES):
        off = pl.multiple_of((j * SEL_PAGES + g) * LANES, LANES)
        keys_ref[:, pl.ds(off, LANES)] = score(pages[g][0, 0].astype(BF16))

    @pl.when(j == pl.num_programs(1) - 1)
    def _():
        rowq = lax.broadcasted_iota(I32, (QROWS, LANES), 0)
        lane = lax.broadcasted_iota(I32, (QROWS, LANES), 1)
        knew = jnp.where((lane < treal) & (lane <= rowq), score(knew_ref[0]), INT_MIN)
        keys_ref[:, (ntiles - 1) * LANES:] = knew

        def count(pred):
            def body(jt, acc):
                off = pl.multiple_of(jt * LANES, LANES)
                return acc + jnp.where(pred(keys_ref[:, pl.ds(off, LANES)], off + lane), 1.0, 0.0)
            acc = lax.fori_loop(0, ntiles, body, jnp.zeros((QROWS, LANES), F32))
            return jnp.sum(acc, axis=1, keepdims=True)

        row_ok = lax.broadcasted_iota(I32, (QROWS, 1), 0) < treal
        thr, jlim = _select(count, (QROWS, 1), nsel, row_ok)

        def body(jt, carry):
            off = pl.multiple_of(jt * LANES, LANES)
            kt = keys_ref[:, pl.ds(off, LANES)]
            sel = (kt > thr) | ((kt == thr) & (off + lane <= jlim))
            mask_ref[0, :, pl.ds(off, LANES)] = jnp.where(sel, 1.0, 0.0)
            return carry

        lax.fori_loop(0, ntiles, body, 0)


def _sel_sample(page_table, qst, wcol, cache_kidx, knew, layer, nsel, treal):
    b, npages = page_table.shape
    nk = (npages + 1) * PAGE
    page_spec = lambda g: pl.BlockSpec(
        (1, 1, IDX_DIM, PAGE), lambda i, j, pt: (layer, pt[i, j * SEL_PAGES + g], 0, 0))
    grid_spec = pltpu.PrefetchScalarGridSpec(
        num_scalar_prefetch=1,
        grid=(b, npages // SEL_PAGES),
        in_specs=[pl.BlockSpec((1, IDX_HEADS * QROWS, IDX_DIM), lambda i, j, pt: (i, 0, 0)),
                  pl.BlockSpec((1, IDX_HEADS * QROWS, 1), lambda i, j, pt: (i, 0, 0))]
                 + [page_spec(g) for g in range(SEL_PAGES)]
                 + [pl.BlockSpec((1, IDX_DIM, PAGE), lambda i, j, pt: (i, 0, 0))],
        out_specs=pl.BlockSpec((1, QROWS, nk), lambda i, j, pt: (i, 0, 0)),
        scratch_shapes=[pltpu.VMEM((QROWS, nk), I32)],
    )
    return pl.pallas_call(
        functools.partial(_sel_sample_kernel, nsel=nsel, treal=treal),
        grid_spec=grid_spec,
        out_shape=jax.ShapeDtypeStruct((b, QROWS, nk), F32),
        compiler_params=_cparams(("parallel", "arbitrary")),
        name="sel_sample",
    )(page_table, qst, wcol, *([cache_kidx] * SEL_PAGES), knew)


def _attn_sample_kernel(pt_ref, far_ref, qh_ref, *refs):
    kp = refs[:ATT_PAGES]
    vp = refs[ATT_PAGES:2 * ATT_PAGES]
    mask_ref, masknew_ref, blast_ref, bnew_ref, knew_ref, vnew_ref, o_ref, m_ref, l_ref, acc_ref = refs[2 * ATT_PAGES:]
    j = pl.program_id(1)
    is_last = j == pl.num_programs(1) - 1

    @pl.when(j == 0)
    def _():
        m_ref[...] = jnp.full(m_ref.shape, M_INIT, F32)
        l_ref[...] = jnp.zeros(l_ref.shape, F32)
        acc_ref[...] = jnp.zeros(acc_ref.shape, F32)

    def update(lgs, vhs):
        prs, alphas = [], []
        for h in range(N_HEADS):
            m_old = m_ref[h]
            m_new = jnp.maximum(m_old, jnp.max(lgs[h], axis=1, keepdims=True))
            alpha = jnp.exp(m_old - m_new)
            pr = jnp.exp(lgs[h] - m_new[:, 0:1])
            l_ref[h] = alpha * l_ref[h] + jnp.sum(pr, axis=1, keepdims=True)
            m_ref[h] = m_new
            prs.append(pr.astype(BF16))
            alphas.append(alpha[:, :HEAD_DIM])
        for h in range(N_HEADS):
            acc_ref[h] = alphas[h] * acc_ref[h] + _dot_nt(prs[h], vhs[h])

    def q_of(h):
        return (qh_ref[0, h].astype(F32) * ATT_SCALE).astype(BF16)

    def head_t(refs_, h):
        return jnp.concatenate([refs_[g][0, 0, h] for g in range(ATT_PAGES)], axis=1).astype(BF16)

    amask = jnp.where(mask_ref[0] > 0.5, 0.0, NEG)
    lgs = []
    for h in range(N_HEADS):
        far = jnp.full((QROWS, LANES), far_ref[h], F32)
        bias = jnp.concatenate([far] * (ATT_PAGES - 1) + [jnp.where(is_last, blast_ref[h], far)], axis=1)
        lgs.append(_dot(q_of(h), head_t(kp, h)) + amask + bias)
    update(lgs, [head_t(vp, h) for h in range(N_HEADS)])

    @pl.when(is_last)
    def _():
        amask_new = jnp.where(masknew_ref[0] > 0.5, 0.0, NEG)
        lgs_new = [_dot(q_of(h), knew_ref[0, h]) + amask_new + bnew_ref[h] for h in range(N_HEADS)]
        update(lgs_new, [vnew_ref[0, h] for h in range(N_HEADS)])
        outs = []
        for h in range(N_HEADS):
            lh = l_ref[h][:, :HEAD_DIM]
            outs.append(acc_ref[h] / jnp.where(lh > 0.0, lh, 1.0))
        o_ref[0] = jnp.concatenate(outs, axis=1)


def _attn_sample(page_table, far, qh, cache_k2, cache_v2, mask, bias_last, bias_new, knew, vnew, layer):
    b, npages = page_table.shape
    nsteps = npages // ATT_PAGES
    page_spec = lambda g: pl.BlockSpec(
        (1, 1, N_HEADS, HEAD_DIM, PAGE), lambda i, j, pt, far: (layer, pt[i, j * ATT_PAGES + g], 0, 0, 0))
    const3 = lambda i, j, pt, far: (0, 0, 0)
    per_b4 = lambda i, j, pt, far: (i, 0, 0, 0)
    grid_spec = pltpu.PrefetchScalarGridSpec(
        num_scalar_prefetch=2,
        grid=(b, nsteps),
        in_specs=[pl.BlockSpec((1, N_HEADS, QROWS, HEAD_DIM), per_b4)]
                 + [page_spec(g) for g in range(ATT_PAGES)] * 2
                 + [pl.BlockSpec((1, QROWS, ATT_PAGES * PAGE), lambda i, j, pt, far: (i, 0, j)),
                    pl.BlockSpec((1, QROWS, PAGE), lambda i, j, pt, far: (i, 0, npages)),
                    pl.BlockSpec((N_HEADS, QROWS, PAGE), const3),
                    pl.BlockSpec((N_HEADS, QROWS, PAGE), const3),
                    pl.BlockSpec((1, N_HEADS, HEAD_DIM, PAGE), per_b4),
                    pl.BlockSpec((1, N_HEADS, HEAD_DIM, PAGE), per_b4)],
        out_specs=pl.BlockSpec((1, QROWS, MIX_DIM), lambda i, j, pt, far: (i, 0, 0)),
        scratch_shapes=[pltpu.VMEM((N_HEADS, QROWS, LANES), F32),
                        pltpu.VMEM((N_HEADS, QROWS, LANES), F32),
                        pltpu.VMEM((N_HEADS, QROWS, HEAD_DIM), F32)],
    )
    return pl.pallas_call(
        _attn_sample_kernel,
        grid_spec=grid_spec,
        out_shape=jax.ShapeDtypeStruct((b, QROWS, MIX_DIM), F32),
        compiler_params=_cparams(("parallel", "arbitrary")),
        name="attn_sample",
    )(page_table, far, qh, *([cache_k2] * ATT_PAGES), *([cache_v2] * ATT_PAGES), mask, mask,
      bias_last, bias_new, knew, vnew)


def _bucket_table(n):
    d = np.arange(n)
    nf = np.maximum(d, 1).astype(np.float32)
    half = REL_BUCKETS // 2
    large = half + (np.log(nf / np.float32(half)) / np.float32(math.log(REL_MAX_DIST / half))
                    * np.float32(REL_BUCKETS - half)).astype(np.int32)
    return np.where(d < half, d, np.minimum(large, REL_BUCKETS - 1)).astype(np.int32)


def _pack_w_in(w):
    w_t = jnp.transpose(w, (0, 2, 1))
    p, z, xbc, dt, q, k, v, qi, ki, wi, u, vs = jnp.split(w_t, np.cumsum(IN_SPLITS)[:-1].tolist(), axis=1)
    zeros = lambda n: jnp.zeros((w.shape[0], n, w.shape[1]), w.dtype)
    packed = jnp.concatenate(
        [p, z, xbc, q, k, v, u, vs, qi, ki, wi, zeros(LANES - IDX_DIM - IDX_HEADS), dt, zeros(LANES - N_HEADS)],
        axis=1)
    return packed.astype(BF16)


def _pad_lanes(x):
    return jnp.pad(x.reshape(1, -1), ((0, 0), (0, LANES - x.shape[-1])))


def _pad_axis(x, axis, size):
    pads = [(0, 0)] * x.ndim
    pads[axis] = (0, size - x.shape[axis])
    return jnp.pad(x, pads)


def kernel(x_prompt, x_sample, state_pool, state_conv, state_ssm, cache_k, cache_v, cache_kidx, page_table,
           rel_bias, norm_mix, w_in, pool_w, pool_scale, conv_w, conv_b, dt_bias, a_log, d_skip, ssm_norm,
           sgu_w, sgu_b, w_out, norm_mlp, mlp_w1, mlp_w2, norm_final):
    depth = w_in.shape[0]
    bp, t, _ = x_prompt.shape
    bs, ts, _ = x_sample.shape
    npages = page_table.shape[1]
    past = npages * PAGE
    n_pool = cache_k.shape[1]
    assert bp == 1 and t % WIDE == 0 and npages % SEL_PAGES == 0 and npages % ATT_PAGES == 0
    assert max(t, past + PAGE) <= 2 ** POS_BITS and ts <= QROWS
    tc = LANES
    nsel_p = min(TOPK_MAX, t // 4)
    nsel_s = min(TOPK_MAX, (past + ts) // 4)

    bucket = _bucket_table(2 * LANES)
    far = rel_bias[REL_BUCKETS - 1]
    kk = np.arange(LANES)[:, None]
    qq = np.arange(LANES)[None, :]
    bias_diag_t = jnp.transpose(rel_bias[bucket[np.maximum(qq - kk, 0)]], (2, 0, 1))
    bias_sub_t = jnp.transpose(rel_bias[bucket[np.minimum(LANES + qq - kk, 2 * LANES - 1)]], (2, 0, 1))
    sq = np.arange(QROWS)[:, None]
    sk = np.arange(PAGE)[None, :]
    bias_last = jnp.transpose(rel_bias[bucket[np.minimum(PAGE + sq - sk, 2 * LANES - 1)]], (2, 0, 1))
    bias_new = jnp.transpose(rel_bias[bucket[np.maximum(sq - sk, 0)]], (2, 0, 1))

    cache_k2 = jnp.transpose(cache_k, (0, 1, 3, 4, 2))
    cache_v2 = jnp.transpose(cache_v, (0, 1, 3, 4, 2))
    cache_kidx2 = jnp.transpose(cache_kidx, (0, 1, 3, 2))
    w_in_t = _pack_w_in(w_in)

    xp = x_prompt.reshape(t, D_MODEL)
    xs = x_sample.reshape(bs * ts, D_MODEL)
    zero_pool = jnp.zeros((1, POOL_HIST, MIX_DIM), F32)
    zero_conv = jnp.zeros((1, CONV_W - 1, XBC_DIM), F32)
    zero_ssm = jnp.zeros((1, N_HEADS, HEAD_DIM, SSM_STATE), F32)
    tm_p = min(512, t)
    tm_s = bs * ts
    outs_p, outs_s = [], []

    for l in range(depth):
        wl_out = w_out[l].astype(BF16)
        w1 = mlp_w1[l].astype(BF16)
        w2 = mlp_w2[l].astype(BF16)
        g_mix = norm_mix[l].reshape(1, D_MODEL)
        g_mlp = norm_mlp[l].reshape(1, D_MODEL)
        g_out = norm_final.reshape(1, D_MODEL) if l == depth - 1 else g_mlp
        pscale = pool_scale[l].reshape(1, MIX_DIM)
        cb = conv_b[l].reshape(1, XBC_DIM)
        dtb, alog, dsk = _pad_lanes(dt_bias[l]), _pad_lanes(a_log[l]), _pad_lanes(d_skip[l])
        snorm = ssm_norm[l].reshape(1, MIX_DIM)
        sgu_b_col = sgu_b[l][:, :, None]
        final = l == depth - 1

        hp = _inproj(xp, g_mix, w_in_t, l, tm=min(1024, t))
        hp3 = hp.reshape(1, t, H_PACKED)
        y_pool, new_pool = _pool(hp3, zero_pool, pool_w[l], pscale, tc=tc, treal=tc, pos0=0)
        y_ssm, new_conv, new_h = _ssd(hp3, zero_conv, zero_ssm, conv_w[l], cb, dtb, alog, dsk, snorm,
                                      tc=tc, treal=tc)
        y_sgu, _ = _sgu(hp3, sgu_w[l], sgu_b_col, tc=tc, treal=tc)
        kf = hp[:, C_K:C_K + MIX_DIM]
        vf = hp[:, C_V:C_V + MIX_DIM]
        kif = hp[:, C_KW:C_KW + IDX_DIM]
        y_att = _attn_prompt(far, hp[:, C_QIDX:C_QIDX + IDX_HEADS * IDX_DIM].astype(BF16), hp,
                             hp[:, C_Q:C_Q + MIX_DIM].astype(BF16), kif.astype(BF16), kf.astype(BF16),
                             jnp.transpose(vf.astype(BF16)), bias_diag_t, bias_sub_t, nsel_p)
        x1, xn = _outproj(xp, [y_pool[0], y_ssm[0], y_att, y_sgu[0]], wl_out, g_mlp, tm=256)
        xp = _mlp(xn, w1, w2, x1, g_out, tm=tm_p, final_norm=final)
        outs_p.append((new_pool, new_conv, new_h, kf.reshape(1, t, N_HEADS, HEAD_DIM),
                       vf.reshape(1, t, N_HEADS, HEAD_DIM), kif.reshape(1, t, IDX_DIM)))

        hs = _inproj(xs, g_mix, w_in_t, l, tm=tm_s)
        hs3 = hs.reshape(bs, ts, H_PACKED)
        y_pool, new_pool = _pool(hs3, state_pool[l], pool_w[l], pscale, tc=tc, treal=ts, pos0=past)
        y_ssm, new_conv, new_h = _ssd(hs3, state_conv[l], state_ssm[l], conv_w[l], cb, dtb, alog, dsk, snorm,
                                      tc=tc, treal=ts)
        y_sgu, v_rows = _sgu(hs3, sgu_w[l], sgu_b_col, tc=tc, treal=ts)
        kf = hs3[:, :, C_K:C_K + MIX_DIM]
        vf = hs3[:, :, C_V:C_V + MIX_DIM]
        kif = hs3[:, :, C_KW:C_KW + IDX_DIM]
        qi = hs3[:, :, C_QIDX:C_QIDX + IDX_HEADS * IDX_DIM].reshape(bs, ts, IDX_HEADS, IDX_DIM)
        qst = _pad_axis(jnp.transpose(qi, (0, 2, 1, 3)), 2, QROWS)
        qst = qst.reshape(bs, IDX_HEADS * QROWS, IDX_DIM).astype(BF16)
        wi = hs3[:, :, C_KW + IDX_DIM:C_KW + IDX_DIM + IDX_HEADS]
        wcol = _pad_axis(jnp.transpose(wi, (0, 2, 1)), 2, QROWS).reshape(bs, IDX_HEADS * QROWS, 1)
        kinew_t = _pad_axis(jnp.transpose(kif, (0, 2, 1)), 2, PAGE).astype(BF16)
        mask = _sel_sample(page_table, qst, wcol, cache_kidx2, kinew_t, l, nsel_s, ts)
        heads = lambda x, perm: jnp.transpose(x.reshape(bs, ts, N_HEADS, HEAD_DIM), perm)
        qh = _pad_axis(heads(hs3[:, :, C_Q:C_Q + MIX_DIM], (0, 2, 1, 3)), 2, QROWS).astype(BF16)
        knew = _pad_axis(heads(kf, (0, 2, 3, 1)), 3, PAGE).astype(BF16)
        vnew = _pad_axis(heads(vf, (0, 2, 3, 1)), 3, PAGE).astype(BF16)
        att = _attn_sample(page_table, far, qh, cache_k2, cache_v2, mask, bias_last, bias_new, knew, vnew, l)
        y_att = att[:, :ts].reshape(bs * ts, MIX_DIM)
        flat = lambda y: y.reshape(bs * ts, MIX_DIM)
        x1, xn = _outproj(xs, [flat(y_pool), flat(y_ssm), y_att, flat(y_sgu)], wl_out, g_mlp, tm=tm_s)
        xs = _mlp(xn, w1, w2, x1, g_out, tm=tm_s, final_norm=final)
        outs_s.append((new_pool, new_conv, new_h, kf.reshape(bs, ts, N_HEADS, HEAD_DIM),
                       vf.reshape(bs, ts, N_HEADS, HEAD_DIM), kif, v_rows))

    stack = lambda outs, i: jnp.stack([o[i] for o in outs])
    return (xp.reshape(1, t, D_MODEL), xs.reshape(bs, ts, D_MODEL),
            *[stack(outs_p, i) for i in range(6)],
            *[stack(outs_s, i) for i in range(7)])
```

```python
import functools
import math

import numpy as np
import jax
import jax.numpy as jnp
from jax import lax
from jax.experimental import pallas as pl
from jax.experimental.pallas import tpu as pltpu

F32 = jnp.float32
BF16 = jnp.bfloat16
I32 = jnp.int32

LANES = 128
SUBLANES = 8
D_MODEL = 2048
N_HEADS = 8
HEAD_DIM = 64
MIX_DIM = 512
POOL_WINDOWS = (2, 4, 8, 16)
POOL_HIST = 15
CONV_W = 4
SSM_GROUPS = 2
SSM_STATE = 128
XBC_DIM = MIX_DIM + 2 * SSM_GROUPS * SSM_STATE
IDX_HEADS = 4
IDX_DIM = 64
TOPK_MAX = 256
PAGE = 128
REL_BUCKETS = 32
REL_MAX_DIST = 128
ATT_SCALE = HEAD_DIM ** -0.5
IDX_SCALE = (IDX_HEADS * IDX_DIM) ** -0.5
SGU_GROUPS = 4
D_FF = 4 * D_MODEL
EPS = 1e-6
NEG = -1e30
INT_MIN = -(2 ** 31)
INT_MAX = 2 ** 31 - 1
IN_SPLITS = (512, 512, 1024, 8, 512, 512, 512, 256, 64, 4, 512, 512)

C_P, C_Z, C_XBC, C_Q, C_K, C_V, C_U, C_VS, C_QIDX, C_KW, C_DT = (
    0, 512, 1024, 2048, 2560, 3072, 3584, 4096, 4608, 4864, 4992)
H_PACKED = 5120
VMEM_LIMIT = 56 * 1024 * 1024


def _cparams(sem):
    return pltpu.CompilerParams(dimension_semantics=sem, vmem_limit_bytes=VMEM_LIMIT)


def _dot(a, b):
    return jnp.dot(a, b, preferred_element_type=F32)


def _dot_nt(a, b):
    return lax.dot_general(a, b, (((1,), (1,)), ((), ())), preferred_element_type=F32)


def _dot_tn(a, b):
    return lax.dot_general(a, b, (((0,), (0,)), ((), ())), preferred_element_type=F32)


def _silu(x):
    return x * (1.0 / (1.0 + jnp.exp(-x)))


def _gelu_tanh(x):
    return 0.5 * x * (1.0 + jnp.tanh(math.sqrt(2.0 / math.pi) * (x + 0.044715 * (x * x * x))))


def _softplus(x):
    return jnp.maximum(x, 0.0) + jnp.log1p(jnp.exp(-jnp.abs(x)))


def _rms(x, g):
    ms = jnp.mean(x * x, axis=-1, keepdims=True)
    return x * lax.rsqrt(ms + EPS) * g


def _inproj_kernel(x_ref, g_ref, w_ref, o_ref, xn_ref):
    @pl.when(pl.program_id(1) == 0)
    def _():
        xn_ref[...] = _rms(x_ref[...], g_ref[...]).astype(BF16)

    o_ref[...] = _dot_nt(xn_ref[...], w_ref[0])


def _inproj(x, g, w_t, layer, tm, tn=512):
    m = x.shape[0]
    n = w_t.shape[1]
    return pl.pallas_call(
        _inproj_kernel,
        grid=(m // tm, n // tn),
        in_specs=[pl.BlockSpec((tm, D_MODEL), lambda i, j: (i, 0)),
                  pl.BlockSpec((1, D_MODEL), lambda i, j: (0, 0)),
                  pl.BlockSpec((1, tn, D_MODEL), lambda i, j: (layer, j, 0))],
        out_specs=pl.BlockSpec((tm, tn), lambda i, j: (i, j)),
        out_shape=jax.ShapeDtypeStruct((m, n), F32),
        scratch_shapes=[pltpu.VMEM((tm, D_MODEL), BF16)],
        compiler_params=_cparams(("parallel", "arbitrary")),
        name="inproj",
    )(x, g, w_t)


def _outproj_kernel(x_ref, y0_ref, y1_ref, y2_ref, y3_ref, w_ref, g_ref, x1_ref, xn_ref):
    y = jnp.concatenate([y0_ref[...], y1_ref[...], y2_ref[...], y3_ref[...]], axis=1).astype(BF16)
    x1 = x_ref[...] + _dot(y, w_ref[0])
    x1_ref[...] = x1
    xn_ref[...] = _rms(x1, g_ref[...]).astype(BF16)


def _outproj(x, ys, w, layer, g, tm):
    m = x.shape[0]
    row = lambda i: (i, 0)
    return pl.pallas_call(
        _outproj_kernel,
        grid=(m // tm,),
        in_specs=[pl.BlockSpec((tm, D_MODEL), row)]
                 + [pl.BlockSpec((tm, MIX_DIM), row)] * 4
                 + [pl.BlockSpec((1, D_MODEL, D_MODEL), lambda i: (layer, 0, 0)),
                    pl.BlockSpec((1, D_MODEL), lambda i: (0, 0))],
        out_specs=[pl.BlockSpec((tm, D_MODEL), row), pl.BlockSpec((tm, D_MODEL), row)],
        out_shape=[jax.ShapeDtypeStruct((m, D_MODEL), F32), jax.ShapeDtypeStruct((m, D_MODEL), BF16)],
        compiler_params=_cparams(("parallel",)),
        name="outproj",
    )(x, *ys, w, g)


def _mlp_kernel(xn_ref, w1_ref, w2_ref, x1_ref, g_ref, o_ref, acc_ref, *, final_norm):
    j = pl.program_id(1)

    @pl.when(j == 0)
    def _():
        acc_ref[...] = jnp.zeros_like(acc_ref)

    hid = _dot(xn_ref[...], w1_ref[0])
    hid = jnp.square(jnp.maximum(hid, 0.0)).astype(BF16)
    acc_ref[...] += _dot(hid, w2_ref[0])

    @pl.when(j == pl.num_programs(1) - 1)
    def _():
        x2 = x1_ref[...] + acc_ref[...]
        o_ref[...] = _rms(x2, g_ref[...]) if final_norm else x2


def _mlp(xn, w1, w2, layer, x1, g, tm, final_norm, tf=512):
    m = xn.shape[0]
    return pl.pallas_call(
        functools.partial(_mlp_kernel, final_norm=final_norm),
        grid=(m // tm, D_FF // tf),
        in_specs=[pl.BlockSpec((tm, D_MODEL), lambda i, j: (i, 0)),
                  pl.BlockSpec((1, D_MODEL, tf), lambda i, j: (layer, 0, j)),
                  pl.BlockSpec((1, tf, D_MODEL), lambda i, j: (layer, j, 0)),
                  pl.BlockSpec((tm, D_MODEL), lambda i, j: (i, 0)),
                  pl.BlockSpec((1, D_MODEL), lambda i, j: (0, 0))],
        out_specs=pl.BlockSpec((tm, D_MODEL), lambda i, j: (i, 0)),
        out_shape=jax.ShapeDtypeStruct((m, D_MODEL), F32),
        scratch_shapes=[pltpu.VMEM((tm, D_MODEL), F32)],
        compiler_params=_cparams(("parallel", "arbitrary")),
        name="mlp",
    )(xn, w1, w2, x1, g)


def _pool_kernel(p_ref, hist_ref, w_ref, scale_ref, y_ref, newhist_ref, ext_ref, *, tc, treal, pos0):
    c = pl.program_id(1)
    hrows = POOL_HIST + 1

    @pl.when(c == 0)
    def _():
        ext_ref[0:1, :] = jnp.zeros((1, MIX_DIM), F32)
        ext_ref[1:hrows, :] = hist_ref[0]

    if treal < tc:
        ext_ref[hrows:hrows + tc, :] = jnp.zeros((tc, MIX_DIM), F32)
    ext_ref[hrows:hrows + treal, :] = p_ref[0]

    pos = pos0 + c * treal + lax.broadcasted_iota(I32, (tc, 1), 0)
    for g, win in enumerate(POOL_WINDOWS):
        cols = slice(g * LANES, (g + 1) * LANES)
        cur = ext_ref[hrows:hrows + tc, cols]
        s = cur
        for j in range(1, win):
            s = s + ext_ref[hrows - j:hrows - j + tc, cols]
        cnt = jnp.minimum(pos + 1, win).astype(F32)
        diff = s / cnt - cur
        y = _dot(diff, w_ref[g]) * scale_ref[:, cols]
        y_ref[0, :, cols] = y[:treal]

    newhist_ref[0] = ext_ref[treal + 1:treal + hrows, :]
    ext_ref[0:hrows, :] = ext_ref[treal:treal + hrows, :]


def _pool(h3, hist, pool_w, pool_scale, *, tc, treal, pos0):
    b, t, _ = h3.shape
    return pl.pallas_call(
        functools.partial(_pool_kernel, tc=tc, treal=treal, pos0=pos0),
        grid=(b, t // treal),
        in_specs=[pl.BlockSpec((1, treal, MIX_DIM), lambda i, c: (i, c, C_P // MIX_DIM)),
                  pl.BlockSpec((1, POOL_HIST, MIX_DIM), lambda i, c: (i, 0, 0)),
                  pl.BlockSpec((4, LANES, LANES), lambda i, c: (0, 0, 0)),
                  pl.BlockSpec((1, MIX_DIM), lambda i, c: (0, 0))],
        out_specs=[pl.BlockSpec((1, treal, MIX_DIM), lambda i, c: (i, c, 0)),
                   pl.BlockSpec((1, POOL_HIST, MIX_DIM), lambda i, c: (i, 0, 0))],
        out_shape=[jax.ShapeDtypeStruct((b, t, MIX_DIM), F32),
                   jax.ShapeDtypeStruct((b, POOL_HIST, MIX_DIM), F32)],
        scratch_shapes=[pltpu.VMEM((POOL_HIST + 1 + tc, MIX_DIM), F32)],
        compiler_params=_cparams(("parallel", "arbitrary")),
        name="pool",
    )(h3, hist, pool_w, pool_scale)


def _sgu_kernel(u_ref, v_ref, w_ref, b_ref, y_ref, vout_ref, ubuf_ref, vbuf_ref, *, tc, treal):
    if treal < tc:
        ubuf_ref[...] = jnp.zeros((tc, MIX_DIM), F32)
        vbuf_ref[...] = jnp.zeros((tc, MIX_DIM), F32)
    ubuf_ref[0:treal, :] = u_ref[0]
    vbuf_ref[0:treal, :] = v_ref[0]
    u = _gelu_tanh(ubuf_ref[...])
    v = _gelu_tanh(vbuf_ref[...])
    vout_ref[0] = v[:treal]
    r = lax.broadcasted_iota(I32, (tc, tc), 0)
    s = lax.broadcasted_iota(I32, (tc, tc), 1)
    for g in range(SGU_GROUPS):
        cols = slice(g * LANES, (g + 1) * LANES)
        w = jnp.where(s <= r, w_ref[g], 0.0)
        mixed = _dot(w, v[:, cols]) + b_ref[g]
        y_ref[0, :, cols] = (u[:, cols] * mixed)[:treal]


def _sgu(h3, sgu_w, sgu_b_col, *, tc, treal):
    b, t, _ = h3.shape
    return pl.pallas_call(
        functools.partial(_sgu_kernel, tc=tc, treal=treal),
        grid=(b, t // treal),
        in_specs=[pl.BlockSpec((1, treal, MIX_DIM), lambda i, c: (i, c, C_U // MIX_DIM)),
                  pl.BlockSpec((1, treal, MIX_DIM), lambda i, c: (i, c, C_VS // MIX_DIM)),
                  pl.BlockSpec((SGU_GROUPS, tc, tc), lambda i, c: (0, 0, 0)),
                  pl.BlockSpec((SGU_GROUPS, tc, 1), lambda i, c: (0, 0, 0))],
        out_specs=[pl.BlockSpec((1, treal, MIX_DIM), lambda i, c: (i, c, 0)),
                   pl.BlockSpec((1, treal, MIX_DIM), lambda i, c: (i, c, 0))],
        out_shape=[jax.ShapeDtypeStruct((b, t, MIX_DIM), F32),
                   jax.ShapeDtypeStruct((b, t, MIX_DIM), F32)],
        scratch_shapes=[pltpu.VMEM((tc, MIX_DIM), F32), pltpu.VMEM((tc, MIX_DIM), F32)],
        compiler_params=_cparams(("parallel", "parallel")),
        name="sgu",
    )(h3, h3, sgu_w, sgu_b_col)


def _cumsum_rows(a, n):
    row = lax.broadcasted_iota(I32, a.shape, 0)
    sh = 1
    while sh < n:
        a = a + jnp.where(row >= sh, pltpu.roll(a, sh, 0), 0.0)
        sh *= 2
    return a


def _ssd_kernel(z_ref, xbc_ref, dt_ref, hc_ref, h0_ref, convw_ref, convb_ref, dtb_ref, alog_ref, dskip_ref,
                norm_ref, y_ref, newconv_ref, newh_ref, ext_ref, zbuf_ref, dtbuf_ref, state_ref, *, tc, treal):
    c = pl.program_id(1)
    pre = SUBLANES
    hist = CONV_W - 1

    @pl.when(c == 0)
    def _():
        ext_ref[0:pre - hist, :] = jnp.zeros((pre - hist, XBC_DIM), F32)
        ext_ref[pre - hist:pre, :] = hc_ref[0]
        state_ref[...] = h0_ref[...]

    if treal < tc:
        ext_ref[pre:pre + tc, :] = jnp.zeros((tc, XBC_DIM), F32)
        zbuf_ref[...] = jnp.zeros((tc, MIX_DIM), F32)
        dtbuf_ref[...] = jnp.zeros((tc, LANES), F32)
    ext_ref[pre:pre + treal, :] = xbc_ref[0]
    zbuf_ref[0:treal, :] = z_ref[0]
    dtbuf_ref[0:treal, :] = dt_ref[0]

    conv = convb_ref[...]
    for j in range(CONV_W):
        conv = conv + ext_ref[pre - hist + j:pre - hist + j + tc, :] * convw_ref[j:j + 1, :]
    conv = _silu(conv)
    xs = conv[:, :MIX_DIM]
    bmat = conv[:, MIX_DIM:MIX_DIM + SSM_GROUPS * SSM_STATE]
    cmat = conv[:, MIX_DIM + SSM_GROUPS * SSM_STATE:]

    rowi = lax.broadcasted_iota(I32, (tc, LANES), 0)
    dt = _softplus(dtbuf_ref[...] + dtb_ref[...])
    if treal < tc:
        dt = jnp.where(rowi < treal, dt, 0.0)
    a = dt * (-jnp.exp(alog_ref[...]))
    acum = _cumsum_rows(a, tc)
    acum_t = jnp.transpose(acum)
    total = acum[tc - 1:tc, :]
    causal = lax.broadcasted_iota(I32, (tc, tc), 1) <= lax.broadcasted_iota(I32, (tc, tc), 0)

    ys = []
    heads_per_group = N_HEADS // SSM_GROUPS
    for g in range(SSM_GROUPS):
        bg = bmat[:, g * SSM_STATE:(g + 1) * SSM_STATE]
        cg = cmat[:, g * SSM_STATE:(g + 1) * SSM_STATE]
        cb = _dot_nt(cg, bg)
        for hh in range(heads_per_group):
            h = g * heads_per_group + hh
            col = acum[:, h:h + 1]
            row = acum_t[h:h + 1, :]
            decay = jnp.exp(jnp.where(causal, col - row, NEG))
            xh = xs[:, h * HEAD_DIM:(h + 1) * HEAD_DIM]
            xdt = xh * dt[:, h:h + 1]
            hprev = state_ref[0, h]
            y = _dot(cb * decay, xdt)
            y = y + _dot_nt(cg, hprev) * jnp.exp(col)
            y = y + dskip_ref[:, h:h + 1] * xh
            ys.append(y)
            tot = total[:, h:h + 1]
            st = _dot_tn(xdt * jnp.exp(tot - col), bg)
            state_ref[0, h] = jnp.exp(tot) * hprev + st
    y = jnp.concatenate(ys, axis=1) * _silu(zbuf_ref[...])
    gw = MIX_DIM // SSM_GROUPS
    outs = []
    for g in range(SSM_GROUPS):
        yg = y[:, g * gw:(g + 1) * gw]
        ms = jnp.mean(yg * yg, axis=-1, keepdims=True)
        outs.append(yg * lax.rsqrt(ms + EPS) * norm_ref[:, g * gw:(g + 1) * gw])
    y_ref[0] = jnp.concatenate(outs, axis=1)[:treal]

    newconv_ref[0] = ext_ref[pre + treal - hist:pre + treal, :]
    ext_ref[0:pre, :] = ext_ref[treal:treal + pre, :]
    newh_ref[...] = state_ref[...]


def _ssd(h3, hist_conv, h0, conv_w, conv_b, dtb, alog, dskip, norm, *, tc, treal):
    b, t, _ = h3.shape
    full2 = lambda i, c: (0, 0)
    return pl.pallas_call(
        functools.partial(_ssd_kernel, tc=tc, treal=treal),
        grid=(b, t // treal),
        in_specs=[pl.BlockSpec((1, treal, MIX_DIM), lambda i, c: (i, c, C_Z // MIX_DIM)),
                  pl.BlockSpec((1, treal, XBC_DIM), lambda i, c: (i, c, C_XBC // XBC_DIM)),
                  pl.BlockSpec((1, treal, LANES), lambda i, c: (i, c, C_DT // LANES)),
                  pl.BlockSpec((1, CONV_W - 1, XBC_DIM), lambda i, c: (i, 0, 0)),
                  pl.BlockSpec((1, N_HEADS, HEAD_DIM, SSM_STATE), lambda i, c: (i, 0, 0, 0)),
                  pl.BlockSpec((CONV_W, XBC_DIM), full2),
                  pl.BlockSpec((1, XBC_DIM), full2),
                  pl.BlockSpec((1, LANES), full2),
                  pl.BlockSpec((1, LANES), full2),
                  pl.BlockSpec((1, LANES), full2),
                  pl.BlockSpec((1, MIX_DIM), full2)],
        out_specs=[pl.BlockSpec((1, treal, MIX_DIM), lambda i, c: (i, c, 0)),
                   pl.BlockSpec((1, CONV_W - 1, XBC_DIM), lambda i, c: (i, 0, 0)),
                   pl.BlockSpec((1, N_HEADS, HEAD_DIM, SSM_STATE), lambda i, c: (i, 0, 0, 0))],
        out_shape=[jax.ShapeDtypeStruct((b, t, MIX_DIM), F32),
                   jax.ShapeDtypeStruct((b, CONV_W - 1, XBC_DIM), F32),
                   jax.ShapeDtypeStruct((b, N_HEADS, HEAD_DIM, SSM_STATE), F32)],
        scratch_shapes=[pltpu.VMEM((SUBLANES + tc, XBC_DIM), F32),
                        pltpu.VMEM((tc, MIX_DIM), F32),
                        pltpu.VMEM((tc, LANES), F32),
                        pltpu.VMEM((1, N_HEADS, HEAD_DIM, SSM_STATE), F32)],
        compiler_params=_cparams(("parallel", "arbitrary")),
        name="ssd",
    )(h3, h3, h3, hist_conv, h0, conv_w, conv_b, dtb, alog, dskip, norm)


M_INIT = 0.5 * NEG
POS_BITS = 14
assert math.frexp(ATT_SCALE)[0] == 0.5, "q is pre-scaled in bf16, exact only for a power-of-two scale"


def _mono_key(s):
    s = jnp.where(s == 0.0, 0.0, s)
    b = lax.bitcast_convert_type(s, I32)
    return b ^ ((b >> 31) & INT_MAX)


def _select(count, stat_shape, nsel, query_ok, tie_limit=None):
    kf = float(nsel)
    c0 = count(lambda kt, pos: kt >= 0)
    nonneg = c0 >= kf
    prefix = jnp.where(nonneg, 0, INT_MIN).astype(I32)

    def bit_body(it, carry):
        prefix, cge = carry
        cand = prefix | jnp.left_shift(jnp.int32(1), 30 - it)
        cnt = count(lambda kt, pos: kt >= cand)
        take = cnt >= kf
        return jnp.where(take, cand, prefix), jnp.where(take, cnt, cge)

    v, cge = lax.fori_loop(0, 31, bit_body, (prefix, jnp.where(nonneg, c0, kf)))
    need = (cge > kf) & (v != INT_MIN) & query_ok

    def bit_tie_limit(v, want):
        def body(it, x):
            cand = x | jnp.left_shift(jnp.int32(1), POS_BITS - 1 - it)
            cnt = count(lambda kt, pos: (kt == v) & (pos < cand))
            return jnp.where(cnt < want, cand, x)
        return lax.fori_loop(0, POS_BITS, body, jnp.zeros(stat_shape, I32))

    def tie_search():
        want = kf - count(lambda kt, pos: kt > v)
        limit = (tie_limit or bit_tie_limit)(v, want)
        return jnp.where(need, limit, INT_MAX)

    any_need = jnp.max(jnp.where(need, 1.0, 0.0)) > 0.0
    jlim = lax.cond(any_need, tie_search, lambda: jnp.full(stat_shape, INT_MAX, I32))
    return v, jlim


WIDE = 4 * LANES
PAIRS = N_HEADS // 2
COUNT_ROWS = 8 * SUBLANES


def _attn_prompt_kernel(far_ref, qidx_ref, kw_ref, q_ref, kidx_ref, k_ref, vt_ref, bd_ref, bs_ref, o_ref,
                        keys_ref, qit_ref, qz_ref, m_ref, l_ref, acc_ref, *, tq, nsel):
    i = pl.program_id(0)
    qpos = i * tq + lax.broadcasted_iota(I32, (1, tq), 1)
    nwide = i // (WIDE // LANES) + 1

    qit_ref[...] = jnp.transpose(qidx_ref[...].astype(F32)).astype(BF16)
    w_t = jnp.transpose(kw_ref[...])[IDX_DIM:IDX_DIM + IDX_HEADS, :] * IDX_SCALE
    q_t = jnp.transpose(q_ref[...].astype(F32) * ATT_SCALE).astype(BF16)
    upper = lax.broadcasted_iota(I32, (LANES, tq), 0) < HEAD_DIM
    zero = jnp.zeros((LANES, tq), BF16)
    for p in range(PAIRS):
        blk = q_t[p * LANES:(p + 1) * LANES, :]
        qz_ref[p, :, 0:tq] = jnp.where(upper, blk, zero)
        qz_ref[p, :, tq:2 * tq] = jnp.where(upper, zero, blk)

    def score_body(c, carry):
        off = pl.multiple_of(c * WIDE, WIDE)
        kt = kidx_ref[pl.ds(off, WIDE), :]
        s = jnp.zeros((WIDE, tq), F32)
        for hh in range(IDX_HEADS):
            sc = _dot(kt, qit_ref[hh * IDX_DIM:(hh + 1) * IDX_DIM, :])
            s = s + jnp.maximum(sc, 0.0) * w_t[hh:hh + 1, :]
        kpos = off + lax.broadcasted_iota(I32, (WIDE, tq), 0)
        keys_ref[pl.ds(off, WIDE), :] = jnp.where(kpos <= qpos, _mono_key(s), INT_MIN)
        return carry

    lax.fori_loop(0, nwide, score_body, 0)

    def count(pred):
        def body(c, acc):
            off = pl.multiple_of(c * WIDE, WIDE)
            kt = keys_ref[pl.ds(off, WIDE), :]
            kpos = off + lax.broadcasted_iota(I32, (WIDE, tq), 0)
            hit = jnp.where(pred(kt, kpos), 1.0, 0.0)
            return acc + jnp.sum(hit.reshape(WIDE // COUNT_ROWS, COUNT_ROWS, tq), axis=0)
        acc = lax.fori_loop(0, nwide, body, jnp.zeros((COUNT_ROWS, tq), F32))
        return jnp.sum(acc, axis=0, keepdims=True)

    def tie_limit(v, want):
        lower = (lax.broadcasted_iota(I32, (WIDE, WIDE), 1) <= lax.broadcasted_iota(I32, (WIDE, WIDE), 0))
        lower = jnp.where(lower, 1.0, 0.0).astype(BF16)

        def body(c, carry):
            base, limit = carry
            off = pl.multiple_of(c * WIDE, WIDE)
            eq = keys_ref[pl.ds(off, WIDE), :] == v
            rank = base + _dot(lower, jnp.where(eq, 1.0, 0.0).astype(BF16))
            kpos = (off + lax.broadcasted_iota(I32, (WIDE, tq), 0)).astype(F32)
            hit = jnp.max(jnp.where(eq & (rank <= want), kpos, -1.0), axis=0, keepdims=True)
            return rank[WIDE - 1:WIDE, :], jnp.maximum(limit, hit)

        zero = jnp.zeros((1, tq), F32)
        return lax.fori_loop(0, nwide, body, (zero, zero - 1.0))[1].astype(I32)

    thr, jlim = _select(count, (1, tq), nsel, True, tie_limit)

    m_ref[...] = jnp.full(m_ref.shape, M_INIT, F32)
    l_ref[...] = jnp.zeros(l_ref.shape, F32)
    acc_ref[...] = jnp.zeros(acc_ref.shape, F32)

    def tile(off, height, bias_ref):
        kt = keys_ref[pl.ds(off, height), :]
        kpos = off + lax.broadcasted_iota(I32, (height, tq), 0)
        sel = ((kt > thr) | ((kt == thr) & (kpos <= jlim))) & (kpos <= qpos)
        amask = jnp.where(sel, 0.0, NEG)
        sts = [_dot(k_ref[pl.ds(off, height), p * LANES:(p + 1) * LANES], qz_ref[p]) for p in range(PAIRS)]
        prs, alphas = [], []
        for h in range(N_HEADS):
            lg = sts[h // 2][:, (h % 2) * tq:(h % 2 + 1) * tq] + amask
            if bias_ref is not None:
                lg = lg + bias_ref[h]
            m_old = m_ref[h]
            m_new = jnp.maximum(m_old, jnp.max(lg, axis=0, keepdims=True))
            alpha = jnp.exp(m_old - m_new)
            pr = jnp.exp(lg - m_new[0:1, :])
            l_ref[h] = alpha * l_ref[h] + jnp.sum(pr, axis=0, keepdims=True)
            m_ref[h] = m_new
            prs.append(pr.astype(BF16))
            alphas.append(alpha[0:1, :])
        for h in range(N_HEADS):
            pv = _dot(vt_ref[h * HEAD_DIM:(h + 1) * HEAD_DIM, pl.ds(off, height)], prs[h])
            acc_ref[h] = alphas[h] * acc_ref[h] + pv

    nfar = jnp.maximum(i - 1, 0)
    nfar_wide = nfar // (WIDE // LANES)

    def far_wide(c, carry):
        tile(pl.multiple_of(c * WIDE, WIDE), WIDE, None)
        return carry

    def far_narrow(j, carry):
        tile(pl.multiple_of(j * LANES, LANES), LANES, None)
        return carry

    lax.fori_loop(0, nfar_wide, far_wide, 0)
    lax.fori_loop(nfar_wide * (WIDE // LANES), nfar, far_narrow, 0)
    for h in range(N_HEADS):
        m_ref[h] = m_ref[h] + far_ref[h]

    @pl.when(i >= 1)
    def _():
        tile(pl.multiple_of((i - 1) * LANES, LANES), LANES, bs_ref)

    tile(pl.multiple_of(i * LANES, LANES), LANES, bd_ref)
    out_t = jnp.concatenate([acc_ref[h] / l_ref[h][0:1, :] for h in range(N_HEADS)], axis=0)
    o_ref[...] = jnp.transpose(out_t)


def _attn_prompt(far, qidx, h2, q, kidx, k, vt, bias_diag_t, bias_sub_t, nsel):
    t = q.shape[0]
    tq = LANES
    whole = lambda i: (0, 0)
    return pl.pallas_call(
        functools.partial(_attn_prompt_kernel, tq=tq, nsel=nsel),
        grid=(t // tq,),
        in_specs=[pl.BlockSpec(memory_space=pltpu.SMEM),
                  pl.BlockSpec((tq, IDX_HEADS * IDX_DIM), lambda i: (i, 0)),
                  pl.BlockSpec((tq, LANES), lambda i: (i, C_KW // LANES)),
                  pl.BlockSpec((tq, MIX_DIM), lambda i: (i, 0)),
                  pl.BlockSpec((t, IDX_DIM), whole),
                  pl.BlockSpec((t, MIX_DIM), whole),
                  pl.BlockSpec((MIX_DIM, t), whole),
                  pl.BlockSpec((N_HEADS, LANES, tq), lambda i: (0, 0, 0)),
                  pl.BlockSpec((N_HEADS, LANES, tq), lambda i: (0, 0, 0))],
        out_specs=pl.BlockSpec((tq, MIX_DIM), lambda i: (i, 0)),
        out_shape=jax.ShapeDtypeStruct((t, MIX_DIM), F32),
        scratch_shapes=[pltpu.VMEM((t, tq), I32),
                        pltpu.VMEM((IDX_HEADS * IDX_DIM, tq), BF16),
                        pltpu.VMEM((PAIRS, LANES, 2 * tq), BF16),
                        pltpu.VMEM((N_HEADS, SUBLANES, tq), F32),
                        pltpu.VMEM((N_HEADS, SUBLANES, tq), F32),
                        pltpu.VMEM((N_HEADS, HEAD_DIM, tq), F32)],
        compiler_params=_cparams(("parallel",)),
        name="attn_prompt",
    )(far, qidx, h2, q, kidx, k, vt, bias_diag_t, bias_sub_t)


QROWS = 8
SEL_PAGES = 8
ATT_PAGES = 8


def _sel_sample_kernel(pt_ref, qst_ref, wcol_ref, *refs, nsel, treal):
    pages = refs[:SEL_PAGES]
    knew_ref, mask_ref, keys_ref = refs[SEL_PAGES:]
    j = pl.program_id(1)
    ntiles = keys_ref.shape[1] // LANES

    def score(kt_t):
        r = jnp.maximum(_dot(qst_ref[0], kt_t), 0.0) * (wcol_ref[0] * IDX_SCALE)
        s = r[0:QROWS]
        for hh in range(1, IDX_HEADS):
            s = s + r[hh * QROWS:(hh + 1) * QROWS]
        return _mono_key(s)

    for g in range(SEL_PAGES):
        off = pl.multiple_of((j * SEL_PAGES + g) * LANES, LANES)
        keys_ref[:, pl.ds(off, LANES)] = score(pages[g][0, 0].astype(BF16))

    @pl.when(j == pl.num_programs(1) - 1)
    def _():
        rowq = lax.broadcasted_iota(I32, (QROWS, LANES), 0)
        lane = lax.broadcasted_iota(I32, (QROWS, LANES), 1)
        knew = jnp.where((lane < treal) & (lane <= rowq), score(knew_ref[0]), INT_MIN)
        keys_ref[:, (ntiles - 1) * LANES:] = knew

        def count(pred):
            accs = [jnp.zeros((QROWS, LANES), F32) for _ in range(4)]
            for jt in range(ntiles):
                kt = keys_ref[:, jt * LANES:(jt + 1) * LANES]
                accs[jt % 4] = accs[jt % 4] + jnp.where(pred(kt, jt * LANES + lane), 1.0, 0.0)
            return jnp.sum((accs[0] + accs[1]) + (accs[2] + accs[3]), axis=1, keepdims=True)

        row_ok = lax.broadcasted_iota(I32, (QROWS, 1), 0) < treal
        thr, jlim = _select(count, (QROWS, 1), nsel, row_ok)

        def body(jt, carry):
            off = pl.multiple_of(jt * LANES, LANES)
            kt = keys_ref[:, pl.ds(off, LANES)]
            sel = (kt > thr) | ((kt == thr) & (off + lane <= jlim))
            mask_ref[0, :, pl.ds(off, LANES)] = jnp.where(sel, 1.0, 0.0)
            return carry

        lax.fori_loop(0, ntiles, body, 0)


def _sel_sample(page_table, qst, wcol, cache_kidx, knew, layer, nsel, treal):
    b, npages = page_table.shape
    nk = (npages + 1) * PAGE
    page_spec = lambda g: pl.BlockSpec(
        (1, 1, IDX_DIM, PAGE), lambda i, j, pt: (layer, pt[i, j * SEL_PAGES + g], 0, 0))
    grid_spec = pltpu.PrefetchScalarGridSpec(
        num_scalar_prefetch=1,
        grid=(b, npages // SEL_PAGES),
        in_specs=[pl.BlockSpec((1, IDX_HEADS * QROWS, IDX_DIM), lambda i, j, pt: (i, 0, 0)),
                  pl.BlockSpec((1, IDX_HEADS * QROWS, 1), lambda i, j, pt: (i, 0, 0))]
                 + [page_spec(g) for g in range(SEL_PAGES)]
                 + [pl.BlockSpec((1, IDX_DIM, PAGE), lambda i, j, pt: (i, 0, 0))],
        out_specs=pl.BlockSpec((1, QROWS, nk), lambda i, j, pt: (i, 0, 0)),
        scratch_shapes=[pltpu.VMEM((QROWS, nk), I32)],
    )
    return pl.pallas_call(
        functools.partial(_sel_sample_kernel, nsel=nsel, treal=treal),
        grid_spec=grid_spec,
        out_shape=jax.ShapeDtypeStruct((b, QROWS, nk), F32),
        compiler_params=_cparams(("parallel", "arbitrary")),
        name="sel_sample",
    )(page_table, qst, wcol, *([cache_kidx] * SEL_PAGES), knew)


def _attn_sample_kernel(pt_ref, far_ref, qh_ref, *refs):
    kp = refs[:ATT_PAGES]
    vp = refs[ATT_PAGES:2 * ATT_PAGES]
    mask_ref, masknew_ref, blast_ref, bnew_ref, knew_ref, vnew_ref, o_ref, m_ref, l_ref, acc_ref = refs[2 * ATT_PAGES:]
    j = pl.program_id(1)
    is_last = j == pl.num_programs(1) - 1

    @pl.when(j == 0)
    def _():
        m_ref[...] = jnp.full(m_ref.shape, M_INIT, F32)
        l_ref[...] = jnp.zeros(l_ref.shape, F32)
        acc_ref[...] = jnp.zeros(acc_ref.shape, F32)

    def update(lgs, vhs):
        prs, alphas = [], []
        for h in range(N_HEADS):
            m_old = m_ref[h]
            m_new = jnp.maximum(m_old, jnp.max(lgs[h], axis=1, keepdims=True))
            alpha = jnp.exp(m_old - m_new)
            pr = jnp.exp(lgs[h] - m_new[:, 0:1])
            l_ref[h] = alpha * l_ref[h] + jnp.sum(pr, axis=1, keepdims=True)
            m_ref[h] = m_new
            prs.append(pr.astype(BF16))
            alphas.append(alpha[:, :HEAD_DIM])
        for h in range(N_HEADS):
            acc_ref[h] = alphas[h] * acc_ref[h] + _dot_nt(prs[h], vhs[h])

    def q_of(h):
        return (qh_ref[0, h].astype(F32) * ATT_SCALE).astype(BF16)

    def head_t(refs_, h):
        return jnp.concatenate([refs_[g][0, 0, h] for g in range(ATT_PAGES)], axis=1).astype(BF16)

    amask = jnp.where(mask_ref[0] > 0.5, 0.0, NEG)
    lgs = []
    for h in range(N_HEADS):
        far = jnp.full((QROWS, LANES), far_ref[h], F32)
        bias = jnp.concatenate([far] * (ATT_PAGES - 1) + [jnp.where(is_last, blast_ref[h], far)], axis=1)
        lgs.append(_dot(q_of(h), head_t(kp, h)) + amask + bias)
    update(lgs, [head_t(vp, h) for h in range(N_HEADS)])

    @pl.when(is_last)
    def _():
        amask_new = jnp.where(masknew_ref[0] > 0.5, 0.0, NEG)
        lgs_new = [_dot(q_of(h), knew_ref[0, h]) + amask_new + bnew_ref[h] for h in range(N_HEADS)]
        update(lgs_new, [vnew_ref[0, h] for h in range(N_HEADS)])
        outs = []
        for h in range(N_HEADS):
            lh = l_ref[h][:, :HEAD_DIM]
            outs.append(acc_ref[h] / jnp.where(lh > 0.0, lh, 1.0))
        o_ref[0] = jnp.concatenate(outs, axis=1)


def _attn_sample(page_table, far, qh, cache_k2, cache_v2, mask, bias_last, bias_new, knew, vnew, layer):
    b, npages = page_table.shape
    nsteps = npages // ATT_PAGES
    page_spec = lambda g: pl.BlockSpec(
        (1, 1, N_HEADS, HEAD_DIM, PAGE), lambda i, j, pt, far: (layer, pt[i, j * ATT_PAGES + g], 0, 0, 0))
    const3 = lambda i, j, pt, far: (0, 0, 0)
    per_b4 = lambda i, j, pt, far: (i, 0, 0, 0)
    grid_spec = pltpu.PrefetchScalarGridSpec(
        num_scalar_prefetch=2,
        grid=(b, nsteps),
        in_specs=[pl.BlockSpec((1, N_HEADS, QROWS, HEAD_DIM), per_b4)]
                 + [page_spec(g) for g in range(ATT_PAGES)] * 2
                 + [pl.BlockSpec((1, QROWS, ATT_PAGES * PAGE), lambda i, j, pt, far: (i, 0, j)),
                    pl.BlockSpec((1, QROWS, PAGE), lambda i, j, pt, far: (i, 0, npages)),
                    pl.BlockSpec((N_HEADS, QROWS, PAGE), const3),
                    pl.BlockSpec((N_HEADS, QROWS, PAGE), const3),
                    pl.BlockSpec((1, N_HEADS, HEAD_DIM, PAGE), per_b4),
                    pl.BlockSpec((1, N_HEADS, HEAD_DIM, PAGE), per_b4)],
        out_specs=pl.BlockSpec((1, QROWS, MIX_DIM), lambda i, j, pt, far: (i, 0, 0)),
        scratch_shapes=[pltpu.VMEM((N_HEADS, QROWS, LANES), F32),
                        pltpu.VMEM((N_HEADS, QROWS, LANES), F32),
                        pltpu.VMEM((N_HEADS, QROWS, HEAD_DIM), F32)],
    )
    return pl.pallas_call(
        _attn_sample_kernel,
        grid_spec=grid_spec,
        out_shape=jax.ShapeDtypeStruct((b, QROWS, MIX_DIM), F32),
        compiler_params=_cparams(("parallel", "arbitrary")),
        name="attn_sample",
    )(page_table, far, qh, *([cache_k2] * ATT_PAGES), *([cache_v2] * ATT_PAGES), mask, mask,
      bias_last, bias_new, knew, vnew)


def _bucket_table(n):
    d = np.arange(n)
    nf = np.maximum(d, 1).astype(np.float32)
    half = REL_BUCKETS // 2
    large = half + (np.log(nf / np.float32(half)) / np.float32(math.log(REL_MAX_DIST / half))
                    * np.float32(REL_BUCKETS - half)).astype(np.int32)
    return np.where(d < half, d, np.minimum(large, REL_BUCKETS - 1)).astype(np.int32)


def _pack_w_in(w):
    w_t = jnp.transpose(w, (0, 2, 1))
    p, z, xbc, dt, q, k, v, qi, ki, wi, u, vs = jnp.split(w_t, np.cumsum(IN_SPLITS)[:-1].tolist(), axis=1)
    zeros = lambda n: jnp.zeros((w.shape[0], n, w.shape[1]), w.dtype)
    packed = jnp.concatenate(
        [p, z, xbc, q, k, v, u, vs, qi, ki, wi, zeros(LANES - IDX_DIM - IDX_HEADS), dt, zeros(LANES - N_HEADS)],
        axis=1)
    return packed.astype(BF16)


def _pad_lanes(x):
    return jnp.pad(x.reshape(1, -1), ((0, 0), (0, LANES - x.shape[-1])))


def _pad_axis(x, axis, size):
    pads = [(0, 0)] * x.ndim
    pads[axis] = (0, size - x.shape[axis])
    return jnp.pad(x, pads)


def kernel(x_prompt, x_sample, state_pool, state_conv, state_ssm, cache_k, cache_v, cache_kidx, page_table,
           rel_bias, norm_mix, w_in, pool_w, pool_scale, conv_w, conv_b, dt_bias, a_log, d_skip, ssm_norm,
           sgu_w, sgu_b, w_out, norm_mlp, mlp_w1, mlp_w2, norm_final):
    depth = w_in.shape[0]
    bp, t, _ = x_prompt.shape
    bs, ts, _ = x_sample.shape
    npages = page_table.shape[1]
    past = npages * PAGE
    n_pool = cache_k.shape[1]
    assert bp == 1 and t % WIDE == 0 and npages % SEL_PAGES == 0 and npages % ATT_PAGES == 0
    assert max(t, past + PAGE) <= 2 ** POS_BITS and ts <= QROWS
    tc = LANES
    nsel_p = min(TOPK_MAX, t // 4)
    nsel_s = min(TOPK_MAX, (past + ts) // 4)

    bucket = _bucket_table(2 * LANES)
    far = rel_bias[REL_BUCKETS - 1]
    kk = np.arange(LANES)[:, None]
    qq = np.arange(LANES)[None, :]
    def bias_of(dist):
        onehot = np.eye(REL_BUCKETS, dtype=np.float32)[bucket[dist].reshape(-1)]
        vals = jnp.dot(onehot, rel_bias, precision=lax.Precision.HIGHEST)
        return jnp.transpose(vals).reshape((N_HEADS,) + dist.shape)

    bias_diag_t = bias_of(np.maximum(qq - kk, 0))
    bias_sub_t = bias_of(np.minimum(LANES + qq - kk, 2 * LANES - 1))
    sq = np.arange(QROWS)[:, None]
    sk = np.arange(PAGE)[None, :]
    bias_last = bias_of(np.minimum(PAGE + sq - sk, 2 * LANES - 1))
    bias_new = bias_of(np.maximum(sq - sk, 0))

    cache_k2 = jnp.transpose(cache_k, (0, 1, 3, 4, 2))
    cache_v2 = jnp.transpose(cache_v, (0, 1, 3, 4, 2))
    cache_kidx2 = jnp.transpose(cache_kidx, (0, 1, 3, 2))
    w_in_t = _pack_w_in(w_in)
    w_out_b, w1_b, w2_b = w_out.astype(BF16), mlp_w1.astype(BF16), mlp_w2.astype(BF16)

    xp = x_prompt.reshape(t, D_MODEL)
    xs = x_sample.reshape(bs * ts, D_MODEL)
    zero_pool = jnp.zeros((1, POOL_HIST, MIX_DIM), F32)
    zero_conv = jnp.zeros((1, CONV_W - 1, XBC_DIM), F32)
    zero_ssm = jnp.zeros((1, N_HEADS, HEAD_DIM, SSM_STATE), F32)
    tm_p = min(512, t)
    tm_s = bs * ts
    outs_p, outs_s = [], []

    for l in range(depth):
        g_mix = norm_mix[l].reshape(1, D_MODEL)
        g_mlp = norm_mlp[l].reshape(1, D_MODEL)
        g_out = norm_final.reshape(1, D_MODEL) if l == depth - 1 else g_mlp
        pscale = pool_scale[l].reshape(1, MIX_DIM)
        cb = conv_b[l].reshape(1, XBC_DIM)
        dtb, alog, dsk = _pad_lanes(dt_bias[l]), _pad_lanes(a_log[l]), _pad_lanes(d_skip[l])
        snorm = ssm_norm[l].reshape(1, MIX_DIM)
        sgu_b_col = sgu_b[l][:, :, None]
        final = l == depth - 1

        hp = _inproj(xp, g_mix, w_in_t, l, tm=min(1024, t))
        hp3 = hp.reshape(1, t, H_PACKED)
        y_pool, new_pool = _pool(hp3, zero_pool, pool_w[l], pscale, tc=tc, treal=tc, pos0=0)
        y_ssm, new_conv, new_h = _ssd(hp3, zero_conv, zero_ssm, conv_w[l], cb, dtb, alog, dsk, snorm,
                                      tc=tc, treal=tc)
        y_sgu, _ = _sgu(hp3, sgu_w[l], sgu_b_col, tc=tc, treal=tc)
        kf = hp[:, C_K:C_K + MIX_DIM]
        vf = hp[:, C_V:C_V + MIX_DIM]
        kif = hp[:, C_KW:C_KW + IDX_DIM]
        y_att = _attn_prompt(far, hp[:, C_QIDX:C_QIDX + IDX_HEADS * IDX_DIM].astype(BF16), hp,
                             hp[:, C_Q:C_Q + MIX_DIM].astype(BF16), kif.astype(BF16), kf.astype(BF16),
                             jnp.transpose(vf.astype(BF16)), bias_diag_t, bias_sub_t, nsel_p)
        x1, xn = _outproj(xp, [y_pool[0], y_ssm[0], y_att, y_sgu[0]], w_out_b, l, g_mlp, tm=256)
        xp = _mlp(xn, w1_b, w2_b, l, x1, g_out, tm=tm_p, final_norm=final)
        outs_p.append((new_pool, new_conv, new_h, kf.reshape(1, t, N_HEADS, HEAD_DIM),
                       vf.reshape(1, t, N_HEADS, HEAD_DIM), kif.reshape(1, t, IDX_DIM)))

        hs = _inproj(xs, g_mix, w_in_t, l, tm=tm_s)
        hs3 = hs.reshape(bs, ts, H_PACKED)
        y_pool, new_pool = _pool(hs3, state_pool[l], pool_w[l], pscale, tc=tc, treal=ts, pos0=past)
        y_ssm, new_conv, new_h = _ssd(hs3, state_conv[l], state_ssm[l], conv_w[l], cb, dtb, alog, dsk, snorm,
                                      tc=tc, treal=ts)
        y_sgu, v_rows = _sgu(hs3, sgu_w[l], sgu_b_col, tc=tc, treal=ts)
        kf = hs3[:, :, C_K:C_K + MIX_DIM]
        vf = hs3[:, :, C_V:C_V + MIX_DIM]
        kif = hs3[:, :, C_KW:C_KW + IDX_DIM]
        qi = hs3[:, :, C_QIDX:C_QIDX + IDX_HEADS * IDX_DIM].reshape(bs, ts, IDX_HEADS, IDX_DIM)
        qst = _pad_axis(jnp.transpose(qi, (0, 2, 1, 3)), 2, QROWS)
        qst = qst.reshape(bs, IDX_HEADS * QROWS, IDX_DIM).astype(BF16)
        wi = hs3[:, :, C_KW + IDX_DIM:C_KW + IDX_DIM + IDX_HEADS]
        wcol = _pad_axis(jnp.transpose(wi, (0, 2, 1)), 2, QROWS).reshape(bs, IDX_HEADS * QROWS, 1)
        kinew_t = _pad_axis(jnp.transpose(kif, (0, 2, 1)), 2, PAGE).astype(BF16)
        mask = _sel_sample(page_table, qst, wcol, cache_kidx2, kinew_t, l, nsel_s, ts)
        heads = lambda x, perm: jnp.transpose(x.reshape(bs, ts, N_HEADS, HEAD_DIM), perm)
        qh = _pad_axis(heads(hs3[:, :, C_Q:C_Q + MIX_DIM], (0, 2, 1, 3)), 2, QROWS).astype(BF16)
        knew = _pad_axis(heads(kf, (0, 2, 3, 1)), 3, PAGE).astype(BF16)
        vnew = _pad_axis(heads(vf, (0, 2, 3, 1)), 3, PAGE).astype(BF16)
        att = _attn_sample(page_table, far, qh, cache_k2, cache_v2, mask, bias_last, bias_new, knew, vnew, l)
        y_att = att[:, :ts].reshape(bs * ts, MIX_DIM)
        flat = lambda y: y.reshape(bs * ts, MIX_DIM)
        x1, xn = _outproj(xs, [flat(y_pool), flat(y_ssm), y_att, flat(y_sgu)], w_out_b, l, g_mlp, tm=tm_s)
        xs = _mlp(xn, w1_b, w2_b, l, x1, g_out, tm=tm_s, final_norm=final)
        outs_s.append((new_pool, new_conv, new_h, kf.reshape(bs, ts, N_HEADS, HEAD_DIM),
                       vf.reshape(bs, ts, N_HEADS, HEAD_DIM), kif, v_rows))

    stack = lambda outs, i: jnp.stack([o[i] for o in outs])
    return (xp.reshape(1, t, D_MODEL), xs.reshape(bs, ts, D_MODEL),
            *[stack(outs_p, i) for i in range(6)],
            *[stack(outs_s, i) for i in range(7)])
```

```python
import functools
import math

import numpy as np
import jax
import jax.numpy as jnp
from jax import lax
from jax.experimental import pallas as pl
from jax.experimental.pallas import tpu as pltpu

F32 = jnp.float32
BF16 = jnp.bfloat16
I32 = jnp.int32

LANES = 128
SUBLANES = 8
D_MODEL = 2048
N_HEADS = 8
HEAD_DIM = 64
MIX_DIM = 512
POOL_WINDOWS = (2, 4, 8, 16)
POOL_HIST = 15
CONV_W = 4
SSM_GROUPS = 2
SSM_STATE = 128
XBC_DIM = MIX_DIM + 2 * SSM_GROUPS * SSM_STATE
IDX_HEADS = 4
IDX_DIM = 64
TOPK_MAX = 256
PAGE = 128
REL_BUCKETS = 32
REL_MAX_DIST = 128
ATT_SCALE = HEAD_DIM ** -0.5
IDX_SCALE = (IDX_HEADS * IDX_DIM) ** -0.5
SGU_GROUPS = 4
D_FF = 4 * D_MODEL
EPS = 1e-6
NEG = -1e30
INT_MIN = -(2 ** 31)
INT_MAX = 2 ** 31 - 1
IN_SPLITS = (512, 512, 1024, 8, 512, 512, 512, 256, 64, 4, 512, 512)

C_P, C_Z, C_XBC, C_Q, C_K, C_V, C_U, C_VS, C_QIDX, C_KW, C_DT = (
    0, 512, 1024, 2048, 2560, 3072, 3584, 4096, 4608, 4864, 4992)
H_PACKED = 5120
VMEM_LIMIT = 56 * 1024 * 1024


def _cparams(sem):
    return pltpu.CompilerParams(dimension_semantics=sem, vmem_limit_bytes=VMEM_LIMIT)


def _dot(a, b):
    return jnp.dot(a, b, preferred_element_type=F32)


def _dot_nt(a, b):
    return lax.dot_general(a, b, (((1,), (1,)), ((), ())), preferred_element_type=F32)


def _dot_tn(a, b):
    return lax.dot_general(a, b, (((0,), (0,)), ((), ())), preferred_element_type=F32)


def _silu(x):
    return x * (1.0 / (1.0 + jnp.exp(-x)))


def _gelu_tanh(x):
    return 0.5 * x * (1.0 + jnp.tanh(math.sqrt(2.0 / math.pi) * (x + 0.044715 * (x * x * x))))


def _softplus(x):
    return jnp.maximum(x, 0.0) + jnp.log1p(jnp.exp(-jnp.abs(x)))


def _rms(x, g):
    ms = jnp.mean(x * x, axis=-1, keepdims=True)
    return x * lax.rsqrt(ms + EPS) * g


def _inproj_kernel(x_ref, g_ref, w_ref, o_ref, *rest, attn_operands, tn):
    xn_ref = rest[-1]

    @pl.when(pl.program_id(1) == 0)
    def _():
        xn_ref[...] = _rms(x_ref[...], g_ref[...]).astype(BF16)

    h = _dot_nt(xn_ref[...], w_ref[0])
    o_ref[...] = h
    if attn_operands:
        ob_ref, vt_ref = rest[:2]
        ob_ref[...] = h.astype(BF16)

        @pl.when(pl.program_id(1) == C_V // tn)
        def _():
            vt_ref[...] = jnp.transpose(h[:, C_V % tn:C_V % tn + MIX_DIM]).astype(BF16)


def _inproj(x, g, w_t, layer, tm, tn=512, attn_operands=False):
    m = x.shape[0]
    n = w_t.shape[1]
    out_specs = [pl.BlockSpec((tm, tn), lambda i, j: (i, j))]
    out_shape = [jax.ShapeDtypeStruct((m, n), F32)]
    if attn_operands:
        out_specs += [pl.BlockSpec((tm, tn), lambda i, j: (i, j)), pl.BlockSpec((MIX_DIM, tm), lambda i, j: (0, i))]
        out_shape += [jax.ShapeDtypeStruct((m, n), BF16), jax.ShapeDtypeStruct((MIX_DIM, m), BF16)]
    return pl.pallas_call(
        functools.partial(_inproj_kernel, attn_operands=attn_operands, tn=tn),
        grid=(m // tm, n // tn),
        in_specs=[pl.BlockSpec((tm, D_MODEL), lambda i, j: (i, 0)),
                  pl.BlockSpec((1, D_MODEL), lambda i, j: (0, 0)),
                  pl.BlockSpec((1, tn, D_MODEL), lambda i, j: (layer, j, 0))],
        out_specs=out_specs,
        out_shape=out_shape,
        scratch_shapes=[pltpu.VMEM((tm, D_MODEL), BF16)],
        compiler_params=_cparams(("parallel", "arbitrary")),
        name="inproj",
    )(x, g, w_t)


def _outproj_kernel(x_ref, y0_ref, y1_ref, y2_ref, y3_ref, w_ref, g_ref, x1_ref, xn_ref):
    y = jnp.concatenate([y0_ref[...], y1_ref[...], y2_ref[...], y3_ref[...]], axis=1).astype(BF16)
    x1 = x_ref[...] + _dot(y, w_ref[0])
    x1_ref[...] = x1
    xn_ref[...] = _rms(x1, g_ref[...]).astype(BF16)


def _outproj(x, ys, w, layer, g, tm):
    m = x.shape[0]
    row = lambda i: (i, 0)
    return pl.pallas_call(
        _outproj_kernel,
        grid=(m // tm,),
        in_specs=[pl.BlockSpec((tm, D_MODEL), row)]
                 + [pl.BlockSpec((tm, MIX_DIM), row)] * 4
                 + [pl.BlockSpec((1, D_MODEL, D_MODEL), lambda i: (layer, 0, 0)),
                    pl.BlockSpec((1, D_MODEL), lambda i: (0, 0))],
        out_specs=[pl.BlockSpec((tm, D_MODEL), row), pl.BlockSpec((tm, D_MODEL), row)],
        out_shape=[jax.ShapeDtypeStruct((m, D_MODEL), F32), jax.ShapeDtypeStruct((m, D_MODEL), BF16)],
        compiler_params=_cparams(("parallel",)),
        name="outproj",
    )(x, *ys, w, g)


def _mlp_kernel(xn_ref, w1_ref, w2_ref, x1_ref, g_ref, o_ref, *, final_norm):
    j = pl.program_id(1)

    @pl.when(j == 0)
    def _():
        o_ref[...] = x1_ref[...]

    hid = _dot(xn_ref[...], w1_ref[0])
    hid = jnp.square(jnp.maximum(hid, 0.0)).astype(BF16)
    o_ref[...] += _dot(hid, w2_ref[0])

    if final_norm:
        @pl.when(j == pl.num_programs(1) - 1)
        def _():
            o_ref[...] = _rms(o_ref[...], g_ref[...])


def _mlp(xn, w1, w2, layer, x1, g, tm, final_norm, tf=512):
    m = xn.shape[0]
    return pl.pallas_call(
        functools.partial(_mlp_kernel, final_norm=final_norm),
        grid=(m // tm, D_FF // tf),
        in_specs=[pl.BlockSpec((tm, D_MODEL), lambda i, j: (i, 0)),
                  pl.BlockSpec((1, D_MODEL, tf), lambda i, j: (layer, 0, j)),
                  pl.BlockSpec((1, tf, D_MODEL), lambda i, j: (layer, j, 0)),
                  pl.BlockSpec((tm, D_MODEL), lambda i, j: (i, 0)),
                  pl.BlockSpec((1, D_MODEL), lambda i, j: (0, 0))],
        out_specs=pl.BlockSpec((tm, D_MODEL), lambda i, j: (i, 0)),
        out_shape=jax.ShapeDtypeStruct((m, D_MODEL), F32),
        compiler_params=_cparams(("parallel", "arbitrary")),
        name="mlp",
    )(xn, w1, w2, x1, g)


def _pool_kernel(p_ref, hist_ref, w_ref, scale_ref, y_ref, newhist_ref, ext_ref, *, tc, treal, pos0):
    c = pl.program_id(1)
    hrows = POOL_HIST + 1

    @pl.when(c == 0)
    def _():
        ext_ref[0:1, :] = jnp.zeros((1, MIX_DIM), F32)
        ext_ref[1:hrows, :] = hist_ref[0]

    if treal < tc:
        ext_ref[hrows:hrows + tc, :] = jnp.zeros((tc, MIX_DIM), F32)
    ext_ref[hrows:hrows + treal, :] = p_ref[0]

    pos = pos0 + c * treal + lax.broadcasted_iota(I32, (tc, 1), 0)
    for g, win in enumerate(POOL_WINDOWS):
        cols = slice(g * LANES, (g + 1) * LANES)
        cur = ext_ref[hrows:hrows + tc, cols]
        s = cur
        for j in range(1, win):
            s = s + ext_ref[hrows - j:hrows - j + tc, cols]
        cnt = jnp.minimum(pos + 1, win).astype(F32)
        diff = s / cnt - cur
        y = _dot(diff, w_ref[g]) * scale_ref[:, cols]
        y_ref[0, :, cols] = y[:treal]

    newhist_ref[0] = ext_ref[treal + 1:treal + hrows, :]
    ext_ref[0:hrows, :] = ext_ref[treal:treal + hrows, :]


def _pool(h3, hist, pool_w, pool_scale, *, tc, treal, pos0):
    b, t, _ = h3.shape
    return pl.pallas_call(
        functools.partial(_pool_kernel, tc=tc, treal=treal, pos0=pos0),
        grid=(b, t // treal),
        in_specs=[pl.BlockSpec((1, treal, MIX_DIM), lambda i, c: (i, c, C_P // MIX_DIM)),
                  pl.BlockSpec((1, POOL_HIST, MIX_DIM), lambda i, c: (i, 0, 0)),
                  pl.BlockSpec((4, LANES, LANES), lambda i, c: (0, 0, 0)),
                  pl.BlockSpec((1, MIX_DIM), lambda i, c: (0, 0))],
        out_specs=[pl.BlockSpec((1, treal, MIX_DIM), lambda i, c: (i, c, 0)),
                   pl.BlockSpec((1, POOL_HIST, MIX_DIM), lambda i, c: (i, 0, 0))],
        out_shape=[jax.ShapeDtypeStruct((b, t, MIX_DIM), F32),
                   jax.ShapeDtypeStruct((b, POOL_HIST, MIX_DIM), F32)],
        scratch_shapes=[pltpu.VMEM((POOL_HIST + 1 + tc, MIX_DIM), F32)],
        compiler_params=_cparams(("parallel", "arbitrary")),
        name="pool",
    )(h3, hist, pool_w, pool_scale)


def _sgu_kernel(u_ref, v_ref, w_ref, b_ref, y_ref, vout_ref, ubuf_ref, vbuf_ref, *, tc, treal):
    if treal < tc:
        ubuf_ref[...] = jnp.zeros((tc, MIX_DIM), F32)
        vbuf_ref[...] = jnp.zeros((tc, MIX_DIM), F32)
    ubuf_ref[0:treal, :] = u_ref[0]
    vbuf_ref[0:treal, :] = v_ref[0]
    u = _gelu_tanh(ubuf_ref[...])
    v = _gelu_tanh(vbuf_ref[...])
    vout_ref[0] = v[:treal]
    r = lax.broadcasted_iota(I32, (tc, tc), 0)
    s = lax.broadcasted_iota(I32, (tc, tc), 1)
    for g in range(SGU_GROUPS):
        cols = slice(g * LANES, (g + 1) * LANES)
        w = jnp.where(s <= r, w_ref[g], 0.0)
        mixed = _dot(w, v[:, cols]) + b_ref[g]
        y_ref[0, :, cols] = (u[:, cols] * mixed)[:treal]


def _sgu(h3, sgu_w, sgu_b_col, *, tc, treal):
    b, t, _ = h3.shape
    return pl.pallas_call(
        functools.partial(_sgu_kernel, tc=tc, treal=treal),
        grid=(b, t // treal),
        in_specs=[pl.BlockSpec((1, treal, MIX_DIM), lambda i, c: (i, c, C_U // MIX_DIM)),
                  pl.BlockSpec((1, treal, MIX_DIM), lambda i, c: (i, c, C_VS // MIX_DIM)),
                  pl.BlockSpec((SGU_GROUPS, tc, tc), lambda i, c: (0, 0, 0)),
                  pl.BlockSpec((SGU_GROUPS, tc, 1), lambda i, c: (0, 0, 0))],
        out_specs=[pl.BlockSpec((1, treal, MIX_DIM), lambda i, c: (i, c, 0)),
                   pl.BlockSpec((1, treal, MIX_DIM), lambda i, c: (i, c, 0))],
        out_shape=[jax.ShapeDtypeStruct((b, t, MIX_DIM), F32),
                   jax.ShapeDtypeStruct((b, t, MIX_DIM), F32)],
        scratch_shapes=[pltpu.VMEM((tc, MIX_DIM), F32), pltpu.VMEM((tc, MIX_DIM), F32)],
        compiler_params=_cparams(("parallel", "parallel")),
        name="sgu",
    )(h3, h3, sgu_w, sgu_b_col)


def _cumsum_rows(a, n):
    row = lax.broadcasted_iota(I32, a.shape, 0)
    sh = 1
    while sh < n:
        a = a + jnp.where(row >= sh, pltpu.roll(a, sh, 0), 0.0)
        sh *= 2
    return a


def _ssd_kernel(z_ref, xbc_ref, dt_ref, hc_ref, h0_ref, convw_ref, convb_ref, dtb_ref, alog_ref, dskip_ref,
                norm_ref, y_ref, newconv_ref, newh_ref, ext_ref, zbuf_ref, dtbuf_ref, state_ref, *, tc, treal):
    c = pl.program_id(1)
    pre = SUBLANES
    hist = CONV_W - 1

    @pl.when(c == 0)
    def _():
        ext_ref[0:pre - hist, :] = jnp.zeros((pre - hist, XBC_DIM), F32)
        ext_ref[pre - hist:pre, :] = hc_ref[0]
        state_ref[...] = h0_ref[...]

    if treal < tc:
        ext_ref[pre:pre + tc, :] = jnp.zeros((tc, XBC_DIM), F32)
        zbuf_ref[...] = jnp.zeros((tc, MIX_DIM), F32)
        dtbuf_ref[...] = jnp.zeros((tc, LANES), F32)
    ext_ref[pre:pre + treal, :] = xbc_ref[0]
    zbuf_ref[0:treal, :] = z_ref[0]
    dtbuf_ref[0:treal, :] = dt_ref[0]

    conv = convb_ref[...]
    for j in range(CONV_W):
        conv = conv + ext_ref[pre - hist + j:pre - hist + j + tc, :] * convw_ref[j:j + 1, :]
    conv = _silu(conv)
    xs = conv[:, :MIX_DIM]
    bmat = conv[:, MIX_DIM:MIX_DIM + SSM_GROUPS * SSM_STATE]
    cmat = conv[:, MIX_DIM + SSM_GROUPS * SSM_STATE:]

    rowi = lax.broadcasted_iota(I32, (tc, LANES), 0)
    dt = _softplus(dtbuf_ref[...] + dtb_ref[...])
    if treal < tc:
        dt = jnp.where(rowi < treal, dt, 0.0)
    a = dt * (-jnp.exp(alog_ref[...]))
    acum = _cumsum_rows(a, tc)
    acum_t = jnp.transpose(acum)
    total = acum[tc - 1:tc, :]
    causal = lax.broadcasted_iota(I32, (tc, tc), 1) <= lax.broadcasted_iota(I32, (tc, tc), 0)

    ys = []
    heads_per_group = N_HEADS // SSM_GROUPS
    for g in range(SSM_GROUPS):
        bg = bmat[:, g * SSM_STATE:(g + 1) * SSM_STATE]
        cg = cmat[:, g * SSM_STATE:(g + 1) * SSM_STATE]
        cb = _dot_nt(cg, bg)
        for hh in range(heads_per_group):
            h = g * heads_per_group + hh
            col = acum[:, h:h + 1]
            row = acum_t[h:h + 1, :]
            decay = jnp.exp(jnp.where(causal, col - row, NEG))
            xh = xs[:, h * HEAD_DIM:(h + 1) * HEAD_DIM]
            xdt = xh * dt[:, h:h + 1]
            hprev = state_ref[0, h]
            y = _dot(cb * decay, xdt)
            y = y + _dot_nt(cg, hprev) * jnp.exp(col)
            y = y + dskip_ref[:, h:h + 1] * xh
            ys.append(y)
            tot = total[:, h:h + 1]
            st = _dot_tn(xdt * jnp.exp(tot - col), bg)
            state_ref[0, h] = jnp.exp(tot) * hprev + st
    y = jnp.concatenate(ys, axis=1) * _silu(zbuf_ref[...])
    gw = MIX_DIM // SSM_GROUPS
    outs = []
    for g in range(SSM_GROUPS):
        yg = y[:, g * gw:(g + 1) * gw]
        ms = jnp.mean(yg * yg, axis=-1, keepdims=True)
        outs.append(yg * lax.rsqrt(ms + EPS) * norm_ref[:, g * gw:(g + 1) * gw])
    y_ref[0] = jnp.concatenate(outs, axis=1)[:treal]

    newconv_ref[0] = ext_ref[pre + treal - hist:pre + treal, :]
    ext_ref[0:pre, :] = ext_ref[treal:treal + pre, :]
    newh_ref[...] = state_ref[...]


def _ssd(h3, hist_conv, h0, conv_w, conv_b, dtb, alog, dskip, norm, *, tc, treal):
    b, t, _ = h3.shape
    full2 = lambda i, c: (0, 0)
    return pl.pallas_call(
        functools.partial(_ssd_kernel, tc=tc, treal=treal),
        grid=(b, t // treal),
        in_specs=[pl.BlockSpec((1, treal, MIX_DIM), lambda i, c: (i, c, C_Z // MIX_DIM)),
                  pl.BlockSpec((1, treal, XBC_DIM), lambda i, c: (i, c, C_XBC // XBC_DIM)),
                  pl.BlockSpec((1, treal, LANES), lambda i, c: (i, c, C_DT // LANES)),
                  pl.BlockSpec((1, CONV_W - 1, XBC_DIM), lambda i, c: (i, 0, 0)),
                  pl.BlockSpec((1, N_HEADS, HEAD_DIM, SSM_STATE), lambda i, c: (i, 0, 0, 0)),
                  pl.BlockSpec((CONV_W, XBC_DIM), full2),
                  pl.BlockSpec((1, XBC_DIM), full2),
                  pl.BlockSpec((1, LANES), full2),
                  pl.BlockSpec((1, LANES), full2),
                  pl.BlockSpec((1, LANES), full2),
                  pl.BlockSpec((1, MIX_DIM), full2)],
        out_specs=[pl.BlockSpec((1, treal, MIX_DIM), lambda i, c: (i, c, 0)),
                   pl.BlockSpec((1, CONV_W - 1, XBC_DIM), lambda i, c: (i, 0, 0)),
                   pl.BlockSpec((1, N_HEADS, HEAD_DIM, SSM_STATE), lambda i, c: (i, 0, 0, 0))],
        out_shape=[jax.ShapeDtypeStruct((b, t, MIX_DIM), F32),
                   jax.ShapeDtypeStruct((b, CONV_W - 1, XBC_DIM), F32),
                   jax.ShapeDtypeStruct((b, N_HEADS, HEAD_DIM, SSM_STATE), F32)],
        scratch_shapes=[pltpu.VMEM((SUBLANES + tc, XBC_DIM), F32),
                        pltpu.VMEM((tc, MIX_DIM), F32),
                        pltpu.VMEM((tc, LANES), F32),
                        pltpu.VMEM((1, N_HEADS, HEAD_DIM, SSM_STATE), F32)],
        compiler_params=_cparams(("parallel", "arbitrary")),
        name="ssd",
    )(h3, h3, h3, hist_conv, h0, conv_w, conv_b, dtb, alog, dskip, norm)


M_INIT = 0.5 * NEG
POS_BITS = 14
assert math.frexp(ATT_SCALE)[0] == 0.5, "q is pre-scaled in bf16, exact only for a power-of-two scale"


def _mono_key(s):
    s = jnp.where(s == 0.0, 0.0, s)
    b = lax.bitcast_convert_type(s, I32)
    return b ^ ((b >> 31) & INT_MAX)


def _select(count, stat_shape, nsel, query_ok, tie_limit=None):
    kf = float(nsel)
    c0 = count(lambda kt, pos: kt >= 0)
    nonneg = c0 >= kf
    prefix = jnp.where(nonneg, 0, INT_MIN).astype(I32)

    def bit_body(it, carry):
        prefix, cge = carry
        cand = prefix | jnp.left_shift(jnp.int32(1), 30 - it)
        cnt = count(lambda kt, pos: kt >= cand)
        take = cnt >= kf
        return jnp.where(take, cand, prefix), jnp.where(take, cnt, cge)

    v, cge = lax.fori_loop(0, 31, bit_body, (prefix, jnp.where(nonneg, c0, kf)))
    need = (cge > kf) & (v != INT_MIN) & query_ok

    def bit_tie_limit(v, want):
        def body(it, x):
            cand = x | jnp.left_shift(jnp.int32(1), POS_BITS - 1 - it)
            cnt = count(lambda kt, pos: (kt == v) & (pos < cand))
            return jnp.where(cnt < want, cand, x)
        return lax.fori_loop(0, POS_BITS, body, jnp.zeros(stat_shape, I32))

    def tie_search():
        want = kf - count(lambda kt, pos: kt > v)
        limit = (tie_limit or bit_tie_limit)(v, want)
        return jnp.where(need, limit, INT_MAX)

    any_need = jnp.max(jnp.where(need, 1.0, 0.0)) > 0.0
    jlim = lax.cond(any_need, tie_search, lambda: jnp.full(stat_shape, INT_MAX, I32))
    return v, jlim


WIDE = 4 * LANES
PAIRS = N_HEADS // 2
COUNT_ROWS = 8 * SUBLANES


def _attn_prompt_kernel(far_ref, qidx_ref, kw_ref, q_ref, kidx_ref, k_ref, vt_ref, bd_ref, bs_ref, o_ref,
                        keys_ref, qit_ref, qz_ref, m_ref, l_ref, acc_ref, *, tq, nsel):
    i = pl.program_id(0)
    qpos = i * tq + lax.broadcasted_iota(I32, (1, tq), 1)
    nwide = i // (WIDE // LANES) + 1

    qit_ref[...] = jnp.transpose(qidx_ref[...].astype(F32)).astype(BF16)
    w_t = jnp.transpose(kw_ref[...])[IDX_DIM:IDX_DIM + IDX_HEADS, :] * IDX_SCALE
    q_t = jnp.transpose(q_ref[...].astype(F32) * ATT_SCALE).astype(BF16)
    upper = lax.broadcasted_iota(I32, (LANES, tq), 0) < HEAD_DIM
    zero = jnp.zeros((LANES, tq), BF16)
    for p in range(PAIRS):
        blk = q_t[p * LANES:(p + 1) * LANES, :]
        qz_ref[p, :, 0:tq] = jnp.where(upper, blk, zero)
        qz_ref[p, :, tq:2 * tq] = jnp.where(upper, zero, blk)

    def score_body(c, carry):
        off = pl.multiple_of(c * WIDE, WIDE)
        kt = kidx_ref[pl.ds(off, WIDE), 0:IDX_DIM]
        s = jnp.zeros((WIDE, tq), F32)
        for hh in range(IDX_HEADS):
            sc = _dot(kt, qit_ref[hh * IDX_DIM:(hh + 1) * IDX_DIM, :])
            s = s + jnp.maximum(sc, 0.0) * w_t[hh:hh + 1, :]
        kpos = off + lax.broadcasted_iota(I32, (WIDE, tq), 0)
        keys_ref[pl.ds(off, WIDE), :] = jnp.where(kpos <= qpos, _mono_key(s), INT_MIN)
        return carry

    lax.fori_loop(0, nwide, score_body, 0)

    def count(pred):
        def body(c, acc):
            off = pl.multiple_of(c * WIDE, WIDE)
            kt = keys_ref[pl.ds(off, WIDE), :]
            kpos = off + lax.broadcasted_iota(I32, (WIDE, tq), 0)
            hit = jnp.where(pred(kt, kpos), 1.0, 0.0)
            return acc + jnp.sum(hit.reshape(WIDE // COUNT_ROWS, COUNT_ROWS, tq), axis=0)
        acc = lax.fori_loop(0, nwide, body, jnp.zeros((COUNT_ROWS, tq), F32))
        return jnp.sum(acc, axis=0, keepdims=True)

    def tie_limit(v, want):
        lower = (lax.broadcasted_iota(I32, (WIDE, WIDE), 1) <= lax.broadcasted_iota(I32, (WIDE, WIDE), 0))
        lower = jnp.where(lower, 1.0, 0.0).astype(BF16)

        def body(c, carry):
            base, limit = carry
            off = pl.multiple_of(c * WIDE, WIDE)
            eq = keys_ref[pl.ds(off, WIDE), :] == v
            rank = base + _dot(lower, jnp.where(eq, 1.0, 0.0).astype(BF16))
            kpos = (off + lax.broadcasted_iota(I32, (WIDE, tq), 0)).astype(F32)
            hit = jnp.max(jnp.where(eq & (rank <= want), kpos, -1.0), axis=0, keepdims=True)
            return rank[WIDE - 1:WIDE, :], jnp.maximum(limit, hit)

        zero = jnp.zeros((1, tq), F32)
        return lax.fori_loop(0, nwide, body, (zero, zero - 1.0))[1].astype(I32)

    thr, jlim = _select(count, (1, tq), nsel, True, tie_limit)

    m_ref[...] = jnp.full(m_ref.shape, M_INIT, F32)
    l_ref[...] = jnp.zeros(l_ref.shape, F32)
    acc_ref[...] = jnp.zeros(acc_ref.shape, F32)

    def tile(off, height, bias_ref):
        kt = keys_ref[pl.ds(off, height), :]
        kpos = off + lax.broadcasted_iota(I32, (height, tq), 0)
        sel = ((kt > thr) | ((kt == thr) & (kpos <= jlim))) & (kpos <= qpos)
        amask = jnp.where(sel, 0.0, NEG)
        sts = [_dot(k_ref[pl.ds(off, height), p * LANES:(p + 1) * LANES], qz_ref[p]) for p in range(PAIRS)]
        prs, alphas = [], []
        for h in range(N_HEADS):
            lg = sts[h // 2][:, (h % 2) * tq:(h % 2 + 1) * tq] + amask
            if bias_ref is not None:
                lg = lg + bias_ref[h]
            m_old = m_ref[h]
            m_new = jnp.maximum(m_old, jnp.max(lg, axis=0, keepdims=True))
            alpha = jnp.exp(m_old - m_new)
            pr = jnp.exp(lg - m_new[0:1, :])
            l_ref[h] = alpha * l_ref[h] + jnp.sum(pr, axis=0, keepdims=True)
            m_ref[h] = m_new
            prs.append(pr.astype(BF16))
            alphas.append(alpha[0:1, :])
        for h in range(N_HEADS):
            pv = _dot(vt_ref[h * HEAD_DIM:(h + 1) * HEAD_DIM, pl.ds(off, height)], prs[h])
            acc_ref[h] = alphas[h] * acc_ref[h] + pv

    nfar = jnp.maximum(i - 1, 0)
    nfar_wide = nfar // (WIDE // LANES)

    def far_wide(c, carry):
        tile(pl.multiple_of(c * WIDE, WIDE), WIDE, None)
        return carry

    def far_narrow(j, carry):
        tile(pl.multiple_of(j * LANES, LANES), LANES, None)
        return carry

    lax.fori_loop(0, nfar_wide, far_wide, 0)
    lax.fori_loop(nfar_wide * (WIDE // LANES), nfar, far_narrow, 0)
    for h in range(N_HEADS):
        m_ref[h] = m_ref[h] + far_ref[h]

    @pl.when(i >= 1)
    def _():
        tile(pl.multiple_of((i - 1) * LANES, LANES), LANES, bs_ref)

    tile(pl.multiple_of(i * LANES, LANES), LANES, bd_ref)
    out_t = jnp.concatenate([acc_ref[h] / l_ref[h][0:1, :] for h in range(N_HEADS)], axis=0)
    o_ref[...] = jnp.transpose(out_t)


def _attn_prompt(far, h2, hb, vt, bias_diag_t, bias_sub_t, nsel):
    t = h2.shape[0]
    tq = LANES
    nq = IDX_HEADS * IDX_DIM
    return pl.pallas_call(
        functools.partial(_attn_prompt_kernel, tq=tq, nsel=nsel),
        grid=(t // tq,),
        in_specs=[pl.BlockSpec(memory_space=pltpu.SMEM),
                  pl.BlockSpec((tq, nq), lambda i: (i, C_QIDX // nq)),
                  pl.BlockSpec((tq, LANES), lambda i: (i, C_KW // LANES)),
                  pl.BlockSpec((tq, MIX_DIM), lambda i: (i, C_Q // MIX_DIM)),
                  pl.BlockSpec((t, LANES), lambda i: (0, C_KW // LANES)),
                  pl.BlockSpec((t, MIX_DIM), lambda i: (0, C_K // MIX_DIM)),
                  pl.BlockSpec((MIX_DIM, t), lambda i: (0, 0)),
                  pl.BlockSpec((N_HEADS, LANES, tq), lambda i: (0, 0, 0)),
                  pl.BlockSpec((N_HEADS, LANES, tq), lambda i: (0, 0, 0))],
        out_specs=pl.BlockSpec((tq, MIX_DIM), lambda i: (i, 0)),
        out_shape=jax.ShapeDtypeStruct((t, MIX_DIM), F32),
        scratch_shapes=[pltpu.VMEM((t, tq), I32),
                        pltpu.VMEM((IDX_HEADS * IDX_DIM, tq), BF16),
                        pltpu.VMEM((PAIRS, LANES, 2 * tq), BF16),
                        pltpu.VMEM((N_HEADS, SUBLANES, tq), F32),
                        pltpu.VMEM((N_HEADS, SUBLANES, tq), F32),
                        pltpu.VMEM((N_HEADS, HEAD_DIM, tq), F32)],
        compiler_params=_cparams(("parallel",)),
        name="attn_prompt",
    )(far, hb, h2, hb, hb, hb, vt, bias_diag_t, bias_sub_t)


QROWS = 8
SEL_PAGES = 16
ATT_PAGES = 8


def _sel_sample_kernel(pt_ref, qst_ref, wcol_ref, *refs, nsel, treal):
    pages = refs[:SEL_PAGES]
    knew_ref, mask_ref, keys_ref = refs[SEL_PAGES:]
    j = pl.program_id(1)
    ntiles = keys_ref.shape[1] // LANES

    def score(kt_t):
        r = jnp.maximum(_dot(qst_ref[0], kt_t), 0.0) * (wcol_ref[0] * IDX_SCALE)
        s = r[0:QROWS]
        for hh in range(1, IDX_HEADS):
            s = s + r[hh * QROWS:(hh + 1) * QROWS]
        return _mono_key(s)

    for g in range(SEL_PAGES):
        off = pl.multiple_of((j * SEL_PAGES + g) * LANES, LANES)
        keys_ref[:, pl.ds(off, LANES)] = score(pages[g][0, 0].astype(BF16))

    @pl.when(j == pl.num_programs(1) - 1)
    def _():
        rowq = lax.broadcasted_iota(I32, (QROWS, LANES), 0)
        lane = lax.broadcasted_iota(I32, (QROWS, LANES), 1)
        knew = jnp.where((lane < treal) & (lane <= rowq), score(knew_ref[0]), INT_MIN)
        keys_ref[:, (ntiles - 1) * LANES:] = knew

        def count(pred):
            accs = [jnp.zeros((QROWS, LANES), F32) for _ in range(4)]
            for jt in range(ntiles):
                kt = keys_ref[:, jt * LANES:(jt + 1) * LANES]
                accs[jt % 4] = accs[jt % 4] + jnp.where(pred(kt, jt * LANES + lane), 1.0, 0.0)
            return jnp.sum((accs[0] + accs[1]) + (accs[2] + accs[3]), axis=1, keepdims=True)

        row_ok = lax.broadcasted_iota(I32, (QROWS, 1), 0) < treal
        thr, jlim = _select(count, (QROWS, 1), nsel, row_ok)

        def body(jt, carry):
            off = pl.multiple_of(jt * LANES, LANES)
            kt = keys_ref[:, pl.ds(off, LANES)]
            sel = (kt > thr) | ((kt == thr) & (off + lane <= jlim))
            mask_ref[0, :, pl.ds(off, LANES)] = jnp.where(sel, 1.0, 0.0)
            return carry

        lax.fori_loop(0, ntiles, body, 0)


def _sel_sample(page_table, qst, wcol, cache_kidx, knew, layer, nsel, treal):
    b, npages = page_table.shape
    nk = (npages + 1) * PAGE
    page_spec = lambda g: pl.BlockSpec(
        (1, 1, IDX_DIM, PAGE), lambda i, j, pt: (layer, pt[i, j * SEL_PAGES + g], 0, 0))
    grid_spec = pltpu.PrefetchScalarGridSpec(
        num_scalar_prefetch=1,
        grid=(b, npages // SEL_PAGES),
        in_specs=[pl.BlockSpec((1, IDX_HEADS * QROWS, IDX_DIM), lambda i, j, pt: (i, 0, 0)),
                  pl.BlockSpec((1, IDX_HEADS * QROWS, 1), lambda i, j, pt: (i, 0, 0))]
                 + [page_spec(g) for g in range(SEL_PAGES)]
                 + [pl.BlockSpec((1, IDX_DIM, PAGE), lambda i, j, pt: (i, 0, 0))],
        out_specs=pl.BlockSpec((1, QROWS, nk), lambda i, j, pt: (i, 0, 0)),
        scratch_shapes=[pltpu.VMEM((QROWS, nk), I32)],
    )
    return pl.pallas_call(
        functools.partial(_sel_sample_kernel, nsel=nsel, treal=treal),
        grid_spec=grid_spec,
        out_shape=jax.ShapeDtypeStruct((b, QROWS, nk), F32),
        compiler_params=_cparams(("parallel", "arbitrary")),
        name="sel_sample",
    )(page_table, qst, wcol, *([cache_kidx] * SEL_PAGES), knew)


def _attn_sample_kernel(pt_ref, far_ref, qh_ref, *refs):
    kp = refs[:ATT_PAGES]
    vp = refs[ATT_PAGES:2 * ATT_PAGES]
    mask_ref, masknew_ref, blast_ref, bnew_ref, knew_ref, vnew_ref, o_ref, m_ref, l_ref, acc_ref = refs[2 * ATT_PAGES:]
    j = pl.program_id(1)
    is_last = j == pl.num_programs(1) - 1

    @pl.when(j == 0)
    def _():
        m_ref[...] = jnp.full(m_ref.shape, M_INIT, F32)
        l_ref[...] = jnp.zeros(l_ref.shape, F32)
        acc_ref[...] = jnp.zeros(acc_ref.shape, F32)

    def update(lgs, vhs):
        prs, alphas = [], []
        for h in range(N_HEADS):
            m_old = m_ref[h]
            m_new = jnp.maximum(m_old, jnp.max(lgs[h], axis=1, keepdims=True))
            alpha = jnp.exp(m_old - m_new)
            pr = jnp.exp(lgs[h] - m_new[:, 0:1])
            l_ref[h] = alpha * l_ref[h] + jnp.sum(pr, axis=1, keepdims=True)
            m_ref[h] = m_new
            prs.append(pr.astype(BF16))
            alphas.append(alpha[:, :HEAD_DIM])
        for h in range(N_HEADS):
            acc_ref[h] = alphas[h] * acc_ref[h] + _dot_nt(prs[h], vhs[h])

    def q_of(h):
        return (qh_ref[0, h].astype(F32) * ATT_SCALE).astype(BF16)

    def head_t(refs_, h):
        return jnp.concatenate([refs_[g][0, 0, h] for g in range(ATT_PAGES)], axis=1).astype(BF16)

    amask = jnp.where(mask_ref[0] > 0.5, 0.0, NEG)
    lgs = []
    for h in range(N_HEADS):
        far = jnp.full((QROWS, LANES), far_ref[h], F32)
        bias = jnp.concatenate([far] * (ATT_PAGES - 1) + [jnp.where(is_last, blast_ref[h], far)], axis=1)
        lgs.append(_dot(q_of(h), head_t(kp, h)) + amask + bias)
    update(lgs, [head_t(vp, h) for h in range(N_HEADS)])

    @pl.when(is_last)
    def _():
        amask_new = jnp.where(masknew_ref[0] > 0.5, 0.0, NEG)
        lgs_new = [_dot(q_of(h), knew_ref[0, h]) + amask_new + bnew_ref[h] for h in range(N_HEADS)]
        update(lgs_new, [vnew_ref[0, h] for h in range(N_HEADS)])
        outs = []
        for h in range(N_HEADS):
            lh = l_ref[h][:, :HEAD_DIM]
            outs.append(acc_ref[h] / jnp.where(lh > 0.0, lh, 1.0))
        o_ref[0] = jnp.concatenate(outs, axis=1)


def _attn_sample(page_table, far, qh, cache_k2, cache_v2, mask, bias_last, bias_new, knew, vnew, layer):
    b, npages = page_table.shape
    nsteps = npages // ATT_PAGES
    page_spec = lambda g: pl.BlockSpec(
        (1, 1, N_HEADS, HEAD_DIM, PAGE), lambda i, j, pt, far: (layer, pt[i, j * ATT_PAGES + g], 0, 0, 0))
    const3 = lambda i, j, pt, far: (0, 0, 0)
    per_b4 = lambda i, j, pt, far: (i, 0, 0, 0)
    grid_spec = pltpu.PrefetchScalarGridSpec(
        num_scalar_prefetch=2,
        grid=(b, nsteps),
        in_specs=[pl.BlockSpec((1, N_HEADS, QROWS, HEAD_DIM), per_b4)]
                 + [page_spec(g) for g in range(ATT_PAGES)] * 2
                 + [pl.BlockSpec((1, QROWS, ATT_PAGES * PAGE), lambda i, j, pt, far: (i, 0, j)),
                    pl.BlockSpec((1, QROWS, PAGE), lambda i, j, pt, far: (i, 0, npages)),
                    pl.BlockSpec((N_HEADS, QROWS, PAGE), const3),
                    pl.BlockSpec((N_HEADS, QROWS, PAGE), const3),
                    pl.BlockSpec((1, N_HEADS, HEAD_DIM, PAGE), per_b4),
                    pl.BlockSpec((1, N_HEADS, HEAD_DIM, PAGE), per_b4)],
        out_specs=pl.BlockSpec((1, QROWS, MIX_DIM), lambda i, j, pt, far: (i, 0, 0)),
        scratch_shapes=[pltpu.VMEM((N_HEADS, QROWS, LANES), F32),
                        pltpu.VMEM((N_HEADS, QROWS, LANES), F32),
                        pltpu.VMEM((N_HEADS, QROWS, HEAD_DIM), F32)],
    )
    return pl.pallas_call(
        _attn_sample_kernel,
        grid_spec=grid_spec,
        out_shape=jax.ShapeDtypeStruct((b, QROWS, MIX_DIM), F32),
        compiler_params=_cparams(("parallel", "arbitrary")),
        name="attn_sample",
    )(page_table, far, qh, *([cache_k2] * ATT_PAGES), *([cache_v2] * ATT_PAGES), mask, mask,
      bias_last, bias_new, knew, vnew)


def _bucket_table(n):
    d = np.arange(n)
    nf = np.maximum(d, 1).astype(np.float32)
    half = REL_BUCKETS // 2
    large = half + (np.log(nf / np.float32(half)) / np.float32(math.log(REL_MAX_DIST / half))
                    * np.float32(REL_BUCKETS - half)).astype(np.int32)
    return np.where(d < half, d, np.minimum(large, REL_BUCKETS - 1)).astype(np.int32)


def _pack_w_in(w):
    w_t = jnp.transpose(w, (0, 2, 1))
    p, z, xbc, dt, q, k, v, qi, ki, wi, u, vs = jnp.split(w_t, np.cumsum(IN_SPLITS)[:-1].tolist(), axis=1)
    zeros = lambda n: jnp.zeros((w.shape[0], n, w.shape[1]), w.dtype)
    packed = jnp.concatenate(
        [p, z, xbc, q, k, v, u, vs, qi, ki, wi, zeros(LANES - IDX_DIM - IDX_HEADS), dt, zeros(LANES - N_HEADS)],
        axis=1)
    return packed.astype(BF16)


def _pad_lanes(x):
    return jnp.pad(x.reshape(1, -1), ((0, 0), (0, LANES - x.shape[-1])))


def _pad_axis(x, axis, size):
    pads = [(0, 0)] * x.ndim
    pads[axis] = (0, size - x.shape[axis])
    return jnp.pad(x, pads)


def kernel(x_prompt, x_sample, state_pool, state_conv, state_ssm, cache_k, cache_v, cache_kidx, page_table,
           rel_bias, norm_mix, w_in, pool_w, pool_scale, conv_w, conv_b, dt_bias, a_log, d_skip, ssm_norm,
           sgu_w, sgu_b, w_out, norm_mlp, mlp_w1, mlp_w2, norm_final):
    depth = w_in.shape[0]
    bp, t, _ = x_prompt.shape
    bs, ts, _ = x_sample.shape
    npages = page_table.shape[1]
    past = npages * PAGE
    n_pool = cache_k.shape[1]
    assert bp == 1 and t % WIDE == 0 and npages % SEL_PAGES == 0 and npages % ATT_PAGES == 0
    assert max(t, past + PAGE) <= 2 ** POS_BITS and ts <= QROWS
    tc = LANES
    nsel_p = min(TOPK_MAX, t // 4)
    nsel_s = min(TOPK_MAX, (past + ts) // 4)

    bucket = _bucket_table(2 * LANES)
    far = rel_bias[REL_BUCKETS - 1]
    kk = np.arange(LANES)[:, None]
    qq = np.arange(LANES)[None, :]
    def bias_of(dist):
        onehot = np.eye(REL_BUCKETS, dtype=np.float32)[bucket[dist].reshape(-1)]
        vals = jnp.dot(onehot, rel_bias, precision=lax.Precision.HIGHEST)
        return jnp.transpose(vals).reshape((N_HEADS,) + dist.shape)

    bias_diag_t = bias_of(np.maximum(qq - kk, 0))
    bias_sub_t = bias_of(np.minimum(LANES + qq - kk, 2 * LANES - 1))
    sq = np.arange(QROWS)[:, None]
    sk = np.arange(PAGE)[None, :]
    bias_last = bias_of(np.minimum(PAGE + sq - sk, 2 * LANES - 1))
    bias_new = bias_of(np.maximum(sq - sk, 0))

    cache_k2 = jnp.transpose(cache_k, (0, 1, 3, 4, 2))
    cache_v2 = jnp.transpose(cache_v, (0, 1, 3, 4, 2))
    cache_kidx2 = jnp.transpose(cache_kidx, (0, 1, 3, 2))
    w_in_t = _pack_w_in(w_in)
    w_out_b, w1_b, w2_b = w_out.astype(BF16), mlp_w1.astype(BF16), mlp_w2.astype(BF16)

    xp = x_prompt.reshape(t, D_MODEL)
    xs = x_sample.reshape(bs * ts, D_MODEL)
    zero_pool = jnp.zeros((1, POOL_HIST, MIX_DIM), F32)
    zero_conv = jnp.zeros((1, CONV_W - 1, XBC_DIM), F32)
    zero_ssm = jnp.zeros((1, N_HEADS, HEAD_DIM, SSM_STATE), F32)
    tm_p = min(1024, t)
    tm_s = bs * ts
    outs_p, outs_s = [], []

    for l in range(depth):
        g_mix = norm_mix[l].reshape(1, D_MODEL)
        g_mlp = norm_mlp[l].reshape(1, D_MODEL)
        g_out = norm_final.reshape(1, D_MODEL) if l == depth - 1 else g_mlp
        pscale = pool_scale[l].reshape(1, MIX_DIM)
        cb = conv_b[l].reshape(1, XBC_DIM)
        dtb, alog, dsk = _pad_lanes(dt_bias[l]), _pad_lanes(a_log[l]), _pad_lanes(d_skip[l])
        snorm = ssm_norm[l].reshape(1, MIX_DIM)
        sgu_b_col = sgu_b[l][:, :, None]
        final = l == depth - 1

        hp, hpb, vt = _inproj(xp, g_mix, w_in_t, l, tm=min(1024, t), attn_operands=True)
        hp3 = hp.reshape(1, t, H_PACKED)
        y_pool, new_pool = _pool(hp3, zero_pool, pool_w[l], pscale, tc=tc, treal=tc, pos0=0)
        y_ssm, new_conv, new_h = _ssd(hp3, zero_conv, zero_ssm, conv_w[l], cb, dtb, alog, dsk, snorm,
                                      tc=tc, treal=tc)
        y_sgu, _ = _sgu(hp3, sgu_w[l], sgu_b_col, tc=tc, treal=tc)
        kf = hp[:, C_K:C_K + MIX_DIM]
        vf = hp[:, C_V:C_V + MIX_DIM]
        kif = hp[:, C_KW:C_KW + IDX_DIM]
        y_att = _attn_prompt(far, hp, hpb, vt, bias_diag_t, bias_sub_t, nsel_p)
        x1, xn = _outproj(xp, [y_pool[0], y_ssm[0], y_att, y_sgu[0]], w_out_b, l, g_mlp, tm=256)
        xp = _mlp(xn, w1_b, w2_b, l, x1, g_out, tm=tm_p, final_norm=final)
        outs_p.append((new_pool, new_conv, new_h, kf.reshape(1, t, N_HEADS, HEAD_DIM),
                       vf.reshape(1, t, N_HEADS, HEAD_DIM), kif.reshape(1, t, IDX_DIM)))

        hs, = _inproj(xs, g_mix, w_in_t, l, tm=tm_s)
        hs3 = hs.reshape(bs, ts, H_PACKED)
        y_pool, new_pool = _pool(hs3, state_pool[l], pool_w[l], pscale, tc=tc, treal=ts, pos0=past)
        y_ssm, new_conv, new_h = _ssd(hs3, state_conv[l], state_ssm[l], conv_w[l], cb, dtb, alog, dsk, snorm,
                                      tc=tc, treal=ts)
        y_sgu, v_rows = _sgu(hs3, sgu_w[l], sgu_b_col, tc=tc, treal=ts)
        kf = hs3[:, :, C_K:C_K + MIX_DIM]
        vf = hs3[:, :, C_V:C_V + MIX_DIM]
        kif = hs3[:, :, C_KW:C_KW + IDX_DIM]
        qi = hs3[:, :, C_QIDX:C_QIDX + IDX_HEADS * IDX_DIM].reshape(bs, ts, IDX_HEADS, IDX_DIM)
        qst = _pad_axis(jnp.transpose(qi, (0, 2, 1, 3)), 2, QROWS)
        qst = qst.reshape(bs, IDX_HEADS * QROWS, IDX_DIM).astype(BF16)
        wi = hs3[:, :, C_KW + IDX_DIM:C_KW + IDX_DIM + IDX_HEADS]
        wcol = _pad_axis(jnp.transpose(wi, (0, 2, 1)), 2, QROWS).reshape(bs, IDX_HEADS * QROWS, 1)
        kinew_t = _pad_axis(jnp.transpose(kif, (0, 2, 1)), 2, PAGE).astype(BF16)
        mask = _sel_sample(page_table, qst, wcol, cache_kidx2, kinew_t, l, nsel_s, ts)
        heads = lambda x, perm: jnp.transpose(x.reshape(bs, ts, N_HEADS, HEAD_DIM), perm)
        qh = _pad_axis(heads(hs3[:, :, C_Q:C_Q + MIX_DIM], (0, 2, 1, 3)), 2, QROWS).astype(BF16)
        knew = _pad_axis(heads(kf, (0, 2, 3, 1)), 3, PAGE).astype(BF16)
        vnew = _pad_axis(heads(vf, (0, 2, 3, 1)), 3, PAGE).astype(BF16)
        att = _attn_sample(page_table, far, qh, cache_k2, cache_v2, mask, bias_last, bias_new, knew, vnew, l)
        y_att = att[:, :ts].reshape(bs * ts, MIX_DIM)
        flat = lambda y: y.reshape(bs * ts, MIX_DIM)
        x1, xn = _outproj(xs, [flat(y_pool), flat(y_ssm), y_att, flat(y_sgu)], w_out_b, l, g_mlp, tm=tm_s)
        xs = _mlp(xn, w1_b, w2_b, l, x1, g_out, tm=tm_s, final_norm=final)
        outs_s.append((new_pool, new_conv, new_h, kf.reshape(bs, ts, N_HEADS, HEAD_DIM),
                       vf.reshape(bs, ts, N_HEADS, HEAD_DIM), kif, v_rows))

    stack = lambda outs, i: jnp.stack([o[i] for o in outs])
    return (xp.reshape(1, t, D_MODEL), xs.reshape(bs, ts, D_MODEL),
            *[stack(outs_p, i) for i in range(6)],
            *[stack(outs_s, i) for i in range(7)])
```

```python
import functools
import math

import numpy as np
import jax
import jax.numpy as jnp
from jax import lax
from jax.experimental import pallas as pl
from jax.experimental.pallas import tpu as pltpu

F32 = jnp.float32
BF16 = jnp.bfloat16
I32 = jnp.int32
I16 = jnp.int16

LANES = 128
SUBLANES = 8
D_MODEL = 2048
N_HEADS = 8
HEAD_DIM = 64
MIX_DIM = 512
POOL_WINDOWS = (2, 4, 8, 16)
POOL_HIST = 15
CONV_W = 4
SSM_GROUPS = 2
SSM_STATE = 128
XBC_DIM = MIX_DIM + 2 * SSM_GROUPS * SSM_STATE
IDX_HEADS = 4
IDX_DIM = 64
TOPK_MAX = 256
PAGE = 128
REL_BUCKETS = 32
REL_MAX_DIST = 128
ATT_SCALE = HEAD_DIM ** -0.5
IDX_SCALE = (IDX_HEADS * IDX_DIM) ** -0.5
SGU_GROUPS = 4
D_FF = 4 * D_MODEL
EPS = 1e-6
NEG = -1e30
INT_MIN = -(2 ** 31)
INT_MAX = 2 ** 31 - 1
IN_SPLITS = (512, 512, 1024, 8, 512, 512, 512, 256, 64, 4, 512, 512)

C_P, C_Z, C_XBC, C_Q, C_K, C_V, C_U, C_VS, C_QIDX, C_KW, C_DT = (
    0, 512, 1024, 2048, 2560, 3072, 3584, 4096, 4608, 4864, 4992)
H_PACKED = 5120
VMEM_LIMIT = 56 * 1024 * 1024


def _cparams(sem):
    return pltpu.CompilerParams(dimension_semantics=sem, vmem_limit_bytes=VMEM_LIMIT)


def _dot(a, b):
    return jnp.dot(a, b, preferred_element_type=F32)


def _dot_nt(a, b):
    return lax.dot_general(a, b, (((1,), (1,)), ((), ())), preferred_element_type=F32)


def _dot_tn(a, b):
    return lax.dot_general(a, b, (((0,), (0,)), ((), ())), preferred_element_type=F32)


def _silu(x):
    return x * (1.0 / (1.0 + jnp.exp(-x)))


def _gelu_tanh(x):
    return 0.5 * x * (1.0 + jnp.tanh(math.sqrt(2.0 / math.pi) * (x + 0.044715 * (x * x * x))))


def _softplus(x):
    return jnp.maximum(x, 0.0) + jnp.log1p(jnp.exp(-jnp.abs(x)))


def _rms(x, g):
    ms = jnp.mean(x * x, axis=-1, keepdims=True)
    return x * lax.rsqrt(ms + EPS) * g


def _inproj_kernel(x_ref, g_ref, w_ref, o_ref, *rest, attn_operands, tn):
    xn_ref = rest[-1]

    @pl.when(pl.program_id(1) == 0)
    def _():
        xn_ref[...] = _rms(x_ref[...], g_ref[...]).astype(BF16)

    h = _dot_nt(xn_ref[...], w_ref[0])
    o_ref[...] = h
    if attn_operands:
        ob_ref, vt_ref = rest[:2]
        ob_ref[...] = h.astype(BF16)

        @pl.when(pl.program_id(1) == C_V // tn)
        def _():
            vt_ref[...] = jnp.transpose(h[:, C_V % tn:C_V % tn + MIX_DIM]).astype(BF16)


def _inproj(x, g, w_t, layer, tm, tn=512, attn_operands=False):
    m = x.shape[0]
    n = w_t.shape[1]
    out_specs = [pl.BlockSpec((tm, tn), lambda i, j: (i, j))]
    out_shape = [jax.ShapeDtypeStruct((m, n), F32)]
    if attn_operands:
        out_specs += [pl.BlockSpec((tm, tn), lambda i, j: (i, j)), pl.BlockSpec((MIX_DIM, tm), lambda i, j: (0, i))]
        out_shape += [jax.ShapeDtypeStruct((m, n), BF16), jax.ShapeDtypeStruct((MIX_DIM, m), BF16)]
    return pl.pallas_call(
        functools.partial(_inproj_kernel, attn_operands=attn_operands, tn=tn),
        grid=(m // tm, n // tn),
        in_specs=[pl.BlockSpec((tm, D_MODEL), lambda i, j: (i, 0)),
                  pl.BlockSpec((1, D_MODEL), lambda i, j: (0, 0)),
                  pl.BlockSpec((1, tn, D_MODEL), lambda i, j: (layer, j, 0))],
        out_specs=out_specs,
        out_shape=out_shape,
        scratch_shapes=[pltpu.VMEM((tm, D_MODEL), BF16)],
        compiler_params=_cparams(("parallel", "arbitrary")),
        name="inproj",
    )(x, g, w_t)


def _outproj_kernel(x_ref, y0_ref, y1_ref, y2_ref, y3_ref, w_ref, g_ref, x1_ref, xn_ref):
    y = jnp.concatenate([y0_ref[...], y1_ref[...], y2_ref[...], y3_ref[...]], axis=1).astype(BF16)
    x1 = x_ref[...] + _dot(y, w_ref[0])
    x1_ref[...] = x1
    xn_ref[...] = _rms(x1, g_ref[...]).astype(BF16)


def _outproj(x, ys, w, layer, g, tm):
    m = x.shape[0]
    row = lambda i: (i, 0)
    return pl.pallas_call(
        _outproj_kernel,
        grid=(m // tm,),
        in_specs=[pl.BlockSpec((tm, D_MODEL), row)]
                 + [pl.BlockSpec((tm, MIX_DIM), row)] * 4
                 + [pl.BlockSpec((1, D_MODEL, D_MODEL), lambda i: (layer, 0, 0)),
                    pl.BlockSpec((1, D_MODEL), lambda i: (0, 0))],
        out_specs=[pl.BlockSpec((tm, D_MODEL), row), pl.BlockSpec((tm, D_MODEL), row)],
        out_shape=[jax.ShapeDtypeStruct((m, D_MODEL), F32), jax.ShapeDtypeStruct((m, D_MODEL), BF16)],
        compiler_params=_cparams(("parallel",)),
        name="outproj",
    )(x, *ys, w, g)


def _mlp_kernel(xn_ref, w1_ref, w2_ref, x1_ref, g_ref, o_ref, *, final_norm):
    j = pl.program_id(1)

    @pl.when(j == 0)
    def _():
        o_ref[...] = x1_ref[...]

    hid = _dot(xn_ref[...], w1_ref[0])
    hid = jnp.square(jnp.maximum(hid, 0.0)).astype(BF16)
    o_ref[...] += _dot(hid, w2_ref[0])

    if final_norm:
        @pl.when(j == pl.num_programs(1) - 1)
        def _():
            o_ref[...] = _rms(o_ref[...], g_ref[...])


def _mlp(xn, w1, w2, layer, x1, g, tm, final_norm, tf=512):
    m = xn.shape[0]
    return pl.pallas_call(
        functools.partial(_mlp_kernel, final_norm=final_norm),
        grid=(m // tm, D_FF // tf),
        in_specs=[pl.BlockSpec((tm, D_MODEL), lambda i, j: (i, 0)),
                  pl.BlockSpec((1, D_MODEL, tf), lambda i, j: (layer, 0, j)),
                  pl.BlockSpec((1, tf, D_MODEL), lambda i, j: (layer, j, 0)),
                  pl.BlockSpec((tm, D_MODEL), lambda i, j: (i, 0)),
                  pl.BlockSpec((1, D_MODEL), lambda i, j: (0, 0))],
        out_specs=pl.BlockSpec((tm, D_MODEL), lambda i, j: (i, 0)),
        out_shape=jax.ShapeDtypeStruct((m, D_MODEL), F32),
        compiler_params=_cparams(("parallel", "arbitrary")),
        name="mlp",
    )(xn, w1, w2, x1, g)


def _pool_kernel(p_ref, hist_ref, w_ref, scale_ref, y_ref, newhist_ref, ext_ref, *, tc, treal, pos0):
    c = pl.program_id(1)
    hrows = POOL_HIST + 1

    @pl.when(c == 0)
    def _():
        ext_ref[0:1, :] = jnp.zeros((1, MIX_DIM), F32)
        ext_ref[1:hrows, :] = hist_ref[0]

    if treal < tc:
        ext_ref[hrows:hrows + tc, :] = jnp.zeros((tc, MIX_DIM), F32)
    ext_ref[hrows:hrows + treal, :] = p_ref[0]

    pos = pos0 + c * treal + lax.broadcasted_iota(I32, (tc, 1), 0)
    for g, win in enumerate(POOL_WINDOWS):
        cols = slice(g * LANES, (g + 1) * LANES)
        cur = ext_ref[hrows:hrows + tc, cols]
        s = cur
        for j in range(1, win):
            s = s + ext_ref[hrows - j:hrows - j + tc, cols]
        cnt = jnp.minimum(pos + 1, win).astype(F32)
        diff = s / cnt - cur
        y = _dot(diff, w_ref[g]) * scale_ref[:, cols]
        y_ref[0, :, cols] = y[:treal]

    newhist_ref[0] = ext_ref[treal + 1:treal + hrows, :]
    ext_ref[0:hrows, :] = ext_ref[treal:treal + hrows, :]


def _pool(h3, hist, pool_w, pool_scale, *, tc, treal, pos0):
    b, t, _ = h3.shape
    return pl.pallas_call(
        functools.partial(_pool_kernel, tc=tc, treal=treal, pos0=pos0),
        grid=(b, t // treal),
        in_specs=[pl.BlockSpec((1, treal, MIX_DIM), lambda i, c: (i, c, C_P // MIX_DIM)),
                  pl.BlockSpec((1, POOL_HIST, MIX_DIM), lambda i, c: (i, 0, 0)),
                  pl.BlockSpec((4, LANES, LANES), lambda i, c: (0, 0, 0)),
                  pl.BlockSpec((1, MIX_DIM), lambda i, c: (0, 0))],
        out_specs=[pl.BlockSpec((1, treal, MIX_DIM), lambda i, c: (i, c, 0)),
                   pl.BlockSpec((1, POOL_HIST, MIX_DIM), lambda i, c: (i, 0, 0))],
        out_shape=[jax.ShapeDtypeStruct((b, t, MIX_DIM), F32),
                   jax.ShapeDtypeStruct((b, POOL_HIST, MIX_DIM), F32)],
        scratch_shapes=[pltpu.VMEM((POOL_HIST + 1 + tc, MIX_DIM), F32)],
        compiler_params=_cparams(("parallel", "arbitrary")),
        name="pool",
    )(h3, hist, pool_w, pool_scale)


def _sgu_kernel(u_ref, v_ref, w_ref, b_ref, y_ref, vout_ref, ubuf_ref, vbuf_ref, *, tc, treal):
    if treal < tc:
        ubuf_ref[...] = jnp.zeros((tc, MIX_DIM), F32)
        vbuf_ref[...] = jnp.zeros((tc, MIX_DIM), F32)
    ubuf_ref[0:treal, :] = u_ref[0]
    vbuf_ref[0:treal, :] = v_ref[0]
    u = _gelu_tanh(ubuf_ref[...])
    v = _gelu_tanh(vbuf_ref[...])
    vout_ref[0] = v[:treal]
    r = lax.broadcasted_iota(I32, (tc, tc), 0)
    s = lax.broadcasted_iota(I32, (tc, tc), 1)
    for g in range(SGU_GROUPS):
        cols = slice(g * LANES, (g + 1) * LANES)
        w = jnp.where(s <= r, w_ref[g], 0.0)
        mixed = _dot(w, v[:, cols]) + b_ref[g]
        y_ref[0, :, cols] = (u[:, cols] * mixed)[:treal]


def _sgu(h3, sgu_w, sgu_b_col, *, tc, treal):
    b, t, _ = h3.shape
    return pl.pallas_call(
        functools.partial(_sgu_kernel, tc=tc, treal=treal),
        grid=(b, t // treal),
        in_specs=[pl.BlockSpec((1, treal, MIX_DIM), lambda i, c: (i, c, C_U // MIX_DIM)),
                  pl.BlockSpec((1, treal, MIX_DIM), lambda i, c: (i, c, C_VS // MIX_DIM)),
                  pl.BlockSpec((SGU_GROUPS, tc, tc), lambda i, c: (0, 0, 0)),
                  pl.BlockSpec((SGU_GROUPS, tc, 1), lambda i, c: (0, 0, 0))],
        out_specs=[pl.BlockSpec((1, treal, MIX_DIM), lambda i, c: (i, c, 0)),
                   pl.BlockSpec((1, treal, MIX_DIM), lambda i, c: (i, c, 0))],
        out_shape=[jax.ShapeDtypeStruct((b, t, MIX_DIM), F32),
                   jax.ShapeDtypeStruct((b, t, MIX_DIM), F32)],
        scratch_shapes=[pltpu.VMEM((tc, MIX_DIM), F32), pltpu.VMEM((tc, MIX_DIM), F32)],
        compiler_params=_cparams(("parallel", "parallel")),
        name="sgu",
    )(h3, h3, sgu_w, sgu_b_col)


def _cumsum_rows(a, n):
    row = lax.broadcasted_iota(I32, a.shape, 0)
    sh = 1
    while sh < n:
        a = a + jnp.where(row >= sh, pltpu.roll(a, sh, 0), 0.0)
        sh *= 2
    return a


def _ssd_kernel(z_ref, xbc_ref, dt_ref, hc_ref, h0_ref, convw_ref, convb_ref, dtb_ref, alog_ref, dskip_ref,
                norm_ref, y_ref, newconv_ref, newh_ref, ext_ref, zbuf_ref, dtbuf_ref, state_ref, *, tc, treal):
    c = pl.program_id(1)
    pre = SUBLANES
    hist = CONV_W - 1

    @pl.when(c == 0)
    def _():
        ext_ref[0:pre - hist, :] = jnp.zeros((pre - hist, XBC_DIM), F32)
        ext_ref[pre - hist:pre, :] = hc_ref[0]
        state_ref[...] = h0_ref[...]

    if treal < tc:
        ext_ref[pre:pre + tc, :] = jnp.zeros((tc, XBC_DIM), F32)
        zbuf_ref[...] = jnp.zeros((tc, MIX_DIM), F32)
        dtbuf_ref[...] = jnp.zeros((tc, LANES), F32)
    ext_ref[pre:pre + treal, :] = xbc_ref[0]
    zbuf_ref[0:treal, :] = z_ref[0]
    dtbuf_ref[0:treal, :] = dt_ref[0]

    conv = convb_ref[...]
    for j in range(CONV_W):
        conv = conv + ext_ref[pre - hist + j:pre - hist + j + tc, :] * convw_ref[j:j + 1, :]
    conv = _silu(conv)
    xs = conv[:, :MIX_DIM]
    bmat = conv[:, MIX_DIM:MIX_DIM + SSM_GROUPS * SSM_STATE]
    cmat = conv[:, MIX_DIM + SSM_GROUPS * SSM_STATE:]

    rowi = lax.broadcasted_iota(I32, (tc, LANES), 0)
    dt = _softplus(dtbuf_ref[...] + dtb_ref[...])
    if treal < tc:
        dt = jnp.where(rowi < treal, dt, 0.0)
    a = dt * (-jnp.exp(alog_ref[...]))
    acum = _cumsum_rows(a, tc)
    acum_t = jnp.transpose(acum)
    total = acum[tc - 1:tc, :]
    causal = lax.broadcasted_iota(I32, (tc, tc), 1) <= lax.broadcasted_iota(I32, (tc, tc), 0)

    ys = []
    heads_per_group = N_HEADS // SSM_GROUPS
    for g in range(SSM_GROUPS):
        bg = bmat[:, g * SSM_STATE:(g + 1) * SSM_STATE]
        cg = cmat[:, g * SSM_STATE:(g + 1) * SSM_STATE]
        cb = _dot_nt(cg, bg)
        for hh in range(heads_per_group):
            h = g * heads_per_group + hh
            col = acum[:, h:h + 1]
            row = acum_t[h:h + 1, :]
            decay = jnp.exp(jnp.where(causal, col - row, NEG))
            xh = xs[:, h * HEAD_DIM:(h + 1) * HEAD_DIM]
            xdt = xh * dt[:, h:h + 1]
            hprev = state_ref[0, h]
            y = _dot(cb * decay, xdt)
            y = y + _dot_nt(cg, hprev) * jnp.exp(col)
            y = y + dskip_ref[:, h:h + 1] * xh
            ys.append(y)
            tot = total[:, h:h + 1]
            st = _dot_tn(xdt * jnp.exp(tot - col), bg)
            state_ref[0, h] = jnp.exp(tot) * hprev + st
    y = jnp.concatenate(ys, axis=1) * _silu(zbuf_ref[...])
    gw = MIX_DIM // SSM_GROUPS
    outs = []
    for g in range(SSM_GROUPS):
        yg = y[:, g * gw:(g + 1) * gw]
        ms = jnp.mean(yg * yg, axis=-1, keepdims=True)
        outs.append(yg * lax.rsqrt(ms + EPS) * norm_ref[:, g * gw:(g + 1) * gw])
    y_ref[0] = jnp.concatenate(outs, axis=1)[:treal]

    newconv_ref[0] = ext_ref[pre + treal - hist:pre + treal, :]
    ext_ref[0:pre, :] = ext_ref[treal:treal + pre, :]
    newh_ref[...] = state_ref[...]


def _ssd(h3, hist_conv, h0, conv_w, conv_b, dtb, alog, dskip, norm, *, tc, treal):
    b, t, _ = h3.shape
    full2 = lambda i, c: (0, 0)
    return pl.pallas_call(
        functools.partial(_ssd_kernel, tc=tc, treal=treal),
        grid=(b, t // treal),
        in_specs=[pl.BlockSpec((1, treal, MIX_DIM), lambda i, c: (i, c, C_Z // MIX_DIM)),
                  pl.BlockSpec((1, treal, XBC_DIM), lambda i, c: (i, c, C_XBC // XBC_DIM)),
                  pl.BlockSpec((1, treal, LANES), lambda i, c: (i, c, C_DT // LANES)),
                  pl.BlockSpec((1, CONV_W - 1, XBC_DIM), lambda i, c: (i, 0, 0)),
                  pl.BlockSpec((1, N_HEADS, HEAD_DIM, SSM_STATE), lambda i, c: (i, 0, 0, 0)),
                  pl.BlockSpec((CONV_W, XBC_DIM), full2),
                  pl.BlockSpec((1, XBC_DIM), full2),
                  pl.BlockSpec((1, LANES), full2),
                  pl.BlockSpec((1, LANES), full2),
                  pl.BlockSpec((1, LANES), full2),
                  pl.BlockSpec((1, MIX_DIM), full2)],
        out_specs=[pl.BlockSpec((1, treal, MIX_DIM), lambda i, c: (i, c, 0)),
                   pl.BlockSpec((1, CONV_W - 1, XBC_DIM), lambda i, c: (i, 0, 0)),
                   pl.BlockSpec((1, N_HEADS, HEAD_DIM, SSM_STATE), lambda i, c: (i, 0, 0, 0))],
        out_shape=[jax.ShapeDtypeStruct((b, t, MIX_DIM), F32),
                   jax.ShapeDtypeStruct((b, CONV_W - 1, XBC_DIM), F32),
                   jax.ShapeDtypeStruct((b, N_HEADS, HEAD_DIM, SSM_STATE), F32)],
        scratch_shapes=[pltpu.VMEM((SUBLANES + tc, XBC_DIM), F32),
                        pltpu.VMEM((tc, MIX_DIM), F32),
                        pltpu.VMEM((tc, LANES), F32),
                        pltpu.VMEM((1, N_HEADS, HEAD_DIM, SSM_STATE), F32)],
        compiler_params=_cparams(("parallel", "arbitrary")),
        name="ssd",
    )(h3, h3, h3, hist_conv, h0, conv_w, conv_b, dtb, alog, dskip, norm)


M_INIT = 0.5 * NEG
POS_BITS = 14
assert math.frexp(ATT_SCALE)[0] == 0.5, "q is pre-scaled in bf16, exact only for a power-of-two scale"


def _mono_key(s):
    s = jnp.where(s == 0.0, 0.0, s)
    b = lax.bitcast_convert_type(s, I32)
    return b ^ ((b >> 31) & INT_MAX)


def _select(count, stat_shape, nsel, query_ok, tie_limit=None, threshold=None):
    kf = float(nsel)

    def bit_threshold():
        c0 = count(lambda kt, pos: kt >= 0)
        nonneg = c0 >= kf
        prefix = jnp.where(nonneg, 0, INT_MIN).astype(I32)

        def bit_body(it, carry):
            prefix, cge = carry
            cand = prefix | jnp.left_shift(jnp.int32(1), 30 - it)
            cnt = count(lambda kt, pos: kt >= cand)
            take = cnt >= kf
            return jnp.where(take, cand, prefix), jnp.where(take, cnt, cge)

        return lax.fori_loop(0, 31, bit_body, (prefix, jnp.where(nonneg, c0, kf)))

    v, cge = (threshold or bit_threshold)()
    need = (cge > kf) & (v != INT_MIN) & query_ok

    def bit_tie_limit(v, want):
        def body(it, x):
            cand = x | jnp.left_shift(jnp.int32(1), POS_BITS - 1 - it)
            cnt = count(lambda kt, pos: (kt == v) & (pos < cand))
            return jnp.where(cnt < want, cand, x)
        return lax.fori_loop(0, POS_BITS, body, jnp.zeros(stat_shape, I32))

    def tie_search():
        want = kf - count(lambda kt, pos: kt > v)
        limit = (tie_limit or bit_tie_limit)(v, want)
        return jnp.where(need, limit, INT_MAX)

    any_need = jnp.max(jnp.where(need, 1.0, 0.0)) > 0.0
    jlim = lax.cond(any_need, tie_search, lambda: jnp.full(stat_shape, INT_MAX, I32))
    return v, jlim


WIDE = 4 * LANES
PAIRS = N_HEADS // 2
COUNT_ROWS = 8 * SUBLANES
HALF = 2 ** 15


def _attn_prompt_kernel(far_ref, qidx_ref, kw_ref, q_ref, kidx_ref, k_ref, vt_ref, bd_ref, bs_ref, o_ref,
                        keys_ref, hi_ref, lo_ref, qit_ref, qz_ref, m_ref, l_ref, acc_ref, *, tq, nsel):
    i = pl.program_id(0)
    qpos = i * tq + lax.broadcasted_iota(I32, (1, tq), 1)
    nwide = i // (WIDE // LANES) + 1

    qi_t = jnp.transpose(qidx_ref[...].astype(F32)).astype(BF16)
    for hh in range(IDX_HEADS):
        qit_ref[:, hh * tq:(hh + 1) * tq] = qi_t[hh * IDX_DIM:(hh + 1) * IDX_DIM, :]
    w_t = jnp.transpose(kw_ref[...])[IDX_DIM:IDX_DIM + IDX_HEADS, :] * IDX_SCALE
    q_t = jnp.transpose(q_ref[...].astype(F32) * ATT_SCALE).astype(BF16)
    upper = lax.broadcasted_iota(I32, (LANES, tq), 0) < HEAD_DIM
    zero = jnp.zeros((LANES, tq), BF16)
    for p in range(PAIRS):
        blk = q_t[p * LANES:(p + 1) * LANES, :]
        qz_ref[p, :, 0:tq] = jnp.where(upper, blk, zero)
        qz_ref[p, :, tq:2 * tq] = jnp.where(upper, zero, blk)

    def score_body(c, carry):
        off = pl.multiple_of(c * WIDE, WIDE)
        kt = kidx_ref[pl.ds(off, WIDE), 0:IDX_DIM]
        sc = _dot(kt, qit_ref[...])
        s = jnp.zeros((WIDE, tq), F32)
        for hh in range(IDX_HEADS):
            s = s + jnp.maximum(sc[:, hh * tq:(hh + 1) * tq], 0.0) * w_t[hh:hh + 1, :]
        kpos = off + lax.broadcasted_iota(I32, (WIDE, tq), 0)
        key = jnp.where(kpos <= qpos, _mono_key(s), INT_MIN)
        keys_ref[pl.ds(off, WIDE), :] = key
        hi_ref[pl.ds(off, WIDE), :] = (key >> 16).astype(I16)
        lo_ref[pl.ds(off, WIDE), :] = ((key & 0xFFFF) - HALF).astype(I16)
        return carry

    lax.fori_loop(0, nwide, score_body, 0)

    def count(pred):
        def body(c, acc):
            off = pl.multiple_of(c * WIDE, WIDE)
            kt = keys_ref[pl.ds(off, WIDE), :]
            kpos = off + lax.broadcasted_iota(I32, (WIDE, tq), 0)
            hit = jnp.where(pred(kt, kpos), 1.0, 0.0)
            return acc + jnp.sum(hit.reshape(WIDE // COUNT_ROWS, COUNT_ROWS, tq), axis=0)
        acc = lax.fori_loop(0, nwide, body, jnp.zeros((COUNT_ROWS, tq), F32))
        return jnp.sum(acc, axis=0, keepdims=True)

    def count16(ref, pred):
        one, nil = jnp.ones((WIDE, tq), I16), jnp.zeros((WIDE, tq), I16)

        def body(c, acc):
            kt = ref[pl.ds(pl.multiple_of(c * WIDE, WIDE), WIDE), :]
            hit = jnp.where(pred(kt), one, nil)
            parts = [hit[r * COUNT_ROWS:(r + 1) * COUNT_ROWS] for r in range(WIDE // COUNT_ROWS)]
            while len(parts) > 1:
                parts = [a + b for a, b in zip(parts[0::2], parts[1::2])]
            return acc + parts[0]
        acc = lax.fori_loop(0, nwide, body, jnp.zeros((COUNT_ROWS, tq), I16))
        return jnp.sum(acc.astype(F32), axis=0, keepdims=True)

    def bcast16(x):
        return jnp.broadcast_to(x, (WIDE, tq)).astype(I16)

    def search16(ref, kf, count_all):
        c0 = count16(ref, lambda kt: kt >= 0)
        nonneg = c0 >= kf
        prefix = jnp.where(nonneg, 0, -HALF).astype(I32)

        def bit_body(it, carry):
            prefix, cge = carry
            cand = prefix | jnp.left_shift(jnp.int32(1), 14 - it)
            cb = bcast16(cand)
            cnt = count16(ref, lambda kt: kt >= cb)
            take = cnt >= kf
            return jnp.where(take, cand, prefix), jnp.where(take, cnt, cge)

        return lax.fori_loop(0, 15, bit_body, (prefix, jnp.where(nonneg, c0, count_all)))

    def threshold16():
        kf = jnp.full((1, tq), float(nsel), F32)
        scanned = (nwide * WIDE).astype(F32) + jnp.zeros((1, tq), F32)
        vh, cge_h = search16(hi_ref, kf, scanned)
        vhb = bcast16(vh)
        cgt_h = count16(hi_ref, lambda kt: kt > vhb)

        def band_body(c, carry):
            rows = pl.ds(pl.multiple_of(c * WIDE, WIDE), WIDE)
            lo_ref[rows, :] = jnp.where(hi_ref[rows, :] == vhb, lo_ref[rows, :], jnp.full((WIDE, tq), -HALF, I16))
            return carry

        lax.fori_loop(0, nwide, band_body, 0)
        vl, cge_l = search16(lo_ref, kf - cgt_h, cge_h - cgt_h)
        return vh * (2 * HALF) + (vl + HALF), cgt_h + cge_l

    def tie_limit(v, want):
        lower = (lax.broadcasted_iota(I32, (WIDE, WIDE), 1) <= lax.broadcasted_iota(I32, (WIDE, WIDE), 0))
        lower = jnp.where(lower, 1.0, 0.0).astype(BF16)

        def body(c, carry):
            base, limit = carry
            off = pl.multiple_of(c * WIDE, WIDE)
            eq = keys_ref[pl.ds(off, WIDE), :] == v
            rank = base + _dot(lower, jnp.where(eq, 1.0, 0.0).astype(BF16))
            kpos = (off + lax.broadcasted_iota(I32, (WIDE, tq), 0)).astype(F32)
            hit = jnp.max(jnp.where(eq & (rank <= want), kpos, -1.0), axis=0, keepdims=True)
            return rank[WIDE - 1:WIDE, :], jnp.maximum(limit, hit)

        zero = jnp.zeros((1, tq), F32)
        return lax.fori_loop(0, nwide, body, (zero, zero - 1.0))[1].astype(I32)

    thr, jlim = _select(count, (1, tq), nsel, True, tie_limit, threshold16)

    m_ref[...] = jnp.full(m_ref.shape, M_INIT, F32)
    l_ref[...] = jnp.zeros(l_ref.shape, F32)
    acc_ref[...] = jnp.zeros(acc_ref.shape, F32)

    def tile(off, height, bias_ref):
        kt = keys_ref[pl.ds(off, height), :]
        kpos = off + lax.broadcasted_iota(I32, (height, tq), 0)
        sel = ((kt > thr) | ((kt == thr) & (kpos <= jlim))) & (kpos <= qpos)
        amask = jnp.where(sel, 0.0, NEG)
        sts = [_dot(k_ref[pl.ds(off, height), p * LANES:(p + 1) * LANES], qz_ref[p]) for p in range(PAIRS)]
        prs, alphas = [], []
        for h in range(N_HEADS):
            lg = sts[h // 2][:, (h % 2) * tq:(h % 2 + 1) * tq] + amask
            if bias_ref is not None:
                lg = lg + bias_ref[h]
            m_old = m_ref[h]
            m_new = jnp.maximum(m_old, jnp.max(lg, axis=0, keepdims=True))
            alpha = jnp.exp(m_old - m_new)
            pr = jnp.exp(lg - m_new[0:1, :])
            l_ref[h] = alpha * l_ref[h] + jnp.sum(pr, axis=0, keepdims=True)
            m_ref[h] = m_new
            prs.append(pr.astype(BF16))
            alphas.append(alpha[0:1, :])
        for h in range(N_HEADS):
            pv = _dot(vt_ref[h * HEAD_DIM:(h + 1) * HEAD_DIM, pl.ds(off, height)], prs[h])
            acc_ref[h] = alphas[h] * acc_ref[h] + pv

    nfar = jnp.maximum(i - 1, 0)
    nfar_wide = nfar // (WIDE // LANES)

    def far_wide(c, carry):
        tile(pl.multiple_of(c * WIDE, WIDE), WIDE, None)
        return carry

    def far_narrow(j, carry):
        tile(pl.multiple_of(j * LANES, LANES), LANES, None)
        return carry

    lax.fori_loop(0, nfar_wide, far_wide, 0)
    lax.fori_loop(nfar_wide * (WIDE // LANES), nfar, far_narrow, 0)
    for h in range(N_HEADS):
        m_ref[h] = m_ref[h] + far_ref[h]

    @pl.when(i >= 1)
    def _():
        tile(pl.multiple_of((i - 1) * LANES, LANES), LANES, bs_ref)

    tile(pl.multiple_of(i * LANES, LANES), LANES, bd_ref)
    out_t = jnp.concatenate([acc_ref[h] / l_ref[h][0:1, :] for h in range(N_HEADS)], axis=0)
    o_ref[...] = jnp.transpose(out_t)


def _attn_prompt(far, h2, hb, vt, bias_diag_t, bias_sub_t, nsel):
    t = h2.shape[0]
    tq = LANES
    nq = IDX_HEADS * IDX_DIM
    return pl.pallas_call(
        functools.partial(_attn_prompt_kernel, tq=tq, nsel=nsel),
        grid=(t // tq,),
        in_specs=[pl.BlockSpec(memory_space=pltpu.SMEM),
                  pl.BlockSpec((tq, nq), lambda i: (i, C_QIDX // nq)),
                  pl.BlockSpec((tq, LANES), lambda i: (i, C_KW // LANES)),
                  pl.BlockSpec((tq, MIX_DIM), lambda i: (i, C_Q // MIX_DIM)),
                  pl.BlockSpec((t, LANES), lambda i: (0, C_KW // LANES)),
                  pl.BlockSpec((t, MIX_DIM), lambda i: (0, C_K // MIX_DIM)),
                  pl.BlockSpec((MIX_DIM, t), lambda i: (0, 0)),
                  pl.BlockSpec((N_HEADS, LANES, tq), lambda i: (0, 0, 0)),
                  pl.BlockSpec((N_HEADS, LANES, tq), lambda i: (0, 0, 0))],
        out_specs=pl.BlockSpec((tq, MIX_DIM), lambda i: (i, 0)),
        out_shape=jax.ShapeDtypeStruct((t, MIX_DIM), F32),
        scratch_shapes=[pltpu.VMEM((t, tq), I32),
                        pltpu.VMEM((t, tq), I16),
                        pltpu.VMEM((t, tq), I16),
                        pltpu.VMEM((IDX_DIM, IDX_HEADS * tq), BF16),
                        pltpu.VMEM((PAIRS, LANES, 2 * tq), BF16),
                        pltpu.VMEM((N_HEADS, SUBLANES, tq), F32),
                        pltpu.VMEM((N_HEADS, SUBLANES, tq), F32),
                        pltpu.VMEM((N_HEADS, HEAD_DIM, tq), F32)],
        compiler_params=_cparams(("parallel",)),
        name="attn_prompt",
    )(far, hb, h2, hb, hb, hb, vt, bias_diag_t, bias_sub_t)


QROWS = 8
SEL_PAGES = 16
ATT_PAGES = 8


def _sel_sample_kernel(pt_ref, qst_ref, wcol_ref, *refs, nsel, treal):
    pages = refs[:SEL_PAGES]
    knew_ref, mask_ref, keys_ref = refs[SEL_PAGES:]
    j = pl.program_id(1)
    ntiles = keys_ref.shape[1] // LANES

    def score(kt_t):
        r = jnp.maximum(_dot(qst_ref[0], kt_t), 0.0) * (wcol_ref[0] * IDX_SCALE)
        s = r[0:QROWS]
        for hh in range(1, IDX_HEADS):
            s = s + r[hh * QROWS:(hh + 1) * QROWS]
        return _mono_key(s)

    for g in range(SEL_PAGES):
        off = pl.multiple_of((j * SEL_PAGES + g) * LANES, LANES)
        keys_ref[:, pl.ds(off, LANES)] = score(pages[g][0, 0].astype(BF16))

    @pl.when(j == pl.num_programs(1) - 1)
    def _():
        rowq = lax.broadcasted_iota(I32, (QROWS, LANES), 0)
        lane = lax.broadcasted_iota(I32, (QROWS, LANES), 1)
        knew = jnp.where((lane < treal) & (lane <= rowq), score(knew_ref[0]), INT_MIN)
        keys_ref[:, (ntiles - 1) * LANES:] = knew

        def count(pred):
            accs = [jnp.zeros((QROWS, LANES), F32) for _ in range(4)]
            for jt in range(ntiles):
                kt = keys_ref[:, jt * LANES:(jt + 1) * LANES]
                accs[jt % 4] = accs[jt % 4] + jnp.where(pred(kt, jt * LANES + lane), 1.0, 0.0)
            return jnp.sum((accs[0] + accs[1]) + (accs[2] + accs[3]), axis=1, keepdims=True)

        row_ok = lax.broadcasted_iota(I32, (QROWS, 1), 0) < treal
        thr, jlim = _select(count, (QROWS, 1), nsel, row_ok)

        def body(jt, carry):
            off = pl.multiple_of(jt * LANES, LANES)
            kt = keys_ref[:, pl.ds(off, LANES)]
            sel = (kt > thr) | ((kt == thr) & (off + lane <= jlim))
            mask_ref[0, :, pl.ds(off, LANES)] = jnp.where(sel, 1.0, 0.0)
            return carry

        lax.fori_loop(0, ntiles, body, 0)


def _sel_sample(page_table, qst, wcol, cache_kidx, knew, layer, nsel, treal):
    b, npages = page_table.shape
    nk = (npages + 1) * PAGE
    page_spec = lambda g: pl.BlockSpec(
        (1, 1, IDX_DIM, PAGE), lambda i, j, pt: (layer, pt[i, j * SEL_PAGES + g], 0, 0))
    grid_spec = pltpu.PrefetchScalarGridSpec(
        num_scalar_prefetch=1,
        grid=(b, npages // SEL_PAGES),
        in_specs=[pl.BlockSpec((1, IDX_HEADS * QROWS, IDX_DIM), lambda i, j, pt: (i, 0, 0)),
                  pl.BlockSpec((1, IDX_HEADS * QROWS, 1), lambda i, j, pt: (i, 0, 0))]
                 + [page_spec(g) for g in range(SEL_PAGES)]
                 + [pl.BlockSpec((1, IDX_DIM, PAGE), lambda i, j, pt: (i, 0, 0))],
        out_specs=pl.BlockSpec((1, QROWS, nk), lambda i, j, pt: (i, 0, 0)),
        scratch_shapes=[pltpu.VMEM((QROWS, nk), I32)],
    )
    return pl.pallas_call(
        functools.partial(_sel_sample_kernel, nsel=nsel, treal=treal),
        grid_spec=grid_spec,
        out_shape=jax.ShapeDtypeStruct((b, QROWS, nk), F32),
        compiler_params=_cparams(("parallel", "arbitrary")),
        name="sel_sample",
    )(page_table, qst, wcol, *([cache_kidx] * SEL_PAGES), knew)


def _attn_sample_kernel(pt_ref, far_ref, qh_ref, *refs):
    kp = refs[:ATT_PAGES]
    vp = refs[ATT_PAGES:2 * ATT_PAGES]
    mask_ref, masknew_ref, blast_ref, bnew_ref, knew_ref, vnew_ref, o_ref, m_ref, l_ref, acc_ref = refs[2 * ATT_PAGES:]
    j = pl.program_id(1)
    is_last = j == pl.num_programs(1) - 1

    @pl.when(j == 0)
    def _():
        m_ref[...] = jnp.full(m_ref.shape, M_INIT, F32)
        l_ref[...] = jnp.zeros(l_ref.shape, F32)
        acc_ref[...] = jnp.zeros(acc_ref.shape, F32)

    def update(lgs, vhs):
        prs, alphas = [], []
        for h in range(N_HEADS):
            m_old = m_ref[h]
            m_new = jnp.maximum(m_old, jnp.max(lgs[h], axis=1, keepdims=True))
            alpha = jnp.exp(m_old - m_new)
            pr = jnp.exp(lgs[h] - m_new[:, 0:1])
            l_ref[h] = alpha * l_ref[h] + jnp.sum(pr, axis=1, keepdims=True)
            m_ref[h] = m_new
            prs.append(pr.astype(BF16))
            alphas.append(alpha[:, :HEAD_DIM])
        for h in range(N_HEADS):
            acc_ref[h] = alphas[h] * acc_ref[h] + _dot_nt(prs[h], vhs[h])

    def q_of(h):
        return (qh_ref[0, h].astype(F32) * ATT_SCALE).astype(BF16)

    def head_t(refs_, h):
        return jnp.concatenate([refs_[g][0, 0, h] for g in range(ATT_PAGES)], axis=1).astype(BF16)

    amask = jnp.where(mask_ref[0] > 0.5, 0.0, NEG)
    lgs = []
    for h in range(N_HEADS):
        far = jnp.full((QROWS, LANES), far_ref[h], F32)
        bias = jnp.concatenate([far] * (ATT_PAGES - 1) + [jnp.where(is_last, blast_ref[h], far)], axis=1)
        lgs.append(_dot(q_of(h), head_t(kp, h)) + amask + bias)
    update(lgs, [head_t(vp, h) for h in range(N_HEADS)])

    @pl.when(is_last)
    def _():
        amask_new = jnp.where(masknew_ref[0] > 0.5, 0.0, NEG)
        lgs_new = [_dot(q_of(h), knew_ref[0, h]) + amask_new + bnew_ref[h] for h in range(N_HEADS)]
        update(lgs_new, [vnew_ref[0, h] for h in range(N_HEADS)])
        outs = []
        for h in range(N_HEADS):
            lh = l_ref[h][:, :HEAD_DIM]
            outs.append(acc_ref[h] / jnp.where(lh > 0.0, lh, 1.0))
        o_ref[0] = jnp.concatenate(outs, axis=1)


def _attn_sample(page_table, far, qh, cache_k2, cache_v2, mask, bias_last, bias_new, knew, vnew, layer):
    b, npages = page_table.shape
    nsteps = npages // ATT_PAGES
    page_spec = lambda g: pl.BlockSpec(
        (1, 1, N_HEADS, HEAD_DIM, PAGE), lambda i, j, pt, far: (layer, pt[i, j * ATT_PAGES + g], 0, 0, 0))
    const3 = lambda i, j, pt, far: (0, 0, 0)
    per_b4 = lambda i, j, pt, far: (i, 0, 0, 0)
    grid_spec = pltpu.PrefetchScalarGridSpec(
        num_scalar_prefetch=2,
        grid=(b, nsteps),
        in_specs=[pl.BlockSpec((1, N_HEADS, QROWS, HEAD_DIM), per_b4)]
                 + [page_spec(g) for g in range(ATT_PAGES)] * 2
                 + [pl.BlockSpec((1, QROWS, ATT_PAGES * PAGE), lambda i, j, pt, far: (i, 0, j)),
                    pl.BlockSpec((1, QROWS, PAGE), lambda i, j, pt, far: (i, 0, npages)),
                    pl.BlockSpec((N_HEADS, QROWS, PAGE), const3),
                    pl.BlockSpec((N_HEADS, QROWS, PAGE), const3),
                    pl.BlockSpec((1, N_HEADS, HEAD_DIM, PAGE), per_b4),
                    pl.BlockSpec((1, N_HEADS, HEAD_DIM, PAGE), per_b4)],
        out_specs=pl.BlockSpec((1, QROWS, MIX_DIM), lambda i, j, pt, far: (i, 0, 0)),
        scratch_shapes=[pltpu.VMEM((N_HEADS, QROWS, LANES), F32),
                        pltpu.VMEM((N_HEADS, QROWS, LANES), F32),
                        pltpu.VMEM((N_HEADS, QROWS, HEAD_DIM), F32)],
    )
    return pl.pallas_call(
        _attn_sample_kernel,
        grid_spec=grid_spec,
        out_shape=jax.ShapeDtypeStruct((b, QROWS, MIX_DIM), F32),
        compiler_params=_cparams(("parallel", "arbitrary")),
        name="attn_sample",
    )(page_table, far, qh, *([cache_k2] * ATT_PAGES), *([cache_v2] * ATT_PAGES), mask, mask,
      bias_last, bias_new, knew, vnew)


def _bucket_table(n):
    d = np.arange(n)
    nf = np.maximum(d, 1).astype(np.float32)
    half = REL_BUCKETS // 2
    large = half + (np.log(nf / np.float32(half)) / np.float32(math.log(REL_MAX_DIST / half))
                    * np.float32(REL_BUCKETS - half)).astype(np.int32)
    return np.where(d < half, d, np.minimum(large, REL_BUCKETS - 1)).astype(np.int32)


def _pack_w_in(w):
    w_t = jnp.transpose(w, (0, 2, 1))
    p, z, xbc, dt, q, k, v, qi, ki, wi, u, vs = jnp.split(w_t, np.cumsum(IN_SPLITS)[:-1].tolist(), axis=1)
    zeros = lambda n: jnp.zeros((w.shape[0], n, w.shape[1]), w.dtype)
    packed = jnp.concatenate(
        [p, z, xbc, q, k, v, u, vs, qi, ki, wi, zeros(LANES - IDX_DIM - IDX_HEADS), dt, zeros(LANES - N_HEADS)],
        axis=1)
    return packed.astype(BF16)


def _pad_lanes(x):
    return jnp.pad(x.reshape(1, -1), ((0, 0), (0, LANES - x.shape[-1])))


def _pad_axis(x, axis, size):
    pads = [(0, 0)] * x.ndim
    pads[axis] = (0, size - x.shape[axis])
    return jnp.pad(x, pads)


def kernel(x_prompt, x_sample, state_pool, state_conv, state_ssm, cache_k, cache_v, cache_kidx, page_table,
           rel_bias, norm_mix, w_in, pool_w, pool_scale, conv_w, conv_b, dt_bias, a_log, d_skip, ssm_norm,
           sgu_w, sgu_b, w_out, norm_mlp, mlp_w1, mlp_w2, norm_final):
    depth = w_in.shape[0]
    bp, t, _ = x_prompt.shape
    bs, ts, _ = x_sample.shape
    npages = page_table.shape[1]
    past = npages * PAGE
    n_pool = cache_k.shape[1]
    assert bp == 1 and t % WIDE == 0 and npages % SEL_PAGES == 0 and npages % ATT_PAGES == 0
    assert max(t, past + PAGE) <= 2 ** POS_BITS and ts <= QROWS
    tc = LANES
    nsel_p = min(TOPK_MAX, t // 4)
    nsel_s = min(TOPK_MAX, (past + ts) // 4)

    bucket = _bucket_table(2 * LANES)
    far = rel_bias[REL_BUCKETS - 1]
    kk = np.arange(LANES)[:, None]
    qq = np.arange(LANES)[None, :]
    def bias_of(dist):
        onehot = np.eye(REL_BUCKETS, dtype=np.float32)[bucket[dist].reshape(-1)]
        vals = jnp.dot(onehot, rel_bias, precision=lax.Precision.HIGHEST)
        return jnp.transpose(vals).reshape((N_HEADS,) + dist.shape)

    bias_diag_t = bias_of(np.maximum(qq - kk, 0))
    bias_sub_t = bias_of(np.minimum(LANES + qq - kk, 2 * LANES - 1))
    sq = np.arange(QROWS)[:, None]
    sk = np.arange(PAGE)[None, :]
    bias_last = bias_of(np.minimum(PAGE + sq - sk, 2 * LANES - 1))
    bias_new = bias_of(np.maximum(sq - sk, 0))

    cache_k2 = jnp.transpose(cache_k, (0, 1, 3, 4, 2))
    cache_v2 = jnp.transpose(cache_v, (0, 1, 3, 4, 2))
    cache_kidx2 = jnp.transpose(cache_kidx, (0, 1, 3, 2))
    w_in_t = _pack_w_in(w_in)
    w_out_b, w1_b, w2_b = w_out.astype(BF16), mlp_w1.astype(BF16), mlp_w2.astype(BF16)

    xp = x_prompt.reshape(t, D_MODEL)
    xs = x_sample.reshape(bs * ts, D_MODEL)
    zero_pool = jnp.zeros((1, POOL_HIST, MIX_DIM), F32)
    zero_conv = jnp.zeros((1, CONV_W - 1, XBC_DIM), F32)
    zero_ssm = jnp.zeros((1, N_HEADS, HEAD_DIM, SSM_STATE), F32)
    tm_p = min(1024, t)
    tm_s = bs * ts
    outs_p, outs_s = [], []

    for l in range(depth):
        g_mix = norm_mix[l].reshape(1, D_MODEL)
        g_mlp = norm_mlp[l].reshape(1, D_MODEL)
        g_out = norm_final.reshape(1, D_MODEL) if l == depth - 1 else g_mlp
        pscale = pool_scale[l].reshape(1, MIX_DIM)
        cb = conv_b[l].reshape(1, XBC_DIM)
        dtb, alog, dsk = _pad_lanes(dt_bias[l]), _pad_lanes(a_log[l]), _pad_lanes(d_skip[l])
        snorm = ssm_norm[l].reshape(1, MIX_DIM)
        sgu_b_col = sgu_b[l][:, :, None]
        final = l == depth - 1

        hp, hpb, vt = _inproj(xp, g_mix, w_in_t, l, tm=min(1024, t), attn_operands=True)
        hp3 = hp.reshape(1, t, H_PACKED)
        y_pool, new_pool = _pool(hp3, zero_pool, pool_w[l], pscale, tc=tc, treal=tc, pos0=0)
        y_ssm, new_conv, new_h = _ssd(hp3, zero_conv, zero_ssm, conv_w[l], cb, dtb, alog, dsk, snorm,
                                      tc=tc, treal=tc)
        y_sgu, _ = _sgu(hp3, sgu_w[l], sgu_b_col, tc=tc, treal=tc)
        kf = hp[:, C_K:C_K + MIX_DIM]
        vf = hp[:, C_V:C_V + MIX_DIM]
        kif = hp[:, C_KW:C_KW + IDX_DIM]
        y_att = _attn_prompt(far, hp, hpb, vt, bias_diag_t, bias_sub_t, nsel_p)
        x1, xn = _outproj(xp, [y_pool[0], y_ssm[0], y_att, y_sgu[0]], w_out_b, l, g_mlp, tm=256)
        xp = _mlp(xn, w1_b, w2_b, l, x1, g_out, tm=tm_p, final_norm=final)
        outs_p.append((new_pool, new_conv, new_h, kf.reshape(1, t, N_HEADS, HEAD_DIM),
                       vf.reshape(1, t, N_HEADS, HEAD_DIM), kif.reshape(1, t, IDX_DIM)))

        hs, = _inproj(xs, g_mix, w_in_t, l, tm=tm_s)
        hs3 = hs.reshape(bs, ts, H_PACKED)
        y_pool, new_pool = _pool(hs3, state_pool[l], pool_w[l], pscale, tc=tc, treal=ts, pos0=past)
        y_ssm, new_conv, new_h = _ssd(hs3, state_conv[l], state_ssm[l], conv_w[l], cb, dtb, alog, dsk, snorm,
                                      tc=tc, treal=ts)
        y_sgu, v_rows = _sgu(hs3, sgu_w[l], sgu_b_col, tc=tc, treal=ts)
        kf = hs3[:, :, C_K:C_K + MIX_DIM]
        vf = hs3[:, :, C_V:C_V + MIX_DIM]
        kif = hs3[:, :, C_KW:C_KW + IDX_DIM]
        qi = hs3[:, :, C_QIDX:C_QIDX + IDX_HEADS * IDX_DIM].reshape(bs, ts, IDX_HEADS, IDX_DIM)
        qst = _pad_axis(jnp.transpose(qi, (0, 2, 1, 3)), 2, QROWS)
        qst = qst.reshape(bs, IDX_HEADS * QROWS, IDX_DIM).astype(BF16)
        wi = hs3[:, :, C_KW + IDX_DIM:C_KW + IDX_DIM + IDX_HEADS]
        wcol = _pad_axis(jnp.transpose(wi, (0, 2, 1)), 2, QROWS).reshape(bs, IDX_HEADS * QROWS, 1)
        kinew_t = _pad_axis(jnp.transpose(kif, (0, 2, 1)), 2, PAGE).astype(BF16)
        mask = _sel_sample(page_table, qst, wcol, cache_kidx2, kinew_t, l, nsel_s, ts)
        heads = lambda x, perm: jnp.transpose(x.reshape(bs, ts, N_HEADS, HEAD_DIM), perm)
        qh = _pad_axis(heads(hs3[:, :, C_Q:C_Q + MIX_DIM], (0, 2, 1, 3)), 2, QROWS).astype(BF16)
        knew = _pad_axis(heads(kf, (0, 2, 3, 1)), 3, PAGE).astype(BF16)
        vnew = _pad_axis(heads(vf, (0, 2, 3, 1)), 3, PAGE).astype(BF16)
        att = _attn_sample(page_table, far, qh, cache_k2, cache_v2, mask, bias_last, bias_new, knew, vnew, l)
        y_att = att[:, :ts].reshape(bs * ts, MIX_DIM)
        flat = lambda y: y.reshape(bs * ts, MIX_DIM)
        x1, xn = _outproj(xs, [flat(y_pool), flat(y_ssm), y_att, flat(y_sgu)], w_out_b, l, g_mlp, tm=tm_s)
        xs = _mlp(xn, w1_b, w2_b, l, x1, g_out, tm=tm_s, final_norm=final)
        outs_s.append((new_pool, new_conv, new_h, kf.reshape(bs, ts, N_HEADS, HEAD_DIM),
                       vf.reshape(bs, ts, N_HEADS, HEAD_DIM), kif, v_rows))

    stack = lambda outs, i: jnp.stack([o[i] for o in outs])
    return (xp.reshape(1, t, D_MODEL), xs.reshape(bs, ts, D_MODEL),
            *[stack(outs_p, i) for i in range(6)],
            *[stack(outs_s, i) for i in range(7)])
```

```python
import functools
import math

import numpy as np
import jax
import jax.numpy as jnp
from jax import lax
from jax.experimental import pallas as pl
from jax.experimental.pallas import tpu as pltpu

F32 = jnp.float32
BF16 = jnp.bfloat16
I32 = jnp.int32

LANES = 128
SUBLANES = 8
D_MODEL = 2048
N_HEADS = 8
HEAD_DIM = 64
MIX_DIM = 512
POOL_WINDOWS = (2, 4, 8, 16)
POOL_HIST = 15
CONV_W = 4
SSM_GROUPS = 2
SSM_STATE = 128
XBC_DIM = MIX_DIM + 2 * SSM_GROUPS * SSM_STATE
IDX_HEADS = 4
IDX_DIM = 64
TOPK_MAX = 256
PAGE = 128
REL_BUCKETS = 32
REL_MAX_DIST = 128
ATT_SCALE = HEAD_DIM ** -0.5
IDX_SCALE = (IDX_HEADS * IDX_DIM) ** -0.5
SGU_GROUPS = 4
D_FF = 4 * D_MODEL
EPS = 1e-6
NEG = -1e30
INT_MIN = -(2 ** 31)
INT_MAX = 2 ** 31 - 1
IN_SPLITS = (512, 512, 1024, 8, 512, 512, 512, 256, 64, 4, 512, 512)

C_P, C_Z, C_XBC, C_Q, C_K, C_V, C_U, C_VS, C_QIDX, C_KW, C_DT = (
    0, 512, 1024, 2048, 2560, 3072, 3584, 4096, 4608, 4864, 4992)
H_PACKED = 5120
VMEM_LIMIT = 56 * 1024 * 1024


def _cparams(sem):
    return pltpu.CompilerParams(dimension_semantics=sem, vmem_limit_bytes=VMEM_LIMIT)


def _dot(a, b):
    return jnp.dot(a, b, preferred_element_type=F32)


def _dot_nt(a, b):
    return lax.dot_general(a, b, (((1,), (1,)), ((), ())), preferred_element_type=F32)


def _dot_tn(a, b):
    return lax.dot_general(a, b, (((0,), (0,)), ((), ())), preferred_element_type=F32)


def _silu(x):
    return x * (1.0 / (1.0 + jnp.exp(-x)))


def _gelu_tanh(x):
    return 0.5 * x * (1.0 + jnp.tanh(math.sqrt(2.0 / math.pi) * (x + 0.044715 * (x * x * x))))


def _softplus(x):
    return jnp.maximum(x, 0.0) + jnp.log1p(jnp.exp(-jnp.abs(x)))


def _rms(x, g):
    ms = jnp.mean(x * x, axis=-1, keepdims=True)
    return x * lax.rsqrt(ms + EPS) * g


def _inproj_kernel(x_ref, g_ref, w_ref, o_ref, *rest, attn_operands, tn):
    xn_ref = rest[-1]

    @pl.when(pl.program_id(1) == 0)
    def _():
        xn_ref[...] = _rms(x_ref[...], g_ref[...]).astype(BF16)

    h = _dot_nt(xn_ref[...], w_ref[0])
    o_ref[...] = h
    if attn_operands:
        ob_ref, vt_ref = rest[:2]
        ob_ref[...] = h.astype(BF16)

        @pl.when(pl.program_id(1) == C_V // tn)
        def _():
            vt_ref[...] = jnp.transpose(h[:, C_V % tn:C_V % tn + MIX_DIM]).astype(BF16)


def _inproj(x, g, w_t, layer, tm, tn=512, attn_operands=False):
    m = x.shape[0]
    n = w_t.shape[1]
    out_specs = [pl.BlockSpec((tm, tn), lambda i, j: (i, j))]
    out_shape = [jax.ShapeDtypeStruct((m, n), F32)]
    if attn_operands:
        out_specs += [pl.BlockSpec((tm, tn), lambda i, j: (i, j)), pl.BlockSpec((MIX_DIM, tm), lambda i, j: (0, i))]
        out_shape += [jax.ShapeDtypeStruct((m, n), BF16), jax.ShapeDtypeStruct((MIX_DIM, m), BF16)]
    return pl.pallas_call(
        functools.partial(_inproj_kernel, attn_operands=attn_operands, tn=tn),
        grid=(m // tm, n // tn),
        in_specs=[pl.BlockSpec((tm, D_MODEL), lambda i, j: (i, 0)),
                  pl.BlockSpec((1, D_MODEL), lambda i, j: (0, 0)),
                  pl.BlockSpec((1, tn, D_MODEL), lambda i, j: (layer, j, 0))],
        out_specs=out_specs,
        out_shape=out_shape,
        scratch_shapes=[pltpu.VMEM((tm, D_MODEL), BF16)],
        compiler_params=_cparams(("parallel", "arbitrary")),
        name="inproj",
    )(x, g, w_t)


def _outproj_kernel(x_ref, y0_ref, y1_ref, y2_ref, y3_ref, w_ref, g_ref, x1_ref, xn_ref):
    y = jnp.concatenate([y0_ref[...], y1_ref[...], y2_ref[...], y3_ref[...]], axis=1).astype(BF16)
    x1 = x_ref[...] + _dot(y, w_ref[0])
    x1_ref[...] = x1
    xn_ref[...] = _rms(x1, g_ref[...]).astype(BF16)


def _outproj(x, ys, w, layer, g, tm):
    m = x.shape[0]
    row = lambda i: (i, 0)
    return pl.pallas_call(
        _outproj_kernel,
        grid=(m // tm,),
        in_specs=[pl.BlockSpec((tm, D_MODEL), row)]
                 + [pl.BlockSpec((tm, MIX_DIM), row)] * 4
                 + [pl.BlockSpec((1, D_MODEL, D_MODEL), lambda i: (layer, 0, 0)),
                    pl.BlockSpec((1, D_MODEL), lambda i: (0, 0))],
        out_specs=[pl.BlockSpec((tm, D_MODEL), row), pl.BlockSpec((tm, D_MODEL), row)],
        out_shape=[jax.ShapeDtypeStruct((m, D_MODEL), F32), jax.ShapeDtypeStruct((m, D_MODEL), BF16)],
        compiler_params=_cparams(("parallel",)),
        name="outproj",
    )(x, *ys, w, g)


def _mlp_kernel(xn_ref, w1_ref, w2_ref, x1_ref, g_ref, o_ref, *, final_norm):
    j = pl.program_id(1)

    @pl.when(j == 0)
    def _():
        o_ref[...] = x1_ref[...]

    hid = _dot(xn_ref[...], w1_ref[0])
    hid = jnp.square(jnp.maximum(hid, 0.0)).astype(BF16)
    o_ref[...] += _dot(hid, w2_ref[0])

    if final_norm:
        @pl.when(j == pl.num_programs(1) - 1)
        def _():
            o_ref[...] = _rms(o_ref[...], g_ref[...])


def _mlp(xn, w1, w2, layer, x1, g, tm, final_norm, tf=512):
    m = xn.shape[0]
    return pl.pallas_call(
        functools.partial(_mlp_kernel, final_norm=final_norm),
        grid=(m // tm, D_FF // tf),
        in_specs=[pl.BlockSpec((tm, D_MODEL), lambda i, j: (i, 0)),
                  pl.BlockSpec((1, D_MODEL, tf), lambda i, j: (layer, 0, j)),
                  pl.BlockSpec((1, tf, D_MODEL), lambda i, j: (layer, j, 0)),
                  pl.BlockSpec((tm, D_MODEL), lambda i, j: (i, 0)),
                  pl.BlockSpec((1, D_MODEL), lambda i, j: (0, 0))],
        out_specs=pl.BlockSpec((tm, D_MODEL), lambda i, j: (i, 0)),
        out_shape=jax.ShapeDtypeStruct((m, D_MODEL), F32),
        compiler_params=_cparams(("parallel", "arbitrary")),
        name="mlp",
    )(xn, w1, w2, x1, g)


def _pool_kernel(p_ref, hist_ref, w_ref, scale_ref, y_ref, newhist_ref, ext_ref, *, tc, treal, pos0):
    c = pl.program_id(1)
    hrows = POOL_HIST + 1

    @pl.when(c == 0)
    def _():
        ext_ref[0:1, :] = jnp.zeros((1, MIX_DIM), F32)
        ext_ref[1:hrows, :] = hist_ref[0]

    if treal < tc:
        ext_ref[hrows:hrows + tc, :] = jnp.zeros((tc, MIX_DIM), F32)
    ext_ref[hrows:hrows + treal, :] = p_ref[0]

    pos = pos0 + c * treal + lax.broadcasted_iota(I32, (tc, 1), 0)
    for g, win in enumerate(POOL_WINDOWS):
        cols = slice(g * LANES, (g + 1) * LANES)
        cur = ext_ref[hrows:hrows + tc, cols]
        s = cur
        for j in range(1, win):
            s = s + ext_ref[hrows - j:hrows - j + tc, cols]
        cnt = jnp.minimum(pos + 1, win).astype(F32)
        diff = s / cnt - cur
        y = _dot(diff, w_ref[g]) * scale_ref[:, cols]
        y_ref[0, :, cols] = y[:treal]

    newhist_ref[0] = ext_ref[treal + 1:treal + hrows, :]
    ext_ref[0:hrows, :] = ext_ref[treal:treal + hrows, :]


def _pool(h3, hist, pool_w, pool_scale, *, tc, treal, pos0):
    b, t, _ = h3.shape
    return pl.pallas_call(
        functools.partial(_pool_kernel, tc=tc, treal=treal, pos0=pos0),
        grid=(b, t // treal),
        in_specs=[pl.BlockSpec((1, treal, MIX_DIM), lambda i, c: (i, c, C_P // MIX_DIM)),
                  pl.BlockSpec((1, POOL_HIST, MIX_DIM), lambda i, c: (i, 0, 0)),
                  pl.BlockSpec((4, LANES, LANES), lambda i, c: (0, 0, 0)),
                  pl.BlockSpec((1, MIX_DIM), lambda i, c: (0, 0))],
        out_specs=[pl.BlockSpec((1, treal, MIX_DIM), lambda i, c: (i, c, 0)),
                   pl.BlockSpec((1, POOL_HIST, MIX_DIM), lambda i, c: (i, 0, 0))],
        out_shape=[jax.ShapeDtypeStruct((b, t, MIX_DIM), F32),
                   jax.ShapeDtypeStruct((b, POOL_HIST, MIX_DIM), F32)],
        scratch_shapes=[pltpu.VMEM((POOL_HIST + 1 + tc, MIX_DIM), F32)],
        compiler_params=_cparams(("parallel", "arbitrary")),
        name="pool",
    )(h3, hist, pool_w, pool_scale)


def _sgu_kernel(u_ref, v_ref, w_ref, b_ref, y_ref, vout_ref, ubuf_ref, vbuf_ref, *, tc, treal):
    if treal < tc:
        ubuf_ref[...] = jnp.zeros((tc, MIX_DIM), F32)
        vbuf_ref[...] = jnp.zeros((tc, MIX_DIM), F32)
    ubuf_ref[0:treal, :] = u_ref[0]
    vbuf_ref[0:treal, :] = v_ref[0]
    u = _gelu_tanh(ubuf_ref[...])
    v = _gelu_tanh(vbuf_ref[...])
    vout_ref[0] = v[:treal]
    r = lax.broadcasted_iota(I32, (tc, tc), 0)
    s = lax.broadcasted_iota(I32, (tc, tc), 1)
    for g in range(SGU_GROUPS):
        cols = slice(g * LANES, (g + 1) * LANES)
        w = jnp.where(s <= r, w_ref[g], 0.0)
        mixed = _dot(w, v[:, cols]) + b_ref[g]
        y_ref[0, :, cols] = (u[:, cols] * mixed)[:treal]


def _sgu(h3, sgu_w, sgu_b_col, *, tc, treal):
    b, t, _ = h3.shape
    return pl.pallas_call(
        functools.partial(_sgu_kernel, tc=tc, treal=treal),
        grid=(b, t // treal),
        in_specs=[pl.BlockSpec((1, treal, MIX_DIM), lambda i, c: (i, c, C_U // MIX_DIM)),
                  pl.BlockSpec((1, treal, MIX_DIM), lambda i, c: (i, c, C_VS // MIX_DIM)),
                  pl.BlockSpec((SGU_GROUPS, tc, tc), lambda i, c: (0, 0, 0)),
                  pl.BlockSpec((SGU_GROUPS, tc, 1), lambda i, c: (0, 0, 0))],
        out_specs=[pl.BlockSpec((1, treal, MIX_DIM), lambda i, c: (i, c, 0)),
                   pl.BlockSpec((1, treal, MIX_DIM), lambda i, c: (i, c, 0))],
        out_shape=[jax.ShapeDtypeStruct((b, t, MIX_DIM), F32),
                   jax.ShapeDtypeStruct((b, t, MIX_DIM), F32)],
        scratch_shapes=[pltpu.VMEM((tc, MIX_DIM), F32), pltpu.VMEM((tc, MIX_DIM), F32)],
        compiler_params=_cparams(("parallel", "parallel")),
        name="sgu",
    )(h3, h3, sgu_w, sgu_b_col)


def _cumsum_rows(a, n):
    row = lax.broadcasted_iota(I32, a.shape, 0)
    sh = 1
    while sh < n:
        a = a + jnp.where(row >= sh, pltpu.roll(a, sh, 0), 0.0)
        sh *= 2
    return a


def _ssd_kernel(z_ref, xbc_ref, dt_ref, hc_ref, h0_ref, convw_ref, convb_ref, dtb_ref, alog_ref, dskip_ref,
                norm_ref, y_ref, newconv_ref, newh_ref, ext_ref, zbuf_ref, dtbuf_ref, state_ref, *, tc, treal):
    c = pl.program_id(1)
    pre = SUBLANES
    hist = CONV_W - 1

    @pl.when(c == 0)
    def _():
        ext_ref[0:pre - hist, :] = jnp.zeros((pre - hist, XBC_DIM), F32)
        ext_ref[pre - hist:pre, :] = hc_ref[0]
        state_ref[...] = h0_ref[...]

    if treal < tc:
        ext_ref[pre:pre + tc, :] = jnp.zeros((tc, XBC_DIM), F32)
        zbuf_ref[...] = jnp.zeros((tc, MIX_DIM), F32)
        dtbuf_ref[...] = jnp.zeros((tc, LANES), F32)
    ext_ref[pre:pre + treal, :] = xbc_ref[0]
    zbuf_ref[0:treal, :] = z_ref[0]
    dtbuf_ref[0:treal, :] = dt_ref[0]

    conv = convb_ref[...]
    for j in range(CONV_W):
        conv = conv + ext_ref[pre - hist + j:pre - hist + j + tc, :] * convw_ref[j:j + 1, :]
    conv = _silu(conv)
    xs = conv[:, :MIX_DIM]
    bmat = conv[:, MIX_DIM:MIX_DIM + SSM_GROUPS * SSM_STATE]
    cmat = conv[:, MIX_DIM + SSM_GROUPS * SSM_STATE:]

    rowi = lax.broadcasted_iota(I32, (tc, LANES), 0)
    dt = _softplus(dtbuf_ref[...] + dtb_ref[...])
    if treal < tc:
        dt = jnp.where(rowi < treal, dt, 0.0)
    a = dt * (-jnp.exp(alog_ref[...]))
    acum = _cumsum_rows(a, tc)
    acum_t = jnp.transpose(acum)
    total = acum[tc - 1:tc, :]
    causal = lax.broadcasted_iota(I32, (tc, tc), 1) <= lax.broadcasted_iota(I32, (tc, tc), 0)

    ys = []
    heads_per_group = N_HEADS // SSM_GROUPS
    for g in range(SSM_GROUPS):
        bg = bmat[:, g * SSM_STATE:(g + 1) * SSM_STATE]
        cg = cmat[:, g * SSM_STATE:(g + 1) * SSM_STATE]
        cb = _dot_nt(cg, bg)
        for hh in range(heads_per_group):
            h = g * heads_per_group + hh
            col = acum[:, h:h + 1]
            row = acum_t[h:h + 1, :]
            decay = jnp.exp(jnp.where(causal, col - row, NEG))
            xh = xs[:, h * HEAD_DIM:(h + 1) * HEAD_DIM]
            xdt = xh * dt[:, h:h + 1]
            hprev = state_ref[0, h]
            y = _dot(cb * decay, xdt)
            y = y + _dot_nt(cg, hprev) * jnp.exp(col)
            y = y + dskip_ref[:, h:h + 1] * xh
            ys.append(y)
            tot = total[:, h:h + 1]
            st = _dot_tn(xdt * jnp.exp(tot - col), bg)
            state_ref[0, h] = jnp.exp(tot) * hprev + st
    y = jnp.concatenate(ys, axis=1) * _silu(zbuf_ref[...])
    gw = MIX_DIM // SSM_GROUPS
    outs = []
    for g in range(SSM_GROUPS):
        yg = y[:, g * gw:(g + 1) * gw]
        ms = jnp.mean(yg * yg, axis=-1, keepdims=True)
        outs.append(yg * lax.rsqrt(ms + EPS) * norm_ref[:, g * gw:(g + 1) * gw])
    y_ref[0] = jnp.concatenate(outs, axis=1)[:treal]

    newconv_ref[0] = ext_ref[pre + treal - hist:pre + treal, :]
    ext_ref[0:pre, :] = ext_ref[treal:treal + pre, :]
    newh_ref[...] = state_ref[...]


def _ssd(h3, hist_conv, h0, conv_w, conv_b, dtb, alog, dskip, norm, *, tc, treal):
    b, t, _ = h3.shape
    full2 = lambda i, c: (0, 0)
    return pl.pallas_call(
        functools.partial(_ssd_kernel, tc=tc, treal=treal),
        grid=(b, t // treal),
        in_specs=[pl.BlockSpec((1, treal, MIX_DIM), lambda i, c: (i, c, C_Z // MIX_DIM)),
                  pl.BlockSpec((1, treal, XBC_DIM), lambda i, c: (i, c, C_XBC // XBC_DIM)),
                  pl.BlockSpec((1, treal, LANES), lambda i, c: (i, c, C_DT // LANES)),
                  pl.BlockSpec((1, CONV_W - 1, XBC_DIM), lambda i, c: (i, 0, 0)),
                  pl.BlockSpec((1, N_HEADS, HEAD_DIM, SSM_STATE), lambda i, c: (i, 0, 0, 0)),
                  pl.BlockSpec((CONV_W, XBC_DIM), full2),
                  pl.BlockSpec((1, XBC_DIM), full2),
                  pl.BlockSpec((1, LANES), full2),
                  pl.BlockSpec((1, LANES), full2),
                  pl.BlockSpec((1, LANES), full2),
                  pl.BlockSpec((1, MIX_DIM), full2)],
        out_specs=[pl.BlockSpec((1, treal, MIX_DIM), lambda i, c: (i, c, 0)),
                   pl.BlockSpec((1, CONV_W - 1, XBC_DIM), lambda i, c: (i, 0, 0)),
                   pl.BlockSpec((1, N_HEADS, HEAD_DIM, SSM_STATE), lambda i, c: (i, 0, 0, 0))],
        out_shape=[jax.ShapeDtypeStruct((b, t, MIX_DIM), F32),
                   jax.ShapeDtypeStruct((b, CONV_W - 1, XBC_DIM), F32),
                   jax.ShapeDtypeStruct((b, N_HEADS, HEAD_DIM, SSM_STATE), F32)],
        scratch_shapes=[pltpu.VMEM((SUBLANES + tc, XBC_DIM), F32),
                        pltpu.VMEM((tc, MIX_DIM), F32),
                        pltpu.VMEM((tc, LANES), F32),
                        pltpu.VMEM((1, N_HEADS, HEAD_DIM, SSM_STATE), F32)],
        compiler_params=_cparams(("parallel", "arbitrary")),
        name="ssd",
    )(h3, h3, h3, hist_conv, h0, conv_w, conv_b, dtb, alog, dskip, norm)


M_INIT = 0.5 * NEG
POS_BITS = 14
assert math.frexp(ATT_SCALE)[0] == 0.5, "q is pre-scaled in bf16, exact only for a power-of-two scale"


def _mono_key(s):
    s = jnp.where(s == 0.0, 0.0, s)
    b = lax.bitcast_convert_type(s, I32)
    return b ^ ((b >> 31) & INT_MAX)


def _select(count, stat_shape, nsel, query_ok, tie_limit=None, threshold=None):
    kf = float(nsel)

    def bit_threshold():
        c0 = count(lambda kt, pos: kt >= 0)
        nonneg = c0 >= kf
        prefix = jnp.where(nonneg, 0, INT_MIN).astype(I32)

        def bit_body(it, carry):
            prefix, cge = carry
            cand = prefix | jnp.left_shift(jnp.int32(1), 30 - it)
            cnt = count(lambda kt, pos: kt >= cand)
            take = cnt >= kf
            return jnp.where(take, cand, prefix), jnp.where(take, cnt, cge)

        return lax.fori_loop(0, 31, bit_body, (prefix, jnp.where(nonneg, c0, kf)))

    v, cge = (threshold or bit_threshold)()
    need = (cge > kf) & (v != INT_MIN) & query_ok

    def bit_tie_limit(v, want):
        def body(it, x):
            cand = x | jnp.left_shift(jnp.int32(1), POS_BITS - 1 - it)
            cnt = count(lambda kt, pos: (kt == v) & (pos < cand))
            return jnp.where(cnt < want, cand, x)
        return lax.fori_loop(0, POS_BITS, body, jnp.zeros(stat_shape, I32))

    def tie_search():
        want = kf - count(lambda kt, pos: kt > v)
        limit = (tie_limit or bit_tie_limit)(v, want)
        return jnp.where(need, limit, INT_MAX)

    any_need = jnp.max(jnp.where(need, 1.0, 0.0)) > 0.0
    jlim = lax.cond(any_need, tie_search, lambda: jnp.full(stat_shape, INT_MAX, I32))
    return v, jlim


WIDE = 4 * LANES
PAIRS = N_HEADS // 2
COUNT_ROWS = 8 * SUBLANES


def _attn_prompt_kernel(far_ref, qidx_ref, kw_ref, q_ref, kidx_ref, k_ref, vt_ref, bd_ref, bs_ref, o_ref,
                        keys_ref, qit_ref, qz_ref, m_ref, l_ref, acc_ref, *, tq, nsel):
    i = pl.program_id(0)
    qpos = i * tq + lax.broadcasted_iota(I32, (1, tq), 1)
    nwide = i // (WIDE // LANES) + 1

    qi_t = jnp.transpose(qidx_ref[...].astype(F32)).astype(BF16)
    for hh in range(IDX_HEADS):
        qit_ref[:, hh * tq:(hh + 1) * tq] = qi_t[hh * IDX_DIM:(hh + 1) * IDX_DIM, :]
    w_t = jnp.transpose(kw_ref[...])[IDX_DIM:IDX_DIM + IDX_HEADS, :] * IDX_SCALE
    q_t = jnp.transpose(q_ref[...].astype(F32) * ATT_SCALE).astype(BF16)
    upper = lax.broadcasted_iota(I32, (LANES, tq), 0) < HEAD_DIM
    zero = jnp.zeros((LANES, tq), BF16)
    for p in range(PAIRS):
        blk = q_t[p * LANES:(p + 1) * LANES, :]
        qz_ref[p, :, 0:tq] = jnp.where(upper, blk, zero)
        qz_ref[p, :, tq:2 * tq] = jnp.where(upper, zero, blk)

    def score_body(c, carry):
        off = pl.multiple_of(c * WIDE, WIDE)
        kt = kidx_ref[pl.ds(off, WIDE), 0:IDX_DIM]
        sc = _dot(kt, qit_ref[...])
        s = jnp.zeros((WIDE, tq), F32)
        for hh in range(IDX_HEADS):
            s = s + jnp.maximum(sc[:, hh * tq:(hh + 1) * tq], 0.0) * w_t[hh:hh + 1, :]
        kpos = off + lax.broadcasted_iota(I32, (WIDE, tq), 0)
        keys_ref[pl.ds(off, WIDE), :] = jnp.where(kpos <= qpos, _mono_key(s), INT_MIN)
        return carry

    lax.fori_loop(0, nwide, score_body, 0)

    def count(pred):
        def body(c, acc):
            off = pl.multiple_of(c * WIDE, WIDE)
            kt = keys_ref[pl.ds(off, WIDE), :]
            kpos = off + lax.broadcasted_iota(I32, (WIDE, tq), 0)
            hit = jnp.where(pred(kt, kpos), 1.0, 0.0)
            return acc + jnp.sum(hit.reshape(WIDE // COUNT_ROWS, COUNT_ROWS, tq), axis=0)
        acc = lax.fori_loop(0, nwide, body, jnp.zeros((COUNT_ROWS, tq), F32))
        return jnp.sum(acc, axis=0, keepdims=True)


    def tie_limit(v, want):
        lower = (lax.broadcasted_iota(I32, (WIDE, WIDE), 1) <= lax.broadcasted_iota(I32, (WIDE, WIDE), 0))
        lower = jnp.where(lower, 1.0, 0.0).astype(BF16)

        def body(c, carry):
            base, limit = carry
            off = pl.multiple_of(c * WIDE, WIDE)
            eq = keys_ref[pl.ds(off, WIDE), :] == v
            rank = base + _dot(lower, jnp.where(eq, 1.0, 0.0).astype(BF16))
            kpos = (off + lax.broadcasted_iota(I32, (WIDE, tq), 0)).astype(F32)
            hit = jnp.max(jnp.where(eq & (rank <= want), kpos, -1.0), axis=0, keepdims=True)
            return rank[WIDE - 1:WIDE, :], jnp.maximum(limit, hit)

        zero = jnp.zeros((1, tq), F32)
        return lax.fori_loop(0, nwide, body, (zero, zero - 1.0))[1].astype(I32)

    thr, jlim = _select(count, (1, tq), nsel, True, tie_limit)

    m_ref[...] = jnp.full(m_ref.shape, M_INIT, F32)
    l_ref[...] = jnp.zeros(l_ref.shape, F32)
    acc_ref[...] = jnp.zeros(acc_ref.shape, F32)

    def tile(off, height, bias_ref):
        kt = keys_ref[pl.ds(off, height), :]
        kpos = off + lax.broadcasted_iota(I32, (height, tq), 0)
        sel = ((kt > thr) | ((kt == thr) & (kpos <= jlim))) & (kpos <= qpos)
        amask = jnp.where(sel, 0.0, NEG)
        def scores(p):
            return _dot(k_ref[pl.ds(off, height), p * LANES:(p + 1) * LANES], qz_ref[p])

        def softmax(p, st):
            out = []
            for e in range(2):
                h = 2 * p + e
                lg = st[:, e * tq:(e + 1) * tq] + amask
                if bias_ref is not None:
                    lg = lg + bias_ref[h]
                m_old = m_ref[h]
                m_new = jnp.maximum(m_old, jnp.max(lg, axis=0, keepdims=True))
                alpha = jnp.exp(m_old - m_new)
                pr = jnp.exp(lg - m_new[0:1, :])
                l_ref[h] = alpha * l_ref[h] + jnp.sum(pr, axis=0, keepdims=True)
                m_ref[h] = m_new
                out.append((h, pr.astype(BF16), alpha[0:1, :]))
            return out

        def values(items):
            for h, pr, alpha in items:
                pv = _dot(vt_ref[h * HEAD_DIM:(h + 1) * HEAD_DIM, pl.ds(off, height)], pr)
                acc_ref[h] = alpha * acc_ref[h] + pv

        sts = {0: scores(0), 1: scores(1)}
        done = None
        for p in range(PAIRS):
            cur = softmax(p, sts.pop(p))
            if p + 2 < PAIRS:
                sts[p + 2] = scores(p + 2)
            if done is not None:
                values(done)
            done = cur
        values(done)

    nfar = jnp.maximum(i - 1, 0)
    nfar_wide = nfar // (WIDE // LANES)

    def far_wide(c, carry):
        tile(pl.multiple_of(c * WIDE, WIDE), WIDE, None)
        return carry

    def far_narrow(j, carry):
        tile(pl.multiple_of(j * LANES, LANES), LANES, None)
        return carry

    lax.fori_loop(0, nfar_wide, far_wide, 0)
    lax.fori_loop(nfar_wide * (WIDE // LANES), nfar, far_narrow, 0)
    for h in range(N_HEADS):
        m_ref[h] = m_ref[h] + far_ref[h]

    @pl.when(i >= 1)
    def _():
        tile(pl.multiple_of((i - 1) * LANES, LANES), LANES, bs_ref)

    tile(pl.multiple_of(i * LANES, LANES), LANES, bd_ref)
    out_t = jnp.concatenate([acc_ref[h] / l_ref[h][0:1, :] for h in range(N_HEADS)], axis=0)
    o_ref[...] = jnp.transpose(out_t)


def _attn_prompt(far, h2, hb, vt, bias_diag_t, bias_sub_t, nsel):
    t = h2.shape[0]
    tq = LANES
    nq = IDX_HEADS * IDX_DIM
    return pl.pallas_call(
        functools.partial(_attn_prompt_kernel, tq=tq, nsel=nsel),
        grid=(t // tq,),
        in_specs=[pl.BlockSpec(memory_space=pltpu.SMEM),
                  pl.BlockSpec((tq, nq), lambda i: (i, C_QIDX // nq)),
                  pl.BlockSpec((tq, LANES), lambda i: (i, C_KW // LANES)),
                  pl.BlockSpec((tq, MIX_DIM), lambda i: (i, C_Q // MIX_DIM)),
                  pl.BlockSpec((t, LANES), lambda i: (0, C_KW // LANES)),
                  pl.BlockSpec((t, MIX_DIM), lambda i: (0, C_K // MIX_DIM)),
                  pl.BlockSpec((MIX_DIM, t), lambda i: (0, 0)),
                  pl.BlockSpec((N_HEADS, LANES, tq), lambda i: (0, 0, 0)),
                  pl.BlockSpec((N_HEADS, LANES, tq), lambda i: (0, 0, 0))],
        out_specs=pl.BlockSpec((tq, MIX_DIM), lambda i: (i, 0)),
        out_shape=jax.ShapeDtypeStruct((t, MIX_DIM), F32),
        scratch_shapes=[pltpu.VMEM((t, tq), I32),
                        pltpu.VMEM((IDX_DIM, IDX_HEADS * tq), BF16),
                        pltpu.VMEM((PAIRS, LANES, 2 * tq), BF16),
                        pltpu.VMEM((N_HEADS, SUBLANES, tq), F32),
                        pltpu.VMEM((N_HEADS, SUBLANES, tq), F32),
                        pltpu.VMEM((N_HEADS, HEAD_DIM, tq), F32)],
        compiler_params=_cparams(("parallel",)),
        name="attn_prompt",
    )(far, hb, h2, hb, hb, hb, vt, bias_diag_t, bias_sub_t)


QROWS = 8
SEL_PAGES = 16
ATT_PAGES = 16


def _sel_sample_kernel(pt_ref, qst_ref, wcol_ref, *refs, nsel, treal):
    pages = refs[:SEL_PAGES]
    knew_ref, mask_ref, keys_ref = refs[SEL_PAGES:]
    j = pl.program_id(1)
    ntiles = keys_ref.shape[1] // LANES

    def score(kt_t):
        r = jnp.maximum(_dot(qst_ref[0], kt_t), 0.0) * (wcol_ref[0] * IDX_SCALE)
        s = r[0:QROWS]
        for hh in range(1, IDX_HEADS):
            s = s + r[hh * QROWS:(hh + 1) * QROWS]
        return _mono_key(s)

    for g in range(SEL_PAGES):
        off = pl.multiple_of((j * SEL_PAGES + g) * LANES, LANES)
        keys_ref[:, pl.ds(off, LANES)] = score(pages[g][0, 0].astype(BF16))

    @pl.when(j == pl.num_programs(1) - 1)
    def _():
        rowq = lax.broadcasted_iota(I32, (QROWS, LANES), 0)
        lane = lax.broadcasted_iota(I32, (QROWS, LANES), 1)
        knew = jnp.where((lane < treal) & (lane <= rowq), score(knew_ref[0]), INT_MIN)
        keys_ref[:, (ntiles - 1) * LANES:] = knew

        def count(pred):
            accs = [jnp.zeros((QROWS, LANES), F32) for _ in range(4)]
            for jt in range(ntiles):
                kt = keys_ref[:, jt * LANES:(jt + 1) * LANES]
                accs[jt % 4] = accs[jt % 4] + jnp.where(pred(kt, jt * LANES + lane), 1.0, 0.0)
            return jnp.sum((accs[0] + accs[1]) + (accs[2] + accs[3]), axis=1, keepdims=True)

        row_ok = lax.broadcasted_iota(I32, (QROWS, 1), 0) < treal
        thr, jlim = _select(count, (QROWS, 1), nsel, row_ok)

        def body(jt, carry):
            off = pl.multiple_of(jt * LANES, LANES)
            kt = keys_ref[:, pl.ds(off, LANES)]
            sel = (kt > thr) | ((kt == thr) & (off + lane <= jlim))
            mask_ref[0, :, pl.ds(off, LANES)] = jnp.where(sel, 1.0, 0.0)
            return carry

        lax.fori_loop(0, ntiles, body, 0)


def _sel_sample(page_table, qst, wcol, cache_kidx, knew, layer, nsel, treal):
    b, npages = page_table.shape
    nk = (npages + 1) * PAGE
    page_spec = lambda g: pl.BlockSpec(
        (1, 1, IDX_DIM, PAGE), lambda i, j, pt: (layer, pt[i, j * SEL_PAGES + g], 0, 0))
    grid_spec = pltpu.PrefetchScalarGridSpec(
        num_scalar_prefetch=1,
        grid=(b, npages // SEL_PAGES),
        in_specs=[pl.BlockSpec((1, IDX_HEADS * QROWS, IDX_DIM), lambda i, j, pt: (i, 0, 0)),
                  pl.BlockSpec((1, IDX_HEADS * QROWS, 1), lambda i, j, pt: (i, 0, 0))]
                 + [page_spec(g) for g in range(SEL_PAGES)]
                 + [pl.BlockSpec((1, IDX_DIM, PAGE), lambda i, j, pt: (i, 0, 0))],
        out_specs=pl.BlockSpec((1, QROWS, nk), lambda i, j, pt: (i, 0, 0)),
        scratch_shapes=[pltpu.VMEM((QROWS, nk), I32)],
    )
    return pl.pallas_call(
        functools.partial(_sel_sample_kernel, nsel=nsel, treal=treal),
        grid_spec=grid_spec,
        out_shape=jax.ShapeDtypeStruct((b, QROWS, nk), F32),
        compiler_params=_cparams(("parallel", "arbitrary")),
        name="sel_sample",
    )(page_table, qst, wcol, *([cache_kidx] * SEL_PAGES), knew)


def _attn_sample_kernel(pt_ref, far_ref, qh_ref, *refs):
    kp = refs[:ATT_PAGES]
    vp = refs[ATT_PAGES:2 * ATT_PAGES]
    mask_ref, masknew_ref, blast_ref, bnew_ref, knew_ref, vnew_ref, o_ref, m_ref, l_ref, acc_ref = refs[2 * ATT_PAGES:]
    j = pl.program_id(1)
    is_last = j == pl.num_programs(1) - 1

    @pl.when(j == 0)
    def _():
        m_ref[...] = jnp.full(m_ref.shape, M_INIT, F32)
        l_ref[...] = jnp.zeros(l_ref.shape, F32)
        acc_ref[...] = jnp.zeros(acc_ref.shape, F32)

    def update(lgs, vhs):
        prs, alphas = [], []
        for h in range(N_HEADS):
            m_old = m_ref[h]
            m_new = jnp.maximum(m_old, jnp.max(lgs[h], axis=1, keepdims=True))
            alpha = jnp.exp(m_old - m_new)
            pr = jnp.exp(lgs[h] - m_new[:, 0:1])
            l_ref[h] = alpha * l_ref[h] + jnp.sum(pr, axis=1, keepdims=True)
            m_ref[h] = m_new
            prs.append(pr.astype(BF16))
            alphas.append(alpha[:, :HEAD_DIM])
        for h in range(N_HEADS):
            acc_ref[h] = alphas[h] * acc_ref[h] + _dot_nt(prs[h], vhs[h])

    def q_of(h):
        return (qh_ref[0, h].astype(F32) * ATT_SCALE).astype(BF16)

    def head_t(refs_, h):
        return jnp.concatenate([refs_[g][0, 0, h] for g in range(ATT_PAGES)], axis=1).astype(BF16)

    amask = jnp.where(mask_ref[0] > 0.5, 0.0, NEG)
    lgs = []
    for h in range(N_HEADS):
        far = jnp.full((QROWS, LANES), far_ref[h], F32)
        bias = jnp.concatenate([far] * (ATT_PAGES - 1) + [jnp.where(is_last, blast_ref[h], far)], axis=1)
        lgs.append(_dot(q_of(h), head_t(kp, h)) + amask + bias)
    update(lgs, [head_t(vp, h) for h in range(N_HEADS)])

    @pl.when(is_last)
    def _():
        amask_new = jnp.where(masknew_ref[0] > 0.5, 0.0, NEG)
        lgs_new = [_dot(q_of(h), knew_ref[0, h]) + amask_new + bnew_ref[h] for h in range(N_HEADS)]
        update(lgs_new, [vnew_ref[0, h] for h in range(N_HEADS)])
        outs = []
        for h in range(N_HEADS):
            lh = l_ref[h][:, :HEAD_DIM]
            outs.append(acc_ref[h] / jnp.where(lh > 0.0, lh, 1.0))
        o_ref[0] = jnp.concatenate(outs, axis=1)


def _attn_sample(page_table, far, qh, cache_k2, cache_v2, mask, bias_last, bias_new, knew, vnew, layer):
    b, npages = page_table.shape
    nsteps = npages // ATT_PAGES
    page_spec = lambda g: pl.BlockSpec(
        (1, 1, N_HEADS, HEAD_DIM, PAGE), lambda i, j, pt, far: (layer, pt[i, j * ATT_PAGES + g], 0, 0, 0))
    const3 = lambda i, j, pt, far: (0, 0, 0)
    per_b4 = lambda i, j, pt, far: (i, 0, 0, 0)
    grid_spec = pltpu.PrefetchScalarGridSpec(
        num_scalar_prefetch=2,
        grid=(b, nsteps),
        in_specs=[pl.BlockSpec((1, N_HEADS, QROWS, HEAD_DIM), per_b4)]
                 + [page_spec(g) for g in range(ATT_PAGES)] * 2
                 + [pl.BlockSpec((1, QROWS, ATT_PAGES * PAGE), lambda i, j, pt, far: (i, 0, j)),
                    pl.BlockSpec((1, QROWS, PAGE), lambda i, j, pt, far: (i, 0, npages)),
                    pl.BlockSpec((N_HEADS, QROWS, PAGE), const3),
                    pl.BlockSpec((N_HEADS, QROWS, PAGE), const3),
                    pl.BlockSpec((1, N_HEADS, HEAD_DIM, PAGE), per_b4),
                    pl.BlockSpec((1, N_HEADS, HEAD_DIM, PAGE), per_b4)],
        out_specs=pl.BlockSpec((1, QROWS, MIX_DIM), lambda i, j, pt, far: (i, 0, 0)),
        scratch_shapes=[pltpu.VMEM((N_HEADS, QROWS, LANES), F32),
                        pltpu.VMEM((N_HEADS, QROWS, LANES), F32),
                        pltpu.VMEM((N_HEADS, QROWS, HEAD_DIM), F32)],
    )
    return pl.pallas_call(
        _attn_sample_kernel,
        grid_spec=grid_spec,
        out_shape=jax.ShapeDtypeStruct((b, QROWS, MIX_DIM), F32),
        compiler_params=_cparams(("parallel", "arbitrary")),
        name="attn_sample",
    )(page_table, far, qh, *([cache_k2] * ATT_PAGES), *([cache_v2] * ATT_PAGES), mask, mask,
      bias_last, bias_new, knew, vnew)


def _bucket_table(n):
    d = np.arange(n)
    nf = np.maximum(d, 1).astype(np.float32)
    half = REL_BUCKETS // 2
    large = half + (np.log(nf / np.float32(half)) / np.float32(math.log(REL_MAX_DIST / half))
                    * np.float32(REL_BUCKETS - half)).astype(np.int32)
    return np.where(d < half, d, np.minimum(large, REL_BUCKETS - 1)).astype(np.int32)


def _pack_w_in(w):
    w_t = jnp.transpose(w, (0, 2, 1))
    p, z, xbc, dt, q, k, v, qi, ki, wi, u, vs = jnp.split(w_t, np.cumsum(IN_SPLITS)[:-1].tolist(), axis=1)
    zeros = lambda n: jnp.zeros((w.shape[0], n, w.shape[1]), w.dtype)
    packed = jnp.concatenate(
        [p, z, xbc, q, k, v, u, vs, qi, ki, wi, zeros(LANES - IDX_DIM - IDX_HEADS), dt, zeros(LANES - N_HEADS)],
        axis=1)
    return packed.astype(BF16)


def _pad_lanes(x):
    return jnp.pad(x.reshape(1, -1), ((0, 0), (0, LANES - x.shape[-1])))


def _pad_axis(x, axis, size):
    pads = [(0, 0)] * x.ndim
    pads[axis] = (0, size - x.shape[axis])
    return jnp.pad(x, pads)


def kernel(x_prompt, x_sample, state_pool, state_conv, state_ssm, cache_k, cache_v, cache_kidx, page_table,
           rel_bias, norm_mix, w_in, pool_w, pool_scale, conv_w, conv_b, dt_bias, a_log, d_skip, ssm_norm,
           sgu_w, sgu_b, w_out, norm_mlp, mlp_w1, mlp_w2, norm_final):
    depth = w_in.shape[0]
    bp, t, _ = x_prompt.shape
    bs, ts, _ = x_sample.shape
    npages = page_table.shape[1]
    past = npages * PAGE
    n_pool = cache_k.shape[1]
    assert bp == 1 and t % WIDE == 0 and npages % SEL_PAGES == 0 and npages % ATT_PAGES == 0
    assert max(t, past + PAGE) <= 2 ** POS_BITS and ts <= QROWS
    tc = LANES
    nsel_p = min(TOPK_MAX, t // 4)
    nsel_s = min(TOPK_MAX, (past + ts) // 4)

    bucket = _bucket_table(2 * LANES)
    far = rel_bias[REL_BUCKETS - 1]
    kk = np.arange(LANES)[:, None]
    qq = np.arange(LANES)[None, :]
    def bias_of(dist):
        onehot = np.eye(REL_BUCKETS, dtype=np.float32)[bucket[dist].reshape(-1)]
        vals = jnp.dot(onehot, rel_bias, precision=lax.Precision.HIGHEST)
        return jnp.transpose(vals).reshape((N_HEADS,) + dist.shape)

    bias_diag_t = bias_of(np.maximum(qq - kk, 0))
    bias_sub_t = bias_of(np.minimum(LANES + qq - kk, 2 * LANES - 1))
    sq = np.arange(QROWS)[:, None]
    sk = np.arange(PAGE)[None, :]
    bias_last = bias_of(np.minimum(PAGE + sq - sk, 2 * LANES - 1))
    bias_new = bias_of(np.maximum(sq - sk, 0))

    cache_k2 = jnp.transpose(cache_k, (0, 1, 3, 4, 2))
    cache_v2 = jnp.transpose(cache_v, (0, 1, 3, 4, 2))
    cache_kidx2 = jnp.transpose(cache_kidx, (0, 1, 3, 2))
    w_in_t = _pack_w_in(w_in)
    w_out_b, w1_b, w2_b = w_out.astype(BF16), mlp_w1.astype(BF16), mlp_w2.astype(BF16)

    xp = x_prompt.reshape(t, D_MODEL)
    xs = x_sample.reshape(bs * ts, D_MODEL)
    zero_pool = jnp.zeros((1, POOL_HIST, MIX_DIM), F32)
    zero_conv = jnp.zeros((1, CONV_W - 1, XBC_DIM), F32)
    zero_ssm = jnp.zeros((1, N_HEADS, HEAD_DIM, SSM_STATE), F32)
    tm_p = min(1024, t)
    tm_s = bs * ts
    outs_p, outs_s = [], []

    for l in range(depth):
        g_mix = norm_mix[l].reshape(1, D_MODEL)
        g_mlp = norm_mlp[l].reshape(1, D_MODEL)
        g_out = norm_final.reshape(1, D_MODEL) if l == depth - 1 else g_mlp
        pscale = pool_scale[l].reshape(1, MIX_DIM)
        cb = conv_b[l].reshape(1, XBC_DIM)
        dtb, alog, dsk = _pad_lanes(dt_bias[l]), _pad_lanes(a_log[l]), _pad_lanes(d_skip[l])
        snorm = ssm_norm[l].reshape(1, MIX_DIM)
        sgu_b_col = sgu_b[l][:, :, None]
        final = l == depth - 1

        hp, hpb, vt = _inproj(xp, g_mix, w_in_t, l, tm=min(1024, t), tn=1024, attn_operands=True)
        hp3 = hp.reshape(1, t, H_PACKED)
        y_pool, new_pool = _pool(hp3, zero_pool, pool_w[l], pscale, tc=tc, treal=tc, pos0=0)
        y_ssm, new_conv, new_h = _ssd(hp3, zero_conv, zero_ssm, conv_w[l], cb, dtb, alog, dsk, snorm,
                                      tc=tc, treal=tc)
        y_sgu, _ = _sgu(hp3, sgu_w[l], sgu_b_col, tc=tc, treal=tc)
        kf = hp[:, C_K:C_K + MIX_DIM]
        vf = hp[:, C_V:C_V + MIX_DIM]
        kif = hp[:, C_KW:C_KW + IDX_DIM]
        y_att = _attn_prompt(far, hp, hpb, vt, bias_diag_t, bias_sub_t, nsel_p)
        x1, xn = _outproj(xp, [y_pool[0], y_ssm[0], y_att, y_sgu[0]], w_out_b, l, g_mlp, tm=256)
        xp = _mlp(xn, w1_b, w2_b, l, x1, g_out, tm=tm_p, final_norm=final)
        outs_p.append((new_pool, new_conv, new_h, kf.reshape(1, t, N_HEADS, HEAD_DIM),
                       vf.reshape(1, t, N_HEADS, HEAD_DIM), kif.reshape(1, t, IDX_DIM)))

        hs, = _inproj(xs, g_mix, w_in_t, l, tm=tm_s)
        hs3 = hs.reshape(bs, ts, H_PACKED)
        y_pool, new_pool = _pool(hs3, state_pool[l], pool_w[l], pscale, tc=tc, treal=ts, pos0=past)
        y_ssm, new_conv, new_h = _ssd(hs3, state_conv[l], state_ssm[l], conv_w[l], cb, dtb, alog, dsk, snorm,
                                      tc=tc, treal=ts)
        y_sgu, v_rows = _sgu(hs3, sgu_w[l], sgu_b_col, tc=tc, treal=ts)
        kf = hs3[:, :, C_K:C_K + MIX_DIM]
        vf = hs3[:, :, C_V:C_V + MIX_DIM]
        kif = hs3[:, :, C_KW:C_KW + IDX_DIM]
        qi = hs3[:, :, C_QIDX:C_QIDX + IDX_HEADS * IDX_DIM].reshape(bs, ts, IDX_HEADS, IDX_DIM)
        qst = _pad_axis(jnp.transpose(qi, (0, 2, 1, 3)), 2, QROWS)
        qst = qst.reshape(bs, IDX_HEADS * QROWS, IDX_DIM).astype(BF16)
        wi = hs3[:, :, C_KW + IDX_DIM:C_KW + IDX_DIM + IDX_HEADS]
        wcol = _pad_axis(jnp.transpose(wi, (0, 2, 1)), 2, QROWS).reshape(bs, IDX_HEADS * QROWS, 1)
        kinew_t = _pad_axis(jnp.transpose(kif, (0, 2, 1)), 2, PAGE).astype(BF16)
        mask = _sel_sample(page_table, qst, wcol, cache_kidx2, kinew_t, l, nsel_s, ts)
        heads = lambda x, perm: jnp.transpose(x.reshape(bs, ts, N_HEADS, HEAD_DIM), perm)
        qh = _pad_axis(heads(hs3[:, :, C_Q:C_Q + MIX_DIM], (0, 2, 1, 3)), 2, QROWS).astype(BF16)
        knew = _pad_axis(heads(kf, (0, 2, 3, 1)), 3, PAGE).astype(BF16)
        vnew = _pad_axis(heads(vf, (0, 2, 3, 1)), 3, PAGE).astype(BF16)
        att = _attn_sample(page_table, far, qh, cache_k2, cache_v2, mask, bias_last, bias_new, knew, vnew, l)
        y_att = att[:, :ts].reshape(bs * ts, MIX_DIM)
        flat = lambda y: y.reshape(bs * ts, MIX_DIM)
        x1, xn = _outproj(xs, [flat(y_pool), flat(y_ssm), y_att, flat(y_sgu)], w_out_b, l, g_mlp, tm=tm_s)
        xs = _mlp(xn, w1_b, w2_b, l, x1, g_out, tm=tm_s, final_norm=final)
        outs_s.append((new_pool, new_conv, new_h, kf.reshape(bs, ts, N_HEADS, HEAD_DIM),
                       vf.reshape(bs, ts, N_HEADS, HEAD_DIM), kif, v_rows))

    stack = lambda outs, i: jnp.stack([o[i] for o in outs])
    return (xp.reshape(1, t, D_MODEL), xs.reshape(bs, ts, D_MODEL),
            *[stack(outs_p, i) for i in range(6)],
            *[stack(outs_s, i) for i in range(7)])
```

```python
import functools
import math

import numpy as np
import jax
import jax.numpy as jnp
from jax import lax
from jax.experimental import pallas as pl
from jax.experimental.pallas import tpu as pltpu

F32 = jnp.float32
BF16 = jnp.bfloat16
I32 = jnp.int32

LANES = 128
SUBLANES = 8
D_MODEL = 2048
N_HEADS = 8
HEAD_DIM = 64
MIX_DIM = 512
POOL_WINDOWS = (2, 4, 8, 16)
POOL_HIST = 15
CONV_W = 4
SSM_GROUPS = 2
SSM_STATE = 128
XBC_DIM = MIX_DIM + 2 * SSM_GROUPS * SSM_STATE
IDX_HEADS = 4
IDX_DIM = 64
TOPK_MAX = 256
PAGE = 128
REL_BUCKETS = 32
REL_MAX_DIST = 128
ATT_SCALE = HEAD_DIM ** -0.5
IDX_SCALE = (IDX_HEADS * IDX_DIM) ** -0.5
SGU_GROUPS = 4
D_FF = 4 * D_MODEL
EPS = 1e-6
NEG = -1e30
INT_MIN = -(2 ** 31)
INT_MAX = 2 ** 31 - 1
IN_SPLITS = (512, 512, 1024, 8, 512, 512, 512, 256, 64, 4, 512, 512)

C_P, C_Z, C_XBC, C_Q, C_K, C_V, C_U, C_VS, C_QIDX, C_KW, C_DT = (
    0, 512, 1024, 2048, 2560, 3072, 3584, 4096, 4608, 4864, 4992)
H_PACKED = 5120
VMEM_LIMIT = 56 * 1024 * 1024


def _cparams(sem):
    return pltpu.CompilerParams(dimension_semantics=sem, vmem_limit_bytes=VMEM_LIMIT)


def _dot(a, b):
    return jnp.dot(a, b, preferred_element_type=F32)


def _dot_nt(a, b):
    return lax.dot_general(a, b, (((1,), (1,)), ((), ())), preferred_element_type=F32)


def _dot_tn(a, b):
    return lax.dot_general(a, b, (((0,), (0,)), ((), ())), preferred_element_type=F32)


def _silu(x):
    return x * (1.0 / (1.0 + jnp.exp(-x)))


def _gelu_tanh(x):
    return 0.5 * x * (1.0 + jnp.tanh(math.sqrt(2.0 / math.pi) * (x + 0.044715 * (x * x * x))))


def _softplus(x):
    return jnp.maximum(x, 0.0) + jnp.log1p(jnp.exp(-jnp.abs(x)))


def _rms(x, g):
    ms = jnp.mean(x * x, axis=-1, keepdims=True)
    return x * lax.rsqrt(ms + EPS) * g


def _inproj_kernel(x_ref, g_ref, w_ref, o_ref, *rest, attn_operands, tn):
    xn_ref = rest[-1]

    @pl.when(pl.program_id(1) == 0)
    def _():
        xn_ref[...] = _rms(x_ref[...], g_ref[...]).astype(BF16)

    h = _dot_nt(xn_ref[...], w_ref[0])
    o_ref[...] = h
    if attn_operands:
        ob_ref, vt_ref = rest[:2]
        ob_ref[...] = h.astype(BF16)

        @pl.when(pl.program_id(1) == C_V // tn)
        def _():
            vt_ref[...] = jnp.transpose(h[:, C_V % tn:C_V % tn + MIX_DIM]).astype(BF16)


def _inproj(x, g, w_t, layer, tm, tn=512, attn_operands=False):
    m = x.shape[0]
    n = w_t.shape[1]
    out_specs = [pl.BlockSpec((tm, tn), lambda i, j: (i, j))]
    out_shape = [jax.ShapeDtypeStruct((m, n), F32)]
    if attn_operands:
        out_specs += [pl.BlockSpec((tm, tn), lambda i, j: (i, j)), pl.BlockSpec((MIX_DIM, tm), lambda i, j: (0, i))]
        out_shape += [jax.ShapeDtypeStruct((m, n), BF16), jax.ShapeDtypeStruct((MIX_DIM, m), BF16)]
    return pl.pallas_call(
        functools.partial(_inproj_kernel, attn_operands=attn_operands, tn=tn),
        grid=(m // tm, n // tn),
        in_specs=[pl.BlockSpec((tm, D_MODEL), lambda i, j: (i, 0)),
                  pl.BlockSpec((1, D_MODEL), lambda i, j: (0, 0)),
                  pl.BlockSpec((1, tn, D_MODEL), lambda i, j: (layer, j, 0))],
        out_specs=out_specs,
        out_shape=out_shape,
        scratch_shapes=[pltpu.VMEM((tm, D_MODEL), BF16)],
        compiler_params=_cparams(("parallel", "arbitrary")),
        name="inproj",
    )(x, g, w_t)


def _outproj_kernel(x_ref, y0_ref, y1_ref, y2_ref, y3_ref, w_ref, g_ref, x1_ref, xn_ref):
    y = jnp.concatenate([y0_ref[...], y1_ref[...], y2_ref[...], y3_ref[...]], axis=1).astype(BF16)
    x1 = x_ref[...] + _dot(y, w_ref[0])
    x1_ref[...] = x1
    xn_ref[...] = _rms(x1, g_ref[...]).astype(BF16)


def _outproj(x, ys, w, layer, g, tm):
    m = x.shape[0]
    row = lambda i: (i, 0)
    return pl.pallas_call(
        _outproj_kernel,
        grid=(m // tm,),
        in_specs=[pl.BlockSpec((tm, D_MODEL), row)]
                 + [pl.BlockSpec((tm, MIX_DIM), row)] * 4
                 + [pl.BlockSpec((1, D_MODEL, D_MODEL), lambda i: (layer, 0, 0)),
                    pl.BlockSpec((1, D_MODEL), lambda i: (0, 0))],
        out_specs=[pl.BlockSpec((tm, D_MODEL), row), pl.BlockSpec((tm, D_MODEL), row)],
        out_shape=[jax.ShapeDtypeStruct((m, D_MODEL), F32), jax.ShapeDtypeStruct((m, D_MODEL), BF16)],
        compiler_params=_cparams(("parallel",)),
        name="outproj",
    )(x, *ys, w, g)


def _mlp_kernel(xn_ref, w1_ref, w2_ref, x1_ref, g_ref, o_ref, *, final_norm):
    j = pl.program_id(1)

    @pl.when(j == 0)
    def _():
        o_ref[...] = x1_ref[...]

    hid = _dot(xn_ref[...], w1_ref[0])
    hid = jnp.square(jnp.maximum(hid, 0.0)).astype(BF16)
    o_ref[...] += _dot(hid, w2_ref[0])

    if final_norm:
        @pl.when(j == pl.num_programs(1) - 1)
        def _():
            o_ref[...] = _rms(o_ref[...], g_ref[...])


def _mlp(xn, w1, w2, layer, x1, g, tm, final_norm, tf=512):
    m = xn.shape[0]
    return pl.pallas_call(
        functools.partial(_mlp_kernel, final_norm=final_norm),
        grid=(m // tm, D_FF // tf),
        in_specs=[pl.BlockSpec((tm, D_MODEL), lambda i, j: (i, 0)),
                  pl.BlockSpec((1, D_MODEL, tf), lambda i, j: (layer, 0, j)),
                  pl.BlockSpec((1, tf, D_MODEL), lambda i, j: (layer, j, 0)),
                  pl.BlockSpec((tm, D_MODEL), lambda i, j: (i, 0)),
                  pl.BlockSpec((1, D_MODEL), lambda i, j: (0, 0))],
        out_specs=pl.BlockSpec((tm, D_MODEL), lambda i, j: (i, 0)),
        out_shape=jax.ShapeDtypeStruct((m, D_MODEL), F32),
        compiler_params=_cparams(("parallel", "arbitrary")),
        name="mlp",
    )(xn, w1, w2, x1, g)


def _pool_kernel(p_ref, hist_ref, w_ref, scale_ref, y_ref, newhist_ref, ext_ref, *, tc, treal, pos0):
    c = pl.program_id(1)
    hrows = POOL_HIST + 1

    @pl.when(c == 0)
    def _():
        ext_ref[0:1, :] = jnp.zeros((1, MIX_DIM), F32)
        ext_ref[1:hrows, :] = hist_ref[0]

    if treal < tc:
        ext_ref[hrows:hrows + tc, :] = jnp.zeros((tc, MIX_DIM), F32)
    ext_ref[hrows:hrows + treal, :] = p_ref[0]

    pos = pos0 + c * treal + lax.broadcasted_iota(I32, (tc, 1), 0)
    for g, win in enumerate(POOL_WINDOWS):
        cols = slice(g * LANES, (g + 1) * LANES)
        cur = ext_ref[hrows:hrows + tc, cols]
        s = cur
        for j in range(1, win):
            s = s + ext_ref[hrows - j:hrows - j + tc, cols]
        cnt = jnp.minimum(pos + 1, win).astype(F32)
        diff = s / cnt - cur
        y = _dot(diff, w_ref[g]) * scale_ref[:, cols]
        y_ref[0, :, cols] = y[:treal]

    newhist_ref[0] = ext_ref[treal + 1:treal + hrows, :]
    ext_ref[0:hrows, :] = ext_ref[treal:treal + hrows, :]


def _pool(h3, hist, pool_w, pool_scale, *, tc, treal, pos0):
    b, t, _ = h3.shape
    return pl.pallas_call(
        functools.partial(_pool_kernel, tc=tc, treal=treal, pos0=pos0),
        grid=(b, t // treal),
        in_specs=[pl.BlockSpec((1, treal, MIX_DIM), lambda i, c: (i, c, C_P // MIX_DIM)),
                  pl.BlockSpec((1, POOL_HIST, MIX_DIM), lambda i, c: (i, 0, 0)),
                  pl.BlockSpec((4, LANES, LANES), lambda i, c: (0, 0, 0)),
                  pl.BlockSpec((1, MIX_DIM), lambda i, c: (0, 0))],
        out_specs=[pl.BlockSpec((1, treal, MIX_DIM), lambda i, c: (i, c, 0)),
                   pl.BlockSpec((1, POOL_HIST, MIX_DIM), lambda i, c: (i, 0, 0))],
        out_shape=[jax.ShapeDtypeStruct((b, t, MIX_DIM), F32),
                   jax.ShapeDtypeStruct((b, POOL_HIST, MIX_DIM), F32)],
        scratch_shapes=[pltpu.VMEM((POOL_HIST + 1 + tc, MIX_DIM), F32)],
        compiler_params=_cparams(("parallel", "arbitrary")),
        name="pool",
    )(h3, hist, pool_w, pool_scale)


def _sgu_kernel(u_ref, v_ref, w_ref, b_ref, y_ref, vout_ref, ubuf_ref, vbuf_ref, *, tc, treal):
    if treal < tc:
        ubuf_ref[...] = jnp.zeros((tc, MIX_DIM), F32)
        vbuf_ref[...] = jnp.zeros((tc, MIX_DIM), F32)
    ubuf_ref[0:treal, :] = u_ref[0]
    vbuf_ref[0:treal, :] = v_ref[0]
    u = _gelu_tanh(ubuf_ref[...])
    v = _gelu_tanh(vbuf_ref[...])
    vout_ref[0] = v[:treal]
    r = lax.broadcasted_iota(I32, (tc, tc), 0)
    s = lax.broadcasted_iota(I32, (tc, tc), 1)
    for g in range(SGU_GROUPS):
        cols = slice(g * LANES, (g + 1) * LANES)
        w = jnp.where(s <= r, w_ref[g], 0.0)
        mixed = _dot(w, v[:, cols]) + b_ref[g]
        y_ref[0, :, cols] = (u[:, cols] * mixed)[:treal]


def _sgu(h3, sgu_w, sgu_b_col, *, tc, treal):
    b, t, _ = h3.shape
    return pl.pallas_call(
        functools.partial(_sgu_kernel, tc=tc, treal=treal),
        grid=(b, t // treal),
        in_specs=[pl.BlockSpec((1, treal, MIX_DIM), lambda i, c: (i, c, C_U // MIX_DIM)),
                  pl.BlockSpec((1, treal, MIX_DIM), lambda i, c: (i, c, C_VS // MIX_DIM)),
                  pl.BlockSpec((SGU_GROUPS, tc, tc), lambda i, c: (0, 0, 0)),
                  pl.BlockSpec((SGU_GROUPS, tc, 1), lambda i, c: (0, 0, 0))],
        out_specs=[pl.BlockSpec((1, treal, MIX_DIM), lambda i, c: (i, c, 0)),
                   pl.BlockSpec((1, treal, MIX_DIM), lambda i, c: (i, c, 0))],
        out_shape=[jax.ShapeDtypeStruct((b, t, MIX_DIM), F32),
                   jax.ShapeDtypeStruct((b, t, MIX_DIM), F32)],
        scratch_shapes=[pltpu.VMEM((tc, MIX_DIM), F32), pltpu.VMEM((tc, MIX_DIM), F32)],
        compiler_params=_cparams(("parallel", "parallel")),
        name="sgu",
    )(h3, h3, sgu_w, sgu_b_col)


def _cumsum_rows(a, n):
    row = lax.broadcasted_iota(I32, a.shape, 0)
    sh = 1
    while sh < n:
        a = a + jnp.where(row >= sh, pltpu.roll(a, sh, 0), 0.0)
        sh *= 2
    return a


def _ssd_kernel(z_ref, xbc_ref, dt_ref, hc_ref, h0_ref, convw_ref, convb_ref, dtb_ref, alog_ref, dskip_ref,
                norm_ref, y_ref, newconv_ref, newh_ref, ext_ref, zbuf_ref, dtbuf_ref, state_ref, *, tc, treal):
    c = pl.program_id(1)
    pre = SUBLANES
    hist = CONV_W - 1

    @pl.when(c == 0)
    def _():
        ext_ref[0:pre - hist, :] = jnp.zeros((pre - hist, XBC_DIM), F32)
        ext_ref[pre - hist:pre, :] = hc_ref[0]
        state_ref[...] = h0_ref[...]

    if treal < tc:
        ext_ref[pre:pre + tc, :] = jnp.zeros((tc, XBC_DIM), F32)
        zbuf_ref[...] = jnp.zeros((tc, MIX_DIM), F32)
        dtbuf_ref[...] = jnp.zeros((tc, LANES), F32)
    ext_ref[pre:pre + treal, :] = xbc_ref[0]
    zbuf_ref[0:treal, :] = z_ref[0]
    dtbuf_ref[0:treal, :] = dt_ref[0]

    conv = convb_ref[...]
    for j in range(CONV_W):
        conv = conv + ext_ref[pre - hist + j:pre - hist + j + tc, :] * convw_ref[j:j + 1, :]
    conv = _silu(conv)
    xs = conv[:, :MIX_DIM]
    bmat = conv[:, MIX_DIM:MIX_DIM + SSM_GROUPS * SSM_STATE]
    cmat = conv[:, MIX_DIM + SSM_GROUPS * SSM_STATE:]

    rowi = lax.broadcasted_iota(I32, (tc, LANES), 0)
    dt = _softplus(dtbuf_ref[...] + dtb_ref[...])
    if treal < tc:
        dt = jnp.where(rowi < treal, dt, 0.0)
    a = dt * (-jnp.exp(alog_ref[...]))
    acum = _cumsum_rows(a, tc)
    acum_t = jnp.transpose(acum)
    total = acum[tc - 1:tc, :]
    causal = lax.broadcasted_iota(I32, (tc, tc), 1) <= lax.broadcasted_iota(I32, (tc, tc), 0)

    ys = []
    heads_per_group = N_HEADS // SSM_GROUPS
    for g in range(SSM_GROUPS):
        bg = bmat[:, g * SSM_STATE:(g + 1) * SSM_STATE]
        cg = cmat[:, g * SSM_STATE:(g + 1) * SSM_STATE]
        cb = _dot_nt(cg, bg)
        for hh in range(heads_per_group):
            h = g * heads_per_group + hh
            col = acum[:, h:h + 1]
            row = acum_t[h:h + 1, :]
            decay = jnp.exp(jnp.where(causal, col - row, NEG))
            xh = xs[:, h * HEAD_DIM:(h + 1) * HEAD_DIM]
            xdt = xh * dt[:, h:h + 1]
            hprev = state_ref[0, h]
            y = _dot(cb * decay, xdt)
            y = y + _dot_nt(cg, hprev) * jnp.exp(col)
            y = y + dskip_ref[:, h:h + 1] * xh
            ys.append(y)
            tot = total[:, h:h + 1]
            st = _dot_tn(xdt * jnp.exp(tot - col), bg)
            state_ref[0, h] = jnp.exp(tot) * hprev + st
    y = jnp.concatenate(ys, axis=1) * _silu(zbuf_ref[...])
    gw = MIX_DIM // SSM_GROUPS
    outs = []
    for g in range(SSM_GROUPS):
        yg = y[:, g * gw:(g + 1) * gw]
        ms = jnp.mean(yg * yg, axis=-1, keepdims=True)
        outs.append(yg * lax.rsqrt(ms + EPS) * norm_ref[:, g * gw:(g + 1) * gw])
    y_ref[0] = jnp.concatenate(outs, axis=1)[:treal]

    newconv_ref[0] = ext_ref[pre + treal - hist:pre + treal, :]
    ext_ref[0:pre, :] = ext_ref[treal:treal + pre, :]
    newh_ref[...] = state_ref[...]


def _ssd(h3, hist_conv, h0, conv_w, conv_b, dtb, alog, dskip, norm, *, tc, treal):
    b, t, _ = h3.shape
    full2 = lambda i, c: (0, 0)
    return pl.pallas_call(
        functools.partial(_ssd_kernel, tc=tc, treal=treal),
        grid=(b, t // treal),
        in_specs=[pl.BlockSpec((1, treal, MIX_DIM), lambda i, c: (i, c, C_Z // MIX_DIM)),
                  pl.BlockSpec((1, treal, XBC_DIM), lambda i, c: (i, c, C_XBC // XBC_DIM)),
                  pl.BlockSpec((1, treal, LANES), lambda i, c: (i, c, C_DT // LANES)),
                  pl.BlockSpec((1, CONV_W - 1, XBC_DIM), lambda i, c: (i, 0, 0)),
                  pl.BlockSpec((1, N_HEADS, HEAD_DIM, SSM_STATE), lambda i, c: (i, 0, 0, 0)),
                  pl.BlockSpec((CONV_W, XBC_DIM), full2),
                  pl.BlockSpec((1, XBC_DIM), full2),
                  pl.BlockSpec((1, LANES), full2),
                  pl.BlockSpec((1, LANES), full2),
                  pl.BlockSpec((1, LANES), full2),
                  pl.BlockSpec((1, MIX_DIM), full2)],
        out_specs=[pl.BlockSpec((1, treal, MIX_DIM), lambda i, c: (i, c, 0)),
                   pl.BlockSpec((1, CONV_W - 1, XBC_DIM), lambda i, c: (i, 0, 0)),
                   pl.BlockSpec((1, N_HEADS, HEAD_DIM, SSM_STATE), lambda i, c: (i, 0, 0, 0))],
        out_shape=[jax.ShapeDtypeStruct((b, t, MIX_DIM), F32),
                   jax.ShapeDtypeStruct((b, CONV_W - 1, XBC_DIM), F32),
                   jax.ShapeDtypeStruct((b, N_HEADS, HEAD_DIM, SSM_STATE), F32)],
        scratch_shapes=[pltpu.VMEM((SUBLANES + tc, XBC_DIM), F32),
                        pltpu.VMEM((tc, MIX_DIM), F32),
                        pltpu.VMEM((tc, LANES), F32),
                        pltpu.VMEM((1, N_HEADS, HEAD_DIM, SSM_STATE), F32)],
        compiler_params=_cparams(("parallel", "arbitrary")),
        name="ssd",
    )(h3, h3, h3, hist_conv, h0, conv_w, conv_b, dtb, alog, dskip, norm)


M_INIT = 0.5 * NEG
POS_BITS = 14
assert math.frexp(ATT_SCALE)[0] == 0.5, "q is pre-scaled in bf16, exact only for a power-of-two scale"


def _mono_key(s):
    s = jnp.where(s == 0.0, 0.0, s)
    b = lax.bitcast_convert_type(s, I32)
    return b ^ ((b >> 31) & INT_MAX)


def _select(count, stat_shape, nsel, query_ok, tie_limit=None, threshold=None):
    kf = float(nsel)

    def bit_threshold():
        c0 = count(lambda kt, pos: kt >= 0)
        nonneg = c0 >= kf
        prefix = jnp.where(nonneg, 0, INT_MIN).astype(I32)

        def bit_body(it, carry):
            prefix, cge = carry
            cand = prefix | jnp.left_shift(jnp.int32(1), 30 - it)
            cnt = count(lambda kt, pos: kt >= cand)
            take = cnt >= kf
            return jnp.where(take, cand, prefix), jnp.where(take, cnt, cge)

        return lax.fori_loop(0, 31, bit_body, (prefix, jnp.where(nonneg, c0, kf)))

    v, cge = (threshold or bit_threshold)()
    need = (cge > kf) & (v != INT_MIN) & query_ok

    def bit_tie_limit(v, want):
        def body(it, x):
            cand = x | jnp.left_shift(jnp.int32(1), POS_BITS - 1 - it)
            cnt = count(lambda kt, pos: (kt == v) & (pos < cand))
            return jnp.where(cnt < want, cand, x)
        return lax.fori_loop(0, POS_BITS, body, jnp.zeros(stat_shape, I32))

    def tie_search():
        want = kf - count(lambda kt, pos: kt > v)
        limit = (tie_limit or bit_tie_limit)(v, want)
        return jnp.where(need, limit, INT_MAX)

    any_need = jnp.max(jnp.where(need, 1.0, 0.0)) > 0.0
    jlim = lax.cond(any_need, tie_search, lambda: jnp.full(stat_shape, INT_MAX, I32))
    return v, jlim


WIDE = 4 * LANES
PAIRS = N_HEADS // 2
COUNT_ROWS = 8 * SUBLANES


def _attn_prompt_kernel(far_ref, qidx_ref, kw_ref, q_ref, kidx_ref, k_ref, vt_ref, bd_ref, bs_ref, o_ref,
                        keys_ref, qit_ref, qz_ref, m_ref, l_ref, acc_ref, *, tq, nsel):
    i = pl.program_id(0)
    qpos = i * tq + lax.broadcasted_iota(I32, (1, tq), 1)
    nwide = i // (WIDE // LANES) + 1

    qi_t = jnp.transpose(qidx_ref[...].astype(F32)).astype(BF16)
    for hh in range(IDX_HEADS):
        qit_ref[:, hh * tq:(hh + 1) * tq] = qi_t[hh * IDX_DIM:(hh + 1) * IDX_DIM, :]
    w_t = jnp.transpose(kw_ref[...])[IDX_DIM:IDX_DIM + IDX_HEADS, :] * IDX_SCALE
    q_t = jnp.transpose(q_ref[...].astype(F32) * ATT_SCALE).astype(BF16)
    upper = lax.broadcasted_iota(I32, (LANES, tq), 0) < HEAD_DIM
    zero = jnp.zeros((LANES, tq), BF16)
    for p in range(PAIRS):
        blk = q_t[p * LANES:(p + 1) * LANES, :]
        qz_ref[p, :, 0:tq] = jnp.where(upper, blk, zero)
        qz_ref[p, :, tq:2 * tq] = jnp.where(upper, zero, blk)

    def score_body(c, carry):
        off = pl.multiple_of(c * WIDE, WIDE)
        kt = kidx_ref[pl.ds(off, WIDE), 0:IDX_DIM]
        sc = _dot(kt, qit_ref[...])
        s = jnp.zeros((WIDE, tq), F32)
        for hh in range(IDX_HEADS):
            s = s + jnp.maximum(sc[:, hh * tq:(hh + 1) * tq], 0.0) * w_t[hh:hh + 1, :]
        kpos = off + lax.broadcasted_iota(I32, (WIDE, tq), 0)
        keys_ref[pl.ds(off, WIDE), :] = jnp.where(kpos <= qpos, _mono_key(s), INT_MIN)
        return carry

    lax.fori_loop(0, nwide, score_body, 0)

    def count(pred):
        def body(c, acc):
            off = pl.multiple_of(c * WIDE, WIDE)
            kt = keys_ref[pl.ds(off, WIDE), :]
            kpos = off + lax.broadcasted_iota(I32, (WIDE, tq), 0)
            hit = jnp.where(pred(kt, kpos), 1.0, 0.0)
            return acc + jnp.sum(hit.reshape(WIDE // COUNT_ROWS, COUNT_ROWS, tq), axis=0)
        acc = lax.fori_loop(0, nwide, body, jnp.zeros((COUNT_ROWS, tq), F32))
        return jnp.sum(acc, axis=0, keepdims=True)


    def tie_limit(v, want):
        lower = (lax.broadcasted_iota(I32, (WIDE, WIDE), 1) <= lax.broadcasted_iota(I32, (WIDE, WIDE), 0))
        lower = jnp.where(lower, 1.0, 0.0).astype(BF16)

        def body(c, carry):
            base, limit = carry
            off = pl.multiple_of(c * WIDE, WIDE)
            eq = keys_ref[pl.ds(off, WIDE), :] == v
            rank = base + _dot(lower, jnp.where(eq, 1.0, 0.0).astype(BF16))
            kpos = (off + lax.broadcasted_iota(I32, (WIDE, tq), 0)).astype(F32)
            hit = jnp.max(jnp.where(eq & (rank <= want), kpos, -1.0), axis=0, keepdims=True)
            return rank[WIDE - 1:WIDE, :], jnp.maximum(limit, hit)

        zero = jnp.zeros((1, tq), F32)
        return lax.fori_loop(0, nwide, body, (zero, zero - 1.0))[1].astype(I32)

    thr, jlim = _select(count, (1, tq), nsel, True, tie_limit)

    m_ref[...] = jnp.full(m_ref.shape, M_INIT, F32)
    l_ref[...] = jnp.zeros(l_ref.shape, F32)
    acc_ref[...] = jnp.zeros(acc_ref.shape, F32)

    def tile(off, height, bias_ref):
        kt = keys_ref[pl.ds(off, height), :]
        kpos = off + lax.broadcasted_iota(I32, (height, tq), 0)
        sel = ((kt > thr) | ((kt == thr) & (kpos <= jlim))) & (kpos <= qpos)
        amask = jnp.where(sel, 0.0, NEG)
        def scores(p):
            return _dot(k_ref[pl.ds(off, height), p * LANES:(p + 1) * LANES], qz_ref[p])

        def softmax(p, st):
            out = []
            for e in range(2):
                h = 2 * p + e
                lg = st[:, e * tq:(e + 1) * tq] + amask
                if bias_ref is not None:
                    lg = lg + bias_ref[h]
                m_old = m_ref[h]
                m_new = jnp.maximum(m_old, jnp.max(lg, axis=0, keepdims=True))
                alpha = jnp.exp(m_old - m_new)
                pr = jnp.exp(lg - m_new[0:1, :])
                l_ref[h] = alpha * l_ref[h] + jnp.sum(pr, axis=0, keepdims=True)
                m_ref[h] = m_new
                out.append((h, pr.astype(BF16), alpha[0:1, :]))
            return out

        def values(items):
            for h, pr, alpha in items:
                pv = _dot(vt_ref[h * HEAD_DIM:(h + 1) * HEAD_DIM, pl.ds(off, height)], pr)
                acc_ref[h] = alpha * acc_ref[h] + pv

        sts = {0: scores(0), 1: scores(1)}
        done = None
        for p in range(PAIRS):
            cur = softmax(p, sts.pop(p))
            if p + 2 < PAIRS:
                sts[p + 2] = scores(p + 2)
            if done is not None:
                values(done)
            done = cur
        values(done)

    nfar = jnp.maximum(i - 1, 0)
    nfar_wide = nfar // (WIDE // LANES)

    def far_wide(c, carry):
        tile(pl.multiple_of(c * WIDE, WIDE), WIDE, None)
        return carry

    def far_narrow(j, carry):
        tile(pl.multiple_of(j * LANES, LANES), LANES, None)
        return carry

    lax.fori_loop(0, nfar_wide, far_wide, 0)
    lax.fori_loop(nfar_wide * (WIDE // LANES), nfar, far_narrow, 0)
    for h in range(N_HEADS):
        m_ref[h] = m_ref[h] + far_ref[h]

    @pl.when(i >= 1)
    def _():
        tile(pl.multiple_of((i - 1) * LANES, LANES), LANES, bs_ref)

    tile(pl.multiple_of(i * LANES, LANES), LANES, bd_ref)
    out_t = jnp.concatenate([acc_ref[h] / l_ref[h][0:1, :] for h in range(N_HEADS)], axis=0)
    o_ref[...] = jnp.transpose(out_t)


def _attn_prompt(far, h2, hb, vt, bias_diag_t, bias_sub_t, nsel):
    t = h2.shape[0]
    tq = LANES
    nq = IDX_HEADS * IDX_DIM
    return pl.pallas_call(
        functools.partial(_attn_prompt_kernel, tq=tq, nsel=nsel),
        grid=(t // tq,),
        in_specs=[pl.BlockSpec(memory_space=pltpu.SMEM),
                  pl.BlockSpec((tq, nq), lambda i: (i, C_QIDX // nq)),
                  pl.BlockSpec((tq, LANES), lambda i: (i, C_KW // LANES)),
                  pl.BlockSpec((tq, MIX_DIM), lambda i: (i, C_Q // MIX_DIM)),
                  pl.BlockSpec((t, LANES), lambda i: (0, C_KW // LANES)),
                  pl.BlockSpec((t, MIX_DIM), lambda i: (0, C_K // MIX_DIM)),
                  pl.BlockSpec((MIX_DIM, t), lambda i: (0, 0)),
                  pl.BlockSpec((N_HEADS, LANES, tq), lambda i: (0, 0, 0)),
                  pl.BlockSpec((N_HEADS, LANES, tq), lambda i: (0, 0, 0))],
        out_specs=pl.BlockSpec((tq, MIX_DIM), lambda i: (i, 0)),
        out_shape=jax.ShapeDtypeStruct((t, MIX_DIM), F32),
        scratch_shapes=[pltpu.VMEM((t, tq), I32),
                        pltpu.VMEM((IDX_DIM, IDX_HEADS * tq), BF16),
                        pltpu.VMEM((PAIRS, LANES, 2 * tq), BF16),
                        pltpu.VMEM((N_HEADS, SUBLANES, tq), F32),
                        pltpu.VMEM((N_HEADS, SUBLANES, tq), F32),
                        pltpu.VMEM((N_HEADS, HEAD_DIM, tq), F32)],
        compiler_params=_cparams(("parallel",)),
        name="attn_prompt",
    )(far, hb, h2, hb, hb, hb, vt, bias_diag_t, bias_sub_t)


QROWS = 8
SEL_PAGES = 16
ATT_PAGES = 16


SEL_ROWS = 8 * QROWS


def _score_sample_kernel(pt_ref, qst_ref, wcol_ref, *refs, treal):
    pages = refs[:SEL_PAGES]
    knew_ref, keys_ref = refs[SEL_PAGES:]
    j = pl.program_id(1)
    ntiles = keys_ref.shape[2] // LANES

    def score(kt_t):
        r = jnp.maximum(_dot(qst_ref[0], kt_t), 0.0) * (wcol_ref[0] * IDX_SCALE)
        s = r[0:QROWS]
        for hh in range(1, IDX_HEADS):
            s = s + r[hh * QROWS:(hh + 1) * QROWS]
        return _mono_key(s)

    for g in range(SEL_PAGES):
        off = pl.multiple_of((j * SEL_PAGES + g) * LANES, LANES)
        keys_ref[0, :, pl.ds(off, LANES)] = score(pages[g][0, 0].astype(BF16))

    @pl.when(j == pl.num_programs(1) - 1)
    def _():
        rowq = lax.broadcasted_iota(I32, (QROWS, LANES), 0)
        lane = lax.broadcasted_iota(I32, (QROWS, LANES), 1)
        knew = jnp.where((lane < treal) & (lane <= rowq), score(knew_ref[0]), INT_MIN)
        keys_ref[0, :, (ntiles - 1) * LANES:] = knew


def _select_sample_kernel(keys_ref, mask_ref, *, nsel, treal):
    rows, nk = keys_ref.shape
    ntiles = nk // LANES
    lane = lax.broadcasted_iota(I32, (rows, LANES), 1)

    def count(pred):
        acc = jnp.zeros((rows, LANES), F32)
        for jt in range(ntiles):
            kt = keys_ref[:, jt * LANES:(jt + 1) * LANES]
            acc = acc + jnp.where(pred(kt, jt * LANES + lane), 1.0, 0.0)
        return jnp.sum(acc, axis=1, keepdims=True)

    def tie_limit(v, want):
        upper = (lax.broadcasted_iota(I32, (LANES, LANES), 0) <= lax.broadcasted_iota(I32, (LANES, LANES), 1))
        upper = jnp.where(upper, 1.0, 0.0).astype(BF16)
        eqs = [keys_ref[:, jt * LANES:(jt + 1) * LANES] == v for jt in range(ntiles)]
        ranks = [_dot(jnp.where(eq, 1.0, 0.0).astype(BF16), upper) for eq in eqs]
        base = jnp.zeros((rows, 1), F32)
        best = jnp.full((rows, LANES), -1.0, F32)
        for jt in range(ntiles):
            ok = eqs[jt] & (base + ranks[jt] <= want)
            best = jnp.maximum(best, jnp.where(ok, (jt * LANES + lane).astype(F32), -1.0))
            base = base + ranks[jt][:, LANES - 1:LANES]
        return jnp.max(best, axis=1, keepdims=True).astype(I32)

    row_ok = lax.broadcasted_iota(I32, (rows, 1), 0) % QROWS < treal
    thr, jlim = _select(count, (rows, 1), nsel, row_ok, tie_limit)
    for jt in range(ntiles):
        kt = keys_ref[:, jt * LANES:(jt + 1) * LANES]
        sel = (kt > thr) | ((kt == thr) & (jt * LANES + lane <= jlim))
        mask_ref[:, jt * LANES:(jt + 1) * LANES] = jnp.where(sel, 1.0, 0.0)


def _sel_sample(page_table, qst, wcol, cache_kidx, knew, layer, nsel, treal):
    b, npages = page_table.shape
    nk = (npages + 1) * PAGE
    page_spec = lambda g: pl.BlockSpec(
        (1, 1, IDX_DIM, PAGE), lambda i, j, pt: (layer, pt[i, j * SEL_PAGES + g], 0, 0))
    grid_spec = pltpu.PrefetchScalarGridSpec(
        num_scalar_prefetch=1,
        grid=(b, npages // SEL_PAGES),
        in_specs=[pl.BlockSpec((1, IDX_HEADS * QROWS, IDX_DIM), lambda i, j, pt: (i, 0, 0)),
                  pl.BlockSpec((1, IDX_HEADS * QROWS, 1), lambda i, j, pt: (i, 0, 0))]
                 + [page_spec(g) for g in range(SEL_PAGES)]
                 + [pl.BlockSpec((1, IDX_DIM, PAGE), lambda i, j, pt: (i, 0, 0))],
        out_specs=pl.BlockSpec((1, QROWS, nk), lambda i, j, pt: (i, 0, 0)),
    )
    keys = pl.pallas_call(
        functools.partial(_score_sample_kernel, treal=treal),
        grid_spec=grid_spec,
        out_shape=jax.ShapeDtypeStruct((b, QROWS, nk), I32),
        compiler_params=_cparams(("parallel", "arbitrary")),
        name="score_sample",
    )(page_table, qst, wcol, *([cache_kidx] * SEL_PAGES), knew)
    rows = b * QROWS
    blk = math.gcd(rows, SEL_ROWS)
    mask = pl.pallas_call(
        functools.partial(_select_sample_kernel, nsel=nsel, treal=treal),
        grid=(rows // blk,),
        in_specs=[pl.BlockSpec((blk, nk), lambda i: (i, 0))],
        out_specs=pl.BlockSpec((blk, nk), lambda i: (i, 0)),
        out_shape=jax.ShapeDtypeStruct((rows, nk), F32),
        compiler_params=_cparams(("parallel",)),
        name="select_sample",
    )(keys.reshape(rows, nk))
    return mask.reshape(b, QROWS, nk)


def _attn_sample_kernel(pt_ref, far_ref, qh_ref, *refs):
    kp = refs[:ATT_PAGES]
    vp = refs[ATT_PAGES:2 * ATT_PAGES]
    mask_ref, masknew_ref, blast_ref, bnew_ref, knew_ref, vnew_ref, o_ref, m_ref, l_ref, acc_ref = refs[2 * ATT_PAGES:]
    j = pl.program_id(1)
    is_last = j == pl.num_programs(1) - 1

    @pl.when(j == 0)
    def _():
        m_ref[...] = jnp.full(m_ref.shape, M_INIT, F32)
        l_ref[...] = jnp.zeros(l_ref.shape, F32)
        acc_ref[...] = jnp.zeros(acc_ref.shape, F32)

    def update(lgs, vhs):
        prs, alphas = [], []
        for h in range(N_HEADS):
            m_old = m_ref[h]
            m_new = jnp.maximum(m_old, jnp.max(lgs[h], axis=1, keepdims=True))
            alpha = jnp.exp(m_old - m_new)
            pr = jnp.exp(lgs[h] - m_new[:, 0:1])
            l_ref[h] = alpha * l_ref[h] + jnp.sum(pr, axis=1, keepdims=True)
            m_ref[h] = m_new
            prs.append(pr.astype(BF16))
            alphas.append(alpha[:, :HEAD_DIM])
        for h in range(N_HEADS):
            acc_ref[h] = alphas[h] * acc_ref[h] + _dot_nt(prs[h], vhs[h])

    def q_of(h):
        return (qh_ref[0, h].astype(F32) * ATT_SCALE).astype(BF16)

    def head_t(refs_, h):
        return jnp.concatenate([refs_[g][0, 0, h] for g in range(ATT_PAGES)], axis=1).astype(BF16)

    amask = jnp.where(mask_ref[0] > 0.5, 0.0, NEG)
    lgs = []
    for h in range(N_HEADS):
        far = jnp.full((QROWS, LANES), far_ref[h], F32)
        bias = jnp.concatenate([far] * (ATT_PAGES - 1) + [jnp.where(is_last, blast_ref[h], far)], axis=1)
        lgs.append(_dot(q_of(h), head_t(kp, h)) + amask + bias)
    update(lgs, [head_t(vp, h) for h in range(N_HEADS)])

    @pl.when(is_last)
    def _():
        amask_new = jnp.where(masknew_ref[0] > 0.5, 0.0, NEG)
        lgs_new = [_dot(q_of(h), knew_ref[0, h]) + amask_new + bnew_ref[h] for h in range(N_HEADS)]
        update(lgs_new, [vnew_ref[0, h] for h in range(N_HEADS)])
        outs = []
        for h in range(N_HEADS):
            lh = l_ref[h][:, :HEAD_DIM]
            outs.append(acc_ref[h] / jnp.where(lh > 0.0, lh, 1.0))
        o_ref[0] = jnp.concatenate(outs, axis=1)


def _attn_sample(page_table, far, qh, cache_k2, cache_v2, mask, bias_last, bias_new, knew, vnew, layer):
    b, npages = page_table.shape
    nsteps = npages // ATT_PAGES
    page_spec = lambda g: pl.BlockSpec(
        (1, 1, N_HEADS, HEAD_DIM, PAGE), lambda i, j, pt, far: (layer, pt[i, j * ATT_PAGES + g], 0, 0, 0))
    const3 = lambda i, j, pt, far: (0, 0, 0)
    per_b4 = lambda i, j, pt, far: (i, 0, 0, 0)
    grid_spec = pltpu.PrefetchScalarGridSpec(
        num_scalar_prefetch=2,
        grid=(b, nsteps),
        in_specs=[pl.BlockSpec((1, N_HEADS, QROWS, HEAD_DIM), per_b4)]
                 + [page_spec(g) for g in range(ATT_PAGES)] * 2
                 + [pl.BlockSpec((1, QROWS, ATT_PAGES * PAGE), lambda i, j, pt, far: (i, 0, j)),
                    pl.BlockSpec((1, QROWS, PAGE), lambda i, j, pt, far: (i, 0, npages)),
                    pl.BlockSpec((N_HEADS, QROWS, PAGE), const3),
                    pl.BlockSpec((N_HEADS, QROWS, PAGE), const3),
                    pl.BlockSpec((1, N_HEADS, HEAD_DIM, PAGE), per_b4),
                    pl.BlockSpec((1, N_HEADS, HEAD_DIM, PAGE), per_b4)],
        out_specs=pl.BlockSpec((1, QROWS, MIX_DIM), lambda i, j, pt, far: (i, 0, 0)),
        scratch_shapes=[pltpu.VMEM((N_HEADS, QROWS, LANES), F32),
                        pltpu.VMEM((N_HEADS, QROWS, LANES), F32),
                        pltpu.VMEM((N_HEADS, QROWS, HEAD_DIM), F32)],
    )
    return pl.pallas_call(
        _attn_sample_kernel,
        grid_spec=grid_spec,
        out_shape=jax.ShapeDtypeStruct((b, QROWS, MIX_DIM), F32),
        compiler_params=_cparams(("parallel", "arbitrary")),
        name="attn_sample",
    )(page_table, far, qh, *([cache_k2] * ATT_PAGES), *([cache_v2] * ATT_PAGES), mask, mask,
      bias_last, bias_new, knew, vnew)


def _bucket_table(n):
    d = np.arange(n)
    nf = np.maximum(d, 1).astype(np.float32)
    half = REL_BUCKETS // 2
    large = half + (np.log(nf / np.float32(half)) / np.float32(math.log(REL_MAX_DIST / half))
                    * np.float32(REL_BUCKETS - half)).astype(np.int32)
    return np.where(d < half, d, np.minimum(large, REL_BUCKETS - 1)).astype(np.int32)


def _pack_w_in(w):
    w_t = jnp.transpose(w, (0, 2, 1))
    p, z, xbc, dt, q, k, v, qi, ki, wi, u, vs = jnp.split(w_t, np.cumsum(IN_SPLITS)[:-1].tolist(), axis=1)
    zeros = lambda n: jnp.zeros((w.shape[0], n, w.shape[1]), w.dtype)
    packed = jnp.concatenate(
        [p, z, xbc, q, k, v, u, vs, qi, ki, wi, zeros(LANES - IDX_DIM - IDX_HEADS), dt, zeros(LANES - N_HEADS)],
        axis=1)
    return packed.astype(BF16)


def _pad_lanes(x):
    return jnp.pad(x.reshape(1, -1), ((0, 0), (0, LANES - x.shape[-1])))


def _pad_axis(x, axis, size):
    pads = [(0, 0)] * x.ndim
    pads[axis] = (0, size - x.shape[axis])
    return jnp.pad(x, pads)


def kernel(x_prompt, x_sample, state_pool, state_conv, state_ssm, cache_k, cache_v, cache_kidx, page_table,
           rel_bias, norm_mix, w_in, pool_w, pool_scale, conv_w, conv_b, dt_bias, a_log, d_skip, ssm_norm,
           sgu_w, sgu_b, w_out, norm_mlp, mlp_w1, mlp_w2, norm_final):
    depth = w_in.shape[0]
    bp, t, _ = x_prompt.shape
    bs, ts, _ = x_sample.shape
    npages = page_table.shape[1]
    past = npages * PAGE
    n_pool = cache_k.shape[1]
    assert bp == 1 and t % WIDE == 0 and npages % SEL_PAGES == 0 and npages % ATT_PAGES == 0
    assert max(t, past + PAGE) <= 2 ** POS_BITS and ts <= QROWS
    tc = LANES
    nsel_p = min(TOPK_MAX, t // 4)
    nsel_s = min(TOPK_MAX, (past + ts) // 4)

    bucket = _bucket_table(2 * LANES)
    far = rel_bias[REL_BUCKETS - 1]
    kk = np.arange(LANES)[:, None]
    qq = np.arange(LANES)[None, :]
    def bias_of(dist):
        onehot = np.eye(REL_BUCKETS, dtype=np.float32)[bucket[dist].reshape(-1)]
        vals = jnp.dot(onehot, rel_bias, precision=lax.Precision.HIGHEST)
        return jnp.transpose(vals).reshape((N_HEADS,) + dist.shape)

    bias_diag_t = bias_of(np.maximum(qq - kk, 0))
    bias_sub_t = bias_of(np.minimum(LANES + qq - kk, 2 * LANES - 1))
    sq = np.arange(QROWS)[:, None]
    sk = np.arange(PAGE)[None, :]
    bias_last = bias_of(np.minimum(PAGE + sq - sk, 2 * LANES - 1))
    bias_new = bias_of(np.maximum(sq - sk, 0))

    cache_k2 = jnp.transpose(cache_k, (0, 1, 3, 4, 2))
    cache_v2 = jnp.transpose(cache_v, (0, 1, 3, 4, 2))
    cache_kidx2 = jnp.transpose(cache_kidx, (0, 1, 3, 2))
    w_in_t = _pack_w_in(w_in)
    w_out_b, w1_b, w2_b = w_out.astype(BF16), mlp_w1.astype(BF16), mlp_w2.astype(BF16)

    xp = x_prompt.reshape(t, D_MODEL)
    xs = x_sample.reshape(bs * ts, D_MODEL)
    zero_pool = jnp.zeros((1, POOL_HIST, MIX_DIM), F32)
    zero_conv = jnp.zeros((1, CONV_W - 1, XBC_DIM), F32)
    zero_ssm = jnp.zeros((1, N_HEADS, HEAD_DIM, SSM_STATE), F32)
    tm_p = min(1024, t)
    tm_s = bs * ts
    outs_p, outs_s = [], []

    for l in range(depth):
        g_mix = norm_mix[l].reshape(1, D_MODEL)
        g_mlp = norm_mlp[l].reshape(1, D_MODEL)
        g_out = norm_final.reshape(1, D_MODEL) if l == depth - 1 else g_mlp
        pscale = pool_scale[l].reshape(1, MIX_DIM)
        cb = conv_b[l].reshape(1, XBC_DIM)
        dtb, alog, dsk = _pad_lanes(dt_bias[l]), _pad_lanes(a_log[l]), _pad_lanes(d_skip[l])
        snorm = ssm_norm[l].reshape(1, MIX_DIM)
        sgu_b_col = sgu_b[l][:, :, None]
        final = l == depth - 1

        hp, hpb, vt = _inproj(xp, g_mix, w_in_t, l, tm=min(1024, t), tn=1024, attn_operands=True)
        hp3 = hp.reshape(1, t, H_PACKED)
        y_pool, new_pool = _pool(hp3, zero_pool, pool_w[l], pscale, tc=tc, treal=tc, pos0=0)
        y_ssm, new_conv, new_h = _ssd(hp3, zero_conv, zero_ssm, conv_w[l], cb, dtb, alog, dsk, snorm,
                                      tc=tc, treal=tc)
        y_sgu, _ = _sgu(hp3, sgu_w[l], sgu_b_col, tc=tc, treal=tc)
        kf = hp[:, C_K:C_K + MIX_DIM]
        vf = hp[:, C_V:C_V + MIX_DIM]
        kif = hp[:, C_KW:C_KW + IDX_DIM]
        y_att = _attn_prompt(far, hp, hpb, vt, bias_diag_t, bias_sub_t, nsel_p)
        x1, xn = _outproj(xp, [y_pool[0], y_ssm[0], y_att, y_sgu[0]], w_out_b, l, g_mlp, tm=256)
        xp = _mlp(xn, w1_b, w2_b, l, x1, g_out, tm=tm_p, final_norm=final)
        outs_p.append((new_pool, new_conv, new_h, kf.reshape(1, t, N_HEADS, HEAD_DIM),
                       vf.reshape(1, t, N_HEADS, HEAD_DIM), kif.reshape(1, t, IDX_DIM)))

        hs, = _inproj(xs, g_mix, w_in_t, l, tm=tm_s)
        hs3 = hs.reshape(bs, ts, H_PACKED)
        y_pool, new_pool = _pool(hs3, state_pool[l], pool_w[l], pscale, tc=tc, treal=ts, pos0=past)
        y_ssm, new_conv, new_h = _ssd(hs3, state_conv[l], state_ssm[l], conv_w[l], cb, dtb, alog, dsk, snorm,
                                      tc=tc, treal=ts)
        y_sgu, v_rows = _sgu(hs3, sgu_w[l], sgu_b_col, tc=tc, treal=ts)
        kf = hs3[:, :, C_K:C_K + MIX_DIM]
        vf = hs3[:, :, C_V:C_V + MIX_DIM]
        kif = hs3[:, :, C_KW:C_KW + IDX_DIM]
        qi = hs3[:, :, C_QIDX:C_QIDX + IDX_HEADS * IDX_DIM].reshape(bs, ts, IDX_HEADS, IDX_DIM)
        qst = _pad_axis(jnp.transpose(qi, (0, 2, 1, 3)), 2, QROWS)
        qst = qst.reshape(bs, IDX_HEADS * QROWS, IDX_DIM).astype(BF16)
        wi = hs3[:, :, C_KW + IDX_DIM:C_KW + IDX_DIM + IDX_HEADS]
        wcol = _pad_axis(jnp.transpose(wi, (0, 2, 1)), 2, QROWS).reshape(bs, IDX_HEADS * QROWS, 1)
        kinew_t = _pad_axis(jnp.transpose(kif, (0, 2, 1)), 2, PAGE).astype(BF16)
        mask = _sel_sample(page_table, qst, wcol, cache_kidx2, kinew_t, l, nsel_s, ts)
        heads = lambda x, perm: jnp.transpose(x.reshape(bs, ts, N_HEADS, HEAD_DIM), perm)
        qh = _pad_axis(heads(hs3[:, :, C_Q:C_Q + MIX_DIM], (0, 2, 1, 3)), 2, QROWS).astype(BF16)
        knew = _pad_axis(heads(kf, (0, 2, 3, 1)), 3, PAGE).astype(BF16)
        vnew = _pad_axis(heads(vf, (0, 2, 3, 1)), 3, PAGE).astype(BF16)
        att = _attn_sample(page_table, far, qh, cache_k2, cache_v2, mask, bias_last, bias_new, knew, vnew, l)
        y_att = att[:, :ts].reshape(bs * ts, MIX_DIM)
        flat = lambda y: y.reshape(bs * ts, MIX_DIM)
        x1, xn = _outproj(xs, [flat(y_pool), flat(y_ssm), y_att, flat(y_sgu)], w_out_b, l, g_mlp, tm=tm_s)
        xs = _mlp(xn, w1_b, w2_b, l, x1, g_out, tm=tm_s, final_norm=final)
        outs_s.append((new_pool, new_conv, new_h, kf.reshape(bs, ts, N_HEADS, HEAD_DIM),
                       vf.reshape(bs, ts, N_HEADS, HEAD_DIM), kif, v_rows))

    stack = lambda outs, i: jnp.stack([o[i] for o in outs])
    return (xp.reshape(1, t, D_MODEL), xs.reshape(bs, ts, D_MODEL),
            *[stack(outs_p, i) for i in range(6)],
            *[stack(outs_s, i) for i in range(7)])
```

```python
import functools
import math

import numpy as np
import jax
import jax.numpy as jnp
from jax import lax
from jax.experimental import pallas as pl
from jax.experimental.pallas import tpu as pltpu

F32 = jnp.float32
BF16 = jnp.bfloat16
I32 = jnp.int32

LANES = 128
SUBLANES = 8
D_MODEL = 2048
N_HEADS = 8
HEAD_DIM = 64
MIX_DIM = 512
POOL_WINDOWS = (2, 4, 8, 16)
POOL_HIST = 15
CONV_W = 4
SSM_GROUPS = 2
SSM_STATE = 128
XBC_DIM = MIX_DIM + 2 * SSM_GROUPS * SSM_STATE
IDX_HEADS = 4
IDX_DIM = 64
TOPK_MAX = 256
PAGE = 128
REL_BUCKETS = 32
REL_MAX_DIST = 128
ATT_SCALE = HEAD_DIM ** -0.5
IDX_SCALE = (IDX_HEADS * IDX_DIM) ** -0.5
SGU_GROUPS = 4
D_FF = 4 * D_MODEL
EPS = 1e-6
NEG = -1e30
INT_MIN = -(2 ** 31)
INT_MAX = 2 ** 31 - 1
IN_SPLITS = (512, 512, 1024, 8, 512, 512, 512, 256, 64, 4, 512, 512)

C_P, C_Z, C_XBC, C_Q, C_K, C_V, C_U, C_VS, C_QIDX, C_KW, C_DT = (
    0, 512, 1024, 2048, 2560, 3072, 3584, 4096, 4608, 4864, 4992)
H_PACKED = 5120
VMEM_LIMIT = 56 * 1024 * 1024


def _cparams(sem):
    return pltpu.CompilerParams(dimension_semantics=sem, vmem_limit_bytes=VMEM_LIMIT)


def _dot(a, b):
    return jnp.dot(a, b, preferred_element_type=F32)


def _dot_nt(a, b):
    return lax.dot_general(a, b, (((1,), (1,)), ((), ())), preferred_element_type=F32)


def _dot_tn(a, b):
    return lax.dot_general(a, b, (((0,), (0,)), ((), ())), preferred_element_type=F32)


def _silu(x):
    return x * (1.0 / (1.0 + jnp.exp(-x)))


def _gelu_tanh(x):
    return 0.5 * x * (1.0 + jnp.tanh(math.sqrt(2.0 / math.pi) * (x + 0.044715 * (x * x * x))))


def _softplus(x):
    return jnp.maximum(x, 0.0) + jnp.log1p(jnp.exp(-jnp.abs(x)))


def _rms(x, g):
    ms = jnp.mean(x * x, axis=-1, keepdims=True)
    return x * lax.rsqrt(ms + EPS) * g


def _inproj_kernel(x_ref, g_ref, w_ref, o_ref, *rest, attn_operands, tn):
    xn_ref = rest[-1]

    @pl.when(pl.program_id(1) == 0)
    def _():
        xn_ref[...] = _rms(x_ref[...], g_ref[...]).astype(BF16)

    h = _dot_nt(xn_ref[...], w_ref[0])
    o_ref[...] = h
    if attn_operands:
        ob_ref, vt_ref = rest[:2]
        ob_ref[...] = h.astype(BF16)

        @pl.when(pl.program_id(1) == C_V // tn)
        def _():
            vt_ref[...] = jnp.transpose(h[:, C_V % tn:C_V % tn + MIX_DIM]).astype(BF16)


def _inproj(x, g, w_t, layer, tm, tn=512, attn_operands=False):
    m = x.shape[0]
    n = w_t.shape[1]
    out_specs = [pl.BlockSpec((tm, tn), lambda i, j: (i, j))]
    out_shape = [jax.ShapeDtypeStruct((m, n), F32)]
    if attn_operands:
        out_specs += [pl.BlockSpec((tm, tn), lambda i, j: (i, j)), pl.BlockSpec((MIX_DIM, tm), lambda i, j: (0, i))]
        out_shape += [jax.ShapeDtypeStruct((m, n), BF16), jax.ShapeDtypeStruct((MIX_DIM, m), BF16)]
    return pl.pallas_call(
        functools.partial(_inproj_kernel, attn_operands=attn_operands, tn=tn),
        grid=(m // tm, n // tn),
        in_specs=[pl.BlockSpec((tm, D_MODEL), lambda i, j: (i, 0)),
                  pl.BlockSpec((1, D_MODEL), lambda i, j: (0, 0)),
                  pl.BlockSpec((1, tn, D_MODEL), lambda i, j: (layer, j, 0))],
        out_specs=out_specs,
        out_shape=out_shape,
        scratch_shapes=[pltpu.VMEM((tm, D_MODEL), BF16)],
        compiler_params=_cparams(("parallel", "arbitrary")),
        name="inproj",
    )(x, g, w_t)


def _outproj_kernel(x_ref, y0_ref, y1_ref, y2_ref, y3_ref, w_ref, g_ref, x1_ref, xn_ref):
    y = jnp.concatenate([y0_ref[...], y1_ref[...], y2_ref[...], y3_ref[...]], axis=1).astype(BF16)
    x1 = x_ref[...] + _dot(y, w_ref[0])
    x1_ref[...] = x1
    xn_ref[...] = _rms(x1, g_ref[...]).astype(BF16)


def _outproj(x, ys, w, layer, g, tm):
    m = x.shape[0]
    row = lambda i: (i, 0)
    return pl.pallas_call(
        _outproj_kernel,
        grid=(m // tm,),
        in_specs=[pl.BlockSpec((tm, D_MODEL), row)]
                 + [pl.BlockSpec((tm, MIX_DIM), row)] * 4
                 + [pl.BlockSpec((1, D_MODEL, D_MODEL), lambda i: (layer, 0, 0)),
                    pl.BlockSpec((1, D_MODEL), lambda i: (0, 0))],
        out_specs=[pl.BlockSpec((tm, D_MODEL), row), pl.BlockSpec((tm, D_MODEL), row)],
        out_shape=[jax.ShapeDtypeStruct((m, D_MODEL), F32), jax.ShapeDtypeStruct((m, D_MODEL), BF16)],
        compiler_params=_cparams(("parallel",)),
        name="outproj",
    )(x, *ys, w, g)


def _mlp_kernel(xn_ref, w1_ref, w2_ref, x1_ref, g_ref, o_ref, *, final_norm):
    j = pl.program_id(1)

    @pl.when(j == 0)
    def _():
        o_ref[...] = x1_ref[...]

    hid = _dot(xn_ref[...], w1_ref[0])
    hid = jnp.square(jnp.maximum(hid, 0.0)).astype(BF16)
    o_ref[...] += _dot(hid, w2_ref[0])

    if final_norm:
        @pl.when(j == pl.num_programs(1) - 1)
        def _():
            o_ref[...] = _rms(o_ref[...], g_ref[...])


def _mlp(xn, w1, w2, layer, x1, g, tm, final_norm, tf=512):
    m = xn.shape[0]
    return pl.pallas_call(
        functools.partial(_mlp_kernel, final_norm=final_norm),
        grid=(m // tm, D_FF // tf),
        in_specs=[pl.BlockSpec((tm, D_MODEL), lambda i, j: (i, 0)),
                  pl.BlockSpec((1, D_MODEL, tf), lambda i, j: (layer, 0, j)),
                  pl.BlockSpec((1, tf, D_MODEL), lambda i, j: (layer, j, 0)),
                  pl.BlockSpec((tm, D_MODEL), lambda i, j: (i, 0)),
                  pl.BlockSpec((1, D_MODEL), lambda i, j: (0, 0))],
        out_specs=pl.BlockSpec((tm, D_MODEL), lambda i, j: (i, 0)),
        out_shape=jax.ShapeDtypeStruct((m, D_MODEL), F32),
        compiler_params=_cparams(("parallel", "arbitrary")),
        name="mlp",
    )(xn, w1, w2, x1, g)


def _pool_kernel(p_ref, hist_ref, w_ref, scale_ref, y_ref, newhist_ref, ext_ref, *, tc, treal, pos0):
    c = pl.program_id(1)
    hrows = POOL_HIST + 1

    @pl.when(c == 0)
    def _():
        ext_ref[0:1, :] = jnp.zeros((1, MIX_DIM), F32)
        ext_ref[1:hrows, :] = hist_ref[0]

    if treal < tc:
        ext_ref[hrows:hrows + tc, :] = jnp.zeros((tc, MIX_DIM), F32)
    ext_ref[hrows:hrows + treal, :] = p_ref[0]

    pos = pos0 + c * treal + lax.broadcasted_iota(I32, (tc, 1), 0)
    for g, win in enumerate(POOL_WINDOWS):
        cols = slice(g * LANES, (g + 1) * LANES)
        cur = ext_ref[hrows:hrows + tc, cols]
        s = cur
        for j in range(1, win):
            s = s + ext_ref[hrows - j:hrows - j + tc, cols]
        cnt = jnp.minimum(pos + 1, win).astype(F32)
        diff = s / cnt - cur
        y = _dot(diff, w_ref[g]) * scale_ref[:, cols]
        y_ref[0, :, cols] = y[:treal]

    newhist_ref[0] = ext_ref[treal + 1:treal + hrows, :]
    ext_ref[0:hrows, :] = ext_ref[treal:treal + hrows, :]


def _pool(h3, hist, pool_w, pool_scale, *, tc, treal, pos0):
    b, t, _ = h3.shape
    return pl.pallas_call(
        functools.partial(_pool_kernel, tc=tc, treal=treal, pos0=pos0),
        grid=(b, t // treal),
        in_specs=[pl.BlockSpec((1, treal, MIX_DIM), lambda i, c: (i, c, C_P // MIX_DIM)),
                  pl.BlockSpec((1, POOL_HIST, MIX_DIM), lambda i, c: (i, 0, 0)),
                  pl.BlockSpec((4, LANES, LANES), lambda i, c: (0, 0, 0)),
                  pl.BlockSpec((1, MIX_DIM), lambda i, c: (0, 0))],
        out_specs=[pl.BlockSpec((1, treal, MIX_DIM), lambda i, c: (i, c, 0)),
                   pl.BlockSpec((1, POOL_HIST, MIX_DIM), lambda i, c: (i, 0, 0))],
        out_shape=[jax.ShapeDtypeStruct((b, t, MIX_DIM), F32),
                   jax.ShapeDtypeStruct((b, POOL_HIST, MIX_DIM), F32)],
        scratch_shapes=[pltpu.VMEM((POOL_HIST + 1 + tc, MIX_DIM), F32)],
        compiler_params=_cparams(("parallel", "arbitrary")),
        name="pool",
    )(h3, hist, pool_w, pool_scale)


def _sgu_kernel(u_ref, v_ref, w_ref, b_ref, y_ref, vout_ref, ubuf_ref, vbuf_ref, *, tc, treal):
    if treal < tc:
        ubuf_ref[...] = jnp.zeros((tc, MIX_DIM), F32)
        vbuf_ref[...] = jnp.zeros((tc, MIX_DIM), F32)
    ubuf_ref[0:treal, :] = u_ref[0]
    vbuf_ref[0:treal, :] = v_ref[0]
    u = _gelu_tanh(ubuf_ref[...])
    v = _gelu_tanh(vbuf_ref[...])
    vout_ref[0] = v[:treal]
    r = lax.broadcasted_iota(I32, (tc, tc), 0)
    s = lax.broadcasted_iota(I32, (tc, tc), 1)
    for g in range(SGU_GROUPS):
        cols = slice(g * LANES, (g + 1) * LANES)
        w = jnp.where(s <= r, w_ref[g], 0.0)
        mixed = _dot(w, v[:, cols]) + b_ref[g]
        y_ref[0, :, cols] = (u[:, cols] * mixed)[:treal]


def _sgu(h3, sgu_w, sgu_b_col, *, tc, treal):
    b, t, _ = h3.shape
    return pl.pallas_call(
        functools.partial(_sgu_kernel, tc=tc, treal=treal),
        grid=(b, t // treal),
        in_specs=[pl.BlockSpec((1, treal, MIX_DIM), lambda i, c: (i, c, C_U // MIX_DIM)),
                  pl.BlockSpec((1, treal, MIX_DIM), lambda i, c: (i, c, C_VS // MIX_DIM)),
                  pl.BlockSpec((SGU_GROUPS, tc, tc), lambda i, c: (0, 0, 0)),
                  pl.BlockSpec((SGU_GROUPS, tc, 1), lambda i, c: (0, 0, 0))],
        out_specs=[pl.BlockSpec((1, treal, MIX_DIM), lambda i, c: (i, c, 0)),
                   pl.BlockSpec((1, treal, MIX_DIM), lambda i, c: (i, c, 0))],
        out_shape=[jax.ShapeDtypeStruct((b, t, MIX_DIM), F32),
                   jax.ShapeDtypeStruct((b, t, MIX_DIM), F32)],
        scratch_shapes=[pltpu.VMEM((tc, MIX_DIM), F32), pltpu.VMEM((tc, MIX_DIM), F32)],
        compiler_params=_cparams(("parallel", "parallel")),
        name="sgu",
    )(h3, h3, sgu_w, sgu_b_col)


def _cumsum_rows(a, n):
    row = lax.broadcasted_iota(I32, a.shape, 0)
    sh = 1
    while sh < n:
        a = a + jnp.where(row >= sh, pltpu.roll(a, sh, 0), 0.0)
        sh *= 2
    return a


def _ssd_kernel(z_ref, xbc_ref, dt_ref, hc_ref, h0_ref, convw_ref, convb_ref, dtb_ref, alog_ref, dskip_ref,
                norm_ref, y_ref, newconv_ref, newh_ref, ext_ref, zbuf_ref, dtbuf_ref, state_ref, *, tc, treal):
    c = pl.program_id(1)
    pre = SUBLANES
    hist = CONV_W - 1

    @pl.when(c == 0)
    def _():
        ext_ref[0:pre - hist, :] = jnp.zeros((pre - hist, XBC_DIM), F32)
        ext_ref[pre - hist:pre, :] = hc_ref[0]
        state_ref[...] = h0_ref[...]

    if treal < tc:
        ext_ref[pre:pre + tc, :] = jnp.zeros((tc, XBC_DIM), F32)
        zbuf_ref[...] = jnp.zeros((tc, MIX_DIM), F32)
        dtbuf_ref[...] = jnp.zeros((tc, LANES), F32)
    ext_ref[pre:pre + treal, :] = xbc_ref[0]
    zbuf_ref[0:treal, :] = z_ref[0]
    dtbuf_ref[0:treal, :] = dt_ref[0]

    conv = convb_ref[...]
    for j in range(CONV_W):
        conv = conv + ext_ref[pre - hist + j:pre - hist + j + tc, :] * convw_ref[j:j + 1, :]
    conv = _silu(conv)
    xs = conv[:, :MIX_DIM]
    bmat = conv[:, MIX_DIM:MIX_DIM + SSM_GROUPS * SSM_STATE]
    cmat = conv[:, MIX_DIM + SSM_GROUPS * SSM_STATE:]

    rowi = lax.broadcasted_iota(I32, (tc, LANES), 0)
    dt = _softplus(dtbuf_ref[...] + dtb_ref[...])
    if treal < tc:
        dt = jnp.where(rowi < treal, dt, 0.0)
    a = dt * (-jnp.exp(alog_ref[...]))
    acum = _cumsum_rows(a, tc)
    acum_t = jnp.transpose(acum)
    total = acum[tc - 1:tc, :]
    causal = lax.broadcasted_iota(I32, (tc, tc), 1) <= lax.broadcasted_iota(I32, (tc, tc), 0)

    ys = []
    heads_per_group = N_HEADS // SSM_GROUPS
    for g in range(SSM_GROUPS):
        bg = bmat[:, g * SSM_STATE:(g + 1) * SSM_STATE]
        cg = cmat[:, g * SSM_STATE:(g + 1) * SSM_STATE]
        cb = _dot_nt(cg, bg)
        for hh in range(heads_per_group):
            h = g * heads_per_group + hh
            col = acum[:, h:h + 1]
            row = acum_t[h:h + 1, :]
            decay = jnp.exp(jnp.where(causal, col - row, NEG))
            xh = xs[:, h * HEAD_DIM:(h + 1) * HEAD_DIM]
            xdt = xh * dt[:, h:h + 1]
            hprev = state_ref[0, h]
            y = _dot(cb * decay, xdt)
            y = y + _dot_nt(cg, hprev) * jnp.exp(col)
            y = y + dskip_ref[:, h:h + 1] * xh
            ys.append(y)
            tot = total[:, h:h + 1]
            st = _dot_tn(xdt * jnp.exp(tot - col), bg)
            state_ref[0, h] = jnp.exp(tot) * hprev + st
    y = jnp.concatenate(ys, axis=1) * _silu(zbuf_ref[...])
    gw = MIX_DIM // SSM_GROUPS
    outs = []
    for g in range(SSM_GROUPS):
        yg = y[:, g * gw:(g + 1) * gw]
        ms = jnp.mean(yg * yg, axis=-1, keepdims=True)
        outs.append(yg * lax.rsqrt(ms + EPS) * norm_ref[:, g * gw:(g + 1) * gw])
    y_ref[0] = jnp.concatenate(outs, axis=1)[:treal]

    newconv_ref[0] = ext_ref[pre + treal - hist:pre + treal, :]
    ext_ref[0:pre, :] = ext_ref[treal:treal + pre, :]
    newh_ref[...] = state_ref[...]


def _ssd(h3, hist_conv, h0, conv_w, conv_b, dtb, alog, dskip, norm, *, tc, treal):
    b, t, _ = h3.shape
    full2 = lambda i, c: (0, 0)
    return pl.pallas_call(
        functools.partial(_ssd_kernel, tc=tc, treal=treal),
        grid=(b, t // treal),
        in_specs=[pl.BlockSpec((1, treal, MIX_DIM), lambda i, c: (i, c, C_Z // MIX_DIM)),
                  pl.BlockSpec((1, treal, XBC_DIM), lambda i, c: (i, c, C_XBC // XBC_DIM)),
                  pl.BlockSpec((1, treal, LANES), lambda i, c: (i, c, C_DT // LANES)),
                  pl.BlockSpec((1, CONV_W - 1, XBC_DIM), lambda i, c: (i, 0, 0)),
                  pl.BlockSpec((1, N_HEADS, HEAD_DIM, SSM_STATE), lambda i, c: (i, 0, 0, 0)),
                  pl.BlockSpec((CONV_W, XBC_DIM), full2),
                  pl.BlockSpec((1, XBC_DIM), full2),
                  pl.BlockSpec((1, LANES), full2),
                  pl.BlockSpec((1, LANES), full2),
                  pl.BlockSpec((1, LANES), full2),
                  pl.BlockSpec((1, MIX_DIM), full2)],
        out_specs=[pl.BlockSpec((1, treal, MIX_DIM), lambda i, c: (i, c, 0)),
                   pl.BlockSpec((1, CONV_W - 1, XBC_DIM), lambda i, c: (i, 0, 0)),
                   pl.BlockSpec((1, N_HEADS, HEAD_DIM, SSM_STATE), lambda i, c: (i, 0, 0, 0))],
        out_shape=[jax.ShapeDtypeStruct((b, t, MIX_DIM), F32),
                   jax.ShapeDtypeStruct((b, CONV_W - 1, XBC_DIM), F32),
                   jax.ShapeDtypeStruct((b, N_HEADS, HEAD_DIM, SSM_STATE), F32)],
        scratch_shapes=[pltpu.VMEM((SUBLANES + tc, XBC_DIM), F32),
                        pltpu.VMEM((tc, MIX_DIM), F32),
                        pltpu.VMEM((tc, LANES), F32),
                        pltpu.VMEM((1, N_HEADS, HEAD_DIM, SSM_STATE), F32)],
        compiler_params=_cparams(("parallel", "arbitrary")),
        name="ssd",
    )(h3, h3, h3, hist_conv, h0, conv_w, conv_b, dtb, alog, dskip, norm)


M_INIT = 0.5 * NEG
POS_BITS = 14
assert math.frexp(ATT_SCALE)[0] == 0.5, "q is pre-scaled in bf16, exact only for a power-of-two scale"


def _mono_key(s):
    s = jnp.where(s == 0.0, 0.0, s)
    b = lax.bitcast_convert_type(s, I32)
    return b ^ ((b >> 31) & INT_MAX)


def _select(count, stat_shape, nsel, query_ok, tie_limit=None, threshold=None):
    kf = float(nsel)

    def bit_threshold():
        c0 = count(lambda kt, pos: kt >= 0)
        nonneg = c0 >= kf
        prefix = jnp.where(nonneg, 0, INT_MIN).astype(I32)

        def bit_body(it, carry):
            prefix, cge = carry
            cand = prefix | jnp.left_shift(jnp.int32(1), 30 - it)
            cnt = count(lambda kt, pos: kt >= cand)
            take = cnt >= kf
            return jnp.where(take, cand, prefix), jnp.where(take, cnt, cge)

        return lax.fori_loop(0, 31, bit_body, (prefix, jnp.where(nonneg, c0, kf)))

    v, cge = (threshold or bit_threshold)()
    need = (cge > kf) & (v != INT_MIN) & query_ok

    def bit_tie_limit(v, want):
        def body(it, x):
            cand = x | jnp.left_shift(jnp.int32(1), POS_BITS - 1 - it)
            cnt = count(lambda kt, pos: (kt == v) & (pos < cand))
            return jnp.where(cnt < want, cand, x)
        return lax.fori_loop(0, POS_BITS, body, jnp.zeros(stat_shape, I32))

    def tie_search():
        want = kf - count(lambda kt, pos: kt > v)
        limit = (tie_limit or bit_tie_limit)(v, want)
        return jnp.where(need, limit, INT_MAX)

    any_need = jnp.max(jnp.where(need, 1.0, 0.0)) > 0.0
    jlim = lax.cond(any_need, tie_search, lambda: jnp.full(stat_shape, INT_MAX, I32))
    return v, jlim


WIDE = 4 * LANES
PAIRS = N_HEADS // 2
COUNT_ROWS = 8 * SUBLANES


def _attn_prompt_kernel(far_ref, qidx_ref, kw_ref, q_ref, kidx_ref, k_ref, vt_ref, bd_ref, bs_ref, o_ref,
                        keys_ref, qit_ref, qz_ref, m_ref, l_ref, acc_ref, *, tq, nsel):
    i = pl.program_id(0)
    qpos = i * tq + lax.broadcasted_iota(I32, (1, tq), 1)
    nwide = i // (WIDE // LANES) + 1

    qi_t = jnp.transpose(qidx_ref[...].astype(F32)).astype(BF16)
    for hh in range(IDX_HEADS):
        qit_ref[:, hh * tq:(hh + 1) * tq] = qi_t[hh * IDX_DIM:(hh + 1) * IDX_DIM, :]
    w_t = jnp.transpose(kw_ref[...])[IDX_DIM:IDX_DIM + IDX_HEADS, :] * IDX_SCALE
    q_t = jnp.transpose(q_ref[...].astype(F32) * ATT_SCALE).astype(BF16)
    upper = lax.broadcasted_iota(I32, (LANES, tq), 0) < HEAD_DIM
    zero = jnp.zeros((LANES, tq), BF16)
    for p in range(PAIRS):
        blk = q_t[p * LANES:(p + 1) * LANES, :]
        qz_ref[p, :, 0:tq] = jnp.where(upper, blk, zero)
        qz_ref[p, :, tq:2 * tq] = jnp.where(upper, zero, blk)

    def score_body(c, carry):
        off = pl.multiple_of(c * WIDE, WIDE)
        kt = kidx_ref[pl.ds(off, WIDE), 0:IDX_DIM]
        sc = _dot(kt, qit_ref[...])
        s = jnp.zeros((WIDE, tq), F32)
        for hh in range(IDX_HEADS):
            s = s + jnp.maximum(sc[:, hh * tq:(hh + 1) * tq], 0.0) * w_t[hh:hh + 1, :]
        kpos = off + lax.broadcasted_iota(I32, (WIDE, tq), 0)
        keys_ref[pl.ds(off, WIDE), :] = jnp.where(kpos <= qpos, _mono_key(s), INT_MIN)
        return carry

    lax.fori_loop(0, nwide, score_body, 0)

    def count(pred):
        def body(c, acc):
            off = pl.multiple_of(c * WIDE, WIDE)
            kt = keys_ref[pl.ds(off, WIDE), :]
            kpos = off + lax.broadcasted_iota(I32, (WIDE, tq), 0)
            hit = jnp.where(pred(kt, kpos), 1.0, 0.0)
            return acc + jnp.sum(hit.reshape(WIDE // COUNT_ROWS, COUNT_ROWS, tq), axis=0)
        acc = lax.fori_loop(0, nwide, body, jnp.zeros((COUNT_ROWS, tq), F32))
        return jnp.sum(acc, axis=0, keepdims=True)


    def tie_limit(v, want):
        lower = (lax.broadcasted_iota(I32, (WIDE, WIDE), 1) <= lax.broadcasted_iota(I32, (WIDE, WIDE), 0))
        lower = jnp.where(lower, 1.0, 0.0).astype(BF16)

        def body(c, carry):
            base, limit = carry
            off = pl.multiple_of(c * WIDE, WIDE)
            eq = keys_ref[pl.ds(off, WIDE), :] == v
            rank = base + _dot(lower, jnp.where(eq, 1.0, 0.0).astype(BF16))
            kpos = (off + lax.broadcasted_iota(I32, (WIDE, tq), 0)).astype(F32)
            hit = jnp.max(jnp.where(eq & (rank <= want), kpos, -1.0), axis=0, keepdims=True)
            return rank[WIDE - 1:WIDE, :], jnp.maximum(limit, hit)

        zero = jnp.zeros((1, tq), F32)
        return lax.fori_loop(0, nwide, body, (zero, zero - 1.0))[1].astype(I32)

    thr, jlim = _select(count, (1, tq), nsel, True, tie_limit)

    m_ref[...] = jnp.full(m_ref.shape, M_INIT, F32)
    l_ref[...] = jnp.zeros(l_ref.shape, F32)
    acc_ref[...] = jnp.zeros(acc_ref.shape, F32)

    def tile(off, height, bias_ref):
        kt = keys_ref[pl.ds(off, height), :]
        kpos = off + lax.broadcasted_iota(I32, (height, tq), 0)
        sel = ((kt > thr) | ((kt == thr) & (kpos <= jlim))) & (kpos <= qpos)
        amask = jnp.where(sel, 0.0, NEG)
        def scores(p):
            return _dot(k_ref[pl.ds(off, height), p * LANES:(p + 1) * LANES], qz_ref[p])

        def softmax(p, st):
            out = []
            for e in range(2):
                h = 2 * p + e
                lg = st[:, e * tq:(e + 1) * tq] + amask
                if bias_ref is not None:
                    lg = lg + bias_ref[h]
                m_old = m_ref[h]
                m_new = jnp.maximum(m_old, jnp.max(lg, axis=0, keepdims=True))
                alpha = jnp.exp(m_old - m_new)
                pr = jnp.exp(lg - m_new[0:1, :])
                l_ref[h] = alpha * l_ref[h] + jnp.sum(pr, axis=0, keepdims=True)
                m_ref[h] = m_new
                out.append((h, pr.astype(BF16), alpha[0:1, :]))
            return out

        def values(items):
            for h, pr, alpha in items:
                pv = _dot(vt_ref[h * HEAD_DIM:(h + 1) * HEAD_DIM, pl.ds(off, height)], pr)
                acc_ref[h] = alpha * acc_ref[h] + pv

        sts = {0: scores(0), 1: scores(1)}
        done = None
        for p in range(PAIRS):
            cur = softmax(p, sts.pop(p))
            if p + 2 < PAIRS:
                sts[p + 2] = scores(p + 2)
            if done is not None:
                values(done)
            done = cur
        values(done)

    nfar = jnp.maximum(i - 1, 0)
    nfar_wide = nfar // (WIDE // LANES)

    def far_wide(c, carry):
        tile(pl.multiple_of(c * WIDE, WIDE), WIDE, None)
        return carry

    def far_narrow(j, carry):
        tile(pl.multiple_of(j * LANES, LANES), LANES, None)
        return carry

    lax.fori_loop(0, nfar_wide, far_wide, 0)
    lax.fori_loop(nfar_wide * (WIDE // LANES), nfar, far_narrow, 0)
    for h in range(N_HEADS):
        m_ref[h] = m_ref[h] + far_ref[h]

    @pl.when(i >= 1)
    def _():
        tile(pl.multiple_of((i - 1) * LANES, LANES), LANES, bs_ref)

    tile(pl.multiple_of(i * LANES, LANES), LANES, bd_ref)
    out_t = jnp.concatenate([acc_ref[h] / l_ref[h][0:1, :] for h in range(N_HEADS)], axis=0)
    o_ref[...] = jnp.transpose(out_t)


def _attn_prompt(far, h2, hb, vt, bias_diag_t, bias_sub_t, nsel):
    t = h2.shape[0]
    tq = LANES
    nq = IDX_HEADS * IDX_DIM
    return pl.pallas_call(
        functools.partial(_attn_prompt_kernel, tq=tq, nsel=nsel),
        grid=(t // tq,),
        in_specs=[pl.BlockSpec(memory_space=pltpu.SMEM),
                  pl.BlockSpec((tq, nq), lambda i: (i, C_QIDX // nq)),
                  pl.BlockSpec((tq, LANES), lambda i: (i, C_KW // LANES)),
                  pl.BlockSpec((tq, MIX_DIM), lambda i: (i, C_Q // MIX_DIM)),
                  pl.BlockSpec((t, LANES), lambda i: (0, C_KW // LANES)),
                  pl.BlockSpec((t, MIX_DIM), lambda i: (0, C_K // MIX_DIM)),
                  pl.BlockSpec((MIX_DIM, t), lambda i: (0, 0)),
                  pl.BlockSpec((N_HEADS, LANES, tq), lambda i: (0, 0, 0)),
                  pl.BlockSpec((N_HEADS, LANES, tq), lambda i: (0, 0, 0))],
        out_specs=pl.BlockSpec((tq, MIX_DIM), lambda i: (i, 0)),
        out_shape=jax.ShapeDtypeStruct((t, MIX_DIM), F32),
        scratch_shapes=[pltpu.VMEM((t, tq), I32),
                        pltpu.VMEM((IDX_DIM, IDX_HEADS * tq), BF16),
                        pltpu.VMEM((PAIRS, LANES, 2 * tq), BF16),
                        pltpu.VMEM((N_HEADS, SUBLANES, tq), F32),
                        pltpu.VMEM((N_HEADS, SUBLANES, tq), F32),
                        pltpu.VMEM((N_HEADS, HEAD_DIM, tq), F32)],
        compiler_params=_cparams(("parallel",)),
        name="attn_prompt",
    )(far, hb, h2, hb, hb, hb, vt, bias_diag_t, bias_sub_t)


QROWS = 8
SEL_PAGES = 32
ATT_PAGES = 32


SEL_ROWS = 8 * QROWS


def _score_sample_kernel(pt_ref, qst_ref, wcol_ref, *refs, treal):
    pages = refs[:SEL_PAGES]
    knew_ref, keys_ref = refs[SEL_PAGES:]
    j = pl.program_id(1)
    ntiles = keys_ref.shape[2] // LANES

    def score(kt_t):
        r = jnp.maximum(_dot(qst_ref[0], kt_t), 0.0) * (wcol_ref[0] * IDX_SCALE)
        s = r[0:QROWS]
        for hh in range(1, IDX_HEADS):
            s = s + r[hh * QROWS:(hh + 1) * QROWS]
        return _mono_key(s)

    for g in range(SEL_PAGES):
        off = pl.multiple_of((j * SEL_PAGES + g) * LANES, LANES)
        keys_ref[0, :, pl.ds(off, LANES)] = score(pages[g][0, 0].astype(BF16))

    @pl.when(j == pl.num_programs(1) - 1)
    def _():
        rowq = lax.broadcasted_iota(I32, (QROWS, LANES), 0)
        lane = lax.broadcasted_iota(I32, (QROWS, LANES), 1)
        knew = jnp.where((lane < treal) & (lane <= rowq), score(knew_ref[0]), INT_MIN)
        keys_ref[0, :, (ntiles - 1) * LANES:] = knew


def _select_sample_kernel(keys_ref, mask_ref, *, nsel, treal):
    rows, nk = keys_ref.shape
    ntiles = nk // LANES
    lane = lax.broadcasted_iota(I32, (rows, LANES), 1)

    def count(pred):
        acc = jnp.zeros((rows, LANES), F32)
        for jt in range(ntiles):
            kt = keys_ref[:, jt * LANES:(jt + 1) * LANES]
            acc = acc + jnp.where(pred(kt, jt * LANES + lane), 1.0, 0.0)
        return jnp.sum(acc, axis=1, keepdims=True)

    def tie_limit(v, want):
        upper = (lax.broadcasted_iota(I32, (LANES, LANES), 0) <= lax.broadcasted_iota(I32, (LANES, LANES), 1))
        upper = jnp.where(upper, 1.0, 0.0).astype(BF16)
        eqs = [keys_ref[:, jt * LANES:(jt + 1) * LANES] == v for jt in range(ntiles)]
        ranks = [_dot(jnp.where(eq, 1.0, 0.0).astype(BF16), upper) for eq in eqs]
        base = jnp.zeros((rows, 1), F32)
        best = jnp.full((rows, LANES), -1.0, F32)
        for jt in range(ntiles):
            ok = eqs[jt] & (base + ranks[jt] <= want)
            best = jnp.maximum(best, jnp.where(ok, (jt * LANES + lane).astype(F32), -1.0))
            base = base + ranks[jt][:, LANES - 1:LANES]
        return jnp.max(best, axis=1, keepdims=True).astype(I32)

    row_ok = lax.broadcasted_iota(I32, (rows, 1), 0) % QROWS < treal
    thr, jlim = _select(count, (rows, 1), nsel, row_ok, tie_limit)
    for jt in range(ntiles):
        kt = keys_ref[:, jt * LANES:(jt + 1) * LANES]
        sel = (kt > thr) | ((kt == thr) & (jt * LANES + lane <= jlim))
        mask_ref[:, jt * LANES:(jt + 1) * LANES] = jnp.where(sel, 1.0, 0.0)


def _sel_sample(page_table, qst, wcol, cache_kidx, knew, layer, nsel, treal):
    b, npages = page_table.shape
    nk = (npages + 1) * PAGE
    page_spec = lambda g: pl.BlockSpec(
        (1, 1, IDX_DIM, PAGE), lambda i, j, pt: (layer, pt[i, j * SEL_PAGES + g], 0, 0))
    grid_spec = pltpu.PrefetchScalarGridSpec(
        num_scalar_prefetch=1,
        grid=(b, npages // SEL_PAGES),
        in_specs=[pl.BlockSpec((1, IDX_HEADS * QROWS, IDX_DIM), lambda i, j, pt: (i, 0, 0)),
                  pl.BlockSpec((1, IDX_HEADS * QROWS, 1), lambda i, j, pt: (i, 0, 0))]
                 + [page_spec(g) for g in range(SEL_PAGES)]
                 + [pl.BlockSpec((1, IDX_DIM, PAGE), lambda i, j, pt: (i, 0, 0))],
        out_specs=pl.BlockSpec((1, QROWS, nk), lambda i, j, pt: (i, 0, 0)),
    )
    keys = pl.pallas_call(
        functools.partial(_score_sample_kernel, treal=treal),
        grid_spec=grid_spec,
        out_shape=jax.ShapeDtypeStruct((b, QROWS, nk), I32),
        compiler_params=_cparams(("parallel", "arbitrary")),
        name="score_sample",
    )(page_table, qst, wcol, *([cache_kidx] * SEL_PAGES), knew)
    rows = b * QROWS
    blk = math.gcd(rows, SEL_ROWS)
    mask = pl.pallas_call(
        functools.partial(_select_sample_kernel, nsel=nsel, treal=treal),
        grid=(rows // blk,),
        in_specs=[pl.BlockSpec((blk, nk), lambda i: (i, 0))],
        out_specs=pl.BlockSpec((blk, nk), lambda i: (i, 0)),
        out_shape=jax.ShapeDtypeStruct((rows, nk), F32),
        compiler_params=_cparams(("parallel",)),
        name="select_sample",
    )(keys.reshape(rows, nk))
    return mask.reshape(b, QROWS, nk)


def _attn_sample_kernel(pt_ref, far_ref, qh_ref, *refs):
    kp = refs[:ATT_PAGES]
    vp = refs[ATT_PAGES:2 * ATT_PAGES]
    mask_ref, masknew_ref, blast_ref, bnew_ref, knew_ref, vnew_ref, o_ref, m_ref, l_ref, acc_ref = refs[2 * ATT_PAGES:]
    j = pl.program_id(1)
    is_last = j == pl.num_programs(1) - 1

    @pl.when(j == 0)
    def _():
        m_ref[...] = jnp.full(m_ref.shape, M_INIT, F32)
        l_ref[...] = jnp.zeros(l_ref.shape, F32)
        acc_ref[...] = jnp.zeros(acc_ref.shape, F32)

    def update(lgs, vhs):
        prs, alphas = [], []
        for h in range(N_HEADS):
            m_old = m_ref[h]
            m_new = jnp.maximum(m_old, jnp.max(lgs[h], axis=1, keepdims=True))
            alpha = jnp.exp(m_old - m_new)
            pr = jnp.exp(lgs[h] - m_new[:, 0:1])
            l_ref[h] = alpha * l_ref[h] + jnp.sum(pr, axis=1, keepdims=True)
            m_ref[h] = m_new
            prs.append(pr.astype(BF16))
            alphas.append(alpha[:, :HEAD_DIM])
        for h in range(N_HEADS):
            acc_ref[h] = alphas[h] * acc_ref[h] + _dot_nt(prs[h], vhs[h])

    def q_of(h):
        return (qh_ref[0, h].astype(F32) * ATT_SCALE).astype(BF16)

    def head_t(refs_, h):
        return jnp.concatenate([refs_[g][0, 0, h] for g in range(ATT_PAGES)], axis=1).astype(BF16)

    amask = jnp.where(mask_ref[0] > 0.5, 0.0, NEG)
    lgs = []
    for h in range(N_HEADS):
        far = jnp.full((QROWS, LANES), far_ref[h], F32)
        bias = jnp.concatenate([far] * (ATT_PAGES - 1) + [jnp.where(is_last, blast_ref[h], far)], axis=1)
        lgs.append(_dot(q_of(h), head_t(kp, h)) + amask + bias)
    update(lgs, [head_t(vp, h) for h in range(N_HEADS)])

    @pl.when(is_last)
    def _():
        amask_new = jnp.where(masknew_ref[0] > 0.5, 0.0, NEG)
        lgs_new = [_dot(q_of(h), knew_ref[0, h]) + amask_new + bnew_ref[h] for h in range(N_HEADS)]
        update(lgs_new, [vnew_ref[0, h] for h in range(N_HEADS)])
        outs = []
        for h in range(N_HEADS):
            lh = l_ref[h][:, :HEAD_DIM]
            outs.append(acc_ref[h] / jnp.where(lh > 0.0, lh, 1.0))
        o_ref[0] = jnp.concatenate(outs, axis=1)


def _attn_sample(page_table, far, qh, cache_k2, cache_v2, mask, bias_last, bias_new, knew, vnew, layer):
    b, npages = page_table.shape
    nsteps = npages // ATT_PAGES
    page_spec = lambda g: pl.BlockSpec(
        (1, 1, N_HEADS, HEAD_DIM, PAGE), lambda i, j, pt, far: (layer, pt[i, j * ATT_PAGES + g], 0, 0, 0))
    const3 = lambda i, j, pt, far: (0, 0, 0)
    per_b4 = lambda i, j, pt, far: (i, 0, 0, 0)
    grid_spec = pltpu.PrefetchScalarGridSpec(
        num_scalar_prefetch=2,
        grid=(b, nsteps),
        in_specs=[pl.BlockSpec((1, N_HEADS, QROWS, HEAD_DIM), per_b4)]
                 + [page_spec(g) for g in range(ATT_PAGES)] * 2
                 + [pl.BlockSpec((1, QROWS, ATT_PAGES * PAGE), lambda i, j, pt, far: (i, 0, j)),
                    pl.BlockSpec((1, QROWS, PAGE), lambda i, j, pt, far: (i, 0, npages)),
                    pl.BlockSpec((N_HEADS, QROWS, PAGE), const3),
                    pl.BlockSpec((N_HEADS, QROWS, PAGE), const3),
                    pl.BlockSpec((1, N_HEADS, HEAD_DIM, PAGE), per_b4),
                    pl.BlockSpec((1, N_HEADS, HEAD_DIM, PAGE), per_b4)],
        out_specs=pl.BlockSpec((1, QROWS, MIX_DIM), lambda i, j, pt, far: (i, 0, 0)),
        scratch_shapes=[pltpu.VMEM((N_HEADS, QROWS, LANES), F32),
                        pltpu.VMEM((N_HEADS, QROWS, LANES), F32),
                        pltpu.VMEM((N_HEADS, QROWS, HEAD_DIM), F32)],
    )
    return pl.pallas_call(
        _attn_sample_kernel,
        grid_spec=grid_spec,
        out_shape=jax.ShapeDtypeStruct((b, QROWS, MIX_DIM), F32),
        compiler_params=_cparams(("parallel", "arbitrary")),
        name="attn_sample",
    )(page_table, far, qh, *([cache_k2] * ATT_PAGES), *([cache_v2] * ATT_PAGES), mask, mask,
      bias_last, bias_new, knew, vnew)


def _bucket_table(n):
    d = np.arange(n)
    nf = np.maximum(d, 1).astype(np.float32)
    half = REL_BUCKETS // 2
    large = half + (np.log(nf / np.float32(half)) / np.float32(math.log(REL_MAX_DIST / half))
                    * np.float32(REL_BUCKETS - half)).astype(np.int32)
    return np.where(d < half, d, np.minimum(large, REL_BUCKETS - 1)).astype(np.int32)


def _pack_w_in(w):
    w_t = jnp.transpose(w, (0, 2, 1))
    p, z, xbc, dt, q, k, v, qi, ki, wi, u, vs = jnp.split(w_t, np.cumsum(IN_SPLITS)[:-1].tolist(), axis=1)
    zeros = lambda n: jnp.zeros((w.shape[0], n, w.shape[1]), w.dtype)
    packed = jnp.concatenate(
        [p, z, xbc, q, k, v, u, vs, qi, ki, wi, zeros(LANES - IDX_DIM - IDX_HEADS), dt, zeros(LANES - N_HEADS)],
        axis=1)
    return packed.astype(BF16)


def _pad_lanes(x):
    return jnp.pad(x.reshape(1, -1), ((0, 0), (0, LANES - x.shape[-1])))


def _pad_axis(x, axis, size):
    pads = [(0, 0)] * x.ndim
    pads[axis] = (0, size - x.shape[axis])
    return jnp.pad(x, pads)


def kernel(x_prompt, x_sample, state_pool, state_conv, state_ssm, cache_k, cache_v, cache_kidx, page_table,
           rel_bias, norm_mix, w_in, pool_w, pool_scale, conv_w, conv_b, dt_bias, a_log, d_skip, ssm_norm,
           sgu_w, sgu_b, w_out, norm_mlp, mlp_w1, mlp_w2, norm_final):
    depth = w_in.shape[0]
    bp, t, _ = x_prompt.shape
    bs, ts, _ = x_sample.shape
    npages = page_table.shape[1]
    past = npages * PAGE
    n_pool = cache_k.shape[1]
    assert bp == 1 and t % WIDE == 0 and npages % SEL_PAGES == 0 and npages % ATT_PAGES == 0
    assert max(t, past + PAGE) <= 2 ** POS_BITS and ts <= QROWS
    tc = LANES
    nsel_p = min(TOPK_MAX, t // 4)
    nsel_s = min(TOPK_MAX, (past + ts) // 4)

    bucket = _bucket_table(2 * LANES)
    far = rel_bias[REL_BUCKETS - 1]
    kk = np.arange(LANES)[:, None]
    qq = np.arange(LANES)[None, :]
    def bias_of(dist):
        onehot = np.eye(REL_BUCKETS, dtype=np.float32)[bucket[dist].reshape(-1)]
        vals = jnp.dot(onehot, rel_bias, precision=lax.Precision.HIGHEST)
        return jnp.transpose(vals).reshape((N_HEADS,) + dist.shape)

    bias_diag_t = bias_of(np.maximum(qq - kk, 0))
    bias_sub_t = bias_of(np.minimum(LANES + qq - kk, 2 * LANES - 1))
    sq = np.arange(QROWS)[:, None]
    sk = np.arange(PAGE)[None, :]
    bias_last = bias_of(np.minimum(PAGE + sq - sk, 2 * LANES - 1))
    bias_new = bias_of(np.maximum(sq - sk, 0))

    cache_k2 = jnp.transpose(cache_k, (0, 1, 3, 4, 2))
    cache_v2 = jnp.transpose(cache_v, (0, 1, 3, 4, 2))
    cache_kidx2 = jnp.transpose(cache_kidx, (0, 1, 3, 2))
    w_in_t = _pack_w_in(w_in)
    w_out_b, w1_b, w2_b = w_out.astype(BF16), mlp_w1.astype(BF16), mlp_w2.astype(BF16)

    xp = x_prompt.reshape(t, D_MODEL)
    xs = x_sample.reshape(bs * ts, D_MODEL)
    zero_pool = jnp.zeros((1, POOL_HIST, MIX_DIM), F32)
    zero_conv = jnp.zeros((1, CONV_W - 1, XBC_DIM), F32)
    zero_ssm = jnp.zeros((1, N_HEADS, HEAD_DIM, SSM_STATE), F32)
    tm_p = min(1024, t)
    tm_s = bs * ts
    outs_p, outs_s = [], []

    for l in range(depth):
        g_mix = norm_mix[l].reshape(1, D_MODEL)
        g_mlp = norm_mlp[l].reshape(1, D_MODEL)
        g_out = norm_final.reshape(1, D_MODEL) if l == depth - 1 else g_mlp
        pscale = pool_scale[l].reshape(1, MIX_DIM)
        cb = conv_b[l].reshape(1, XBC_DIM)
        dtb, alog, dsk = _pad_lanes(dt_bias[l]), _pad_lanes(a_log[l]), _pad_lanes(d_skip[l])
        snorm = ssm_norm[l].reshape(1, MIX_DIM)
        sgu_b_col = sgu_b[l][:, :, None]
        final = l == depth - 1

        hp, hpb, vt = _inproj(xp, g_mix, w_in_t, l, tm=min(1024, t), tn=1024, attn_operands=True)
        hp3 = hp.reshape(1, t, H_PACKED)
        y_pool, new_pool = _pool(hp3, zero_pool, pool_w[l], pscale, tc=tc, treal=tc, pos0=0)
        y_ssm, new_conv, new_h = _ssd(hp3, zero_conv, zero_ssm, conv_w[l], cb, dtb, alog, dsk, snorm,
                                      tc=tc, treal=tc)
        y_sgu, _ = _sgu(hp3, sgu_w[l], sgu_b_col, tc=tc, treal=tc)
        kf = hp[:, C_K:C_K + MIX_DIM]
        vf = hp[:, C_V:C_V + MIX_DIM]
        kif = hp[:, C_KW:C_KW + IDX_DIM]
        y_att = _attn_prompt(far, hp, hpb, vt, bias_diag_t, bias_sub_t, nsel_p)
        x1, xn = _outproj(xp, [y_pool[0], y_ssm[0], y_att, y_sgu[0]], w_out_b, l, g_mlp, tm=256)
        xp = _mlp(xn, w1_b, w2_b, l, x1, g_out, tm=tm_p, final_norm=final)
        outs_p.append((new_pool, new_conv, new_h, kf.reshape(1, t, N_HEADS, HEAD_DIM),
                       vf.reshape(1, t, N_HEADS, HEAD_DIM), kif.reshape(1, t, IDX_DIM)))

        hs, = _inproj(xs, g_mix, w_in_t, l, tm=tm_s)
        hs3 = hs.reshape(bs, ts, H_PACKED)
        y_pool, new_pool = _pool(hs3, state_pool[l], pool_w[l], pscale, tc=tc, treal=ts, pos0=past)
        y_ssm, new_conv, new_h = _ssd(hs3, state_conv[l], state_ssm[l], conv_w[l], cb, dtb, alog, dsk, snorm,
                                      tc=tc, treal=ts)
        y_sgu, v_rows = _sgu(hs3, sgu_w[l], sgu_b_col, tc=tc, treal=ts)
        kf = hs3[:, :, C_K:C_K + MIX_DIM]
        vf = hs3[:, :, C_V:C_V + MIX_DIM]
        kif = hs3[:, :, C_KW:C_KW + IDX_DIM]
        qi = hs3[:, :, C_QIDX:C_QIDX + IDX_HEADS * IDX_DIM].reshape(bs, ts, IDX_HEADS, IDX_DIM)
        qst = _pad_axis(jnp.transpose(qi, (0, 2, 1, 3)), 2, QROWS)
        qst = qst.reshape(bs, IDX_HEADS * QROWS, IDX_DIM).astype(BF16)
        wi = hs3[:, :, C_KW + IDX_DIM:C_KW + IDX_DIM + IDX_HEADS]
        wcol = _pad_axis(jnp.transpose(wi, (0, 2, 1)), 2, QROWS).reshape(bs, IDX_HEADS * QROWS, 1)
        kinew_t = _pad_axis(jnp.transpose(kif, (0, 2, 1)), 2, PAGE).astype(BF16)
        mask = _sel_sample(page_table, qst, wcol, cache_kidx2, kinew_t, l, nsel_s, ts)
        heads = lambda x, perm: jnp.transpose(x.reshape(bs, ts, N_HEADS, HEAD_DIM), perm)
        qh = _pad_axis(heads(hs3[:, :, C_Q:C_Q + MIX_DIM], (0, 2, 1, 3)), 2, QROWS).astype(BF16)
        knew = _pad_axis(heads(kf, (0, 2, 3, 1)), 3, PAGE).astype(BF16)
        vnew = _pad_axis(heads(vf, (0, 2, 3, 1)), 3, PAGE).astype(BF16)
        att = _attn_sample(page_table, far, qh, cache_k2, cache_v2, mask, bias_last, bias_new, knew, vnew, l)
        y_att = att[:, :ts].reshape(bs * ts, MIX_DIM)
        flat = lambda y: y.reshape(bs * ts, MIX_DIM)
        x1, xn = _outproj(xs, [flat(y_pool), flat(y_ssm), y_att, flat(y_sgu)], w_out_b, l, g_mlp, tm=tm_s)
        xs = _mlp(xn, w1_b, w2_b, l, x1, g_out, tm=tm_s, final_norm=final)
        outs_s.append((new_pool, new_conv, new_h, kf.reshape(bs, ts, N_HEADS, HEAD_DIM),
                       vf.reshape(bs, ts, N_HEADS, HEAD_DIM), kif, v_rows))

    stack = lambda outs, i: jnp.stack([o[i] for o in outs])
    return (xp.reshape(1, t, D_MODEL), xs.reshape(bs, ts, D_MODEL),
            *[stack(outs_p, i) for i in range(6)],
            *[stack(outs_s, i) for i in range(7)])
```

```python
import functools
import math

import numpy as np
import jax
import jax.numpy as jnp
from jax import lax
from jax.experimental import pallas as pl
from jax.experimental.pallas import tpu as pltpu

F32 = jnp.float32
BF16 = jnp.bfloat16
I32 = jnp.int32

LANES = 128
SUBLANES = 8
D_MODEL = 2048
N_HEADS = 8
HEAD_DIM = 64
MIX_DIM = 512
POOL_WINDOWS = (2, 4, 8, 16)
POOL_HIST = 15
CONV_W = 4
SSM_GROUPS = 2
SSM_STATE = 128
XBC_DIM = MIX_DIM + 2 * SSM_GROUPS * SSM_STATE
IDX_HEADS = 4
IDX_DIM = 64
TOPK_MAX = 256
PAGE = 128
REL_BUCKETS = 32
REL_MAX_DIST = 128
ATT_SCALE = HEAD_DIM ** -0.5
IDX_SCALE = (IDX_HEADS * IDX_DIM) ** -0.5
SGU_GROUPS = 4
D_FF = 4 * D_MODEL
EPS = 1e-6
NEG = -1e30
INT_MIN = -(2 ** 31)
INT_MAX = 2 ** 31 - 1
IN_SPLITS = (512, 512, 1024, 8, 512, 512, 512, 256, 64, 4, 512, 512)

C_P, C_Z, C_XBC, C_Q, C_K, C_V, C_U, C_VS, C_QIDX, C_KW, C_DT = (
    0, 512, 1024, 2048, 2560, 3072, 3584, 4096, 4608, 4864, 4992)
H_PACKED = 5120
VMEM_LIMIT = 56 * 1024 * 1024


def _cparams(sem):
    return pltpu.CompilerParams(dimension_semantics=sem, vmem_limit_bytes=VMEM_LIMIT)


def _dot(a, b):
    return jnp.dot(a, b, preferred_element_type=F32)


def _dot_nt(a, b):
    return lax.dot_general(a, b, (((1,), (1,)), ((), ())), preferred_element_type=F32)


def _dot_tn(a, b):
    return lax.dot_general(a, b, (((0,), (0,)), ((), ())), preferred_element_type=F32)


def _silu(x):
    return x * (1.0 / (1.0 + jnp.exp(-x)))


def _gelu_tanh(x):
    return 0.5 * x * (1.0 + jnp.tanh(math.sqrt(2.0 / math.pi) * (x + 0.044715 * (x * x * x))))


def _softplus(x):
    return jnp.maximum(x, 0.0) + jnp.log1p(jnp.exp(-jnp.abs(x)))


def _rms(x, g):
    ms = jnp.mean(x * x, axis=-1, keepdims=True)
    return x * lax.rsqrt(ms + EPS) * g


def _inproj_kernel(x_ref, g_ref, w_ref, o_ref, *rest, attn_operands, tn):
    xn_ref = rest[-1]

    @pl.when(pl.program_id(1) == 0)
    def _():
        xn_ref[...] = _rms(x_ref[...], g_ref[...]).astype(BF16)

    h = _dot_nt(xn_ref[...], w_ref[0])
    o_ref[...] = h
    if attn_operands:
        ob_ref, vt_ref = rest[:2]
        ob_ref[...] = h.astype(BF16)

        @pl.when(pl.program_id(1) == C_V // tn)
        def _():
            vt_ref[...] = jnp.transpose(h[:, C_V % tn:C_V % tn + MIX_DIM]).astype(BF16)


def _inproj(x, g, w_t, layer, tm, tn=512, attn_operands=False):
    m = x.shape[0]
    n = w_t.shape[1]
    out_specs = [pl.BlockSpec((tm, tn), lambda i, j: (i, j))]
    out_shape = [jax.ShapeDtypeStruct((m, n), F32)]
    if attn_operands:
        out_specs += [pl.BlockSpec((tm, tn), lambda i, j: (i, j)), pl.BlockSpec((MIX_DIM, tm), lambda i, j: (0, i))]
        out_shape += [jax.ShapeDtypeStruct((m, n), BF16), jax.ShapeDtypeStruct((MIX_DIM, m), BF16)]
    return pl.pallas_call(
        functools.partial(_inproj_kernel, attn_operands=attn_operands, tn=tn),
        grid=(m // tm, n // tn),
        in_specs=[pl.BlockSpec((tm, D_MODEL), lambda i, j: (i, 0)),
                  pl.BlockSpec((1, D_MODEL), lambda i, j: (0, 0)),
                  pl.BlockSpec((1, tn, D_MODEL), lambda i, j: (layer, j, 0))],
        out_specs=out_specs,
        out_shape=out_shape,
        scratch_shapes=[pltpu.VMEM((tm, D_MODEL), BF16)],
        compiler_params=_cparams(("parallel", "arbitrary")),
        name="inproj",
    )(x, g, w_t)


def _outproj_kernel(x_ref, y0_ref, y1_ref, y2_ref, y3_ref, w_ref, g_ref, x1_ref, xn_ref):
    y = jnp.concatenate([y0_ref[...], y1_ref[...], y2_ref[...], y3_ref[...]], axis=1).astype(BF16)
    x1 = x_ref[...] + _dot(y, w_ref[0])
    x1_ref[...] = x1
    xn_ref[...] = _rms(x1, g_ref[...]).astype(BF16)


def _outproj(x, ys, w, layer, g, tm):
    m = x.shape[0]
    row = lambda i: (i, 0)
    return pl.pallas_call(
        _outproj_kernel,
        grid=(m // tm,),
        in_specs=[pl.BlockSpec((tm, D_MODEL), row)]
                 + [pl.BlockSpec((tm, MIX_DIM), row)] * 4
                 + [pl.BlockSpec((1, D_MODEL, D_MODEL), lambda i: (layer, 0, 0)),
                    pl.BlockSpec((1, D_MODEL), lambda i: (0, 0))],
        out_specs=[pl.BlockSpec((tm, D_MODEL), row), pl.BlockSpec((tm, D_MODEL), row)],
        out_shape=[jax.ShapeDtypeStruct((m, D_MODEL), F32), jax.ShapeDtypeStruct((m, D_MODEL), BF16)],
        compiler_params=_cparams(("parallel",)),
        name="outproj",
    )(x, *ys, w, g)


def _mlp_kernel(xn_ref, w1_ref, w2_ref, x1_ref, g_ref, o_ref, *, final_norm):
    j = pl.program_id(1)

    @pl.when(j == 0)
    def _():
        o_ref[...] = x1_ref[...]

    hid = _dot(xn_ref[...], w1_ref[0])
    hid = jnp.square(jnp.maximum(hid, 0.0)).astype(BF16)
    o_ref[...] += _dot(hid, w2_ref[0])

    if final_norm:
        @pl.when(j == pl.num_programs(1) - 1)
        def _():
            o_ref[...] = _rms(o_ref[...], g_ref[...])


def _mlp(xn, w1, w2, layer, x1, g, tm, final_norm, tf=512):
    m = xn.shape[0]
    return pl.pallas_call(
        functools.partial(_mlp_kernel, final_norm=final_norm),
        grid=(m // tm, D_FF // tf),
        in_specs=[pl.BlockSpec((tm, D_MODEL), lambda i, j: (i, 0)),
                  pl.BlockSpec((1, D_MODEL, tf), lambda i, j: (layer, 0, j)),
                  pl.BlockSpec((1, tf, D_MODEL), lambda i, j: (layer, j, 0)),
                  pl.BlockSpec((tm, D_MODEL), lambda i, j: (i, 0)),
                  pl.BlockSpec((1, D_MODEL), lambda i, j: (0, 0))],
        out_specs=pl.BlockSpec((tm, D_MODEL), lambda i, j: (i, 0)),
        out_shape=jax.ShapeDtypeStruct((m, D_MODEL), F32),
        compiler_params=_cparams(("parallel", "arbitrary")),
        name="mlp",
    )(xn, w1, w2, x1, g)


def _pool_kernel(p_ref, hist_ref, w_ref, scale_ref, y_ref, newhist_ref, ext_ref, *, tc, treal, pos0):
    c = pl.program_id(1)
    hrows = POOL_HIST + 1

    @pl.when(c == 0)
    def _():
        ext_ref[0:1, :] = jnp.zeros((1, MIX_DIM), F32)
        ext_ref[1:hrows, :] = hist_ref[0]

    if treal < tc:
        ext_ref[hrows:hrows + tc, :] = jnp.zeros((tc, MIX_DIM), F32)
    ext_ref[hrows:hrows + treal, :] = p_ref[0]

    pos = pos0 + c * treal + lax.broadcasted_iota(I32, (tc, 1), 0)
    for g, win in enumerate(POOL_WINDOWS):
        cols = slice(g * LANES, (g + 1) * LANES)
        cur = ext_ref[hrows:hrows + tc, cols]
        s = cur
        for j in range(1, win):
            s = s + ext_ref[hrows - j:hrows - j + tc, cols]
        cnt = jnp.minimum(pos + 1, win).astype(F32)
        diff = s / cnt - cur
        y = _dot(diff, w_ref[g]) * scale_ref[:, cols]
        y_ref[0, :, cols] = y[:treal]

    newhist_ref[0] = ext_ref[treal + 1:treal + hrows, :]
    ext_ref[0:hrows, :] = ext_ref[treal:treal + hrows, :]


def _pool(h3, hist, pool_w, pool_scale, *, tc, treal, pos0):
    b, t, _ = h3.shape
    return pl.pallas_call(
        functools.partial(_pool_kernel, tc=tc, treal=treal, pos0=pos0),
        grid=(b, t // treal),
        in_specs=[pl.BlockSpec((1, treal, MIX_DIM), lambda i, c: (i, c, C_P // MIX_DIM)),
                  pl.BlockSpec((1, POOL_HIST, MIX_DIM), lambda i, c: (i, 0, 0)),
                  pl.BlockSpec((4, LANES, LANES), lambda i, c: (0, 0, 0)),
                  pl.BlockSpec((1, MIX_DIM), lambda i, c: (0, 0))],
        out_specs=[pl.BlockSpec((1, treal, MIX_DIM), lambda i, c: (i, c, 0)),
                   pl.BlockSpec((1, POOL_HIST, MIX_DIM), lambda i, c: (i, 0, 0))],
        out_shape=[jax.ShapeDtypeStruct((b, t, MIX_DIM), F32),
                   jax.ShapeDtypeStruct((b, POOL_HIST, MIX_DIM), F32)],
        scratch_shapes=[pltpu.VMEM((POOL_HIST + 1 + tc, MIX_DIM), F32)],
        compiler_params=_cparams(("parallel", "arbitrary")),
        name="pool",
    )(h3, hist, pool_w, pool_scale)


def _sgu_kernel(u_ref, v_ref, w_ref, b_ref, y_ref, *rest, tc, treal, nsub, emit_v):
    ubuf_ref, vbuf_ref = rest[-2:]
    r = lax.broadcasted_iota(I32, (tc, tc), 0)
    s = lax.broadcasted_iota(I32, (tc, tc), 1)
    for sub in range(nsub):
        rows = slice(sub * treal, (sub + 1) * treal)
        if treal < tc:
            ubuf_ref[...] = jnp.zeros((tc, MIX_DIM), F32)
            vbuf_ref[...] = jnp.zeros((tc, MIX_DIM), F32)
        ubuf_ref[0:treal, :] = u_ref[0, rows, :]
        vbuf_ref[0:treal, :] = v_ref[0, rows, :]
        u = _gelu_tanh(ubuf_ref[...])
        v = _gelu_tanh(vbuf_ref[...])
        if emit_v:
            rest[0][0, rows, :] = v[:treal]
        for g in range(SGU_GROUPS):
            cols = slice(g * LANES, (g + 1) * LANES)
            w = jnp.where(s <= r, w_ref[g], 0.0)
            mixed = _dot(w, v[:, cols]) + b_ref[g]
            y_ref[0, rows, cols] = (u[:, cols] * mixed)[:treal]


def _sgu(h3, sgu_w, sgu_b_col, *, tc, treal, nsub=1, emit_v=False):
    b, t, _ = h3.shape
    rows = nsub * treal
    out_specs = [pl.BlockSpec((1, rows, MIX_DIM), lambda i, c: (i, c, 0))]
    out_shape = [jax.ShapeDtypeStruct((b, t, MIX_DIM), F32)]
    return pl.pallas_call(
        functools.partial(_sgu_kernel, tc=tc, treal=treal, nsub=nsub, emit_v=emit_v),
        grid=(b, t // rows),
        in_specs=[pl.BlockSpec((1, rows, MIX_DIM), lambda i, c: (i, c, C_U // MIX_DIM)),
                  pl.BlockSpec((1, rows, MIX_DIM), lambda i, c: (i, c, C_VS // MIX_DIM)),
                  pl.BlockSpec((SGU_GROUPS, tc, tc), lambda i, c: (0, 0, 0)),
                  pl.BlockSpec((SGU_GROUPS, tc, 1), lambda i, c: (0, 0, 0))],
        out_specs=out_specs * (2 if emit_v else 1),
        out_shape=out_shape * (2 if emit_v else 1),
        scratch_shapes=[pltpu.VMEM((tc, MIX_DIM), F32), pltpu.VMEM((tc, MIX_DIM), F32)],
        compiler_params=_cparams(("parallel", "parallel")),
        name="sgu",
    )(h3, h3, sgu_w, sgu_b_col)


def _cumsum_rows(a, n):
    row = lax.broadcasted_iota(I32, a.shape, 0)
    sh = 1
    while sh < n:
        a = a + jnp.where(row >= sh, pltpu.roll(a, sh, 0), 0.0)
        sh *= 2
    return a


def _ssd_kernel(z_ref, xbc_ref, dt_ref, hc_ref, h0_ref, convw_ref, convb_ref, dtb_ref, alog_ref, dskip_ref,
                norm_ref, y_ref, newconv_ref, newh_ref, ext_ref, zbuf_ref, dtbuf_ref, state_ref, *, tc, treal):
    c = pl.program_id(1)
    pre = SUBLANES
    hist = CONV_W - 1

    @pl.when(c == 0)
    def _():
        ext_ref[0:pre - hist, :] = jnp.zeros((pre - hist, XBC_DIM), F32)
        ext_ref[pre - hist:pre, :] = hc_ref[0]
        state_ref[...] = h0_ref[...]

    if treal < tc:
        ext_ref[pre:pre + tc, :] = jnp.zeros((tc, XBC_DIM), F32)
        zbuf_ref[...] = jnp.zeros((tc, MIX_DIM), F32)
        dtbuf_ref[...] = jnp.zeros((tc, LANES), F32)
    ext_ref[pre:pre + treal, :] = xbc_ref[0]
    zbuf_ref[0:treal, :] = z_ref[0]
    dtbuf_ref[0:treal, :] = dt_ref[0]

    conv = convb_ref[...]
    for j in range(CONV_W):
        conv = conv + ext_ref[pre - hist + j:pre - hist + j + tc, :] * convw_ref[j:j + 1, :]
    conv = _silu(conv)
    xs = conv[:, :MIX_DIM]
    bmat = conv[:, MIX_DIM:MIX_DIM + SSM_GROUPS * SSM_STATE]
    cmat = conv[:, MIX_DIM + SSM_GROUPS * SSM_STATE:]

    rowi = lax.broadcasted_iota(I32, (tc, LANES), 0)
    dt = _softplus(dtbuf_ref[...] + dtb_ref[...])
    if treal < tc:
        dt = jnp.where(rowi < treal, dt, 0.0)
    a = dt * (-jnp.exp(alog_ref[...]))
    acum = _cumsum_rows(a, tc)
    acum_t = jnp.transpose(acum)
    total = acum[tc - 1:tc, :]
    causal = lax.broadcasted_iota(I32, (tc, tc), 1) <= lax.broadcasted_iota(I32, (tc, tc), 0)

    ys = []
    heads_per_group = N_HEADS // SSM_GROUPS
    for g in range(SSM_GROUPS):
        bg = bmat[:, g * SSM_STATE:(g + 1) * SSM_STATE]
        cg = cmat[:, g * SSM_STATE:(g + 1) * SSM_STATE]
        cb = _dot_nt(cg, bg)
        for hh in range(heads_per_group):
            h = g * heads_per_group + hh
            col = acum[:, h:h + 1]
            row = acum_t[h:h + 1, :]
            decay = jnp.exp(jnp.where(causal, col - row, NEG))
            xh = xs[:, h * HEAD_DIM:(h + 1) * HEAD_DIM]
            xdt = xh * dt[:, h:h + 1]
            hprev = state_ref[0, h]
            y = _dot(cb * decay, xdt)
            y = y + _dot_nt(cg, hprev) * jnp.exp(col)
            y = y + dskip_ref[:, h:h + 1] * xh
            ys.append(y)
            tot = total[:, h:h + 1]
            st = _dot_tn(xdt * jnp.exp(tot - col), bg)
            state_ref[0, h] = jnp.exp(tot) * hprev + st
    y = jnp.concatenate(ys, axis=1) * _silu(zbuf_ref[...])
    gw = MIX_DIM // SSM_GROUPS
    outs = []
    for g in range(SSM_GROUPS):
        yg = y[:, g * gw:(g + 1) * gw]
        ms = jnp.mean(yg * yg, axis=-1, keepdims=True)
        outs.append(yg * lax.rsqrt(ms + EPS) * norm_ref[:, g * gw:(g + 1) * gw])
    y_ref[0] = jnp.concatenate(outs, axis=1)[:treal]

    newconv_ref[0] = ext_ref[pre + treal - hist:pre + treal, :]
    ext_ref[0:pre, :] = ext_ref[treal:treal + pre, :]
    newh_ref[...] = state_ref[...]


def _ssd(h3, hist_conv, h0, conv_w, conv_b, dtb, alog, dskip, norm, *, tc, treal):
    b, t, _ = h3.shape
    full2 = lambda i, c: (0, 0)
    return pl.pallas_call(
        functools.partial(_ssd_kernel, tc=tc, treal=treal),
        grid=(b, t // treal),
        in_specs=[pl.BlockSpec((1, treal, MIX_DIM), lambda i, c: (i, c, C_Z // MIX_DIM)),
                  pl.BlockSpec((1, treal, XBC_DIM), lambda i, c: (i, c, C_XBC // XBC_DIM)),
                  pl.BlockSpec((1, treal, LANES), lambda i, c: (i, c, C_DT // LANES)),
                  pl.BlockSpec((1, CONV_W - 1, XBC_DIM), lambda i, c: (i, 0, 0)),
                  pl.BlockSpec((1, N_HEADS, HEAD_DIM, SSM_STATE), lambda i, c: (i, 0, 0, 0)),
                  pl.BlockSpec((CONV_W, XBC_DIM), full2),
                  pl.BlockSpec((1, XBC_DIM), full2),
                  pl.BlockSpec((1, LANES), full2),
                  pl.BlockSpec((1, LANES), full2),
                  pl.BlockSpec((1, LANES), full2),
                  pl.BlockSpec((1, MIX_DIM), full2)],
        out_specs=[pl.BlockSpec((1, treal, MIX_DIM), lambda i, c: (i, c, 0)),
                   pl.BlockSpec((1, CONV_W - 1, XBC_DIM), lambda i, c: (i, 0, 0)),
                   pl.BlockSpec((1, N_HEADS, HEAD_DIM, SSM_STATE), lambda i, c: (i, 0, 0, 0))],
        out_shape=[jax.ShapeDtypeStruct((b, t, MIX_DIM), F32),
                   jax.ShapeDtypeStruct((b, CONV_W - 1, XBC_DIM), F32),
                   jax.ShapeDtypeStruct((b, N_HEADS, HEAD_DIM, SSM_STATE), F32)],
        scratch_shapes=[pltpu.VMEM((SUBLANES + tc, XBC_DIM), F32),
                        pltpu.VMEM((tc, MIX_DIM), F32),
                        pltpu.VMEM((tc, LANES), F32),
                        pltpu.VMEM((1, N_HEADS, HEAD_DIM, SSM_STATE), F32)],
        compiler_params=_cparams(("parallel", "arbitrary")),
        name="ssd",
    )(h3, h3, h3, hist_conv, h0, conv_w, conv_b, dtb, alog, dskip, norm)


M_INIT = 0.5 * NEG
POS_BITS = 14
assert math.frexp(ATT_SCALE)[0] == 0.5, "q is pre-scaled in bf16, exact only for a power-of-two scale"


def _mono_key(s):
    s = jnp.where(s == 0.0, 0.0, s)
    b = lax.bitcast_convert_type(s, I32)
    return b ^ ((b >> 31) & INT_MAX)


def _select(count, stat_shape, nsel, query_ok, tie_limit=None, threshold=None):
    kf = float(nsel)

    def bit_threshold():
        c0 = count(lambda kt, pos: kt >= 0)
        nonneg = c0 >= kf
        prefix = jnp.where(nonneg, 0, INT_MIN).astype(I32)

        def bit_body(it, carry):
            prefix, cge = carry
            cand = prefix | jnp.left_shift(jnp.int32(1), 30 - it)
            cnt = count(lambda kt, pos: kt >= cand)
            take = cnt >= kf
            return jnp.where(take, cand, prefix), jnp.where(take, cnt, cge)

        return lax.fori_loop(0, 31, bit_body, (prefix, jnp.where(nonneg, c0, kf)))

    v, cge = (threshold or bit_threshold)()
    need = (cge > kf) & (v != INT_MIN) & query_ok

    def bit_tie_limit(v, want):
        def body(it, x):
            cand = x | jnp.left_shift(jnp.int32(1), POS_BITS - 1 - it)
            cnt = count(lambda kt, pos: (kt == v) & (pos < cand))
            return jnp.where(cnt < want, cand, x)
        return lax.fori_loop(0, POS_BITS, body, jnp.zeros(stat_shape, I32))

    def tie_search():
        want = kf - count(lambda kt, pos: kt > v)
        limit = (tie_limit or bit_tie_limit)(v, want)
        return jnp.where(need, limit, INT_MAX)

    any_need = jnp.max(jnp.where(need, 1.0, 0.0)) > 0.0
    jlim = lax.cond(any_need, tie_search, lambda: jnp.full(stat_shape, INT_MAX, I32))
    return v, jlim


WIDE = 4 * LANES
PAIRS = N_HEADS // 2
COUNT_ROWS = 8 * SUBLANES


def _attn_prompt_kernel(far_ref, qidx_ref, kw_ref, q_ref, kidx_ref, k_ref, vt_ref, bd_ref, bs_ref, o_ref,
                        keys_ref, qit_ref, qz_ref, m_ref, l_ref, acc_ref, *, tq, nsel):
    i = pl.program_id(0)
    qpos = i * tq + lax.broadcasted_iota(I32, (1, tq), 1)
    nwide = i // (WIDE // LANES) + 1

    qi_t = jnp.transpose(qidx_ref[...].astype(F32)).astype(BF16)
    for hh in range(IDX_HEADS):
        qit_ref[:, hh * tq:(hh + 1) * tq] = qi_t[hh * IDX_DIM:(hh + 1) * IDX_DIM, :]
    w_t = jnp.transpose(kw_ref[...])[IDX_DIM:IDX_DIM + IDX_HEADS, :] * IDX_SCALE
    q_t = jnp.transpose(q_ref[...].astype(F32) * ATT_SCALE).astype(BF16)
    upper = lax.broadcasted_iota(I32, (LANES, tq), 0) < HEAD_DIM
    zero = jnp.zeros((LANES, tq), BF16)
    for p in range(PAIRS):
        blk = q_t[p * LANES:(p + 1) * LANES, :]
        qz_ref[p, :, 0:tq] = jnp.where(upper, blk, zero)
        qz_ref[p, :, tq:2 * tq] = jnp.where(upper, zero, blk)

    def score_body(c, carry):
        off = pl.multiple_of(c * WIDE, WIDE)
        kt = kidx_ref[pl.ds(off, WIDE), 0:IDX_DIM]
        sc = _dot(kt, qit_ref[...])
        s = jnp.zeros((WIDE, tq), F32)
        for hh in range(IDX_HEADS):
            s = s + jnp.maximum(sc[:, hh * tq:(hh + 1) * tq], 0.0) * w_t[hh:hh + 1, :]
        kpos = off + lax.broadcasted_iota(I32, (WIDE, tq), 0)
        keys_ref[pl.ds(off, WIDE), :] = jnp.where(kpos <= qpos, _mono_key(s), INT_MIN)
        return carry

    lax.fori_loop(0, nwide, score_body, 0)

    def count(pred):
        def body(c, acc):
            off = pl.multiple_of(c * WIDE, WIDE)
            kt = keys_ref[pl.ds(off, WIDE), :]
            kpos = off + lax.broadcasted_iota(I32, (WIDE, tq), 0)
            hit = jnp.where(pred(kt, kpos), 1.0, 0.0)
            return acc + jnp.sum(hit.reshape(WIDE // COUNT_ROWS, COUNT_ROWS, tq), axis=0)
        acc = lax.fori_loop(0, nwide, body, jnp.zeros((COUNT_ROWS, tq), F32))
        return jnp.sum(acc, axis=0, keepdims=True)


    def tie_limit(v, want):
        lower = (lax.broadcasted_iota(I32, (WIDE, WIDE), 1) <= lax.broadcasted_iota(I32, (WIDE, WIDE), 0))
        lower = jnp.where(lower, 1.0, 0.0).astype(BF16)

        def body(c, carry):
            base, limit = carry
            off = pl.multiple_of(c * WIDE, WIDE)
            eq = keys_ref[pl.ds(off, WIDE), :] == v
            rank = base + _dot(lower, jnp.where(eq, 1.0, 0.0).astype(BF16))
            kpos = (off + lax.broadcasted_iota(I32, (WIDE, tq), 0)).astype(F32)
            hit = jnp.max(jnp.where(eq & (rank <= want), kpos, -1.0), axis=0, keepdims=True)
            return rank[WIDE - 1:WIDE, :], jnp.maximum(limit, hit)

        zero = jnp.zeros((1, tq), F32)
        return lax.fori_loop(0, nwide, body, (zero, zero - 1.0))[1].astype(I32)

    thr, jlim = _select(count, (1, tq), nsel, True, tie_limit)

    m_ref[...] = jnp.full(m_ref.shape, M_INIT, F32)
    l_ref[...] = jnp.zeros(l_ref.shape, F32)
    acc_ref[...] = jnp.zeros(acc_ref.shape, F32)

    def tile(off, height, bias_ref):
        kt = keys_ref[pl.ds(off, height), :]
        kpos = off + lax.broadcasted_iota(I32, (height, tq), 0)
        sel = ((kt > thr) | ((kt == thr) & (kpos <= jlim))) & (kpos <= qpos)
        amask = jnp.where(sel, 0.0, NEG)
        def scores(p):
            return _dot(k_ref[pl.ds(off, height), p * LANES:(p + 1) * LANES], qz_ref[p])

        def softmax(p, st):
            out = []
            for e in range(2):
                h = 2 * p + e
                lg = st[:, e * tq:(e + 1) * tq] + amask
                if bias_ref is not None:
                    lg = lg + bias_ref[h]
                m_old = m_ref[h]
                m_new = jnp.maximum(m_old, jnp.max(lg, axis=0, keepdims=True))
                alpha = jnp.exp(m_old - m_new)
                pr = jnp.exp(lg - m_new[0:1, :])
                l_ref[h] = alpha * l_ref[h] + jnp.sum(pr, axis=0, keepdims=True)
                m_ref[h] = m_new
                out.append((h, pr.astype(BF16), alpha[0:1, :]))
            return out

        def values(items):
            for h, pr, alpha in items:
                pv = _dot(vt_ref[h * HEAD_DIM:(h + 1) * HEAD_DIM, pl.ds(off, height)], pr)
                acc_ref[h] = alpha * acc_ref[h] + pv

        sts = {0: scores(0), 1: scores(1)}
        done = None
        for p in range(PAIRS):
            cur = softmax(p, sts.pop(p))
            if p + 2 < PAIRS:
                sts[p + 2] = scores(p + 2)
            if done is not None:
                values(done)
            done = cur
        values(done)

    nfar = jnp.maximum(i - 1, 0)
    nfar_wide = nfar // (WIDE // LANES)

    def far_wide(c, carry):
        tile(pl.multiple_of(c * WIDE, WIDE), WIDE, None)
        return carry

    def far_narrow(j, carry):
        tile(pl.multiple_of(j * LANES, LANES), LANES, None)
        return carry

    lax.fori_loop(0, nfar_wide, far_wide, 0)
    lax.fori_loop(nfar_wide * (WIDE // LANES), nfar, far_narrow, 0)
    for h in range(N_HEADS):
        m_ref[h] = m_ref[h] + far_ref[h]

    @pl.when(i >= 1)
    def _():
        tile(pl.multiple_of((i - 1) * LANES, LANES), LANES, bs_ref)

    tile(pl.multiple_of(i * LANES, LANES), LANES, bd_ref)
    out_t = jnp.concatenate([acc_ref[h] / l_ref[h][0:1, :] for h in range(N_HEADS)], axis=0)
    o_ref[...] = jnp.transpose(out_t)


def _attn_prompt(far, h2, hb, vt, bias_diag_t, bias_sub_t, nsel):
    t = h2.shape[0]
    tq = LANES
    nq = IDX_HEADS * IDX_DIM
    return pl.pallas_call(
        functools.partial(_attn_prompt_kernel, tq=tq, nsel=nsel),
        grid=(t // tq,),
        in_specs=[pl.BlockSpec(memory_space=pltpu.SMEM),
                  pl.BlockSpec((tq, nq), lambda i: (i, C_QIDX // nq)),
                  pl.BlockSpec((tq, LANES), lambda i: (i, C_KW // LANES)),
                  pl.BlockSpec((tq, MIX_DIM), lambda i: (i, C_Q // MIX_DIM)),
                  pl.BlockSpec((t, LANES), lambda i: (0, C_KW // LANES)),
                  pl.BlockSpec((t, MIX_DIM), lambda i: (0, C_K // MIX_DIM)),
                  pl.BlockSpec((MIX_DIM, t), lambda i: (0, 0)),
                  pl.BlockSpec((N_HEADS, LANES, tq), lambda i: (0, 0, 0)),
                  pl.BlockSpec((N_HEADS, LANES, tq), lambda i: (0, 0, 0))],
        out_specs=pl.BlockSpec((tq, MIX_DIM), lambda i: (i, 0)),
        out_shape=jax.ShapeDtypeStruct((t, MIX_DIM), F32),
        scratch_shapes=[pltpu.VMEM((t, tq), I32),
                        pltpu.VMEM((IDX_DIM, IDX_HEADS * tq), BF16),
                        pltpu.VMEM((PAIRS, LANES, 2 * tq), BF16),
                        pltpu.VMEM((N_HEADS, SUBLANES, tq), F32),
                        pltpu.VMEM((N_HEADS, SUBLANES, tq), F32),
                        pltpu.VMEM((N_HEADS, HEAD_DIM, tq), F32)],
        compiler_params=_cparams(("parallel",)),
        name="attn_prompt",
    )(far, hb, h2, hb, hb, hb, vt, bias_diag_t, bias_sub_t)


QROWS = 8
SEL_PAGES = 64
ATT_PAGES = 32


SEL_ROWS = 8 * QROWS


def _score_sample_kernel(pt_ref, qst_ref, wcol_ref, *refs, treal):
    pages = refs[:SEL_PAGES]
    knew_ref, keys_ref = refs[SEL_PAGES:]
    j = pl.program_id(1)
    ntiles = keys_ref.shape[2] // LANES

    def score(kt_t):
        r = jnp.maximum(_dot(qst_ref[0], kt_t), 0.0) * (wcol_ref[0] * IDX_SCALE)
        s = r[0:QROWS]
        for hh in range(1, IDX_HEADS):
            s = s + r[hh * QROWS:(hh + 1) * QROWS]
        return _mono_key(s)

    for g in range(SEL_PAGES):
        off = pl.multiple_of((j * SEL_PAGES + g) * LANES, LANES)
        keys_ref[0, :, pl.ds(off, LANES)] = score(pages[g][0, 0].astype(BF16))

    @pl.when(j == pl.num_programs(1) - 1)
    def _():
        rowq = lax.broadcasted_iota(I32, (QROWS, LANES), 0)
        lane = lax.broadcasted_iota(I32, (QROWS, LANES), 1)
        knew = jnp.where((lane < treal) & (lane <= rowq), score(knew_ref[0]), INT_MIN)
        keys_ref[0, :, (ntiles - 1) * LANES:] = knew


def _select_sample_kernel(keys_ref, mask_ref, *, nsel, treal):
    rows, nk = keys_ref.shape
    ntiles = nk // LANES
    lane = lax.broadcasted_iota(I32, (rows, LANES), 1)

    def count(pred):
        acc = jnp.zeros((rows, LANES), F32)
        for jt in range(ntiles):
            kt = keys_ref[:, jt * LANES:(jt + 1) * LANES]
            acc = acc + jnp.where(pred(kt, jt * LANES + lane), 1.0, 0.0)
        return jnp.sum(acc, axis=1, keepdims=True)

    def tie_limit(v, want):
        upper = (lax.broadcasted_iota(I32, (LANES, LANES), 0) <= lax.broadcasted_iota(I32, (LANES, LANES), 1))
        upper = jnp.where(upper, 1.0, 0.0).astype(BF16)
        eqs = [keys_ref[:, jt * LANES:(jt + 1) * LANES] == v for jt in range(ntiles)]
        ranks = [_dot(jnp.where(eq, 1.0, 0.0).astype(BF16), upper) for eq in eqs]
        base = jnp.zeros((rows, 1), F32)
        best = jnp.full((rows, LANES), -1.0, F32)
        for jt in range(ntiles):
            ok = eqs[jt] & (base + ranks[jt] <= want)
            best = jnp.maximum(best, jnp.where(ok, (jt * LANES + lane).astype(F32), -1.0))
            base = base + ranks[jt][:, LANES - 1:LANES]
        return jnp.max(best, axis=1, keepdims=True).astype(I32)

    row_ok = lax.broadcasted_iota(I32, (rows, 1), 0) % QROWS < treal
    thr, jlim = _select(count, (rows, 1), nsel, row_ok, tie_limit)
    for jt in range(ntiles):
        kt = keys_ref[:, jt * LANES:(jt + 1) * LANES]
        sel = (kt > thr) | ((kt == thr) & (jt * LANES + lane <= jlim))
        mask_ref[:, jt * LANES:(jt + 1) * LANES] = jnp.where(sel, 1.0, 0.0)


def _sel_sample(page_table, qst, wcol, cache_kidx, knew, layer, nsel, treal):
    b, npages = page_table.shape
    nk = (npages + 1) * PAGE
    page_spec = lambda g: pl.BlockSpec(
        (1, 1, IDX_DIM, PAGE), lambda i, j, pt: (layer, pt[i, j * SEL_PAGES + g], 0, 0))
    grid_spec = pltpu.PrefetchScalarGridSpec(
        num_scalar_prefetch=1,
        grid=(b, npages // SEL_PAGES),
        in_specs=[pl.BlockSpec((1, IDX_HEADS * QROWS, IDX_DIM), lambda i, j, pt: (i, 0, 0)),
                  pl.BlockSpec((1, IDX_HEADS * QROWS, 1), lambda i, j, pt: (i, 0, 0))]
                 + [page_spec(g) for g in range(SEL_PAGES)]
                 + [pl.BlockSpec((1, IDX_DIM, PAGE), lambda i, j, pt: (i, 0, 0))],
        out_specs=pl.BlockSpec((1, QROWS, nk), lambda i, j, pt: (i, 0, 0)),
    )
    keys = pl.pallas_call(
        functools.partial(_score_sample_kernel, treal=treal),
        grid_spec=grid_spec,
        out_shape=jax.ShapeDtypeStruct((b, QROWS, nk), I32),
        compiler_params=_cparams(("parallel", "arbitrary")),
        name="score_sample",
    )(page_table, qst, wcol, *([cache_kidx] * SEL_PAGES), knew)
    rows = b * QROWS
    blk = math.gcd(rows, SEL_ROWS)
    mask = pl.pallas_call(
        functools.partial(_select_sample_kernel, nsel=nsel, treal=treal),
        grid=(rows // blk,),
        in_specs=[pl.BlockSpec((blk, nk), lambda i: (i, 0))],
        out_specs=pl.BlockSpec((blk, nk), lambda i: (i, 0)),
        out_shape=jax.ShapeDtypeStruct((rows, nk), F32),
        compiler_params=_cparams(("parallel",)),
        name="select_sample",
    )(keys.reshape(rows, nk))
    return mask.reshape(b, QROWS, nk)


def _attn_sample_kernel(pt_ref, far_ref, qh_ref, *refs):
    kp = refs[:ATT_PAGES]
    vp = refs[ATT_PAGES:2 * ATT_PAGES]
    mask_ref, masknew_ref, blast_ref, bnew_ref, knew_ref, vnew_ref, o_ref, m_ref, l_ref, acc_ref = refs[2 * ATT_PAGES:]
    j = pl.program_id(1)
    is_last = j == pl.num_programs(1) - 1

    @pl.when(j == 0)
    def _():
        m_ref[...] = jnp.full(m_ref.shape, M_INIT, F32)
        l_ref[...] = jnp.zeros(l_ref.shape, F32)
        acc_ref[...] = jnp.zeros(acc_ref.shape, F32)

    def update(lgs, vhs):
        prs, alphas = [], []
        for h in range(N_HEADS):
            m_old = m_ref[h]
            m_new = jnp.maximum(m_old, jnp.max(lgs[h], axis=1, keepdims=True))
            alpha = jnp.exp(m_old - m_new)
            pr = jnp.exp(lgs[h] - m_new[:, 0:1])
            l_ref[h] = alpha * l_ref[h] + jnp.sum(pr, axis=1, keepdims=True)
            m_ref[h] = m_new
            prs.append(pr.astype(BF16))
            alphas.append(alpha[:, :HEAD_DIM])
        for h in range(N_HEADS):
            acc_ref[h] = alphas[h] * acc_ref[h] + _dot_nt(prs[h], vhs[h])

    def q_of(h):
        return (qh_ref[0, h].astype(F32) * ATT_SCALE).astype(BF16)

    def head_t(refs_, h):
        return jnp.concatenate([refs_[g][0, 0, h] for g in range(ATT_PAGES)], axis=1).astype(BF16)

    amask = jnp.where(mask_ref[0] > 0.5, 0.0, NEG)
    lgs = []
    for h in range(N_HEADS):
        far = jnp.full((QROWS, LANES), far_ref[h], F32)
        bias = jnp.concatenate([far] * (ATT_PAGES - 1) + [jnp.where(is_last, blast_ref[h], far)], axis=1)
        lgs.append(_dot(q_of(h), head_t(kp, h)) + amask + bias)
    update(lgs, [head_t(vp, h) for h in range(N_HEADS)])

    @pl.when(is_last)
    def _():
        amask_new = jnp.where(masknew_ref[0] > 0.5, 0.0, NEG)
        lgs_new = [_dot(q_of(h), knew_ref[0, h]) + amask_new + bnew_ref[h] for h in range(N_HEADS)]
        update(lgs_new, [vnew_ref[0, h] for h in range(N_HEADS)])
        outs = []
        for h in range(N_HEADS):
            lh = l_ref[h][:, :HEAD_DIM]
            outs.append(acc_ref[h] / jnp.where(lh > 0.0, lh, 1.0))
        o_ref[0] = jnp.concatenate(outs, axis=1)


def _attn_sample(page_table, far, qh, cache_k2, cache_v2, mask, bias_last, bias_new, knew, vnew, layer):
    b, npages = page_table.shape
    nsteps = npages // ATT_PAGES
    page_spec = lambda g: pl.BlockSpec(
        (1, 1, N_HEADS, HEAD_DIM, PAGE), lambda i, j, pt, far: (layer, pt[i, j * ATT_PAGES + g], 0, 0, 0))
    const3 = lambda i, j, pt, far: (0, 0, 0)
    per_b4 = lambda i, j, pt, far: (i, 0, 0, 0)
    grid_spec = pltpu.PrefetchScalarGridSpec(
        num_scalar_prefetch=2,
        grid=(b, nsteps),
        in_specs=[pl.BlockSpec((1, N_HEADS, QROWS, HEAD_DIM), per_b4)]
                 + [page_spec(g) for g in range(ATT_PAGES)] * 2
                 + [pl.BlockSpec((1, QROWS, ATT_PAGES * PAGE), lambda i, j, pt, far: (i, 0, j)),
                    pl.BlockSpec((1, QROWS, PAGE), lambda i, j, pt, far: (i, 0, npages)),
                    pl.BlockSpec((N_HEADS, QROWS, PAGE), const3),
                    pl.BlockSpec((N_HEADS, QROWS, PAGE), const3),
                    pl.BlockSpec((1, N_HEADS, HEAD_DIM, PAGE), per_b4),
                    pl.BlockSpec((1, N_HEADS, HEAD_DIM, PAGE), per_b4)],
        out_specs=pl.BlockSpec((1, QROWS, MIX_DIM), lambda i, j, pt, far: (i, 0, 0)),
        scratch_shapes=[pltpu.VMEM((N_HEADS, QROWS, LANES), F32),
                        pltpu.VMEM((N_HEADS, QROWS, LANES), F32),
                        pltpu.VMEM((N_HEADS, QROWS, HEAD_DIM), F32)],
    )
    return pl.pallas_call(
        _attn_sample_kernel,
        grid_spec=grid_spec,
        out_shape=jax.ShapeDtypeStruct((b, QROWS, MIX_DIM), F32),
        compiler_params=_cparams(("parallel", "arbitrary")),
        name="attn_sample",
    )(page_table, far, qh, *([cache_k2] * ATT_PAGES), *([cache_v2] * ATT_PAGES), mask, mask,
      bias_last, bias_new, knew, vnew)


def _bucket_table(n):
    d = np.arange(n)
    nf = np.maximum(d, 1).astype(np.float32)
    half = REL_BUCKETS // 2
    large = half + (np.log(nf / np.float32(half)) / np.float32(math.log(REL_MAX_DIST / half))
                    * np.float32(REL_BUCKETS - half)).astype(np.int32)
    return np.where(d < half, d, np.minimum(large, REL_BUCKETS - 1)).astype(np.int32)


def _pack_w_in(w):
    w_t = jnp.transpose(w, (0, 2, 1))
    p, z, xbc, dt, q, k, v, qi, ki, wi, u, vs = jnp.split(w_t, np.cumsum(IN_SPLITS)[:-1].tolist(), axis=1)
    zeros = lambda n: jnp.zeros((w.shape[0], n, w.shape[1]), w.dtype)
    packed = jnp.concatenate(
        [p, z, xbc, q, k, v, u, vs, qi, ki, wi, zeros(LANES - IDX_DIM - IDX_HEADS), dt, zeros(LANES - N_HEADS)],
        axis=1)
    return packed.astype(BF16)


def _pad_lanes(x):
    return jnp.pad(x.reshape(1, -1), ((0, 0), (0, LANES - x.shape[-1])))


def _pad_axis(x, axis, size):
    pads = [(0, 0)] * x.ndim
    pads[axis] = (0, size - x.shape[axis])
    return jnp.pad(x, pads)


def kernel(x_prompt, x_sample, state_pool, state_conv, state_ssm, cache_k, cache_v, cache_kidx, page_table,
           rel_bias, norm_mix, w_in, pool_w, pool_scale, conv_w, conv_b, dt_bias, a_log, d_skip, ssm_norm,
           sgu_w, sgu_b, w_out, norm_mlp, mlp_w1, mlp_w2, norm_final):
    depth = w_in.shape[0]
    bp, t, _ = x_prompt.shape
    bs, ts, _ = x_sample.shape
    npages = page_table.shape[1]
    past = npages * PAGE
    n_pool = cache_k.shape[1]
    assert bp == 1 and t % WIDE == 0 and npages % SEL_PAGES == 0 and npages % ATT_PAGES == 0
    assert max(t, past + PAGE) <= 2 ** POS_BITS and ts <= QROWS
    tc = LANES
    nsel_p = min(TOPK_MAX, t // 4)
    nsel_s = min(TOPK_MAX, (past + ts) // 4)

    bucket = _bucket_table(2 * LANES)
    far = rel_bias[REL_BUCKETS - 1]
    kk = np.arange(LANES)[:, None]
    qq = np.arange(LANES)[None, :]
    def bias_of(dist):
        onehot = np.eye(REL_BUCKETS, dtype=np.float32)[bucket[dist].reshape(-1)]
        vals = jnp.dot(onehot, rel_bias, precision=lax.Precision.HIGHEST)
        return jnp.transpose(vals).reshape((N_HEADS,) + dist.shape)

    bias_diag_t = bias_of(np.maximum(qq - kk, 0))
    bias_sub_t = bias_of(np.minimum(LANES + qq - kk, 2 * LANES - 1))
    sq = np.arange(QROWS)[:, None]
    sk = np.arange(PAGE)[None, :]
    bias_last = bias_of(np.minimum(PAGE + sq - sk, 2 * LANES - 1))
    bias_new = bias_of(np.maximum(sq - sk, 0))

    cache_k2 = jnp.transpose(cache_k, (0, 1, 3, 4, 2))
    cache_v2 = jnp.transpose(cache_v, (0, 1, 3, 4, 2))
    cache_kidx2 = jnp.transpose(cache_kidx, (0, 1, 3, 2))
    w_in_t = _pack_w_in(w_in)
    w_out_b, w1_b, w2_b = w_out.astype(BF16), mlp_w1.astype(BF16), mlp_w2.astype(BF16)

    xp = x_prompt.reshape(t, D_MODEL)
    xs = x_sample.reshape(bs * ts, D_MODEL)
    zero_pool = jnp.zeros((1, POOL_HIST, MIX_DIM), F32)
    zero_conv = jnp.zeros((1, CONV_W - 1, XBC_DIM), F32)
    zero_ssm = jnp.zeros((1, N_HEADS, HEAD_DIM, SSM_STATE), F32)
    tm_p = min(1024, t)
    tm_s = bs * ts
    outs_p, outs_s = [], []

    for l in range(depth):
        g_mix = norm_mix[l].reshape(1, D_MODEL)
        g_mlp = norm_mlp[l].reshape(1, D_MODEL)
        g_out = norm_final.reshape(1, D_MODEL) if l == depth - 1 else g_mlp
        pscale = pool_scale[l].reshape(1, MIX_DIM)
        cb = conv_b[l].reshape(1, XBC_DIM)
        dtb, alog, dsk = _pad_lanes(dt_bias[l]), _pad_lanes(a_log[l]), _pad_lanes(d_skip[l])
        snorm = ssm_norm[l].reshape(1, MIX_DIM)
        sgu_b_col = sgu_b[l][:, :, None]
        final = l == depth - 1

        hp, hpb, vt = _inproj(xp, g_mix, w_in_t, l, tm=min(1024, t), tn=1024, attn_operands=True)
        hp3 = hp.reshape(1, t, H_PACKED)
        y_pool, new_pool = _pool(hp3, zero_pool, pool_w[l], pscale, tc=WIDE, treal=WIDE, pos0=0)
        y_ssm, new_conv, new_h = _ssd(hp3, zero_conv, zero_ssm, conv_w[l], cb, dtb, alog, dsk, snorm,
                                      tc=tc, treal=tc)
        y_sgu, = _sgu(hp3, sgu_w[l], sgu_b_col, tc=tc, treal=tc, nsub=WIDE // tc)
        kf = hp[:, C_K:C_K + MIX_DIM]
        vf = hp[:, C_V:C_V + MIX_DIM]
        kif = hp[:, C_KW:C_KW + IDX_DIM]
        y_att = _attn_prompt(far, hp, hpb, vt, bias_diag_t, bias_sub_t, nsel_p)
        x1, xn = _outproj(xp, [y_pool[0], y_ssm[0], y_att, y_sgu[0]], w_out_b, l, g_mlp, tm=256)
        xp = _mlp(xn, w1_b, w2_b, l, x1, g_out, tm=tm_p, final_norm=final)
        outs_p.append((new_pool, new_conv, new_h, kf.reshape(1, t, N_HEADS, HEAD_DIM),
                       vf.reshape(1, t, N_HEADS, HEAD_DIM), kif.reshape(1, t, IDX_DIM)))

        hs, = _inproj(xs, g_mix, w_in_t, l, tm=tm_s)
        hs3 = hs.reshape(bs, ts, H_PACKED)
        y_pool, new_pool = _pool(hs3, state_pool[l], pool_w[l], pscale, tc=tc, treal=ts, pos0=past)
        y_ssm, new_conv, new_h = _ssd(hs3, state_conv[l], state_ssm[l], conv_w[l], cb, dtb, alog, dsk, snorm,
                                      tc=tc, treal=ts)
        y_sgu, v_rows = _sgu(hs3, sgu_w[l], sgu_b_col, tc=tc, treal=ts, emit_v=True)
        kf = hs3[:, :, C_K:C_K + MIX_DIM]
        vf = hs3[:, :, C_V:C_V + MIX_DIM]
        kif = hs3[:, :, C_KW:C_KW + IDX_DIM]
        qi = hs3[:, :, C_QIDX:C_QIDX + IDX_HEADS * IDX_DIM].reshape(bs, ts, IDX_HEADS, IDX_DIM)
        qst = _pad_axis(jnp.transpose(qi, (0, 2, 1, 3)), 2, QROWS)
        qst = qst.reshape(bs, IDX_HEADS * QROWS, IDX_DIM).astype(BF16)
        wi = hs3[:, :, C_KW + IDX_DIM:C_KW + IDX_DIM + IDX_HEADS]
        wcol = _pad_axis(jnp.transpose(wi, (0, 2, 1)), 2, QROWS).reshape(bs, IDX_HEADS * QROWS, 1)
        kinew_t = _pad_axis(jnp.transpose(kif, (0, 2, 1)), 2, PAGE).astype(BF16)
        mask = _sel_sample(page_table, qst, wcol, cache_kidx2, kinew_t, l, nsel_s, ts)
        heads = lambda x, perm: jnp.transpose(x.reshape(bs, ts, N_HEADS, HEAD_DIM), perm)
        qh = _pad_axis(heads(hs3[:, :, C_Q:C_Q + MIX_DIM], (0, 2, 1, 3)), 2, QROWS).astype(BF16)
        knew = _pad_axis(heads(kf, (0, 2, 3, 1)), 3, PAGE).astype(BF16)
        vnew = _pad_axis(heads(vf, (0, 2, 3, 1)), 3, PAGE).astype(BF16)
        att = _attn_sample(page_table, far, qh, cache_k2, cache_v2, mask, bias_last, bias_new, knew, vnew, l)
        y_att = att[:, :ts].reshape(bs * ts, MIX_DIM)
        flat = lambda y: y.reshape(bs * ts, MIX_DIM)
        x1, xn = _outproj(xs, [flat(y_pool), flat(y_ssm), y_att, flat(y_sgu)], w_out_b, l, g_mlp, tm=tm_s)
        xs = _mlp(xn, w1_b, w2_b, l, x1, g_out, tm=tm_s, final_norm=final)
        outs_s.append((new_pool, new_conv, new_h, kf.reshape(bs, ts, N_HEADS, HEAD_DIM),
                       vf.reshape(bs, ts, N_HEADS, HEAD_DIM), kif, v_rows))

    stack = lambda outs, i: jnp.stack([o[i] for o in outs])
    return (xp.reshape(1, t, D_MODEL), xs.reshape(bs, ts, D_MODEL),
            *[stack(outs_p, i) for i in range(6)],
            *[stack(outs_s, i) for i in range(7)])
```

```python
import functools
import math

import numpy as np
import jax
import jax.numpy as jnp
from jax import lax
from jax.experimental import pallas as pl
from jax.experimental.pallas import tpu as pltpu

F32 = jnp.float32
BF16 = jnp.bfloat16
I32 = jnp.int32

LANES = 128
SUBLANES = 8
D_MODEL = 2048
N_HEADS = 8
HEAD_DIM = 64
MIX_DIM = 512
POOL_WINDOWS = (2, 4, 8, 16)
POOL_HIST = 15
CONV_W = 4
SSM_GROUPS = 2
SSM_STATE = 128
XBC_DIM = MIX_DIM + 2 * SSM_GROUPS * SSM_STATE
IDX_HEADS = 4
IDX_DIM = 64
TOPK_MAX = 256
PAGE = 128
REL_BUCKETS = 32
REL_MAX_DIST = 128
ATT_SCALE = HEAD_DIM ** -0.5
IDX_SCALE = (IDX_HEADS * IDX_DIM) ** -0.5
SGU_GROUPS = 4
D_FF = 4 * D_MODEL
EPS = 1e-6
NEG = -1e30
INT_MIN = -(2 ** 31)
INT_MAX = 2 ** 31 - 1
IN_SPLITS = (512, 512, 1024, 8, 512, 512, 512, 256, 64, 4, 512, 512)

C_P, C_Z, C_XBC, C_Q, C_K, C_V, C_U, C_VS, C_QIDX, C_KW, C_DT = (
    0, 512, 1024, 2048, 2560, 3072, 3584, 4096, 4608, 4864, 4992)
H_PACKED = 5120
VMEM_LIMIT = 56 * 1024 * 1024


def _cparams(sem):
    return pltpu.CompilerParams(dimension_semantics=sem, vmem_limit_bytes=VMEM_LIMIT)


def _dot(a, b):
    return jnp.dot(a, b, preferred_element_type=F32)


def _dot_nt(a, b):
    return lax.dot_general(a, b, (((1,), (1,)), ((), ())), preferred_element_type=F32)


def _dot_tn(a, b):
    return lax.dot_general(a, b, (((0,), (0,)), ((), ())), preferred_element_type=F32)


def _silu(x):
    return x * (1.0 / (1.0 + jnp.exp(-x)))


def _gelu_tanh(x):
    return 0.5 * x * (1.0 + jnp.tanh(math.sqrt(2.0 / math.pi) * (x + 0.044715 * (x * x * x))))


def _softplus(x):
    return jnp.maximum(x, 0.0) + jnp.log1p(jnp.exp(-jnp.abs(x)))


def _rms(x, g):
    ms = jnp.mean(x * x, axis=-1, keepdims=True)
    return x * lax.rsqrt(ms + EPS) * g


IN_TILE = 512
XBC_END = sum(IN_SPLITS[:3])
Q_SRC = sum(IN_SPLITS[:4])
N_MAIN = (XBC_END + 3 * MIX_DIM) // IN_TILE
N_TAIL = (H_PACKED - N_MAIN * IN_TILE) // IN_TILE


def _inproj_kernel(x_ref, g_ref, wmain_ref, wtail_ref, o_ref, *rest, attn_operands, tn):
    xn_ref, wbuf_ref = rest[-2:]
    j = pl.program_id(1)

    @pl.when(j == 0)
    def _():
        xn_ref[...] = _rms(x_ref[...], g_ref[...]).astype(BF16)

    @pl.when(j < N_MAIN)
    def _():
        wbuf_ref[...] = wmain_ref[0].astype(BF16)

    @pl.when(j >= N_MAIN)
    def _():
        wbuf_ref[...] = wtail_ref[0].astype(BF16)

    h = _dot_nt(xn_ref[...], wbuf_ref[...])
    o_ref[...] = h
    if attn_operands:
        ob_ref, vt_ref = rest[:2]
        ob_ref[...] = h.astype(BF16)

        @pl.when(pl.program_id(1) == C_V // tn)
        def _():
            vt_ref[...] = jnp.transpose(h[:, C_V % tn:C_V % tn + MIX_DIM]).astype(BF16)


def _inproj(x, g, w_t, w_tail, layer, tm, attn_operands=False):
    m = x.shape[0]
    tn = IN_TILE
    n = H_PACKED

    def main_start(j):
        jj = jnp.minimum(j, N_MAIN - 1)
        start = jnp.where(jj < XBC_END // tn, jj * tn, Q_SRC + (jj - XBC_END // tn) * tn)
        return pl.multiple_of(start, SUBLANES)

    out_specs = [pl.BlockSpec((tm, tn), lambda i, j: (i, j))]
    out_shape = [jax.ShapeDtypeStruct((m, n), F32)]
    if attn_operands:
        out_specs += [pl.BlockSpec((tm, tn), lambda i, j: (i, j)), pl.BlockSpec((MIX_DIM, tm), lambda i, j: (0, i))]
        out_shape += [jax.ShapeDtypeStruct((m, n), BF16), jax.ShapeDtypeStruct((MIX_DIM, m), BF16)]
    return pl.pallas_call(
        functools.partial(_inproj_kernel, attn_operands=attn_operands, tn=tn),
        grid=(m // tm, n // tn),
        in_specs=[pl.BlockSpec((tm, D_MODEL), lambda i, j: (i, 0)),
                  pl.BlockSpec((1, D_MODEL), lambda i, j: (0, 0)),
                  pl.BlockSpec((pl.Element(1), pl.Element(tn), pl.Element(D_MODEL)),
                               lambda i, j: (layer, main_start(j), 0)),
                  pl.BlockSpec((1, tn, D_MODEL), lambda i, j: (layer, jnp.maximum(j - N_MAIN, 0), 0))],
        out_specs=out_specs,
        out_shape=out_shape,
        scratch_shapes=[pltpu.VMEM((tm, D_MODEL), BF16), pltpu.VMEM((tn, D_MODEL), BF16)],
        compiler_params=_cparams(("parallel", "arbitrary")),
        name="inproj",
    )(x, g, w_t, w_tail)


def _outproj_kernel(x_ref, y0_ref, y1_ref, y2_ref, y3_ref, w_ref, g_ref, x1_ref, xn_ref):
    y = jnp.concatenate([y0_ref[...], y1_ref[...], y2_ref[...], y3_ref[...]], axis=1).astype(BF16)
    x1 = x_ref[...] + _dot(y, w_ref[0])
    x1_ref[...] = x1
    xn_ref[...] = _rms(x1, g_ref[...]).astype(BF16)


def _outproj(x, ys, w, layer, g, tm):
    m = x.shape[0]
    row = lambda i: (i, 0)
    return pl.pallas_call(
        _outproj_kernel,
        grid=(m // tm,),
        in_specs=[pl.BlockSpec((tm, D_MODEL), row)]
                 + [pl.BlockSpec((tm, MIX_DIM), row)] * 4
                 + [pl.BlockSpec((1, D_MODEL, D_MODEL), lambda i: (layer, 0, 0)),
                    pl.BlockSpec((1, D_MODEL), lambda i: (0, 0))],
        out_specs=[pl.BlockSpec((tm, D_MODEL), row), pl.BlockSpec((tm, D_MODEL), row)],
        out_shape=[jax.ShapeDtypeStruct((m, D_MODEL), F32), jax.ShapeDtypeStruct((m, D_MODEL), BF16)],
        compiler_params=_cparams(("parallel",)),
        name="outproj",
    )(x, *ys, w, g)


def _mlp_kernel(xn_ref, w1_ref, w2_ref, x1_ref, g_ref, o_ref, *, final_norm):
    j = pl.program_id(1)

    @pl.when(j == 0)
    def _():
        o_ref[...] = x1_ref[...]

    hid = _dot(xn_ref[...], w1_ref[0])
    hid = jnp.square(jnp.maximum(hid, 0.0)).astype(BF16)
    o_ref[...] += _dot(hid, w2_ref[0])

    if final_norm:
        @pl.when(j == pl.num_programs(1) - 1)
        def _():
            o_ref[...] = _rms(o_ref[...], g_ref[...])


def _mlp(xn, w1, w2, layer, x1, g, tm, final_norm, tf=512):
    m = xn.shape[0]
    return pl.pallas_call(
        functools.partial(_mlp_kernel, final_norm=final_norm),
        grid=(m // tm, D_FF // tf),
        in_specs=[pl.BlockSpec((tm, D_MODEL), lambda i, j: (i, 0)),
                  pl.BlockSpec((1, D_MODEL, tf), lambda i, j: (layer, 0, j)),
                  pl.BlockSpec((1, tf, D_MODEL), lambda i, j: (layer, j, 0)),
                  pl.BlockSpec((tm, D_MODEL), lambda i, j: (i, 0)),
                  pl.BlockSpec((1, D_MODEL), lambda i, j: (0, 0))],
        out_specs=pl.BlockSpec((tm, D_MODEL), lambda i, j: (i, 0)),
        out_shape=jax.ShapeDtypeStruct((m, D_MODEL), F32),
        compiler_params=_cparams(("parallel", "arbitrary")),
        name="mlp",
    )(xn, w1, w2, x1, g)


def _pool_kernel(p_ref, hist_ref, w_ref, scale_ref, y_ref, newhist_ref, ext_ref, *, tc, treal, pos0):
    c = pl.program_id(1)
    hrows = POOL_HIST + 1

    @pl.when(c == 0)
    def _():
        ext_ref[0:1, :] = jnp.zeros((1, MIX_DIM), F32)
        ext_ref[1:hrows, :] = hist_ref[0]

    if treal < tc:
        ext_ref[hrows:hrows + tc, :] = jnp.zeros((tc, MIX_DIM), F32)
    ext_ref[hrows:hrows + treal, :] = p_ref[0]

    pos = pos0 + c * treal + lax.broadcasted_iota(I32, (tc, 1), 0)
    for g, win in enumerate(POOL_WINDOWS):
        cols = slice(g * LANES, (g + 1) * LANES)
        cur = ext_ref[hrows:hrows + tc, cols]
        s = cur
        for j in range(1, win):
            s = s + ext_ref[hrows - j:hrows - j + tc, cols]
        cnt = jnp.minimum(pos + 1, win).astype(F32)
        diff = s / cnt - cur
        y = _dot(diff, w_ref[g]) * scale_ref[:, cols]
        y_ref[0, :, cols] = y[:treal]

    newhist_ref[0] = ext_ref[treal + 1:treal + hrows, :]
    ext_ref[0:hrows, :] = ext_ref[treal:treal + hrows, :]


def _pool(h3, hist, pool_w, pool_scale, *, tc, treal, pos0):
    b, t, _ = h3.shape
    return pl.pallas_call(
        functools.partial(_pool_kernel, tc=tc, treal=treal, pos0=pos0),
        grid=(b, t // treal),
        in_specs=[pl.BlockSpec((1, treal, MIX_DIM), lambda i, c: (i, c, C_P // MIX_DIM)),
                  pl.BlockSpec((1, POOL_HIST, MIX_DIM), lambda i, c: (i, 0, 0)),
                  pl.BlockSpec((4, LANES, LANES), lambda i, c: (0, 0, 0)),
                  pl.BlockSpec((1, MIX_DIM), lambda i, c: (0, 0))],
        out_specs=[pl.BlockSpec((1, treal, MIX_DIM), lambda i, c: (i, c, 0)),
                   pl.BlockSpec((1, POOL_HIST, MIX_DIM), lambda i, c: (i, 0, 0))],
        out_shape=[jax.ShapeDtypeStruct((b, t, MIX_DIM), F32),
                   jax.ShapeDtypeStruct((b, POOL_HIST, MIX_DIM), F32)],
        scratch_shapes=[pltpu.VMEM((POOL_HIST + 1 + tc, MIX_DIM), F32)],
        compiler_params=_cparams(("parallel", "arbitrary")),
        name="pool",
    )(h3, hist, pool_w, pool_scale)


def _sgu_kernel(u_ref, v_ref, w_ref, b_ref, y_ref, *rest, tc, treal, nsub, emit_v):
    ubuf_ref, vbuf_ref = rest[-2:]
    r = lax.broadcasted_iota(I32, (tc, tc), 0)
    s = lax.broadcasted_iota(I32, (tc, tc), 1)
    for sub in range(nsub):
        rows = slice(sub * treal, (sub + 1) * treal)
        if treal < tc:
            ubuf_ref[...] = jnp.zeros((tc, MIX_DIM), F32)
            vbuf_ref[...] = jnp.zeros((tc, MIX_DIM), F32)
        ubuf_ref[0:treal, :] = u_ref[0, rows, :]
        vbuf_ref[0:treal, :] = v_ref[0, rows, :]
        u = _gelu_tanh(ubuf_ref[...])
        v = _gelu_tanh(vbuf_ref[...])
        if emit_v:
            rest[0][0, rows, :] = v[:treal]
        for g in range(SGU_GROUPS):
            cols = slice(g * LANES, (g + 1) * LANES)
            w = jnp.where(s <= r, w_ref[g], 0.0)
            mixed = _dot(w, v[:, cols]) + b_ref[g]
            y_ref[0, rows, cols] = (u[:, cols] * mixed)[:treal]


def _sgu(h3, sgu_w, sgu_b_col, *, tc, treal, nsub=1, emit_v=False):
    b, t, _ = h3.shape
    rows = nsub * treal
    out_specs = [pl.BlockSpec((1, rows, MIX_DIM), lambda i, c: (i, c, 0))]
    out_shape = [jax.ShapeDtypeStruct((b, t, MIX_DIM), F32)]
    return pl.pallas_call(
        functools.partial(_sgu_kernel, tc=tc, treal=treal, nsub=nsub, emit_v=emit_v),
        grid=(b, t // rows),
        in_specs=[pl.BlockSpec((1, rows, MIX_DIM), lambda i, c: (i, c, C_U // MIX_DIM)),
                  pl.BlockSpec((1, rows, MIX_DIM), lambda i, c: (i, c, C_VS // MIX_DIM)),
                  pl.BlockSpec((SGU_GROUPS, tc, tc), lambda i, c: (0, 0, 0)),
                  pl.BlockSpec((SGU_GROUPS, tc, 1), lambda i, c: (0, 0, 0))],
        out_specs=out_specs * (2 if emit_v else 1),
        out_shape=out_shape * (2 if emit_v else 1),
        scratch_shapes=[pltpu.VMEM((tc, MIX_DIM), F32), pltpu.VMEM((tc, MIX_DIM), F32)],
        compiler_params=_cparams(("parallel", "parallel")),
        name="sgu",
    )(h3, h3, sgu_w, sgu_b_col)


def _cumsum_rows(a, n):
    row = lax.broadcasted_iota(I32, a.shape, 0)
    sh = 1
    while sh < n:
        a = a + jnp.where(row >= sh, pltpu.roll(a, sh, 0), 0.0)
        sh *= 2
    return a


def _ssd_kernel(z_ref, xbc_ref, dt_ref, hc_ref, h0_ref, convw_ref, convb_ref, dtb_ref, alog_ref, dskip_ref,
                norm_ref, y_ref, newconv_ref, newh_ref, ext_ref, zbuf_ref, dtbuf_ref, state_ref, *, tc, treal):
    c = pl.program_id(1)
    pre = SUBLANES
    hist = CONV_W - 1

    @pl.when(c == 0)
    def _():
        ext_ref[0:pre - hist, :] = jnp.zeros((pre - hist, XBC_DIM), F32)
        ext_ref[pre - hist:pre, :] = hc_ref[0]
        state_ref[...] = h0_ref[...]

    if treal < tc:
        ext_ref[pre:pre + tc, :] = jnp.zeros((tc, XBC_DIM), F32)
        zbuf_ref[...] = jnp.zeros((tc, MIX_DIM), F32)
        dtbuf_ref[...] = jnp.zeros((tc, LANES), F32)
    ext_ref[pre:pre + treal, :] = xbc_ref[0]
    zbuf_ref[0:treal, :] = z_ref[0]
    dtbuf_ref[0:treal, :] = dt_ref[0]

    conv = convb_ref[...]
    for j in range(CONV_W):
        conv = conv + ext_ref[pre - hist + j:pre - hist + j + tc, :] * convw_ref[j:j + 1, :]
    conv = _silu(conv)
    xs = conv[:, :MIX_DIM]
    bmat = conv[:, MIX_DIM:MIX_DIM + SSM_GROUPS * SSM_STATE]
    cmat = conv[:, MIX_DIM + SSM_GROUPS * SSM_STATE:]

    rowi = lax.broadcasted_iota(I32, (tc, LANES), 0)
    dt = _softplus(dtbuf_ref[...] + dtb_ref[...])
    if treal < tc:
        dt = jnp.where(rowi < treal, dt, 0.0)
    a = dt * (-jnp.exp(alog_ref[...]))
    acum = _cumsum_rows(a, tc)
    acum_t = jnp.transpose(acum)
    total = acum[tc - 1:tc, :]
    causal = lax.broadcasted_iota(I32, (tc, tc), 1) <= lax.broadcasted_iota(I32, (tc, tc), 0)

    ys = []
    heads_per_group = N_HEADS // SSM_GROUPS
    for g in range(SSM_GROUPS):
        bg = bmat[:, g * SSM_STATE:(g + 1) * SSM_STATE]
        cg = cmat[:, g * SSM_STATE:(g + 1) * SSM_STATE]
        cb = _dot_nt(cg, bg)
        for hh in range(heads_per_group):
            h = g * heads_per_group + hh
            col = acum[:, h:h + 1]
            row = acum_t[h:h + 1, :]
            decay = jnp.exp(jnp.where(causal, col - row, NEG))
            xh = xs[:, h * HEAD_DIM:(h + 1) * HEAD_DIM]
            xdt = xh * dt[:, h:h + 1]
            hprev = state_ref[0, h]
            y = _dot(cb * decay, xdt)
            y = y + _dot_nt(cg, hprev) * jnp.exp(col)
            y = y + dskip_ref[:, h:h + 1] * xh
            ys.append(y)
            tot = total[:, h:h + 1]
            st = _dot_tn(xdt * jnp.exp(tot - col), bg)
            state_ref[0, h] = jnp.exp(tot) * hprev + st
    y = jnp.concatenate(ys, axis=1) * _silu(zbuf_ref[...])
    gw = MIX_DIM // SSM_GROUPS
    outs = []
    for g in range(SSM_GROUPS):
        yg = y[:, g * gw:(g + 1) * gw]
        ms = jnp.mean(yg * yg, axis=-1, keepdims=True)
        outs.append(yg * lax.rsqrt(ms + EPS) * norm_ref[:, g * gw:(g + 1) * gw])
    y_ref[0] = jnp.concatenate(outs, axis=1)[:treal]

    newconv_ref[0] = ext_ref[pre + treal - hist:pre + treal, :]
    ext_ref[0:pre, :] = ext_ref[treal:treal + pre, :]
    newh_ref[...] = state_ref[...]


def _ssd(h3, hist_conv, h0, conv_w, conv_b, dtb, alog, dskip, norm, *, tc, treal):
    b, t, _ = h3.shape
    full2 = lambda i, c: (0, 0)
    return pl.pallas_call(
        functools.partial(_ssd_kernel, tc=tc, treal=treal),
        grid=(b, t // treal),
        in_specs=[pl.BlockSpec((1, treal, MIX_DIM), lambda i, c: (i, c, C_Z // MIX_DIM)),
                  pl.BlockSpec((1, treal, XBC_DIM), lambda i, c: (i, c, C_XBC // XBC_DIM)),
                  pl.BlockSpec((1, treal, LANES), lambda i, c: (i, c, C_DT // LANES)),
                  pl.BlockSpec((1, CONV_W - 1, XBC_DIM), lambda i, c: (i, 0, 0)),
                  pl.BlockSpec((1, N_HEADS, HEAD_DIM, SSM_STATE), lambda i, c: (i, 0, 0, 0)),
                  pl.BlockSpec((CONV_W, XBC_DIM), full2),
                  pl.BlockSpec((1, XBC_DIM), full2),
                  pl.BlockSpec((1, LANES), full2),
                  pl.BlockSpec((1, LANES), full2),
                  pl.BlockSpec((1, LANES), full2),
                  pl.BlockSpec((1, MIX_DIM), full2)],
        out_specs=[pl.BlockSpec((1, treal, MIX_DIM), lambda i, c: (i, c, 0)),
                   pl.BlockSpec((1, CONV_W - 1, XBC_DIM), lambda i, c: (i, 0, 0)),
                   pl.BlockSpec((1, N_HEADS, HEAD_DIM, SSM_STATE), lambda i, c: (i, 0, 0, 0))],
        out_shape=[jax.ShapeDtypeStruct((b, t, MIX_DIM), F32),
                   jax.ShapeDtypeStruct((b, CONV_W - 1, XBC_DIM), F32),
                   jax.ShapeDtypeStruct((b, N_HEADS, HEAD_DIM, SSM_STATE), F32)],
        scratch_shapes=[pltpu.VMEM((SUBLANES + tc, XBC_DIM), F32),
                        pltpu.VMEM((tc, MIX_DIM), F32),
                        pltpu.VMEM((tc, LANES), F32),
                        pltpu.VMEM((1, N_HEADS, HEAD_DIM, SSM_STATE), F32)],
        compiler_params=_cparams(("parallel", "arbitrary")),
        name="ssd",
    )(h3, h3, h3, hist_conv, h0, conv_w, conv_b, dtb, alog, dskip, norm)


M_INIT = 0.5 * NEG
POS_BITS = 14
assert math.frexp(ATT_SCALE)[0] == 0.5, "q is pre-scaled in bf16, exact only for a power-of-two scale"


def _mono_key(s):
    s = jnp.where(s == 0.0, 0.0, s)
    b = lax.bitcast_convert_type(s, I32)
    return b ^ ((b >> 31) & INT_MAX)


def _select(count, stat_shape, nsel, query_ok, tie_limit=None, threshold=None):
    kf = float(nsel)

    def bit_threshold():
        c0 = count(lambda kt, pos: kt >= 0)
        nonneg = c0 >= kf
        prefix = jnp.where(nonneg, 0, INT_MIN).astype(I32)

        def bit_body(it, carry):
            prefix, cge = carry
            cand = prefix | jnp.left_shift(jnp.int32(1), 30 - it)
            cnt = count(lambda kt, pos: kt >= cand)
            take = cnt >= kf
            return jnp.where(take, cand, prefix), jnp.where(take, cnt, cge)

        return lax.fori_loop(0, 31, bit_body, (prefix, jnp.where(nonneg, c0, kf)))

    v, cge = (threshold or bit_threshold)()
    need = (cge > kf) & (v != INT_MIN) & query_ok

    def bit_tie_limit(v, want):
        def body(it, x):
            cand = x | jnp.left_shift(jnp.int32(1), POS_BITS - 1 - it)
            cnt = count(lambda kt, pos: (kt == v) & (pos < cand))
            return jnp.where(cnt < want, cand, x)
        return lax.fori_loop(0, POS_BITS, body, jnp.zeros(stat_shape, I32))

    def tie_search():
        want = kf - count(lambda kt, pos: kt > v)
        limit = (tie_limit or bit_tie_limit)(v, want)
        return jnp.where(need, limit, INT_MAX)

    any_need = jnp.max(jnp.where(need, 1.0, 0.0)) > 0.0
    jlim = lax.cond(any_need, tie_search, lambda: jnp.full(stat_shape, INT_MAX, I32))
    return v, jlim


WIDE = 4 * LANES
PAIRS = N_HEADS // 2
COUNT_ROWS = 8 * SUBLANES


def _attn_prompt_kernel(far_ref, qidx_ref, kw_ref, q_ref, kidx_ref, k_ref, vt_ref, bd_ref, bs_ref, o_ref,
                        keys_ref, qit_ref, qz_ref, m_ref, l_ref, acc_ref, *, tq, nsel):
    i = pl.program_id(0)
    qpos = i * tq + lax.broadcasted_iota(I32, (1, tq), 1)
    nwide = i // (WIDE // LANES) + 1

    qi_t = jnp.transpose(qidx_ref[...].astype(F32)).astype(BF16)
    for hh in range(IDX_HEADS):
        qit_ref[:, hh * tq:(hh + 1) * tq] = qi_t[hh * IDX_DIM:(hh + 1) * IDX_DIM, :]
    w_t = jnp.transpose(kw_ref[...])[IDX_DIM:IDX_DIM + IDX_HEADS, :] * IDX_SCALE
    q_t = jnp.transpose(q_ref[...].astype(F32) * ATT_SCALE).astype(BF16)
    upper = lax.broadcasted_iota(I32, (LANES, tq), 0) < HEAD_DIM
    zero = jnp.zeros((LANES, tq), BF16)
    for p in range(PAIRS):
        blk = q_t[p * LANES:(p + 1) * LANES, :]
        qz_ref[p, :, 0:tq] = jnp.where(upper, blk, zero)
        qz_ref[p, :, tq:2 * tq] = jnp.where(upper, zero, blk)

    def score_body(c, carry):
        off = pl.multiple_of(c * WIDE, WIDE)
        kt = kidx_ref[pl.ds(off, WIDE), 0:IDX_DIM]
        sc = _dot(kt, qit_ref[...])
        s = jnp.zeros((WIDE, tq), F32)
        for hh in range(IDX_HEADS):
            s = s + jnp.maximum(sc[:, hh * tq:(hh + 1) * tq], 0.0) * w_t[hh:hh + 1, :]
        kpos = off + lax.broadcasted_iota(I32, (WIDE, tq), 0)
        keys_ref[pl.ds(off, WIDE), :] = jnp.where(kpos <= qpos, _mono_key(s), INT_MIN)
        return carry

    lax.fori_loop(0, nwide, score_body, 0)

    def count(pred):
        def body(c, acc):
            off = pl.multiple_of(c * WIDE, WIDE)
            kt = keys_ref[pl.ds(off, WIDE), :]
            kpos = off + lax.broadcasted_iota(I32, (WIDE, tq), 0)
            hit = jnp.where(pred(kt, kpos), 1.0, 0.0)
            return acc + jnp.sum(hit.reshape(WIDE // COUNT_ROWS, COUNT_ROWS, tq), axis=0)
        acc = lax.fori_loop(0, nwide, body, jnp.zeros((COUNT_ROWS, tq), F32))
        return jnp.sum(acc, axis=0, keepdims=True)


    def tie_limit(v, want):
        lower = (lax.broadcasted_iota(I32, (WIDE, WIDE), 1) <= lax.broadcasted_iota(I32, (WIDE, WIDE), 0))
        lower = jnp.where(lower, 1.0, 0.0).astype(BF16)

        def body(c, carry):
            base, limit = carry
            off = pl.multiple_of(c * WIDE, WIDE)
            eq = keys_ref[pl.ds(off, WIDE), :] == v
            rank = base + _dot(lower, jnp.where(eq, 1.0, 0.0).astype(BF16))
            kpos = (off + lax.broadcasted_iota(I32, (WIDE, tq), 0)).astype(F32)
            hit = jnp.max(jnp.where(eq & (rank <= want), kpos, -1.0), axis=0, keepdims=True)
            return rank[WIDE - 1:WIDE, :], jnp.maximum(limit, hit)

        zero = jnp.zeros((1, tq), F32)
        return lax.fori_loop(0, nwide, body, (zero, zero - 1.0))[1].astype(I32)

    thr, jlim = _select(count, (1, tq), nsel, True, tie_limit)

    m_ref[...] = jnp.full(m_ref.shape, M_INIT, F32)
    l_ref[...] = jnp.zeros(l_ref.shape, F32)
    acc_ref[...] = jnp.zeros(acc_ref.shape, F32)

    def tile(off, height, bias_ref):
        kt = keys_ref[pl.ds(off, height), :]
        kpos = off + lax.broadcasted_iota(I32, (height, tq), 0)
        sel = ((kt > thr) | ((kt == thr) & (kpos <= jlim))) & (kpos <= qpos)
        amask = jnp.where(sel, 0.0, NEG)
        def scores(p):
            return _dot(k_ref[pl.ds(off, height), p * LANES:(p + 1) * LANES], qz_ref[p])

        def softmax(p, st):
            out = []
            for e in range(2):
                h = 2 * p + e
                lg = st[:, e * tq:(e + 1) * tq] + amask
                if bias_ref is not None:
                    lg = lg + bias_ref[h]
                m_old = m_ref[h]
                m_new = jnp.maximum(m_old, jnp.max(lg, axis=0, keepdims=True))
                alpha = jnp.exp(m_old - m_new)
                pr = jnp.exp(lg - m_new[0:1, :])
                l_ref[h] = alpha * l_ref[h] + jnp.sum(pr, axis=0, keepdims=True)
                m_ref[h] = m_new
                out.append((h, pr.astype(BF16), alpha[0:1, :]))
            return out

        def values(items):
            for h, pr, alpha in items:
                pv = _dot(vt_ref[h * HEAD_DIM:(h + 1) * HEAD_DIM, pl.ds(off, height)], pr)
                acc_ref[h] = alpha * acc_ref[h] + pv

        sts = {0: scores(0), 1: scores(1)}
        done = None
        for p in range(PAIRS):
            cur = softmax(p, sts.pop(p))
            if p + 2 < PAIRS:
                sts[p + 2] = scores(p + 2)
            if done is not None:
                values(done)
            done = cur
        values(done)

    nfar = jnp.maximum(i - 1, 0)
    nfar_wide = nfar // (WIDE // LANES)

    def far_wide(c, carry):
        tile(pl.multiple_of(c * WIDE, WIDE), WIDE, None)
        return carry

    def far_narrow(j, carry):
        tile(pl.multiple_of(j * LANES, LANES), LANES, None)
        return carry

    lax.fori_loop(0, nfar_wide, far_wide, 0)
    lax.fori_loop(nfar_wide * (WIDE // LANES), nfar, far_narrow, 0)
    for h in range(N_HEADS):
        m_ref[h] = m_ref[h] + far_ref[h]

    @pl.when(i >= 1)
    def _():
        tile(pl.multiple_of((i - 1) * LANES, LANES), LANES, bs_ref)

    tile(pl.multiple_of(i * LANES, LANES), LANES, bd_ref)
    out_t = jnp.concatenate([acc_ref[h] / l_ref[h][0:1, :] for h in range(N_HEADS)], axis=0)
    o_ref[...] = jnp.transpose(out_t)


def _attn_prompt(far, h2, hb, vt, bias_diag_t, bias_sub_t, nsel):
    t = h2.shape[0]
    tq = LANES
    nq = IDX_HEADS * IDX_DIM
    return pl.pallas_call(
        functools.partial(_attn_prompt_kernel, tq=tq, nsel=nsel),
        grid=(t // tq,),
        in_specs=[pl.BlockSpec(memory_space=pltpu.SMEM),
                  pl.BlockSpec((tq, nq), lambda i: (i, C_QIDX // nq)),
                  pl.BlockSpec((tq, LANES), lambda i: (i, C_KW // LANES)),
                  pl.BlockSpec((tq, MIX_DIM), lambda i: (i, C_Q // MIX_DIM)),
                  pl.BlockSpec((t, LANES), lambda i: (0, C_KW // LANES)),
                  pl.BlockSpec((t, MIX_DIM), lambda i: (0, C_K // MIX_DIM)),
                  pl.BlockSpec((MIX_DIM, t), lambda i: (0, 0)),
                  pl.BlockSpec((N_HEADS, LANES, tq), lambda i: (0, 0, 0)),
                  pl.BlockSpec((N_HEADS, LANES, tq), lambda i: (0, 0, 0))],
        out_specs=pl.BlockSpec((tq, MIX_DIM), lambda i: (i, 0)),
        out_shape=jax.ShapeDtypeStruct((t, MIX_DIM), F32),
        scratch_shapes=[pltpu.VMEM((t, tq), I32),
                        pltpu.VMEM((IDX_DIM, IDX_HEADS * tq), BF16),
                        pltpu.VMEM((PAIRS, LANES, 2 * tq), BF16),
                        pltpu.VMEM((N_HEADS, SUBLANES, tq), F32),
                        pltpu.VMEM((N_HEADS, SUBLANES, tq), F32),
                        pltpu.VMEM((N_HEADS, HEAD_DIM, tq), F32)],
        compiler_params=_cparams(("parallel",)),
        name="attn_prompt",
    )(far, hb, h2, hb, hb, hb, vt, bias_diag_t, bias_sub_t)


QROWS = 8
SEL_PAGES = 64
ATT_PAGES = 32


SEL_ROWS = 8 * QROWS


def _score_sample_kernel(pt_ref, qst_ref, wcol_ref, *refs, treal):
    pages = refs[:SEL_PAGES]
    knew_ref, keys_ref = refs[SEL_PAGES:]
    j = pl.program_id(1)
    ntiles = keys_ref.shape[2] // LANES

    def score(kt_t):
        r = jnp.maximum(_dot(qst_ref[0], kt_t), 0.0) * (wcol_ref[0] * IDX_SCALE)
        s = r[0:QROWS]
        for hh in range(1, IDX_HEADS):
            s = s + r[hh * QROWS:(hh + 1) * QROWS]
        return _mono_key(s)

    for g in range(SEL_PAGES):
        off = pl.multiple_of((j * SEL_PAGES + g) * LANES, LANES)
        keys_ref[0, :, pl.ds(off, LANES)] = score(pages[g][0, 0].astype(BF16))

    @pl.when(j == pl.num_programs(1) - 1)
    def _():
        rowq = lax.broadcasted_iota(I32, (QROWS, LANES), 0)
        lane = lax.broadcasted_iota(I32, (QROWS, LANES), 1)
        knew = jnp.where((lane < treal) & (lane <= rowq), score(knew_ref[0]), INT_MIN)
        keys_ref[0, :, (ntiles - 1) * LANES:] = knew


def _select_sample_kernel(keys_ref, mask_ref, *, nsel, treal):
    rows, nk = keys_ref.shape
    ntiles = nk // LANES
    lane = lax.broadcasted_iota(I32, (rows, LANES), 1)

    def count(pred):
        acc = jnp.zeros((rows, LANES), F32)
        for jt in range(ntiles):
            kt = keys_ref[:, jt * LANES:(jt + 1) * LANES]
            acc = acc + jnp.where(pred(kt, jt * LANES + lane), 1.0, 0.0)
        return jnp.sum(acc, axis=1, keepdims=True)

    def tie_limit(v, want):
        upper = (lax.broadcasted_iota(I32, (LANES, LANES), 0) <= lax.broadcasted_iota(I32, (LANES, LANES), 1))
        upper = jnp.where(upper, 1.0, 0.0).astype(BF16)
        eqs = [keys_ref[:, jt * LANES:(jt + 1) * LANES] == v for jt in range(ntiles)]
        ranks = [_dot(jnp.where(eq, 1.0, 0.0).astype(BF16), upper) for eq in eqs]
        base = jnp.zeros((rows, 1), F32)
        best = jnp.full((rows, LANES), -1.0, F32)
        for jt in range(ntiles):
            ok = eqs[jt] & (base + ranks[jt] <= want)
            best = jnp.maximum(best, jnp.where(ok, (jt * LANES + lane).astype(F32), -1.0))
            base = base + ranks[jt][:, LANES - 1:LANES]
        return jnp.max(best, axis=1, keepdims=True).astype(I32)

    row_ok = lax.broadcasted_iota(I32, (rows, 1), 0) % QROWS < treal
    thr, jlim = _select(count, (rows, 1), nsel, row_ok, tie_limit)
    for jt in range(ntiles):
        kt = keys_ref[:, jt * LANES:(jt + 1) * LANES]
        sel = (kt > thr) | ((kt == thr) & (jt * LANES + lane <= jlim))
        mask_ref[:, jt * LANES:(jt + 1) * LANES] = jnp.where(sel, 1.0, 0.0)


def _sel_sample(page_table, qst, wcol, cache_kidx, knew, layer, nsel, treal):
    b, npages = page_table.shape
    nk = (npages + 1) * PAGE
    page_spec = lambda g: pl.BlockSpec(
        (1, 1, IDX_DIM, PAGE), lambda i, j, pt: (layer, pt[i, j * SEL_PAGES + g], 0, 0))
    grid_spec = pltpu.PrefetchScalarGridSpec(
        num_scalar_prefetch=1,
        grid=(b, npages // SEL_PAGES),
        in_specs=[pl.BlockSpec((1, IDX_HEADS * QROWS, IDX_DIM), lambda i, j, pt: (i, 0, 0)),
                  pl.BlockSpec((1, IDX_HEADS * QROWS, 1), lambda i, j, pt: (i, 0, 0))]
                 + [page_spec(g) for g in range(SEL_PAGES)]
                 + [pl.BlockSpec((1, IDX_DIM, PAGE), lambda i, j, pt: (i, 0, 0))],
        out_specs=pl.BlockSpec((1, QROWS, nk), lambda i, j, pt: (i, 0, 0)),
    )
    keys = pl.pallas_call(
        functools.partial(_score_sample_kernel, treal=treal),
        grid_spec=grid_spec,
        out_shape=jax.ShapeDtypeStruct((b, QROWS, nk), I32),
        compiler_params=_cparams(("parallel", "arbitrary")),
        name="score_sample",
    )(page_table, qst, wcol, *([cache_kidx] * SEL_PAGES), knew)
    rows = b * QROWS
    blk = math.gcd(rows, SEL_ROWS)
    mask = pl.pallas_call(
        functools.partial(_select_sample_kernel, nsel=nsel, treal=treal),
        grid=(rows // blk,),
        in_specs=[pl.BlockSpec((blk, nk), lambda i: (i, 0))],
        out_specs=pl.BlockSpec((blk, nk), lambda i: (i, 0)),
        out_shape=jax.ShapeDtypeStruct((rows, nk), F32),
        compiler_params=_cparams(("parallel",)),
        name="select_sample",
    )(keys.reshape(rows, nk))
    return mask.reshape(b, QROWS, nk)


def _attn_sample_kernel(pt_ref, far_ref, qh_ref, *refs):
    kp = refs[:ATT_PAGES]
    vp = refs[ATT_PAGES:2 * ATT_PAGES]
    mask_ref, masknew_ref, blast_ref, bnew_ref, knew_ref, vnew_ref, o_ref, m_ref, l_ref, acc_ref = refs[2 * ATT_PAGES:]
    j = pl.program_id(1)
    is_last = j == pl.num_programs(1) - 1

    @pl.when(j == 0)
    def _():
        m_ref[...] = jnp.full(m_ref.shape, M_INIT, F32)
        l_ref[...] = jnp.zeros(l_ref.shape, F32)
        acc_ref[...] = jnp.zeros(acc_ref.shape, F32)

    def update(lgs, vhs):
        prs, alphas = [], []
        for h in range(N_HEADS):
            m_old = m_ref[h]
            m_new = jnp.maximum(m_old, jnp.max(lgs[h], axis=1, keepdims=True))
            alpha = jnp.exp(m_old - m_new)
            pr = jnp.exp(lgs[h] - m_new[:, 0:1])
            l_ref[h] = alpha * l_ref[h] + jnp.sum(pr, axis=1, keepdims=True)
            m_ref[h] = m_new
            prs.append(pr.astype(BF16))
            alphas.append(alpha[:, :HEAD_DIM])
        for h in range(N_HEADS):
            acc_ref[h] = alphas[h] * acc_ref[h] + _dot_nt(prs[h], vhs[h])

    def q_of(h):
        return (qh_ref[0, h].astype(F32) * ATT_SCALE).astype(BF16)

    def head_t(refs_, h):
        return jnp.concatenate([refs_[g][0, 0, h] for g in range(ATT_PAGES)], axis=1).astype(BF16)

    amask = jnp.where(mask_ref[0] > 0.5, 0.0, NEG)
    lgs = []
    for h in range(N_HEADS):
        far = jnp.full((QROWS, LANES), far_ref[h], F32)
        bias = jnp.concatenate([far] * (ATT_PAGES - 1) + [jnp.where(is_last, blast_ref[h], far)], axis=1)
        lgs.append(_dot(q_of(h), head_t(kp, h)) + amask + bias)
    update(lgs, [head_t(vp, h) for h in range(N_HEADS)])

    @pl.when(is_last)
    def _():
        amask_new = jnp.where(masknew_ref[0] > 0.5, 0.0, NEG)
        lgs_new = [_dot(q_of(h), knew_ref[0, h]) + amask_new + bnew_ref[h] for h in range(N_HEADS)]
        update(lgs_new, [vnew_ref[0, h] for h in range(N_HEADS)])
        outs = []
        for h in range(N_HEADS):
            lh = l_ref[h][:, :HEAD_DIM]
            outs.append(acc_ref[h] / jnp.where(lh > 0.0, lh, 1.0))
        o_ref[0] = jnp.concatenate(outs, axis=1)


def _attn_sample(page_table, far, qh, cache_k2, cache_v2, mask, bias_last, bias_new, knew, vnew, layer):
    b, npages = page_table.shape
    nsteps = npages // ATT_PAGES
    page_spec = lambda g: pl.BlockSpec(
        (1, 1, N_HEADS, HEAD_DIM, PAGE), lambda i, j, pt, far: (layer, pt[i, j * ATT_PAGES + g], 0, 0, 0))
    const3 = lambda i, j, pt, far: (0, 0, 0)
    per_b4 = lambda i, j, pt, far: (i, 0, 0, 0)
    grid_spec = pltpu.PrefetchScalarGridSpec(
        num_scalar_prefetch=2,
        grid=(b, nsteps),
        in_specs=[pl.BlockSpec((1, N_HEADS, QROWS, HEAD_DIM), per_b4)]
                 + [page_spec(g) for g in range(ATT_PAGES)] * 2
                 + [pl.BlockSpec((1, QROWS, ATT_PAGES * PAGE), lambda i, j, pt, far: (i, 0, j)),
                    pl.BlockSpec((1, QROWS, PAGE), lambda i, j, pt, far: (i, 0, npages)),
                    pl.BlockSpec((N_HEADS, QROWS, PAGE), const3),
                    pl.BlockSpec((N_HEADS, QROWS, PAGE), const3),
                    pl.BlockSpec((1, N_HEADS, HEAD_DIM, PAGE), per_b4),
                    pl.BlockSpec((1, N_HEADS, HEAD_DIM, PAGE), per_b4)],
        out_specs=pl.BlockSpec((1, QROWS, MIX_DIM), lambda i, j, pt, far: (i, 0, 0)),
        scratch_shapes=[pltpu.VMEM((N_HEADS, QROWS, LANES), F32),
                        pltpu.VMEM((N_HEADS, QROWS, LANES), F32),
                        pltpu.VMEM((N_HEADS, QROWS, HEAD_DIM), F32)],
    )
    return pl.pallas_call(
        _attn_sample_kernel,
        grid_spec=grid_spec,
        out_shape=jax.ShapeDtypeStruct((b, QROWS, MIX_DIM), F32),
        compiler_params=_cparams(("parallel", "arbitrary")),
        name="attn_sample",
    )(page_table, far, qh, *([cache_k2] * ATT_PAGES), *([cache_v2] * ATT_PAGES), mask, mask,
      bias_last, bias_new, knew, vnew)


def _bucket_table(n):
    d = np.arange(n)
    nf = np.maximum(d, 1).astype(np.float32)
    half = REL_BUCKETS // 2
    large = half + (np.log(nf / np.float32(half)) / np.float32(math.log(REL_MAX_DIST / half))
                    * np.float32(REL_BUCKETS - half)).astype(np.int32)
    return np.where(d < half, d, np.minimum(large, REL_BUCKETS - 1)).astype(np.int32)


def _pack_w_in(w):
    w_t = jnp.transpose(w, (0, 2, 1))
    cuts = np.cumsum(IN_SPLITS)
    piece = lambda i: w_t[:, cuts[i] - IN_SPLITS[i]:cuts[i], :]
    zeros = lambda n: jnp.zeros((w.shape[0], n, w.shape[1]), w.dtype)
    tail = jnp.concatenate([piece(10), piece(11), piece(7), piece(8), piece(9),
                            zeros(LANES - IDX_DIM - IDX_HEADS), piece(3), zeros(LANES - N_HEADS)], axis=1)
    return w_t, tail


def _pad_lanes(x):
    return jnp.pad(x.reshape(1, -1), ((0, 0), (0, LANES - x.shape[-1])))


def _pad_axis(x, axis, size):
    pads = [(0, 0)] * x.ndim
    pads[axis] = (0, size - x.shape[axis])
    return jnp.pad(x, pads)


def kernel(x_prompt, x_sample, state_pool, state_conv, state_ssm, cache_k, cache_v, cache_kidx, page_table,
           rel_bias, norm_mix, w_in, pool_w, pool_scale, conv_w, conv_b, dt_bias, a_log, d_skip, ssm_norm,
           sgu_w, sgu_b, w_out, norm_mlp, mlp_w1, mlp_w2, norm_final):
    depth = w_in.shape[0]
    bp, t, _ = x_prompt.shape
    bs, ts, _ = x_sample.shape
    npages = page_table.shape[1]
    past = npages * PAGE
    n_pool = cache_k.shape[1]
    assert bp == 1 and t % WIDE == 0 and npages % SEL_PAGES == 0 and npages % ATT_PAGES == 0
    assert max(t, past + PAGE) <= 2 ** POS_BITS and ts <= QROWS
    tc = LANES
    nsel_p = min(TOPK_MAX, t // 4)
    nsel_s = min(TOPK_MAX, (past + ts) // 4)

    bucket = _bucket_table(2 * LANES)
    far = rel_bias[REL_BUCKETS - 1]
    kk = np.arange(LANES)[:, None]
    qq = np.arange(LANES)[None, :]
    def bias_of(dist):
        onehot = np.eye(REL_BUCKETS, dtype=np.float32)[bucket[dist].reshape(-1)]
        vals = jnp.dot(onehot, rel_bias, precision=lax.Precision.HIGHEST)
        return jnp.transpose(vals).reshape((N_HEADS,) + dist.shape)

    bias_diag_t = bias_of(np.maximum(qq - kk, 0))
    bias_sub_t = bias_of(np.minimum(LANES + qq - kk, 2 * LANES - 1))
    sq = np.arange(QROWS)[:, None]
    sk = np.arange(PAGE)[None, :]
    bias_last = bias_of(np.minimum(PAGE + sq - sk, 2 * LANES - 1))
    bias_new = bias_of(np.maximum(sq - sk, 0))

    cache_k2 = jnp.transpose(cache_k, (0, 1, 3, 4, 2))
    cache_v2 = jnp.transpose(cache_v, (0, 1, 3, 4, 2))
    cache_kidx2 = jnp.transpose(cache_kidx, (0, 1, 3, 2))
    w_in_t, w_in_last = _pack_w_in(w_in)
    w_out_b, w1_b, w2_b = w_out.astype(BF16), mlp_w1.astype(BF16), mlp_w2.astype(BF16)

    xp = x_prompt.reshape(t, D_MODEL)
    xs = x_sample.reshape(bs * ts, D_MODEL)
    zero_pool = jnp.zeros((1, POOL_HIST, MIX_DIM), F32)
    zero_conv = jnp.zeros((1, CONV_W - 1, XBC_DIM), F32)
    zero_ssm = jnp.zeros((1, N_HEADS, HEAD_DIM, SSM_STATE), F32)
    tm_p = min(1024, t)
    tm_s = bs * ts
    outs_p, outs_s = [], []

    for l in range(depth):
        g_mix = norm_mix[l].reshape(1, D_MODEL)
        g_mlp = norm_mlp[l].reshape(1, D_MODEL)
        g_out = norm_final.reshape(1, D_MODEL) if l == depth - 1 else g_mlp
        pscale = pool_scale[l].reshape(1, MIX_DIM)
        cb = conv_b[l].reshape(1, XBC_DIM)
        dtb, alog, dsk = _pad_lanes(dt_bias[l]), _pad_lanes(a_log[l]), _pad_lanes(d_skip[l])
        snorm = ssm_norm[l].reshape(1, MIX_DIM)
        sgu_b_col = sgu_b[l][:, :, None]
        final = l == depth - 1

        hp, hpb, vt = _inproj(xp, g_mix, w_in_t, w_in_last, l, tm=min(1024, t), attn_operands=True)
        hp3 = hp.reshape(1, t, H_PACKED)
        y_pool, new_pool = _pool(hp3, zero_pool, pool_w[l], pscale, tc=WIDE, treal=WIDE, pos0=0)
        y_ssm, new_conv, new_h = _ssd(hp3, zero_conv, zero_ssm, conv_w[l], cb, dtb, alog, dsk, snorm,
                                      tc=tc, treal=tc)
        y_sgu, = _sgu(hp3, sgu_w[l], sgu_b_col, tc=tc, treal=tc, nsub=WIDE // tc)
        kf = hp[:, C_K:C_K + MIX_DIM]
        vf = hp[:, C_V:C_V + MIX_DIM]
        kif = hp[:, C_KW:C_KW + IDX_DIM]
        y_att = _attn_prompt(far, hp, hpb, vt, bias_diag_t, bias_sub_t, nsel_p)
        x1, xn = _outproj(xp, [y_pool[0], y_ssm[0], y_att, y_sgu[0]], w_out_b, l, g_mlp, tm=256)
        xp = _mlp(xn, w1_b, w2_b, l, x1, g_out, tm=tm_p, final_norm=final)
        outs_p.append((new_pool, new_conv, new_h, kf.reshape(1, t, N_HEADS, HEAD_DIM),
                       vf.reshape(1, t, N_HEADS, HEAD_DIM), kif.reshape(1, t, IDX_DIM)))

        hs, = _inproj(xs, g_mix, w_in_t, w_in_last, l, tm=tm_s)
        hs3 = hs.reshape(bs, ts, H_PACKED)
        y_pool, new_pool = _pool(hs3, state_pool[l], pool_w[l], pscale, tc=tc, treal=ts, pos0=past)
        y_ssm, new_conv, new_h = _ssd(hs3, state_conv[l], state_ssm[l], conv_w[l], cb, dtb, alog, dsk, snorm,
                                      tc=tc, treal=ts)
        y_sgu, v_rows = _sgu(hs3, sgu_w[l], sgu_b_col, tc=tc, treal=ts, emit_v=True)
        kf = hs3[:, :, C_K:C_K + MIX_DIM]
        vf = hs3[:, :, C_V:C_V + MIX_DIM]
        kif = hs3[:, :, C_KW:C_KW + IDX_DIM]
        qi = hs3[:, :, C_QIDX:C_QIDX + IDX_HEADS * IDX_DIM].reshape(bs, ts, IDX_HEADS, IDX_DIM)
        qst = _pad_axis(jnp.transpose(qi, (0, 2, 1, 3)), 2, QROWS)
        qst = qst.reshape(bs, IDX_HEADS * QROWS, IDX_DIM).astype(BF16)
        wi = hs3[:, :, C_KW + IDX_DIM:C_KW + IDX_DIM + IDX_HEADS]
        wcol = _pad_axis(jnp.transpose(wi, (0, 2, 1)), 2, QROWS).reshape(bs, IDX_HEADS * QROWS, 1)
        kinew_t = _pad_axis(jnp.transpose(kif, (0, 2, 1)), 2, PAGE).astype(BF16)
        mask = _sel_sample(page_table, qst, wcol, cache_kidx2, kinew_t, l, nsel_s, ts)
        heads = lambda x, perm: jnp.transpose(x.reshape(bs, ts, N_HEADS, HEAD_DIM), perm)
        qh = _pad_axis(heads(hs3[:, :, C_Q:C_Q + MIX_DIM], (0, 2, 1, 3)), 2, QROWS).astype(BF16)
        knew = _pad_axis(heads(kf, (0, 2, 3, 1)), 3, PAGE).astype(BF16)
        vnew = _pad_axis(heads(vf, (0, 2, 3, 1)), 3, PAGE).astype(BF16)
        att = _attn_sample(page_table, far, qh, cache_k2, cache_v2, mask, bias_last, bias_new, knew, vnew, l)
        y_att = att[:, :ts].reshape(bs * ts, MIX_DIM)
        flat = lambda y: y.reshape(bs * ts, MIX_DIM)
        x1, xn = _outproj(xs, [flat(y_pool), flat(y_ssm), y_att, flat(y_sgu)], w_out_b, l, g_mlp, tm=tm_s)
        xs = _mlp(xn, w1_b, w2_b, l, x1, g_out, tm=tm_s, final_norm=final)
        outs_s.append((new_pool, new_conv, new_h, kf.reshape(bs, ts, N_HEADS, HEAD_DIM),
                       vf.reshape(bs, ts, N_HEADS, HEAD_DIM), kif, v_rows))

    stack = lambda outs, i: jnp.stack([o[i] for o in outs])
    return (xp.reshape(1, t, D_MODEL), xs.reshape(bs, ts, D_MODEL),
            *[stack(outs_p, i) for i in range(6)],
            *[stack(outs_s, i) for i in range(7)])
```

```python
import functools
import math

import numpy as np
import jax
import jax.numpy as jnp
from jax import lax
from jax.experimental import pallas as pl
from jax.experimental.pallas import tpu as pltpu

F32 = jnp.float32
BF16 = jnp.bfloat16
I32 = jnp.int32

LANES = 128
SUBLANES = 8
D_MODEL = 2048
N_HEADS = 8
HEAD_DIM = 64
MIX_DIM = 512
POOL_WINDOWS = (2, 4, 8, 16)
POOL_HIST = 15
CONV_W = 4
SSM_GROUPS = 2
SSM_STATE = 128
XBC_DIM = MIX_DIM + 2 * SSM_GROUPS * SSM_STATE
IDX_HEADS = 4
IDX_DIM = 64
TOPK_MAX = 256
PAGE = 128
REL_BUCKETS = 32
REL_MAX_DIST = 128
ATT_SCALE = HEAD_DIM ** -0.5
IDX_SCALE = (IDX_HEADS * IDX_DIM) ** -0.5
SGU_GROUPS = 4
D_FF = 4 * D_MODEL
EPS = 1e-6
NEG = -1e30
INT_MIN = -(2 ** 31)
INT_MAX = 2 ** 31 - 1
IN_SPLITS = (512, 512, 1024, 8, 512, 512, 512, 256, 64, 4, 512, 512)

C_P, C_Z, C_XBC, C_Q, C_K, C_V, C_U, C_VS, C_QIDX, C_KW, C_DT = (
    0, 512, 1024, 2048, 2560, 3072, 3584, 4096, 4608, 4864, 4992)
H_PACKED = 5120
VMEM_LIMIT = 56 * 1024 * 1024


def _cparams(sem):
    return pltpu.CompilerParams(dimension_semantics=sem, vmem_limit_bytes=VMEM_LIMIT)


def _dot(a, b):
    return jnp.dot(a, b, preferred_element_type=F32)


def _dot_nt(a, b):
    return lax.dot_general(a, b, (((1,), (1,)), ((), ())), preferred_element_type=F32)


def _dot_tn(a, b):
    return lax.dot_general(a, b, (((0,), (0,)), ((), ())), preferred_element_type=F32)


def _silu(x):
    return x * (1.0 / (1.0 + jnp.exp(-x)))


def _gelu_tanh(x):
    return 0.5 * x * (1.0 + jnp.tanh(math.sqrt(2.0 / math.pi) * (x + 0.044715 * (x * x * x))))


def _softplus(x):
    return jnp.maximum(x, 0.0) + jnp.log1p(jnp.exp(-jnp.abs(x)))


def _rms(x, g):
    ms = jnp.mean(x * x, axis=-1, keepdims=True)
    return x * lax.rsqrt(ms + EPS) * g


def _inproj_kernel(x_ref, g_ref, w_ref, o_ref, *rest, attn_operands, tn):
    xn_ref = rest[-1]

    @pl.when(pl.program_id(1) == 0)
    def _():
        xn_ref[...] = _rms(x_ref[...], g_ref[...]).astype(BF16)

    h = _dot_nt(xn_ref[...], w_ref[0])
    o_ref[...] = h
    if attn_operands:
        ob_ref, vt_ref = rest[:2]
        ob_ref[...] = h.astype(BF16)

        @pl.when(pl.program_id(1) == C_V // tn)
        def _():
            vt_ref[...] = jnp.transpose(h[:, C_V % tn:C_V % tn + MIX_DIM]).astype(BF16)


def _inproj(x, g, w_t, layer, tm, tn=512, attn_operands=False):
    m = x.shape[0]
    n = w_t.shape[1]
    out_specs = [pl.BlockSpec((tm, tn), lambda i, j: (i, j))]
    out_shape = [jax.ShapeDtypeStruct((m, n), F32)]
    if attn_operands:
        out_specs += [pl.BlockSpec((tm, tn), lambda i, j: (i, j)), pl.BlockSpec((MIX_DIM, tm), lambda i, j: (0, i))]
        out_shape += [jax.ShapeDtypeStruct((m, n), BF16), jax.ShapeDtypeStruct((MIX_DIM, m), BF16)]
    return pl.pallas_call(
        functools.partial(_inproj_kernel, attn_operands=attn_operands, tn=tn),
        grid=(m // tm, n // tn),
        in_specs=[pl.BlockSpec((tm, D_MODEL), lambda i, j: (i, 0)),
                  pl.BlockSpec((1, D_MODEL), lambda i, j: (0, 0)),
                  pl.BlockSpec((1, tn, D_MODEL), lambda i, j: (layer, j, 0))],
        out_specs=out_specs,
        out_shape=out_shape,
        scratch_shapes=[pltpu.VMEM((tm, D_MODEL), BF16)],
        compiler_params=_cparams(("parallel", "arbitrary")),
        name="inproj",
    )(x, g, w_t)


def _outproj_kernel(x_ref, y0_ref, y1_ref, y2_ref, y3_ref, w_ref, g_ref, x1_ref, xn_ref):
    y = jnp.concatenate([y0_ref[...], y1_ref[...], y2_ref[...], y3_ref[...]], axis=1).astype(BF16)
    x1 = x_ref[...] + _dot(y, w_ref[0])
    x1_ref[...] = x1
    xn_ref[...] = _rms(x1, g_ref[...]).astype(BF16)


def _outproj(x, ys, w, layer, g, tm):
    m = x.shape[0]
    row = lambda i: (i, 0)
    return pl.pallas_call(
        _outproj_kernel,
        grid=(m // tm,),
        in_specs=[pl.BlockSpec((tm, D_MODEL), row)]
                 + [pl.BlockSpec((tm, MIX_DIM), row)] * 4
                 + [pl.BlockSpec((1, D_MODEL, D_MODEL), lambda i: (layer, 0, 0)),
                    pl.BlockSpec((1, D_MODEL), lambda i: (0, 0))],
        out_specs=[pl.BlockSpec((tm, D_MODEL), row), pl.BlockSpec((tm, D_MODEL), row)],
        out_shape=[jax.ShapeDtypeStruct((m, D_MODEL), F32), jax.ShapeDtypeStruct((m, D_MODEL), BF16)],
        compiler_params=_cparams(("parallel",)),
        name="outproj",
    )(x, *ys, w, g)


def _mlp_kernel(xn_ref, w1_ref, w2_ref, x1_ref, g_ref, o_ref, *, final_norm):
    j = pl.program_id(1)

    @pl.when(j == 0)
    def _():
        o_ref[...] = x1_ref[...]

    hid = _dot(xn_ref[...], w1_ref[0])
    hid = jnp.square(jnp.maximum(hid, 0.0)).astype(BF16)
    o_ref[...] += _dot(hid, w2_ref[0])

    if final_norm:
        @pl.when(j == pl.num_programs(1) - 1)
        def _():
            o_ref[...] = _rms(o_ref[...], g_ref[...])


def _mlp(xn, w1, w2, layer, x1, g, tm, final_norm, tf=512):
    m = xn.shape[0]
    return pl.pallas_call(
        functools.partial(_mlp_kernel, final_norm=final_norm),
        grid=(m // tm, D_FF // tf),
        in_specs=[pl.BlockSpec((tm, D_MODEL), lambda i, j: (i, 0)),
                  pl.BlockSpec((1, D_MODEL, tf), lambda i, j: (layer, 0, j)),
                  pl.BlockSpec((1, tf, D_MODEL), lambda i, j: (layer, j, 0)),
                  pl.BlockSpec((tm, D_MODEL), lambda i, j: (i, 0)),
                  pl.BlockSpec((1, D_MODEL), lambda i, j: (0, 0))],
        out_specs=pl.BlockSpec((tm, D_MODEL), lambda i, j: (i, 0)),
        out_shape=jax.ShapeDtypeStruct((m, D_MODEL), F32),
        compiler_params=_cparams(("parallel", "arbitrary")),
        name="mlp",
    )(xn, w1, w2, x1, g)


def _pool_kernel(p_ref, hist_ref, w_ref, scale_ref, y_ref, newhist_ref, ext_ref, *, tc, treal, pos0):
    c = pl.program_id(1)
    hrows = POOL_HIST + 1

    @pl.when(c == 0)
    def _():
        ext_ref[0:1, :] = jnp.zeros((1, MIX_DIM), F32)
        ext_ref[1:hrows, :] = hist_ref[0]

    if treal < tc:
        ext_ref[hrows:hrows + tc, :] = jnp.zeros((tc, MIX_DIM), F32)
    ext_ref[hrows:hrows + treal, :] = p_ref[0]

    pos = pos0 + c * treal + lax.broadcasted_iota(I32, (tc, 1), 0)
    for g, win in enumerate(POOL_WINDOWS):
        cols = slice(g * LANES, (g + 1) * LANES)
        cur = ext_ref[hrows:hrows + tc, cols]
        s = cur
        for j in range(1, win):
            s = s + ext_ref[hrows - j:hrows - j + tc, cols]
        cnt = jnp.minimum(pos + 1, win).astype(F32)
        diff = s / cnt - cur
        y = _dot(diff, w_ref[g]) * scale_ref[:, cols]
        y_ref[0, :, cols] = y[:treal]

    newhist_ref[0] = ext_ref[treal + 1:treal + hrows, :]
    ext_ref[0:hrows, :] = ext_ref[treal:treal + hrows, :]


def _pool(h3, hist, pool_w, pool_scale, *, tc, treal, pos0):
    b, t, _ = h3.shape
    return pl.pallas_call(
        functools.partial(_pool_kernel, tc=tc, treal=treal, pos0=pos0),
        grid=(b, t // treal),
        in_specs=[pl.BlockSpec((1, treal, MIX_DIM), lambda i, c: (i, c, C_P // MIX_DIM)),
                  pl.BlockSpec((1, POOL_HIST, MIX_DIM), lambda i, c: (i, 0, 0)),
                  pl.BlockSpec((4, LANES, LANES), lambda i, c: (0, 0, 0)),
                  pl.BlockSpec((1, MIX_DIM), lambda i, c: (0, 0))],
        out_specs=[pl.BlockSpec((1, treal, MIX_DIM), lambda i, c: (i, c, 0)),
                   pl.BlockSpec((1, POOL_HIST, MIX_DIM), lambda i, c: (i, 0, 0))],
        out_shape=[jax.ShapeDtypeStruct((b, t, MIX_DIM), F32),
                   jax.ShapeDtypeStruct((b, POOL_HIST, MIX_DIM), F32)],
        scratch_shapes=[pltpu.VMEM((POOL_HIST + 1 + tc, MIX_DIM), F32)],
        compiler_params=_cparams(("parallel", "arbitrary")),
        name="pool",
    )(h3, hist, pool_w, pool_scale)


def _sgu_kernel(u_ref, v_ref, w_ref, b_ref, y_ref, *rest, tc, treal, nsub, emit_v):
    ubuf_ref, vbuf_ref = rest[-2:]
    r = lax.broadcasted_iota(I32, (tc, tc), 0)
    s = lax.broadcasted_iota(I32, (tc, tc), 1)
    for sub in range(nsub):
        rows = slice(sub * treal, (sub + 1) * treal)
        if treal < tc:
            ubuf_ref[...] = jnp.zeros((tc, MIX_DIM), F32)
            vbuf_ref[...] = jnp.zeros((tc, MIX_DIM), F32)
        ubuf_ref[0:treal, :] = u_ref[0, rows, :]
        vbuf_ref[0:treal, :] = v_ref[0, rows, :]
        u = _gelu_tanh(ubuf_ref[...])
        v = _gelu_tanh(vbuf_ref[...])
        if emit_v:
            rest[0][0, rows, :] = v[:treal]
        for g in range(SGU_GROUPS):
            cols = slice(g * LANES, (g + 1) * LANES)
            w = jnp.where(s <= r, w_ref[g], 0.0)
            mixed = _dot(w, v[:, cols]) + b_ref[g]
            y_ref[0, rows, cols] = (u[:, cols] * mixed)[:treal]


def _sgu(h3, sgu_w, sgu_b_col, *, tc, treal, nsub=1, emit_v=False):
    b, t, _ = h3.shape
    rows = nsub * treal
    out_specs = [pl.BlockSpec((1, rows, MIX_DIM), lambda i, c: (i, c, 0))]
    out_shape = [jax.ShapeDtypeStruct((b, t, MIX_DIM), F32)]
    return pl.pallas_call(
        functools.partial(_sgu_kernel, tc=tc, treal=treal, nsub=nsub, emit_v=emit_v),
        grid=(b, t // rows),
        in_specs=[pl.BlockSpec((1, rows, MIX_DIM), lambda i, c: (i, c, C_U // MIX_DIM)),
                  pl.BlockSpec((1, rows, MIX_DIM), lambda i, c: (i, c, C_VS // MIX_DIM)),
                  pl.BlockSpec((SGU_GROUPS, tc, tc), lambda i, c: (0, 0, 0)),
                  pl.BlockSpec((SGU_GROUPS, tc, 1), lambda i, c: (0, 0, 0))],
        out_specs=out_specs * (2 if emit_v else 1),
        out_shape=out_shape * (2 if emit_v else 1),
        scratch_shapes=[pltpu.VMEM((tc, MIX_DIM), F32), pltpu.VMEM((tc, MIX_DIM), F32)],
        compiler_params=_cparams(("parallel", "parallel")),
        name="sgu",
    )(h3, h3, sgu_w, sgu_b_col)


def _cumsum_rows(a, n):
    row = lax.broadcasted_iota(I32, a.shape, 0)
    sh = 1
    while sh < n:
        a = a + jnp.where(row >= sh, pltpu.roll(a, sh, 0), 0.0)
        sh *= 2
    return a


def _ssd_kernel(z_ref, xbc_ref, dt_ref, hc_ref, h0_ref, convw_ref, convb_ref, dtb_ref, alog_ref, dskip_ref,
                norm_ref, y_ref, newconv_ref, newh_ref, ext_ref, zbuf_ref, dtbuf_ref, state_ref, *, tc, treal):
    c = pl.program_id(1)
    pre = SUBLANES
    hist = CONV_W - 1

    @pl.when(c == 0)
    def _():
        ext_ref[0:pre - hist, :] = jnp.zeros((pre - hist, XBC_DIM), F32)
        ext_ref[pre - hist:pre, :] = hc_ref[0]
        state_ref[...] = h0_ref[...]

    if treal < tc:
        ext_ref[pre:pre + tc, :] = jnp.zeros((tc, XBC_DIM), F32)
        zbuf_ref[...] = jnp.zeros((tc, MIX_DIM), F32)
        dtbuf_ref[...] = jnp.zeros((tc, LANES), F32)
    ext_ref[pre:pre + treal, :] = xbc_ref[0]
    zbuf_ref[0:treal, :] = z_ref[0]
    dtbuf_ref[0:treal, :] = dt_ref[0]

    conv = convb_ref[...]
    for j in range(CONV_W):
        conv = conv + ext_ref[pre - hist + j:pre - hist + j + tc, :] * convw_ref[j:j + 1, :]
    conv = _silu(conv)
    xs = conv[:, :MIX_DIM]
    bmat = conv[:, MIX_DIM:MIX_DIM + SSM_GROUPS * SSM_STATE]
    cmat = conv[:, MIX_DIM + SSM_GROUPS * SSM_STATE:]

    rowi = lax.broadcasted_iota(I32, (tc, LANES), 0)
    dt = _softplus(dtbuf_ref[...] + dtb_ref[...])
    if treal < tc:
        dt = jnp.where(rowi < treal, dt, 0.0)
    a = dt * (-jnp.exp(alog_ref[...]))
    acum = _cumsum_rows(a, tc)
    acum_t = jnp.transpose(acum)
    total = acum[tc - 1:tc, :]
    causal = lax.broadcasted_iota(I32, (tc, tc), 1) <= lax.broadcasted_iota(I32, (tc, tc), 0)

    ys = []
    heads_per_group = N_HEADS // SSM_GROUPS
    for g in range(SSM_GROUPS):
        bg = bmat[:, g * SSM_STATE:(g + 1) * SSM_STATE]
        cg = cmat[:, g * SSM_STATE:(g + 1) * SSM_STATE]
        cb = _dot_nt(cg, bg)
        for hh in range(heads_per_group):
            h = g * heads_per_group + hh
            col = acum[:, h:h + 1]
            row = acum_t[h:h + 1, :]
            decay = jnp.exp(jnp.where(causal, col - row, NEG))
            xh = xs[:, h * HEAD_DIM:(h + 1) * HEAD_DIM]
            xdt = xh * dt[:, h:h + 1]
            hprev = state_ref[0, h]
            y = _dot(cb * decay, xdt)
            y = y + _dot_nt(cg, hprev) * jnp.exp(col)
            y = y + dskip_ref[:, h:h + 1] * xh
            ys.append(y)
            tot = total[:, h:h + 1]
            st = _dot_tn(xdt * jnp.exp(tot - col), bg)
            state_ref[0, h] = jnp.exp(tot) * hprev + st
    y = jnp.concatenate(ys, axis=1) * _silu(zbuf_ref[...])
    gw = MIX_DIM // SSM_GROUPS
    outs = []
    for g in range(SSM_GROUPS):
        yg = y[:, g * gw:(g + 1) * gw]
        ms = jnp.mean(yg * yg, axis=-1, keepdims=True)
        outs.append(yg * lax.rsqrt(ms + EPS) * norm_ref[:, g * gw:(g + 1) * gw])
    y_ref[0] = jnp.concatenate(outs, axis=1)[:treal]

    newconv_ref[0] = ext_ref[pre + treal - hist:pre + treal, :]
    ext_ref[0:pre, :] = ext_ref[treal:treal + pre, :]
    newh_ref[...] = state_ref[...]


def _ssd(h3, hist_conv, h0, conv_w, conv_b, dtb, alog, dskip, norm, *, tc, treal):
    b, t, _ = h3.shape
    full2 = lambda i, c: (0, 0)
    return pl.pallas_call(
        functools.partial(_ssd_kernel, tc=tc, treal=treal),
        grid=(b, t // treal),
        in_specs=[pl.BlockSpec((1, treal, MIX_DIM), lambda i, c: (i, c, C_Z // MIX_DIM)),
                  pl.BlockSpec((1, treal, XBC_DIM), lambda i, c: (i, c, C_XBC // XBC_DIM)),
                  pl.BlockSpec((1, treal, LANES), lambda i, c: (i, c, C_DT // LANES)),
                  pl.BlockSpec((1, CONV_W - 1, XBC_DIM), lambda i, c: (i, 0, 0)),
                  pl.BlockSpec((1, N_HEADS, HEAD_DIM, SSM_STATE), lambda i, c: (i, 0, 0, 0)),
                  pl.BlockSpec((CONV_W, XBC_DIM), full2),
                  pl.BlockSpec((1, XBC_DIM), full2),
                  pl.BlockSpec((1, LANES), full2),
                  pl.BlockSpec((1, LANES), full2),
                  pl.BlockSpec((1, LANES), full2),
                  pl.BlockSpec((1, MIX_DIM), full2)],
        out_specs=[pl.BlockSpec((1, treal, MIX_DIM), lambda i, c: (i, c, 0)),
                   pl.BlockSpec((1, CONV_W - 1, XBC_DIM), lambda i, c: (i, 0, 0)),
                   pl.BlockSpec((1, N_HEADS, HEAD_DIM, SSM_STATE), lambda i, c: (i, 0, 0, 0))],
        out_shape=[jax.ShapeDtypeStruct((b, t, MIX_DIM), F32),
                   jax.ShapeDtypeStruct((b, CONV_W - 1, XBC_DIM), F32),
                   jax.ShapeDtypeStruct((b, N_HEADS, HEAD_DIM, SSM_STATE), F32)],
        scratch_shapes=[pltpu.VMEM((SUBLANES + tc, XBC_DIM), F32),
                        pltpu.VMEM((tc, MIX_DIM), F32),
                        pltpu.VMEM((tc, LANES), F32),
                        pltpu.VMEM((1, N_HEADS, HEAD_DIM, SSM_STATE), F32)],
        compiler_params=_cparams(("parallel", "arbitrary")),
        name="ssd",
    )(h3, h3, h3, hist_conv, h0, conv_w, conv_b, dtb, alog, dskip, norm)


M_INIT = 0.5 * NEG
POS_BITS = 14
assert math.frexp(ATT_SCALE)[0] == 0.5, "q is pre-scaled in bf16, exact only for a power-of-two scale"


def _mono_key(s):
    s = jnp.where(s == 0.0, 0.0, s)
    b = lax.bitcast_convert_type(s, I32)
    return b ^ ((b >> 31) & INT_MAX)


def _select(count, stat_shape, nsel, query_ok, tie_limit=None, threshold=None):
    kf = float(nsel)

    def bit_threshold():
        c0 = count(lambda kt, pos: kt >= 0)
        nonneg = c0 >= kf
        prefix = jnp.where(nonneg, 0, INT_MIN).astype(I32)

        def bit_body(it, carry):
            prefix, cge = carry
            cand = prefix | jnp.left_shift(jnp.int32(1), 30 - it)
            cnt = count(lambda kt, pos: kt >= cand)
            take = cnt >= kf
            return jnp.where(take, cand, prefix), jnp.where(take, cnt, cge)

        return lax.fori_loop(0, 31, bit_body, (prefix, jnp.where(nonneg, c0, kf)))

    v, cge = (threshold or bit_threshold)()
    need = (cge > kf) & (v != INT_MIN) & query_ok

    def bit_tie_limit(v, want):
        def body(it, x):
            cand = x | jnp.left_shift(jnp.int32(1), POS_BITS - 1 - it)
            cnt = count(lambda kt, pos: (kt == v) & (pos < cand))
            return jnp.where(cnt < want, cand, x)
        return lax.fori_loop(0, POS_BITS, body, jnp.zeros(stat_shape, I32))

    def tie_search():
        want = kf - count(lambda kt, pos: kt > v)
        limit = (tie_limit or bit_tie_limit)(v, want)
        return jnp.where(need, limit, INT_MAX)

    any_need = jnp.max(jnp.where(need, 1.0, 0.0)) > 0.0
    jlim = lax.cond(any_need, tie_search, lambda: jnp.full(stat_shape, INT_MAX, I32))
    return v, jlim


WIDE = 4 * LANES
PAIRS = N_HEADS // 2
COUNT_ROWS = 8 * SUBLANES


def _attn_prompt_kernel(far_ref, qidx_ref, kw_ref, q_ref, kidx_ref, k_ref, vt_ref, bd_ref, bs_ref, o_ref,
                        keys_ref, qit_ref, qz_ref, m_ref, l_ref, acc_ref, *, tq, nsel):
    i = pl.program_id(0)
    qpos = i * tq + lax.broadcasted_iota(I32, (1, tq), 1)
    nwide = i // (WIDE // LANES) + 1

    qi_t = jnp.transpose(qidx_ref[...].astype(F32)).astype(BF16)
    for hh in range(IDX_HEADS):
        qit_ref[:, hh * tq:(hh + 1) * tq] = qi_t[hh * IDX_DIM:(hh + 1) * IDX_DIM, :]
    w_t = jnp.transpose(kw_ref[...])[IDX_DIM:IDX_DIM + IDX_HEADS, :] * IDX_SCALE
    q_t = jnp.transpose(q_ref[...].astype(F32) * ATT_SCALE).astype(BF16)
    upper = lax.broadcasted_iota(I32, (LANES, tq), 0) < HEAD_DIM
    zero = jnp.zeros((LANES, tq), BF16)
    for p in range(PAIRS):
        blk = q_t[p * LANES:(p + 1) * LANES, :]
        qz_ref[p, :, 0:tq] = jnp.where(upper, blk, zero)
        qz_ref[p, :, tq:2 * tq] = jnp.where(upper, zero, blk)

    def score_body(c, carry):
        off = pl.multiple_of(c * WIDE, WIDE)
        kt = kidx_ref[pl.ds(off, WIDE), 0:IDX_DIM]
        sc = _dot(kt, qit_ref[...])
        s = jnp.zeros((WIDE, tq), F32)
        for hh in range(IDX_HEADS):
            s = s + jnp.maximum(sc[:, hh * tq:(hh + 1) * tq], 0.0) * w_t[hh:hh + 1, :]
        kpos = off + lax.broadcasted_iota(I32, (WIDE, tq), 0)
        keys_ref[pl.ds(off, WIDE), :] = jnp.where(kpos <= qpos, _mono_key(s), INT_MIN)
        return carry

    lax.fori_loop(0, nwide, score_body, 0)

    def count(pred):
        def body(c, acc):
            off = pl.multiple_of(c * WIDE, WIDE)
            kt = keys_ref[pl.ds(off, WIDE), :]
            kpos = off + lax.broadcasted_iota(I32, (WIDE, tq), 0)
            hit = jnp.where(pred(kt, kpos), 1.0, 0.0)
            return acc + jnp.sum(hit.reshape(WIDE // COUNT_ROWS, COUNT_ROWS, tq), axis=0)
        acc = lax.fori_loop(0, nwide, body, jnp.zeros((COUNT_ROWS, tq), F32))
        return jnp.sum(acc, axis=0, keepdims=True)


    def tie_limit(v, want):
        lower = (lax.broadcasted_iota(I32, (WIDE, WIDE), 1) <= lax.broadcasted_iota(I32, (WIDE, WIDE), 0))
        lower = jnp.where(lower, 1.0, 0.0).astype(BF16)

        def body(c, carry):
            base, limit = carry
            off = pl.multiple_of(c * WIDE, WIDE)
            eq = keys_ref[pl.ds(off, WIDE), :] == v
            rank = base + _dot(lower, jnp.where(eq, 1.0, 0.0).astype(BF16))
            kpos = (off + lax.broadcasted_iota(I32, (WIDE, tq), 0)).astype(F32)
            hit = jnp.max(jnp.where(eq & (rank <= want), kpos, -1.0), axis=0, keepdims=True)
            return rank[WIDE - 1:WIDE, :], jnp.maximum(limit, hit)

        zero = jnp.zeros((1, tq), F32)
        return lax.fori_loop(0, nwide, body, (zero, zero - 1.0))[1].astype(I32)

    thr, jlim = _select(count, (1, tq), nsel, True, tie_limit)

    m_ref[...] = jnp.full(m_ref.shape, M_INIT, F32)
    l_ref[...] = jnp.zeros(l_ref.shape, F32)
    acc_ref[...] = jnp.zeros(acc_ref.shape, F32)

    def tile(off, height, bias_ref):
        kt = keys_ref[pl.ds(off, height), :]
        kpos = off + lax.broadcasted_iota(I32, (height, tq), 0)
        sel = ((kt > thr) | ((kt == thr) & (kpos <= jlim))) & (kpos <= qpos)
        amask = jnp.where(sel, 0.0, NEG)
        def scores(p):
            return _dot(k_ref[pl.ds(off, height), p * LANES:(p + 1) * LANES], qz_ref[p])

        def softmax(p, st):
            out = []
            for e in range(2):
                h = 2 * p + e
                lg = st[:, e * tq:(e + 1) * tq] + amask
                if bias_ref is not None:
                    lg = lg + bias_ref[h]
                m_old = m_ref[h]
                m_new = jnp.maximum(m_old, jnp.max(lg, axis=0, keepdims=True))
                alpha = jnp.exp(m_old - m_new)
                pr = jnp.exp(lg - m_new[0:1, :])
                l_ref[h] = alpha * l_ref[h] + jnp.sum(pr, axis=0, keepdims=True)
                m_ref[h] = m_new
                out.append((h, pr.astype(BF16), alpha[0:1, :]))
            return out

        def values(items):
            for h, pr, alpha in items:
                pv = _dot(vt_ref[h * HEAD_DIM:(h + 1) * HEAD_DIM, pl.ds(off, height)], pr)
                acc_ref[h] = alpha * acc_ref[h] + pv

        sts = {0: scores(0), 1: scores(1)}
        done = None
        for p in range(PAIRS):
            cur = softmax(p, sts.pop(p))
            if p + 2 < PAIRS:
                sts[p + 2] = scores(p + 2)
            if done is not None:
                values(done)
            done = cur
        values(done)

    nfar = jnp.maximum(i - 1, 0)
    nfar_wide = nfar // (WIDE // LANES)

    def far_wide(c, carry):
        tile(pl.multiple_of(c * WIDE, WIDE), WIDE, None)
        return carry

    def far_narrow(j, carry):
        tile(pl.multiple_of(j * LANES, LANES), LANES, None)
        return carry

    lax.fori_loop(0, nfar_wide, far_wide, 0)
    lax.fori_loop(nfar_wide * (WIDE // LANES), nfar, far_narrow, 0)
    for h in range(N_HEADS):
        m_ref[h] = m_ref[h] + far_ref[h]

    @pl.when(i >= 1)
    def _():
        tile(pl.multiple_of((i - 1) * LANES, LANES), LANES, bs_ref)

    tile(pl.multiple_of(i * LANES, LANES), LANES, bd_ref)
    out_t = jnp.concatenate([acc_ref[h] / l_ref[h][0:1, :] for h in range(N_HEADS)], axis=0)
    o_ref[...] = jnp.transpose(out_t)


def _attn_prompt(far, h2, hb, vt, bias_diag_t, bias_sub_t, nsel):
    t = h2.shape[0]
    tq = LANES
    nq = IDX_HEADS * IDX_DIM
    return pl.pallas_call(
        functools.partial(_attn_prompt_kernel, tq=tq, nsel=nsel),
        grid=(t // tq,),
        in_specs=[pl.BlockSpec(memory_space=pltpu.SMEM),
                  pl.BlockSpec((tq, nq), lambda i: (i, C_QIDX // nq)),
                  pl.BlockSpec((tq, LANES), lambda i: (i, C_KW // LANES)),
                  pl.BlockSpec((tq, MIX_DIM), lambda i: (i, C_Q // MIX_DIM)),
                  pl.BlockSpec((t, LANES), lambda i: (0, C_KW // LANES)),
                  pl.BlockSpec((t, MIX_DIM), lambda i: (0, C_K // MIX_DIM)),
                  pl.BlockSpec((MIX_DIM, t), lambda i: (0, 0)),
                  pl.BlockSpec((N_HEADS, LANES, tq), lambda i: (0, 0, 0)),
                  pl.BlockSpec((N_HEADS, LANES, tq), lambda i: (0, 0, 0))],
        out_specs=pl.BlockSpec((tq, MIX_DIM), lambda i: (i, 0)),
        out_shape=jax.ShapeDtypeStruct((t, MIX_DIM), F32),
        scratch_shapes=[pltpu.VMEM((t, tq), I32),
                        pltpu.VMEM((IDX_DIM, IDX_HEADS * tq), BF16),
                        pltpu.VMEM((PAIRS, LANES, 2 * tq), BF16),
                        pltpu.VMEM((N_HEADS, SUBLANES, tq), F32),
                        pltpu.VMEM((N_HEADS, SUBLANES, tq), F32),
                        pltpu.VMEM((N_HEADS, HEAD_DIM, tq), F32)],
        compiler_params=_cparams(("parallel",)),
        name="attn_prompt",
    )(far, hb, h2, hb, hb, hb, vt, bias_diag_t, bias_sub_t)


QROWS = 8
SEL_PAGES = 64
ATT_PAGES = 32


SEL_ROWS = 8 * QROWS


def _score_sample_kernel(pt_ref, qst_ref, wcol_ref, *refs, treal):
    pages = refs[:SEL_PAGES]
    knew_ref, keys_ref = refs[SEL_PAGES:]
    j = pl.program_id(1)
    ntiles = keys_ref.shape[2] // LANES

    def score(kt_t):
        r = jnp.maximum(_dot(qst_ref[0], kt_t), 0.0) * (wcol_ref[0] * IDX_SCALE)
        s = r[0:QROWS]
        for hh in range(1, IDX_HEADS):
            s = s + r[hh * QROWS:(hh + 1) * QROWS]
        return _mono_key(s)

    for g in range(SEL_PAGES):
        off = pl.multiple_of((j * SEL_PAGES + g) * LANES, LANES)
        keys_ref[0, :, pl.ds(off, LANES)] = score(pages[g][0, 0].astype(BF16))

    @pl.when(j == pl.num_programs(1) - 1)
    def _():
        rowq = lax.broadcasted_iota(I32, (QROWS, LANES), 0)
        lane = lax.broadcasted_iota(I32, (QROWS, LANES), 1)
        knew = jnp.where((lane < treal) & (lane <= rowq), score(knew_ref[0]), INT_MIN)
        keys_ref[0, :, (ntiles - 1) * LANES:] = knew


def _select_sample_kernel(keys_ref, mask_ref, *, nsel, treal):
    rows, nk = keys_ref.shape
    ntiles = nk // LANES
    lane = lax.broadcasted_iota(I32, (rows, LANES), 1)

    def count(pred):
        acc = jnp.zeros((rows, LANES), F32)
        for jt in range(ntiles):
            kt = keys_ref[:, jt * LANES:(jt + 1) * LANES]
            acc = acc + jnp.where(pred(kt, jt * LANES + lane), 1.0, 0.0)
        return jnp.sum(acc, axis=1, keepdims=True)

    def tie_limit(v, want):
        upper = (lax.broadcasted_iota(I32, (LANES, LANES), 0) <= lax.broadcasted_iota(I32, (LANES, LANES), 1))
        upper = jnp.where(upper, 1.0, 0.0).astype(BF16)
        eqs = [keys_ref[:, jt * LANES:(jt + 1) * LANES] == v for jt in range(ntiles)]
        ranks = [_dot(jnp.where(eq, 1.0, 0.0).astype(BF16), upper) for eq in eqs]
        base = jnp.zeros((rows, 1), F32)
        best = jnp.full((rows, LANES), -1.0, F32)
        for jt in range(ntiles):
            ok = eqs[jt] & (base + ranks[jt] <= want)
            best = jnp.maximum(best, jnp.where(ok, (jt * LANES + lane).astype(F32), -1.0))
            base = base + ranks[jt][:, LANES - 1:LANES]
        return jnp.max(best, axis=1, keepdims=True).astype(I32)

    row_ok = lax.broadcasted_iota(I32, (rows, 1), 0) % QROWS < treal
    thr, jlim = _select(count, (rows, 1), nsel, row_ok, tie_limit)
    for jt in range(ntiles):
        kt = keys_ref[:, jt * LANES:(jt + 1) * LANES]
        sel = (kt > thr) | ((kt == thr) & (jt * LANES + lane <= jlim))
        mask_ref[:, jt * LANES:(jt + 1) * LANES] = jnp.where(sel, 1.0, 0.0)


def _sel_sample(page_table, qst, wcol, cache_kidx, knew, layer, nsel, treal):
    b, npages = page_table.shape
    nk = (npages + 1) * PAGE
    page_spec = lambda g: pl.BlockSpec(
        (1, 1, IDX_DIM, PAGE), lambda i, j, pt: (layer, pt[i, j * SEL_PAGES + g], 0, 0))
    grid_spec = pltpu.PrefetchScalarGridSpec(
        num_scalar_prefetch=1,
        grid=(b, npages // SEL_PAGES),
        in_specs=[pl.BlockSpec((1, IDX_HEADS * QROWS, IDX_DIM), lambda i, j, pt: (i, 0, 0)),
                  pl.BlockSpec((1, IDX_HEADS * QROWS, 1), lambda i, j, pt: (i, 0, 0))]
                 + [page_spec(g) for g in range(SEL_PAGES)]
                 + [pl.BlockSpec((1, IDX_DIM, PAGE), lambda i, j, pt: (i, 0, 0))],
        out_specs=pl.BlockSpec((1, QROWS, nk), lambda i, j, pt: (i, 0, 0)),
    )
    keys = pl.pallas_call(
        functools.partial(_score_sample_kernel, treal=treal),
        grid_spec=grid_spec,
        out_shape=jax.ShapeDtypeStruct((b, QROWS, nk), I32),
        compiler_params=_cparams(("parallel", "arbitrary")),
        name="score_sample",
    )(page_table, qst, wcol, *([cache_kidx] * SEL_PAGES), knew)
    rows = b * QROWS
    blk = math.gcd(rows, SEL_ROWS)
    mask = pl.pallas_call(
        functools.partial(_select_sample_kernel, nsel=nsel, treal=treal),
        grid=(rows // blk,),
        in_specs=[pl.BlockSpec((blk, nk), lambda i: (i, 0))],
        out_specs=pl.BlockSpec((blk, nk), lambda i: (i, 0)),
        out_shape=jax.ShapeDtypeStruct((rows, nk), F32),
        compiler_params=_cparams(("parallel",)),
        name="select_sample",
    )(keys.reshape(rows, nk))
    return mask.reshape(b, QROWS, nk)


def _attn_sample_kernel(pt_ref, far_ref, qh_ref, *refs):
    kp = refs[:ATT_PAGES]
    vp = refs[ATT_PAGES:2 * ATT_PAGES]
    mask_ref, masknew_ref, blast_ref, bnew_ref, knew_ref, vnew_ref, o_ref, m_ref, l_ref, acc_ref = refs[2 * ATT_PAGES:]
    j = pl.program_id(1)
    is_last = j == pl.num_programs(1) - 1

    @pl.when(j == 0)
    def _():
        m_ref[...] = jnp.full(m_ref.shape, M_INIT, F32)
        l_ref[...] = jnp.zeros(l_ref.shape, F32)
        acc_ref[...] = jnp.zeros(acc_ref.shape, F32)

    def update(lgs, vhs):
        prs, alphas = [], []
        for h in range(N_HEADS):
            m_old = m_ref[h]
            m_new = jnp.maximum(m_old, jnp.max(lgs[h], axis=1, keepdims=True))
            alpha = jnp.exp(m_old - m_new)
            pr = jnp.exp(lgs[h] - m_new[:, 0:1])
            l_ref[h] = alpha * l_ref[h] + jnp.sum(pr, axis=1, keepdims=True)
            m_ref[h] = m_new
            prs.append(pr.astype(BF16))
            alphas.append(alpha[:, :HEAD_DIM])
        for h in range(N_HEADS):
            acc_ref[h] = alphas[h] * acc_ref[h] + _dot_nt(prs[h], vhs[h])

    def q_of(h):
        return (qh_ref[0, h].astype(F32) * ATT_SCALE).astype(BF16)

    def head_t(refs_, h):
        return jnp.concatenate([refs_[g][0, 0, h] for g in range(ATT_PAGES)], axis=1).astype(BF16)

    amask = jnp.where(mask_ref[0] > 0.5, 0.0, NEG)
    lgs = []
    for h in range(N_HEADS):
        far = jnp.full((QROWS, LANES), far_ref[h], F32)
        bias = jnp.concatenate([far] * (ATT_PAGES - 1) + [jnp.where(is_last, blast_ref[h], far)], axis=1)
        lgs.append(_dot(q_of(h), head_t(kp, h)) + amask + bias)
    update(lgs, [head_t(vp, h) for h in range(N_HEADS)])

    @pl.when(is_last)
    def _():
        amask_new = jnp.where(masknew_ref[0] > 0.5, 0.0, NEG)
        lgs_new = [_dot(q_of(h), knew_ref[0, h]) + amask_new + bnew_ref[h] for h in range(N_HEADS)]
        update(lgs_new, [vnew_ref[0, h] for h in range(N_HEADS)])
        outs = []
        for h in range(N_HEADS):
            lh = l_ref[h][:, :HEAD_DIM]
            outs.append(acc_ref[h] / jnp.where(lh > 0.0, lh, 1.0))
        o_ref[0] = jnp.concatenate(outs, axis=1)


def _attn_sample(page_table, far, qh, cache_k2, cache_v2, mask, bias_last, bias_new, knew, vnew, layer):
    b, npages = page_table.shape
    nsteps = npages // ATT_PAGES
    page_spec = lambda g: pl.BlockSpec(
        (1, 1, N_HEADS, HEAD_DIM, PAGE), lambda i, j, pt, far: (layer, pt[i, j * ATT_PAGES + g], 0, 0, 0))
    const3 = lambda i, j, pt, far: (0, 0, 0)
    per_b4 = lambda i, j, pt, far: (i, 0, 0, 0)
    grid_spec = pltpu.PrefetchScalarGridSpec(
        num_scalar_prefetch=2,
        grid=(b, nsteps),
        in_specs=[pl.BlockSpec((1, N_HEADS, QROWS, HEAD_DIM), per_b4)]
                 + [page_spec(g) for g in range(ATT_PAGES)] * 2
                 + [pl.BlockSpec((1, QROWS, ATT_PAGES * PAGE), lambda i, j, pt, far: (i, 0, j)),
                    pl.BlockSpec((1, QROWS, PAGE), lambda i, j, pt, far: (i, 0, npages)),
                    pl.BlockSpec((N_HEADS, QROWS, PAGE), const3),
                    pl.BlockSpec((N_HEADS, QROWS, PAGE), const3),
                    pl.BlockSpec((1, N_HEADS, HEAD_DIM, PAGE), per_b4),
                    pl.BlockSpec((1, N_HEADS, HEAD_DIM, PAGE), per_b4)],
        out_specs=pl.BlockSpec((1, QROWS, MIX_DIM), lambda i, j, pt, far: (i, 0, 0)),
        scratch_shapes=[pltpu.VMEM((N_HEADS, QROWS, LANES), F32),
                        pltpu.VMEM((N_HEADS, QROWS, LANES), F32),
                        pltpu.VMEM((N_HEADS, QROWS, HEAD_DIM), F32)],
    )
    return pl.pallas_call(
        _attn_sample_kernel,
        grid_spec=grid_spec,
        out_shape=jax.ShapeDtypeStruct((b, QROWS, MIX_DIM), F32),
        compiler_params=_cparams(("parallel", "arbitrary")),
        name="attn_sample",
    )(page_table, far, qh, *([cache_k2] * ATT_PAGES), *([cache_v2] * ATT_PAGES), mask, mask,
      bias_last, bias_new, knew, vnew)


def _bucket_table(n):
    d = np.arange(n)
    nf = np.maximum(d, 1).astype(np.float32)
    half = REL_BUCKETS // 2
    large = half + (np.log(nf / np.float32(half)) / np.float32(math.log(REL_MAX_DIST / half))
                    * np.float32(REL_BUCKETS - half)).astype(np.int32)
    return np.where(d < half, d, np.minimum(large, REL_BUCKETS - 1)).astype(np.int32)


def _pack_w_in(w):
    w_t = jnp.transpose(w, (0, 2, 1))
    p, z, xbc, dt, q, k, v, qi, ki, wi, u, vs = jnp.split(w_t, np.cumsum(IN_SPLITS)[:-1].tolist(), axis=1)
    zeros = lambda n: jnp.zeros((w.shape[0], n, w.shape[1]), w.dtype)
    packed = jnp.concatenate(
        [p, z, xbc, q, k, v, u, vs, qi, ki, wi, zeros(LANES - IDX_DIM - IDX_HEADS), dt, zeros(LANES - N_HEADS)],
        axis=1)
    return packed.astype(BF16)


def _pad_lanes(x):
    return jnp.pad(x.reshape(1, -1), ((0, 0), (0, LANES - x.shape[-1])))


def _pad_axis(x, axis, size):
    pads = [(0, 0)] * x.ndim
    pads[axis] = (0, size - x.shape[axis])
    return jnp.pad(x, pads)


def kernel(x_prompt, x_sample, state_pool, state_conv, state_ssm, cache_k, cache_v, cache_kidx, page_table,
           rel_bias, norm_mix, w_in, pool_w, pool_scale, conv_w, conv_b, dt_bias, a_log, d_skip, ssm_norm,
           sgu_w, sgu_b, w_out, norm_mlp, mlp_w1, mlp_w2, norm_final):
    depth = w_in.shape[0]
    bp, t, _ = x_prompt.shape
    bs, ts, _ = x_sample.shape
    npages = page_table.shape[1]
    past = npages * PAGE
    n_pool = cache_k.shape[1]
    assert bp == 1 and t % WIDE == 0 and npages % SEL_PAGES == 0 and npages % ATT_PAGES == 0
    assert max(t, past + PAGE) <= 2 ** POS_BITS and ts <= QROWS
    tc = LANES
    nsel_p = min(TOPK_MAX, t // 4)
    nsel_s = min(TOPK_MAX, (past + ts) // 4)

    bucket = _bucket_table(2 * LANES)
    far = rel_bias[REL_BUCKETS - 1]
    kk = np.arange(LANES)[:, None]
    qq = np.arange(LANES)[None, :]
    def bias_of(dist):
        onehot = np.eye(REL_BUCKETS, dtype=np.float32)[bucket[dist].reshape(-1)]
        vals = jnp.dot(onehot, rel_bias, precision=lax.Precision.HIGHEST)
        return jnp.transpose(vals).reshape((N_HEADS,) + dist.shape)

    bias_diag_t = bias_of(np.maximum(qq - kk, 0))
    bias_sub_t = bias_of(np.minimum(LANES + qq - kk, 2 * LANES - 1))
    sq = np.arange(QROWS)[:, None]
    sk = np.arange(PAGE)[None, :]
    bias_last = bias_of(np.minimum(PAGE + sq - sk, 2 * LANES - 1))
    bias_new = bias_of(np.maximum(sq - sk, 0))

    cache_k2 = jnp.transpose(cache_k, (0, 1, 3, 4, 2))
    cache_v2 = jnp.transpose(cache_v, (0, 1, 3, 4, 2))
    cache_kidx2 = jnp.transpose(cache_kidx, (0, 1, 3, 2))
    w_in_t = _pack_w_in(w_in)
    w_out_b, w1_b, w2_b = w_out.astype(BF16), mlp_w1.astype(BF16), mlp_w2.astype(BF16)

    xp = x_prompt.reshape(t, D_MODEL)
    xs = x_sample.reshape(bs * ts, D_MODEL)
    zero_pool = jnp.zeros((1, POOL_HIST, MIX_DIM), F32)
    zero_conv = jnp.zeros((1, CONV_W - 1, XBC_DIM), F32)
    zero_ssm = jnp.zeros((1, N_HEADS, HEAD_DIM, SSM_STATE), F32)
    tm_p = min(1024, t)
    tm_s = bs * ts
    outs_p, outs_s = [], []

    for l in range(depth):
        g_mix = norm_mix[l].reshape(1, D_MODEL)
        g_mlp = norm_mlp[l].reshape(1, D_MODEL)
        g_out = norm_final.reshape(1, D_MODEL) if l == depth - 1 else g_mlp
        pscale = pool_scale[l].reshape(1, MIX_DIM)
        cb = conv_b[l].reshape(1, XBC_DIM)
        dtb, alog, dsk = _pad_lanes(dt_bias[l]), _pad_lanes(a_log[l]), _pad_lanes(d_skip[l])
        snorm = ssm_norm[l].reshape(1, MIX_DIM)
        sgu_b_col = sgu_b[l][:, :, None]
        final = l == depth - 1

        hp, hpb, vt = _inproj(xp, g_mix, w_in_t, l, tm=min(1024, t), tn=1024, attn_operands=True)
        hp3 = hp.reshape(1, t, H_PACKED)
        y_pool, new_pool = _pool(hp3, zero_pool, pool_w[l], pscale, tc=WIDE, treal=WIDE, pos0=0)
        y_ssm, new_conv, new_h = _ssd(hp3, zero_conv, zero_ssm, conv_w[l], cb, dtb, alog, dsk, snorm,
                                      tc=tc, treal=tc)
        y_sgu, = _sgu(hp3, sgu_w[l], sgu_b_col, tc=tc, treal=tc, nsub=WIDE // tc)
        kf = hp[:, C_K:C_K + MIX_DIM]
        vf = hp[:, C_V:C_V + MIX_DIM]
        kif = hp[:, C_KW:C_KW + IDX_DIM]
        y_att = _attn_prompt(far, hp, hpb, vt, bias_diag_t, bias_sub_t, nsel_p)
        x1, xn = _outproj(xp, [y_pool[0], y_ssm[0], y_att, y_sgu[0]], w_out_b, l, g_mlp, tm=min(512, t))
        xp = _mlp(xn, w1_b, w2_b, l, x1, g_out, tm=tm_p, final_norm=final)
        outs_p.append((new_pool, new_conv, new_h, kf.reshape(1, t, N_HEADS, HEAD_DIM),
                       vf.reshape(1, t, N_HEADS, HEAD_DIM), kif.reshape(1, t, IDX_DIM)))

        hs, = _inproj(xs, g_mix, w_in_t, l, tm=tm_s)
        hs3 = hs.reshape(bs, ts, H_PACKED)
        y_pool, new_pool = _pool(hs3, state_pool[l], pool_w[l], pscale, tc=tc, treal=ts, pos0=past)
        y_ssm, new_conv, new_h = _ssd(hs3, state_conv[l], state_ssm[l], conv_w[l], cb, dtb, alog, dsk, snorm,
                                      tc=tc, treal=ts)
        y_sgu, v_rows = _sgu(hs3, sgu_w[l], sgu_b_col, tc=tc, treal=ts, emit_v=True)
        kf = hs3[:, :, C_K:C_K + MIX_DIM]
        vf = hs3[:, :, C_V:C_V + MIX_DIM]
        kif = hs3[:, :, C_KW:C_KW + IDX_DIM]
        qi = hs3[:, :, C_QIDX:C_QIDX + IDX_HEADS * IDX_DIM].reshape(bs, ts, IDX_HEADS, IDX_DIM)
        qst = _pad_axis(jnp.transpose(qi, (0, 2, 1, 3)), 2, QROWS)
        qst = qst.reshape(bs, IDX_HEADS * QROWS, IDX_DIM).astype(BF16)
        wi = hs3[:, :, C_KW + IDX_DIM:C_KW + IDX_DIM + IDX_HEADS]
        wcol = _pad_axis(jnp.transpose(wi, (0, 2, 1)), 2, QROWS).reshape(bs, IDX_HEADS * QROWS, 1)
        kinew_t = _pad_axis(jnp.transpose(kif, (0, 2, 1)), 2, PAGE).astype(BF16)
        mask = _sel_sample(page_table, qst, wcol, cache_kidx2, kinew_t, l, nsel_s, ts)
        heads = lambda x, perm: jnp.transpose(x.reshape(bs, ts, N_HEADS, HEAD_DIM), perm)
        qh = _pad_axis(heads(hs3[:, :, C_Q:C_Q + MIX_DIM], (0, 2, 1, 3)), 2, QROWS).astype(BF16)
        knew = _pad_axis(heads(kf, (0, 2, 3, 1)), 3, PAGE).astype(BF16)
        vnew = _pad_axis(heads(vf, (0, 2, 3, 1)), 3, PAGE).astype(BF16)
        att = _attn_sample(page_table, far, qh, cache_k2, cache_v2, mask, bias_last, bias_new, knew, vnew, l)
        y_att = att[:, :ts].reshape(bs * ts, MIX_DIM)
        flat = lambda y: y.reshape(bs * ts, MIX_DIM)
        x1, xn = _outproj(xs, [flat(y_pool), flat(y_ssm), y_att, flat(y_sgu)], w_out_b, l, g_mlp, tm=tm_s)
        xs = _mlp(xn, w1_b, w2_b, l, x1, g_out, tm=tm_s, final_norm=final)
        outs_s.append((new_pool, new_conv, new_h, kf.reshape(bs, ts, N_HEADS, HEAD_DIM),
                       vf.reshape(bs, ts, N_HEADS, HEAD_DIM), kif, v_rows))

    stack = lambda outs, i: jnp.stack([o[i] for o in outs])
    return (xp.reshape(1, t, D_MODEL), xs.reshape(bs, ts, D_MODEL),
            *[stack(outs_p, i) for i in range(6)],
            *[stack(outs_s, i) for i in range(7)])
```

```python
import functools
import math

import numpy as np
import jax
import jax.numpy as jnp
from jax import lax
from jax.experimental import pallas as pl
from jax.experimental.pallas import tpu as pltpu

F32 = jnp.float32
BF16 = jnp.bfloat16
I32 = jnp.int32

LANES = 128
SUBLANES = 8
D_MODEL = 2048
N_HEADS = 8
HEAD_DIM = 64
MIX_DIM = 512
POOL_WINDOWS = (2, 4, 8, 16)
POOL_HIST = 15
CONV_W = 4
SSM_GROUPS = 2
SSM_STATE = 128
XBC_DIM = MIX_DIM + 2 * SSM_GROUPS * SSM_STATE
IDX_HEADS = 4
IDX_DIM = 64
TOPK_MAX = 256
PAGE = 128
REL_BUCKETS = 32
REL_MAX_DIST = 128
ATT_SCALE = HEAD_DIM ** -0.5
IDX_SCALE = (IDX_HEADS * IDX_DIM) ** -0.5
SGU_GROUPS = 4
D_FF = 4 * D_MODEL
EPS = 1e-6
NEG = -1e30
INT_MIN = -(2 ** 31)
INT_MAX = 2 ** 31 - 1
IN_SPLITS = (512, 512, 1024, 8, 512, 512, 512, 256, 64, 4, 512, 512)

C_P, C_Z, C_XBC, C_Q, C_K, C_V, C_U, C_VS, C_QIDX, C_KW, C_DT = (
    0, 512, 1024, 2048, 2560, 3072, 3584, 4096, 4608, 4864, 4992)
H_PACKED = 5120
VMEM_LIMIT = 56 * 1024 * 1024


def _cparams(sem):
    return pltpu.CompilerParams(dimension_semantics=sem, vmem_limit_bytes=VMEM_LIMIT)


def _dot(a, b):
    return jnp.dot(a, b, preferred_element_type=F32)


def _dot_nt(a, b):
    return lax.dot_general(a, b, (((1,), (1,)), ((), ())), preferred_element_type=F32)


def _dot_tn(a, b):
    return lax.dot_general(a, b, (((0,), (0,)), ((), ())), preferred_element_type=F32)


def _silu(x):
    return x * (1.0 / (1.0 + jnp.exp(-x)))


def _gelu_tanh(x):
    return 0.5 * x * (1.0 + jnp.tanh(math.sqrt(2.0 / math.pi) * (x + 0.044715 * (x * x * x))))


def _softplus(x):
    return jnp.maximum(x, 0.0) + jnp.log1p(jnp.exp(-jnp.abs(x)))


def _rms(x, g):
    ms = jnp.mean(x * x, axis=-1, keepdims=True)
    return x * lax.rsqrt(ms + EPS) * g


def _inproj_kernel(x_ref, g_ref, w_ref, o_ref, *rest, attn_operands, tn):
    xn_ref = rest[-1]

    @pl.when(pl.program_id(1) == 0)
    def _():
        xn_ref[...] = _rms(x_ref[...], g_ref[...]).astype(BF16)

    h = _dot_nt(xn_ref[...], w_ref[0])
    o_ref[...] = h
    if attn_operands:
        ob_ref, vt_ref = rest[:2]
        ob_ref[...] = h.astype(BF16)

        @pl.when(pl.program_id(1) == C_V // tn)
        def _():
            vt_ref[...] = jnp.transpose(h[:, C_V % tn:C_V % tn + MIX_DIM]).astype(BF16)


def _inproj(x, g, w_t, layer, tm, tn=512, attn_operands=False):
    m = x.shape[0]
    n = w_t.shape[1]
    out_specs = [pl.BlockSpec((tm, tn), lambda i, j: (i, j))]
    out_shape = [jax.ShapeDtypeStruct((m, n), F32)]
    if attn_operands:
        out_specs += [pl.BlockSpec((tm, tn), lambda i, j: (i, j)), pl.BlockSpec((MIX_DIM, tm), lambda i, j: (0, i))]
        out_shape += [jax.ShapeDtypeStruct((m, n), BF16), jax.ShapeDtypeStruct((MIX_DIM, m), BF16)]
    return pl.pallas_call(
        functools.partial(_inproj_kernel, attn_operands=attn_operands, tn=tn),
        grid=(m // tm, n // tn),
        in_specs=[pl.BlockSpec((tm, D_MODEL), lambda i, j: (i, 0)),
                  pl.BlockSpec((1, D_MODEL), lambda i, j: (0, 0)),
                  pl.BlockSpec((1, tn, D_MODEL), lambda i, j: (layer, j, 0))],
        out_specs=out_specs,
        out_shape=out_shape,
        scratch_shapes=[pltpu.VMEM((tm, D_MODEL), BF16)],
        compiler_params=_cparams(("parallel", "arbitrary")),
        name="inproj",
    )(x, g, w_t)


def _outproj_kernel(x_ref, y0_ref, y1_ref, y2_ref, y3_ref, w_ref, g_ref, x1_ref, xn_ref):
    y = jnp.concatenate([y0_ref[...], y1_ref[...], y2_ref[...], y3_ref[...]], axis=1).astype(BF16)
    x1 = x_ref[...] + _dot(y, w_ref[0])
    x1_ref[...] = x1
    xn_ref[...] = _rms(x1, g_ref[...]).astype(BF16)


def _outproj(x, ys, w, layer, g, tm):
    m = x.shape[0]
    row = lambda i: (i, 0)
    return pl.pallas_call(
        _outproj_kernel,
        grid=(m // tm,),
        in_specs=[pl.BlockSpec((tm, D_MODEL), row)]
                 + [pl.BlockSpec((tm, MIX_DIM), row)] * 4
                 + [pl.BlockSpec((1, D_MODEL, D_MODEL), lambda i: (layer, 0, 0)),
                    pl.BlockSpec((1, D_MODEL), lambda i: (0, 0))],
        out_specs=[pl.BlockSpec((tm, D_MODEL), row), pl.BlockSpec((tm, D_MODEL), row)],
        out_shape=[jax.ShapeDtypeStruct((m, D_MODEL), F32), jax.ShapeDtypeStruct((m, D_MODEL), BF16)],
        compiler_params=_cparams(("parallel",)),
        name="outproj",
    )(x, *ys, w, g)


def _mlp_kernel(xn_ref, w1_ref, w2_ref, x1_ref, g_ref, o_ref, *, final_norm):
    j = pl.program_id(1)

    @pl.when(j == 0)
    def _():
        o_ref[...] = x1_ref[...]

    hid = _dot(xn_ref[...], w1_ref[0])
    hid = jnp.square(jnp.maximum(hid, 0.0)).astype(BF16)
    o_ref[...] += _dot(hid, w2_ref[0])

    if final_norm:
        @pl.when(j == pl.num_programs(1) - 1)
        def _():
            o_ref[...] = _rms(o_ref[...], g_ref[...])


def _mlp(xn, w1, w2, layer, x1, g, tm, final_norm, tf=512):
    m = xn.shape[0]
    return pl.pallas_call(
        functools.partial(_mlp_kernel, final_norm=final_norm),
        grid=(m // tm, D_FF // tf),
        in_specs=[pl.BlockSpec((tm, D_MODEL), lambda i, j: (i, 0)),
                  pl.BlockSpec((1, D_MODEL, tf), lambda i, j: (layer, 0, j)),
                  pl.BlockSpec((1, tf, D_MODEL), lambda i, j: (layer, j, 0)),
                  pl.BlockSpec((tm, D_MODEL), lambda i, j: (i, 0)),
                  pl.BlockSpec((1, D_MODEL), lambda i, j: (0, 0))],
        out_specs=pl.BlockSpec((tm, D_MODEL), lambda i, j: (i, 0)),
        out_shape=jax.ShapeDtypeStruct((m, D_MODEL), F32),
        compiler_params=_cparams(("parallel", "arbitrary")),
        name="mlp",
    )(xn, w1, w2, x1, g)


def _pool_kernel(p_ref, hist_ref, w_ref, scale_ref, y_ref, newhist_ref, ext_ref, *, tc, treal, pos0):
    c = pl.program_id(1)
    hrows = POOL_HIST + 1

    @pl.when(c == 0)
    def _():
        ext_ref[0:1, :] = jnp.zeros((1, MIX_DIM), F32)
        ext_ref[1:hrows, :] = hist_ref[0]

    if treal < tc:
        ext_ref[hrows:hrows + tc, :] = jnp.zeros((tc, MIX_DIM), F32)
    ext_ref[hrows:hrows + treal, :] = p_ref[0]

    pos = pos0 + c * treal + lax.broadcasted_iota(I32, (tc, 1), 0)
    for g, win in enumerate(POOL_WINDOWS):
        cols = slice(g * LANES, (g + 1) * LANES)
        cur = ext_ref[hrows:hrows + tc, cols]
        s = cur
        for j in range(1, win):
            s = s + ext_ref[hrows - j:hrows - j + tc, cols]
        cnt = jnp.minimum(pos + 1, win).astype(F32)
        diff = s / cnt - cur
        y = _dot(diff, w_ref[g]) * scale_ref[:, cols]
        y_ref[0, :, cols] = y[:treal]

    newhist_ref[0] = ext_ref[treal + 1:treal + hrows, :]
    ext_ref[0:hrows, :] = ext_ref[treal:treal + hrows, :]


def _pool(h3, hist, pool_w, pool_scale, *, tc, treal, pos0):
    b, t, _ = h3.shape
    return pl.pallas_call(
        functools.partial(_pool_kernel, tc=tc, treal=treal, pos0=pos0),
        grid=(b, t // treal),
        in_specs=[pl.BlockSpec((1, treal, MIX_DIM), lambda i, c: (i, c, C_P // MIX_DIM)),
                  pl.BlockSpec((1, POOL_HIST, MIX_DIM), lambda i, c: (i, 0, 0)),
                  pl.BlockSpec((4, LANES, LANES), lambda i, c: (0, 0, 0)),
                  pl.BlockSpec((1, MIX_DIM), lambda i, c: (0, 0))],
        out_specs=[pl.BlockSpec((1, treal, MIX_DIM), lambda i, c: (i, c, 0)),
                   pl.BlockSpec((1, POOL_HIST, MIX_DIM), lambda i, c: (i, 0, 0))],
        out_shape=[jax.ShapeDtypeStruct((b, t, MIX_DIM), F32),
                   jax.ShapeDtypeStruct((b, POOL_HIST, MIX_DIM), F32)],
        scratch_shapes=[pltpu.VMEM((POOL_HIST + 1 + tc, MIX_DIM), F32)],
        compiler_params=_cparams(("parallel", "arbitrary")),
        name="pool",
    )(h3, hist, pool_w, pool_scale)


def _sgu_kernel(u_ref, v_ref, w_ref, b_ref, y_ref, *rest, tc, treal, nsub, emit_v):
    ubuf_ref, vbuf_ref = rest[-2:]
    r = lax.broadcasted_iota(I32, (tc, tc), 0)
    s = lax.broadcasted_iota(I32, (tc, tc), 1)
    for sub in range(nsub):
        rows = slice(sub * treal, (sub + 1) * treal)
        if treal < tc:
            ubuf_ref[...] = jnp.zeros((tc, MIX_DIM), F32)
            vbuf_ref[...] = jnp.zeros((tc, MIX_DIM), F32)
        ubuf_ref[0:treal, :] = u_ref[0, rows, :]
        vbuf_ref[0:treal, :] = v_ref[0, rows, :]
        u = _gelu_tanh(ubuf_ref[...])
        v = _gelu_tanh(vbuf_ref[...])
        if emit_v:
            rest[0][0, rows, :] = v[:treal]
        for g in range(SGU_GROUPS):
            cols = slice(g * LANES, (g + 1) * LANES)
            w = jnp.where(s <= r, w_ref[g], 0.0)
            mixed = _dot(w, v[:, cols]) + b_ref[g]
            y_ref[0, rows, cols] = (u[:, cols] * mixed)[:treal]


def _sgu(h3, sgu_w, sgu_b_col, *, tc, treal, nsub=1, emit_v=False):
    b, t, _ = h3.shape
    rows = nsub * treal
    out_specs = [pl.BlockSpec((1, rows, MIX_DIM), lambda i, c: (i, c, 0))]
    out_shape = [jax.ShapeDtypeStruct((b, t, MIX_DIM), F32)]
    return pl.pallas_call(
        functools.partial(_sgu_kernel, tc=tc, treal=treal, nsub=nsub, emit_v=emit_v),
        grid=(b, t // rows),
        in_specs=[pl.BlockSpec((1, rows, MIX_DIM), lambda i, c: (i, c, C_U // MIX_DIM)),
                  pl.BlockSpec((1, rows, MIX_DIM), lambda i, c: (i, c, C_VS // MIX_DIM)),
                  pl.BlockSpec((SGU_GROUPS, tc, tc), lambda i, c: (0, 0, 0)),
                  pl.BlockSpec((SGU_GROUPS, tc, 1), lambda i, c: (0, 0, 0))],
        out_specs=out_specs * (2 if emit_v else 1),
        out_shape=out_shape * (2 if emit_v else 1),
        scratch_shapes=[pltpu.VMEM((tc, MIX_DIM), F32), pltpu.VMEM((tc, MIX_DIM), F32)],
        compiler_params=_cparams(("parallel", "parallel")),
        name="sgu",
    )(h3, h3, sgu_w, sgu_b_col)


def _cumsum_rows(a, n):
    row = lax.broadcasted_iota(I32, a.shape, 0)
    sh = 1
    while sh < n:
        a = a + jnp.where(row >= sh, pltpu.roll(a, sh, 0), 0.0)
        sh *= 2
    return a


def _ssd_kernel(z_ref, xbc_ref, dt_ref, hc_ref, h0_ref, convw_ref, convb_ref, dtb_ref, alog_ref, dskip_ref,
                norm_ref, y_ref, newconv_ref, newh_ref, ext_ref, zbuf_ref, dtbuf_ref, state_ref, *, tc, treal):
    c = pl.program_id(1)
    pre = SUBLANES
    hist = CONV_W - 1

    @pl.when(c == 0)
    def _():
        ext_ref[0:pre - hist, :] = jnp.zeros((pre - hist, XBC_DIM), F32)
        ext_ref[pre - hist:pre, :] = hc_ref[0]
        state_ref[...] = h0_ref[...]

    if treal < tc:
        ext_ref[pre:pre + tc, :] = jnp.zeros((tc, XBC_DIM), F32)
        zbuf_ref[...] = jnp.zeros((tc, MIX_DIM), F32)
        dtbuf_ref[...] = jnp.zeros((tc, LANES), F32)
    ext_ref[pre:pre + treal, :] = xbc_ref[0]
    zbuf_ref[0:treal, :] = z_ref[0]
    dtbuf_ref[0:treal, :] = dt_ref[0]

    conv = convb_ref[...]
    for j in range(CONV_W):
        conv = conv + ext_ref[pre - hist + j:pre - hist + j + tc, :] * convw_ref[j:j + 1, :]
    conv = _silu(conv)
    xs = conv[:, :MIX_DIM]
    bmat = conv[:, MIX_DIM:MIX_DIM + SSM_GROUPS * SSM_STATE]
    cmat = conv[:, MIX_DIM + SSM_GROUPS * SSM_STATE:]

    rowi = lax.broadcasted_iota(I32, (tc, LANES), 0)
    dt = _softplus(dtbuf_ref[...] + dtb_ref[...])
    if treal < tc:
        dt = jnp.where(rowi < treal, dt, 0.0)
    a = dt * (-jnp.exp(alog_ref[...]))
    acum = _cumsum_rows(a, tc)
    acum_t = jnp.transpose(acum)
    total = acum[tc - 1:tc, :]
    causal = lax.broadcasted_iota(I32, (tc, tc), 1) <= lax.broadcasted_iota(I32, (tc, tc), 0)

    ys = []
    heads_per_group = N_HEADS // SSM_GROUPS
    for g in range(SSM_GROUPS):
        bg = bmat[:, g * SSM_STATE:(g + 1) * SSM_STATE]
        cg = cmat[:, g * SSM_STATE:(g + 1) * SSM_STATE]
        cb = _dot_nt(cg, bg)
        for hh in range(heads_per_group):
            h = g * heads_per_group + hh
            col = acum[:, h:h + 1]
            row = acum_t[h:h + 1, :]
            decay = jnp.exp(jnp.where(causal, col - row, NEG))
            xh = xs[:, h * HEAD_DIM:(h + 1) * HEAD_DIM]
            xdt = xh * dt[:, h:h + 1]
            hprev = state_ref[0, h]
            y = _dot(cb * decay, xdt)
            y = y + _dot_nt(cg, hprev) * jnp.exp(col)
            y = y + dskip_ref[:, h:h + 1] * xh
            ys.append(y)
            tot = total[:, h:h + 1]
            st = _dot_tn(xdt * jnp.exp(tot - col), bg)
            state_ref[0, h] = jnp.exp(tot) * hprev + st
    y = jnp.concatenate(ys, axis=1) * _silu(zbuf_ref[...])
    gw = MIX_DIM // SSM_GROUPS
    outs = []
    for g in range(SSM_GROUPS):
        yg = y[:, g * gw:(g + 1) * gw]
        ms = jnp.mean(yg * yg, axis=-1, keepdims=True)
        outs.append(yg * lax.rsqrt(ms + EPS) * norm_ref[:, g * gw:(g + 1) * gw])
    y_ref[0] = jnp.concatenate(outs, axis=1)[:treal]

    newconv_ref[0] = ext_ref[pre + treal - hist:pre + treal, :]
    ext_ref[0:pre, :] = ext_ref[treal:treal + pre, :]
    newh_ref[...] = state_ref[...]


def _ssd(h3, hist_conv, h0, conv_w, conv_b, dtb, alog, dskip, norm, *, tc, treal):
    b, t, _ = h3.shape
    full2 = lambda i, c: (0, 0)
    return pl.pallas_call(
        functools.partial(_ssd_kernel, tc=tc, treal=treal),
        grid=(b, t // treal),
        in_specs=[pl.BlockSpec((1, treal, MIX_DIM), lambda i, c: (i, c, C_Z // MIX_DIM)),
                  pl.BlockSpec((1, treal, XBC_DIM), lambda i, c: (i, c, C_XBC // XBC_DIM)),
                  pl.BlockSpec((1, treal, LANES), lambda i, c: (i, c, C_DT // LANES)),
                  pl.BlockSpec((1, CONV_W - 1, XBC_DIM), lambda i, c: (i, 0, 0)),
                  pl.BlockSpec((1, N_HEADS, HEAD_DIM, SSM_STATE), lambda i, c: (i, 0, 0, 0)),
                  pl.BlockSpec((CONV_W, XBC_DIM), full2),
                  pl.BlockSpec((1, XBC_DIM), full2),
                  pl.BlockSpec((1, LANES), full2),
                  pl.BlockSpec((1, LANES), full2),
                  pl.BlockSpec((1, LANES), full2),
                  pl.BlockSpec((1, MIX_DIM), full2)],
        out_specs=[pl.BlockSpec((1, treal, MIX_DIM), lambda i, c: (i, c, 0)),
                   pl.BlockSpec((1, CONV_W - 1, XBC_DIM), lambda i, c: (i, 0, 0)),
                   pl.BlockSpec((1, N_HEADS, HEAD_DIM, SSM_STATE), lambda i, c: (i, 0, 0, 0))],
        out_shape=[jax.ShapeDtypeStruct((b, t, MIX_DIM), F32),
                   jax.ShapeDtypeStruct((b, CONV_W - 1, XBC_DIM), F32),
                   jax.ShapeDtypeStruct((b, N_HEADS, HEAD_DIM, SSM_STATE), F32)],
        scratch_shapes=[pltpu.VMEM((SUBLANES + tc, XBC_DIM), F32),
                        pltpu.VMEM((tc, MIX_DIM), F32),
                        pltpu.VMEM((tc, LANES), F32),
                        pltpu.VMEM((1, N_HEADS, HEAD_DIM, SSM_STATE), F32)],
        compiler_params=_cparams(("parallel", "arbitrary")),
        name="ssd",
    )(h3, h3, h3, hist_conv, h0, conv_w, conv_b, dtb, alog, dskip, norm)


M_INIT = 0.5 * NEG
POS_BITS = 14
assert math.frexp(ATT_SCALE)[0] == 0.5, "q is pre-scaled in bf16, exact only for a power-of-two scale"


def _mono_key(s):
    s = jnp.where(s == 0.0, 0.0, s)
    b = lax.bitcast_convert_type(s, I32)
    return b ^ ((b >> 31) & INT_MAX)


def _select(count, stat_shape, nsel, query_ok, tie_limit=None, threshold=None):
    kf = float(nsel)

    def bit_threshold():
        c0 = count(lambda kt, pos: kt >= 0)
        nonneg = c0 >= kf
        prefix = jnp.where(nonneg, 0, INT_MIN).astype(I32)

        def bit_body(it, carry):
            prefix, cge = carry
            cand = prefix | jnp.left_shift(jnp.int32(1), 30 - it)
            cnt = count(lambda kt, pos: kt >= cand)
            take = cnt >= kf
            return jnp.where(take, cand, prefix), jnp.where(take, cnt, cge)

        return lax.fori_loop(0, 31, bit_body, (prefix, jnp.where(nonneg, c0, kf)))

    v, cge = (threshold or bit_threshold)()
    need = (cge > kf) & (v != INT_MIN) & query_ok

    def bit_tie_limit(v, want):
        def body(it, x):
            cand = x | jnp.left_shift(jnp.int32(1), POS_BITS - 1 - it)
            cnt = count(lambda kt, pos: (kt == v) & (pos < cand))
            return jnp.where(cnt < want, cand, x)
        return lax.fori_loop(0, POS_BITS, body, jnp.zeros(stat_shape, I32))

    def tie_search():
        want = kf - count(lambda kt, pos: kt > v)
        limit = (tie_limit or bit_tie_limit)(v, want)
        return jnp.where(need, limit, INT_MAX)

    any_need = jnp.max(jnp.where(need, 1.0, 0.0)) > 0.0
    jlim = lax.cond(any_need, tie_search, lambda: jnp.full(stat_shape, INT_MAX, I32))
    return v, jlim


WIDE = 4 * LANES
PAIRS = N_HEADS // 2
COUNT_ROWS = 8 * SUBLANES


def _attn_prompt_kernel(far_ref, qidx_ref, kw_ref, q_ref, kidx_ref, k_ref, vt_ref, bd_ref, bs_ref, o_ref,
                        keys_ref, qit_ref, qz_ref, m_ref, l_ref, acc_ref, *, tq, nsel):
    i = pl.program_id(0)
    qpos = i * tq + lax.broadcasted_iota(I32, (1, tq), 1)
    nwide = i // (WIDE // LANES) + 1

    qi_t = jnp.transpose(qidx_ref[...].astype(F32)).astype(BF16)
    for hh in range(IDX_HEADS):
        qit_ref[:, hh * tq:(hh + 1) * tq] = qi_t[hh * IDX_DIM:(hh + 1) * IDX_DIM, :]
    w_t = jnp.transpose(kw_ref[...])[IDX_DIM:IDX_DIM + IDX_HEADS, :] * IDX_SCALE
    q_t = jnp.transpose(q_ref[...].astype(F32) * ATT_SCALE).astype(BF16)
    upper = lax.broadcasted_iota(I32, (LANES, tq), 0) < HEAD_DIM
    zero = jnp.zeros((LANES, tq), BF16)
    for p in range(PAIRS):
        blk = q_t[p * LANES:(p + 1) * LANES, :]
        qz_ref[p, :, 0:tq] = jnp.where(upper, blk, zero)
        qz_ref[p, :, tq:2 * tq] = jnp.where(upper, zero, blk)

    def score_body(c, carry):
        off = pl.multiple_of(c * WIDE, WIDE)
        kt = kidx_ref[pl.ds(off, WIDE), 0:IDX_DIM]
        sc = _dot(kt, qit_ref[...])
        s = jnp.zeros((WIDE, tq), F32)
        for hh in range(IDX_HEADS):
            s = s + jnp.maximum(sc[:, hh * tq:(hh + 1) * tq], 0.0) * w_t[hh:hh + 1, :]
        kpos = off + lax.broadcasted_iota(I32, (WIDE, tq), 0)
        keys_ref[pl.ds(off, WIDE), :] = jnp.where(kpos <= qpos, _mono_key(s), INT_MIN)
        return carry

    lax.fori_loop(0, nwide, score_body, 0)

    def count(pred):
        def body(c, acc):
            off = pl.multiple_of(c * WIDE, WIDE)
            kt = keys_ref[pl.ds(off, WIDE), :]
            kpos = off + lax.broadcasted_iota(I32, (WIDE, tq), 0)
            hit = jnp.where(pred(kt, kpos), 1.0, 0.0)
            return acc + jnp.sum(hit.reshape(WIDE // COUNT_ROWS, COUNT_ROWS, tq), axis=0)
        acc = lax.fori_loop(0, nwide, body, jnp.zeros((COUNT_ROWS, tq), F32))
        return jnp.sum(acc, axis=0, keepdims=True)


    def tie_limit(v, want):
        lower = (lax.broadcasted_iota(I32, (WIDE, WIDE), 1) <= lax.broadcasted_iota(I32, (WIDE, WIDE), 0))
        lower = jnp.where(lower, 1.0, 0.0).astype(BF16)

        def body(c, carry):
            base, limit = carry
            off = pl.multiple_of(c * WIDE, WIDE)
            eq = keys_ref[pl.ds(off, WIDE), :] == v
            rank = base + _dot(lower, jnp.where(eq, 1.0, 0.0).astype(BF16))
            kpos = (off + lax.broadcasted_iota(I32, (WIDE, tq), 0)).astype(F32)
            hit = jnp.max(jnp.where(eq & (rank <= want), kpos, -1.0), axis=0, keepdims=True)
            return rank[WIDE - 1:WIDE, :], jnp.maximum(limit, hit)

        zero = jnp.zeros((1, tq), F32)
        return lax.fori_loop(0, nwide, body, (zero, zero - 1.0))[1].astype(I32)

    thr, jlim = _select(count, (1, tq), nsel, True, tie_limit)

    m_ref[...] = jnp.full(m_ref.shape, M_INIT, F32)
    l_ref[...] = jnp.zeros(l_ref.shape, F32)
    acc_ref[...] = jnp.zeros(acc_ref.shape, F32)

    def tile(off, height, bias_ref):
        kt = keys_ref[pl.ds(off, height), :]
        kpos = off + lax.broadcasted_iota(I32, (height, tq), 0)
        sel = ((kt > thr) | ((kt == thr) & (kpos <= jlim))) & (kpos <= qpos)
        amask = jnp.where(sel, 0.0, NEG)
        def scores(p):
            return _dot(k_ref[pl.ds(off, height), p * LANES:(p + 1) * LANES], qz_ref[p])

        def softmax(p, st):
            out = []
            for e in range(2):
                h = 2 * p + e
                lg = st[:, e * tq:(e + 1) * tq] + amask
                if bias_ref is not None:
                    lg = lg + bias_ref[h]
                m_old = m_ref[h]
                m_new = jnp.maximum(m_old, jnp.max(lg, axis=0, keepdims=True))
                alpha = jnp.exp(m_old - m_new)
                pr = jnp.exp(lg - m_new[0:1, :])
                l_ref[h] = alpha * l_ref[h] + jnp.sum(pr, axis=0, keepdims=True)
                m_ref[h] = m_new
                out.append((h, pr.astype(BF16), alpha[0:1, :]))
            return out

        def values(items):
            for h, pr, alpha in items:
                pv = _dot(vt_ref[h * HEAD_DIM:(h + 1) * HEAD_DIM, pl.ds(off, height)], pr)
                acc_ref[h] = alpha * acc_ref[h] + pv

        sts = {0: scores(0), 1: scores(1)}
        done = None
        for p in range(PAIRS):
            cur = softmax(p, sts.pop(p))
            if p + 2 < PAIRS:
                sts[p + 2] = scores(p + 2)
            if done is not None:
                values(done)
            done = cur
        values(done)

    nfar = jnp.maximum(i - 1, 0)
    nfar_wide = nfar // (WIDE // LANES)

    nfar_xwide = nfar_wide // 2

    def far_xwide(c, carry):
        tile(pl.multiple_of(c * (2 * WIDE), 2 * WIDE), 2 * WIDE, None)
        return carry

    def far_wide(c, carry):
        tile(pl.multiple_of(c * WIDE, WIDE), WIDE, None)
        return carry

    def far_narrow(j, carry):
        tile(pl.multiple_of(j * LANES, LANES), LANES, None)
        return carry

    lax.fori_loop(0, nfar_xwide, far_xwide, 0)
    lax.fori_loop(2 * nfar_xwide, nfar_wide, far_wide, 0)
    lax.fori_loop(nfar_wide * (WIDE // LANES), nfar, far_narrow, 0)
    for h in range(N_HEADS):
        m_ref[h] = m_ref[h] + far_ref[h]

    @pl.when(i >= 1)
    def _():
        tile(pl.multiple_of((i - 1) * LANES, LANES), LANES, bs_ref)

    tile(pl.multiple_of(i * LANES, LANES), LANES, bd_ref)
    out_t = jnp.concatenate([acc_ref[h] / l_ref[h][0:1, :] for h in range(N_HEADS)], axis=0)
    o_ref[...] = jnp.transpose(out_t)


def _attn_prompt(far, h2, hb, vt, bias_diag_t, bias_sub_t, nsel):
    t = h2.shape[0]
    tq = LANES
    nq = IDX_HEADS * IDX_DIM
    return pl.pallas_call(
        functools.partial(_attn_prompt_kernel, tq=tq, nsel=nsel),
        grid=(t // tq,),
        in_specs=[pl.BlockSpec(memory_space=pltpu.SMEM),
                  pl.BlockSpec((tq, nq), lambda i: (i, C_QIDX // nq)),
                  pl.BlockSpec((tq, LANES), lambda i: (i, C_KW // LANES)),
                  pl.BlockSpec((tq, MIX_DIM), lambda i: (i, C_Q // MIX_DIM)),
                  pl.BlockSpec((t, LANES), lambda i: (0, C_KW // LANES)),
                  pl.BlockSpec((t, MIX_DIM), lambda i: (0, C_K // MIX_DIM)),
                  pl.BlockSpec((MIX_DIM, t), lambda i: (0, 0)),
                  pl.BlockSpec((N_HEADS, LANES, tq), lambda i: (0, 0, 0)),
                  pl.BlockSpec((N_HEADS, LANES, tq), lambda i: (0, 0, 0))],
        out_specs=pl.BlockSpec((tq, MIX_DIM), lambda i: (i, 0)),
        out_shape=jax.ShapeDtypeStruct((t, MIX_DIM), F32),
        scratch_shapes=[pltpu.VMEM((t, tq), I32),
                        pltpu.VMEM((IDX_DIM, IDX_HEADS * tq), BF16),
                        pltpu.VMEM((PAIRS, LANES, 2 * tq), BF16),
                        pltpu.VMEM((N_HEADS, SUBLANES, tq), F32),
                        pltpu.VMEM((N_HEADS, SUBLANES, tq), F32),
                        pltpu.VMEM((N_HEADS, HEAD_DIM, tq), F32)],
        compiler_params=_cparams(("parallel",)),
        name="attn_prompt",
    )(far, hb, h2, hb, hb, hb, vt, bias_diag_t, bias_sub_t)


QROWS = 8
SEL_PAGES = 64
ATT_PAGES = 32


SEL_ROWS = 8 * QROWS


def _score_sample_kernel(pt_ref, qst_ref, wcol_ref, *refs, treal):
    pages = refs[:SEL_PAGES]
    knew_ref, keys_ref = refs[SEL_PAGES:]
    j = pl.program_id(1)
    ntiles = keys_ref.shape[2] // LANES

    def score(kt_t):
        r = jnp.maximum(_dot(qst_ref[0], kt_t), 0.0) * (wcol_ref[0] * IDX_SCALE)
        s = r[0:QROWS]
        for hh in range(1, IDX_HEADS):
            s = s + r[hh * QROWS:(hh + 1) * QROWS]
        return _mono_key(s)

    for g in range(SEL_PAGES):
        off = pl.multiple_of((j * SEL_PAGES + g) * LANES, LANES)
        keys_ref[0, :, pl.ds(off, LANES)] = score(pages[g][0, 0].astype(BF16))

    @pl.when(j == pl.num_programs(1) - 1)
    def _():
        rowq = lax.broadcasted_iota(I32, (QROWS, LANES), 0)
        lane = lax.broadcasted_iota(I32, (QROWS, LANES), 1)
        knew = jnp.where((lane < treal) & (lane <= rowq), score(knew_ref[0]), INT_MIN)
        keys_ref[0, :, (ntiles - 1) * LANES:] = knew


def _select_sample_kernel(keys_ref, mask_ref, *, nsel, treal):
    rows, nk = keys_ref.shape
    ntiles = nk // LANES
    lane = lax.broadcasted_iota(I32, (rows, LANES), 1)

    def count(pred):
        acc = jnp.zeros((rows, LANES), F32)
        for jt in range(ntiles):
            kt = keys_ref[:, jt * LANES:(jt + 1) * LANES]
            acc = acc + jnp.where(pred(kt, jt * LANES + lane), 1.0, 0.0)
        return jnp.sum(acc, axis=1, keepdims=True)

    def tie_limit(v, want):
        upper = (lax.broadcasted_iota(I32, (LANES, LANES), 0) <= lax.broadcasted_iota(I32, (LANES, LANES), 1))
        upper = jnp.where(upper, 1.0, 0.0).astype(BF16)
        eqs = [keys_ref[:, jt * LANES:(jt + 1) * LANES] == v for jt in range(ntiles)]
        ranks = [_dot(jnp.where(eq, 1.0, 0.0).astype(BF16), upper) for eq in eqs]
        base = jnp.zeros((rows, 1), F32)
        best = jnp.full((rows, LANES), -1.0, F32)
        for jt in range(ntiles):
            ok = eqs[jt] & (base + ranks[jt] <= want)
            best = jnp.maximum(best, jnp.where(ok, (jt * LANES + lane).astype(F32), -1.0))
            base = base + ranks[jt][:, LANES - 1:LANES]
        return jnp.max(best, axis=1, keepdims=True).astype(I32)

    row_ok = lax.broadcasted_iota(I32, (rows, 1), 0) % QROWS < treal
    thr, jlim = _select(count, (rows, 1), nsel, row_ok, tie_limit)
    for jt in range(ntiles):
        kt = keys_ref[:, jt * LANES:(jt + 1) * LANES]
        sel = (kt > thr) | ((kt == thr) & (jt * LANES + lane <= jlim))
        mask_ref[:, jt * LANES:(jt + 1) * LANES] = jnp.where(sel, 1.0, 0.0)


def _sel_sample(page_table, qst, wcol, cache_kidx, knew, layer, nsel, treal):
    b, npages = page_table.shape
    nk = (npages + 1) * PAGE
    page_spec = lambda g: pl.BlockSpec(
        (1, 1, IDX_DIM, PAGE), lambda i, j, pt: (layer, pt[i, j * SEL_PAGES + g], 0, 0))
    grid_spec = pltpu.PrefetchScalarGridSpec(
        num_scalar_prefetch=1,
        grid=(b, npages // SEL_PAGES),
        in_specs=[pl.BlockSpec((1, IDX_HEADS * QROWS, IDX_DIM), lambda i, j, pt: (i, 0, 0)),
                  pl.BlockSpec((1, IDX_HEADS * QROWS, 1), lambda i, j, pt: (i, 0, 0))]
                 + [page_spec(g) for g in range(SEL_PAGES)]
                 + [pl.BlockSpec((1, IDX_DIM, PAGE), lambda i, j, pt: (i, 0, 0))],
        out_specs=pl.BlockSpec((1, QROWS, nk), lambda i, j, pt: (i, 0, 0)),
    )
    keys = pl.pallas_call(
        functools.partial(_score_sample_kernel, treal=treal),
        grid_spec=grid_spec,
        out_shape=jax.ShapeDtypeStruct((b, QROWS, nk), I32),
        compiler_params=_cparams(("parallel", "arbitrary")),
        name="score_sample",
    )(page_table, qst, wcol, *([cache_kidx] * SEL_PAGES), knew)
    rows = b * QROWS
    blk = math.gcd(rows, SEL_ROWS)
    mask = pl.pallas_call(
        functools.partial(_select_sample_kernel, nsel=nsel, treal=treal),
        grid=(rows // blk,),
        in_specs=[pl.BlockSpec((blk, nk), lambda i: (i, 0))],
        out_specs=pl.BlockSpec((blk, nk), lambda i: (i, 0)),
        out_shape=jax.ShapeDtypeStruct((rows, nk), F32),
        compiler_params=_cparams(("parallel",)),
        name="select_sample",
    )(keys.reshape(rows, nk))
    return mask.reshape(b, QROWS, nk)


def _attn_sample_kernel(pt_ref, far_ref, qh_ref, *refs):
    kp = refs[:ATT_PAGES]
    vp = refs[ATT_PAGES:2 * ATT_PAGES]
    mask_ref, masknew_ref, blast_ref, bnew_ref, knew_ref, vnew_ref, o_ref, m_ref, l_ref, acc_ref = refs[2 * ATT_PAGES:]
    j = pl.program_id(1)
    is_last = j == pl.num_programs(1) - 1

    @pl.when(j == 0)
    def _():
        m_ref[...] = jnp.full(m_ref.shape, M_INIT, F32)
        l_ref[...] = jnp.zeros(l_ref.shape, F32)
        acc_ref[...] = jnp.zeros(acc_ref.shape, F32)

    def update(lgs, vhs):
        prs, alphas = [], []
        for h in range(N_HEADS):
            m_old = m_ref[h]
            m_new = jnp.maximum(m_old, jnp.max(lgs[h], axis=1, keepdims=True))
            alpha = jnp.exp(m_old - m_new)
            pr = jnp.exp(lgs[h] - m_new[:, 0:1])
            l_ref[h] = alpha * l_ref[h] + jnp.sum(pr, axis=1, keepdims=True)
            m_ref[h] = m_new
            prs.append(pr.astype(BF16))
            alphas.append(alpha[:, :HEAD_DIM])
        for h in range(N_HEADS):
            acc_ref[h] = alphas[h] * acc_ref[h] + _dot_nt(prs[h], vhs[h])

    def q_of(h):
        return (qh_ref[0, h].astype(F32) * ATT_SCALE).astype(BF16)

    def head_t(refs_, h):
        return jnp.concatenate([refs_[g][0, 0, h] for g in range(ATT_PAGES)], axis=1).astype(BF16)

    amask = jnp.where(mask_ref[0] > 0.5, 0.0, NEG)
    lgs = []
    for h in range(N_HEADS):
        far = jnp.full((QROWS, LANES), far_ref[h], F32)
        bias = jnp.concatenate([far] * (ATT_PAGES - 1) + [jnp.where(is_last, blast_ref[h], far)], axis=1)
        lgs.append(_dot(q_of(h), head_t(kp, h)) + amask + bias)
    update(lgs, [head_t(vp, h) for h in range(N_HEADS)])

    @pl.when(is_last)
    def _():
        amask_new = jnp.where(masknew_ref[0] > 0.5, 0.0, NEG)
        lgs_new = [_dot(q_of(h), knew_ref[0, h]) + amask_new + bnew_ref[h] for h in range(N_HEADS)]
        update(lgs_new, [vnew_ref[0, h] for h in range(N_HEADS)])
        outs = []
        for h in range(N_HEADS):
            lh = l_ref[h][:, :HEAD_DIM]
            outs.append(acc_ref[h] / jnp.where(lh > 0.0, lh, 1.0))
        o_ref[0] = jnp.concatenate(outs, axis=1)


def _attn_sample(page_table, far, qh, cache_k2, cache_v2, mask, bias_last, bias_new, knew, vnew, layer):
    b, npages = page_table.shape
    nsteps = npages // ATT_PAGES
    page_spec = lambda g: pl.BlockSpec(
        (1, 1, N_HEADS, HEAD_DIM, PAGE), lambda i, j, pt, far: (layer, pt[i, j * ATT_PAGES + g], 0, 0, 0))
    const3 = lambda i, j, pt, far: (0, 0, 0)
    per_b4 = lambda i, j, pt, far: (i, 0, 0, 0)
    grid_spec = pltpu.PrefetchScalarGridSpec(
        num_scalar_prefetch=2,
        grid=(b, nsteps),
        in_specs=[pl.BlockSpec((1, N_HEADS, QROWS, HEAD_DIM), per_b4)]
                 + [page_spec(g) for g in range(ATT_PAGES)] * 2
                 + [pl.BlockSpec((1, QROWS, ATT_PAGES * PAGE), lambda i, j, pt, far: (i, 0, j)),
                    pl.BlockSpec((1, QROWS, PAGE), lambda i, j, pt, far: (i, 0, npages)),
                    pl.BlockSpec((N_HEADS, QROWS, PAGE), const3),
                    pl.BlockSpec((N_HEADS, QROWS, PAGE), const3),
                    pl.BlockSpec((1, N_HEADS, HEAD_DIM, PAGE), per_b4),
                    pl.BlockSpec((1, N_HEADS, HEAD_DIM, PAGE), per_b4)],
        out_specs=pl.BlockSpec((1, QROWS, MIX_DIM), lambda i, j, pt, far: (i, 0, 0)),
        scratch_shapes=[pltpu.VMEM((N_HEADS, QROWS, LANES), F32),
                        pltpu.VMEM((N_HEADS, QROWS, LANES), F32),
                        pltpu.VMEM((N_HEADS, QROWS, HEAD_DIM), F32)],
    )
    return pl.pallas_call(
        _attn_sample_kernel,
        grid_spec=grid_spec,
        out_shape=jax.ShapeDtypeStruct((b, QROWS, MIX_DIM), F32),
        compiler_params=_cparams(("parallel", "arbitrary")),
        name="attn_sample",
    )(page_table, far, qh, *([cache_k2] * ATT_PAGES), *([cache_v2] * ATT_PAGES), mask, mask,
      bias_last, bias_new, knew, vnew)


def _bucket_table(n):
    d = np.arange(n)
    nf = np.maximum(d, 1).astype(np.float32)
    half = REL_BUCKETS // 2
    large = half + (np.log(nf / np.float32(half)) / np.float32(math.log(REL_MAX_DIST / half))
                    * np.float32(REL_BUCKETS - half)).astype(np.int32)
    return np.where(d < half, d, np.minimum(large, REL_BUCKETS - 1)).astype(np.int32)


def _pack_w_in(w):
    w_t = jnp.transpose(w, (0, 2, 1))
    p, z, xbc, dt, q, k, v, qi, ki, wi, u, vs = jnp.split(w_t, np.cumsum(IN_SPLITS)[:-1].tolist(), axis=1)
    zeros = lambda n: jnp.zeros((w.shape[0], n, w.shape[1]), w.dtype)
    packed = jnp.concatenate(
        [p, z, xbc, q, k, v, u, vs, qi, ki, wi, zeros(LANES - IDX_DIM - IDX_HEADS), dt, zeros(LANES - N_HEADS)],
        axis=1)
    return packed.astype(BF16)


def _pad_lanes(x):
    return jnp.pad(x.reshape(1, -1), ((0, 0), (0, LANES - x.shape[-1])))


def _pad_axis(x, axis, size):
    pads = [(0, 0)] * x.ndim
    pads[axis] = (0, size - x.shape[axis])
    return jnp.pad(x, pads)


def kernel(x_prompt, x_sample, state_pool, state_conv, state_ssm, cache_k, cache_v, cache_kidx, page_table,
           rel_bias, norm_mix, w_in, pool_w, pool_scale, conv_w, conv_b, dt_bias, a_log, d_skip, ssm_norm,
           sgu_w, sgu_b, w_out, norm_mlp, mlp_w1, mlp_w2, norm_final):
    depth = w_in.shape[0]
    bp, t, _ = x_prompt.shape
    bs, ts, _ = x_sample.shape
    npages = page_table.shape[1]
    past = npages * PAGE
    n_pool = cache_k.shape[1]
    assert bp == 1 and t % WIDE == 0 and npages % SEL_PAGES == 0 and npages % ATT_PAGES == 0
    assert max(t, past + PAGE) <= 2 ** POS_BITS and ts <= QROWS
    tc = LANES
    nsel_p = min(TOPK_MAX, t // 4)
    nsel_s = min(TOPK_MAX, (past + ts) // 4)

    bucket = _bucket_table(2 * LANES)
    far = rel_bias[REL_BUCKETS - 1]
    kk = np.arange(LANES)[:, None]
    qq = np.arange(LANES)[None, :]
    def bias_of(dist):
        onehot = np.eye(REL_BUCKETS, dtype=np.float32)[bucket[dist].reshape(-1)]
        vals = jnp.dot(onehot, rel_bias, precision=lax.Precision.HIGHEST)
        return jnp.transpose(vals).reshape((N_HEADS,) + dist.shape)

    bias_diag_t = bias_of(np.maximum(qq - kk, 0))
    bias_sub_t = bias_of(np.minimum(LANES + qq - kk, 2 * LANES - 1))
    sq = np.arange(QROWS)[:, None]
    sk = np.arange(PAGE)[None, :]
    bias_last = bias_of(np.minimum(PAGE + sq - sk, 2 * LANES - 1))
    bias_new = bias_of(np.maximum(sq - sk, 0))

    cache_k2 = jnp.transpose(cache_k, (0, 1, 3, 4, 2))
    cache_v2 = jnp.transpose(cache_v, (0, 1, 3, 4, 2))
    cache_kidx2 = jnp.transpose(cache_kidx, (0, 1, 3, 2))
    w_in_t = _pack_w_in(w_in)
    w_out_b, w1_b, w2_b = w_out.astype(BF16), mlp_w1.astype(BF16), mlp_w2.astype(BF16)

    xp = x_prompt.reshape(t, D_MODEL)
    xs = x_sample.reshape(bs * ts, D_MODEL)
    zero_pool = jnp.zeros((1, POOL_HIST, MIX_DIM), F32)
    zero_conv = jnp.zeros((1, CONV_W - 1, XBC_DIM), F32)
    zero_ssm = jnp.zeros((1, N_HEADS, HEAD_DIM, SSM_STATE), F32)
    tm_p = min(1024, t)
    tm_s = bs * ts
    outs_p, outs_s = [], []

    for l in range(depth):
        g_mix = norm_mix[l].reshape(1, D_MODEL)
        g_mlp = norm_mlp[l].reshape(1, D_MODEL)
        g_out = norm_final.reshape(1, D_MODEL) if l == depth - 1 else g_mlp
        pscale = pool_scale[l].reshape(1, MIX_DIM)
        cb = conv_b[l].reshape(1, XBC_DIM)
        dtb, alog, dsk = _pad_lanes(dt_bias[l]), _pad_lanes(a_log[l]), _pad_lanes(d_skip[l])
        snorm = ssm_norm[l].reshape(1, MIX_DIM)
        sgu_b_col = sgu_b[l][:, :, None]
        final = l == depth - 1

        hp, hpb, vt = _inproj(xp, g_mix, w_in_t, l, tm=min(1024, t), tn=1024, attn_operands=True)
        hp3 = hp.reshape(1, t, H_PACKED)
        y_pool, new_pool = _pool(hp3, zero_pool, pool_w[l], pscale, tc=WIDE, treal=WIDE, pos0=0)
        y_ssm, new_conv, new_h = _ssd(hp3, zero_conv, zero_ssm, conv_w[l], cb, dtb, alog, dsk, snorm,
                                      tc=tc, treal=tc)
        y_sgu, = _sgu(hp3, sgu_w[l], sgu_b_col, tc=tc, treal=tc, nsub=WIDE // tc)
        kf = hp[:, C_K:C_K + MIX_DIM]
        vf = hp[:, C_V:C_V + MIX_DIM]
        kif = hp[:, C_KW:C_KW + IDX_DIM]
        y_att = _attn_prompt(far, hp, hpb, vt, bias_diag_t, bias_sub_t, nsel_p)
        x1, xn = _outproj(xp, [y_pool[0], y_ssm[0], y_att, y_sgu[0]], w_out_b, l, g_mlp, tm=min(512, t))
        xp = _mlp(xn, w1_b, w2_b, l, x1, g_out, tm=tm_p, final_norm=final)
        outs_p.append((new_pool, new_conv, new_h, kf.reshape(1, t, N_HEADS, HEAD_DIM),
                       vf.reshape(1, t, N_HEADS, HEAD_DIM), kif.reshape(1, t, IDX_DIM)))

        hs, = _inproj(xs, g_mix, w_in_t, l, tm=tm_s)
        hs3 = hs.reshape(bs, ts, H_PACKED)
        y_pool, new_pool = _pool(hs3, state_pool[l], pool_w[l], pscale, tc=tc, treal=ts, pos0=past)
        y_ssm, new_conv, new_h = _ssd(hs3, state_conv[l], state_ssm[l], conv_w[l], cb, dtb, alog, dsk, snorm,
                                      tc=tc, treal=ts)
        y_sgu, v_rows = _sgu(hs3, sgu_w[l], sgu_b_col, tc=tc, treal=ts, emit_v=True)
        kf = hs3[:, :, C_K:C_K + MIX_DIM]
        vf = hs3[:, :, C_V:C_V + MIX_DIM]
        kif = hs3[:, :, C_KW:C_KW + IDX_DIM]
        qi = hs3[:, :, C_QIDX:C_QIDX + IDX_HEADS * IDX_DIM].reshape(bs, ts, IDX_HEADS, IDX_DIM)
        qst = _pad_axis(jnp.transpose(qi, (0, 2, 1, 3)), 2, QROWS)
        qst = qst.reshape(bs, IDX_HEADS * QROWS, IDX_DIM).astype(BF16)
        wi = hs3[:, :, C_KW + IDX_DIM:C_KW + IDX_DIM + IDX_HEADS]
        wcol = _pad_axis(jnp.transpose(wi, (0, 2, 1)), 2, QROWS).reshape(bs, IDX_HEADS * QROWS, 1)
        kinew_t = _pad_axis(jnp.transpose(kif, (0, 2, 1)), 2, PAGE).astype(BF16)
        mask = _sel_sample(page_table, qst, wcol, cache_kidx2, kinew_t, l, nsel_s, ts)
        heads = lambda x, perm: jnp.transpose(x.reshape(bs, ts, N_HEADS, HEAD_DIM), perm)
        qh = _pad_axis(heads(hs3[:, :, C_Q:C_Q + MIX_DIM], (0, 2, 1, 3)), 2, QROWS).astype(BF16)
        knew = _pad_axis(heads(kf, (0, 2, 3, 1)), 3, PAGE).astype(BF16)
        vnew = _pad_axis(heads(vf, (0, 2, 3, 1)), 3, PAGE).astype(BF16)
        att = _attn_sample(page_table, far, qh, cache_k2, cache_v2, mask, bias_last, bias_new, knew, vnew, l)
        y_att = att[:, :ts].reshape(bs * ts, MIX_DIM)
        flat = lambda y: y.reshape(bs * ts, MIX_DIM)
        x1, xn = _outproj(xs, [flat(y_pool), flat(y_ssm), y_att, flat(y_sgu)], w_out_b, l, g_mlp, tm=tm_s)
        xs = _mlp(xn, w1_b, w2_b, l, x1, g_out, tm=tm_s, final_norm=final)
        outs_s.append((new_pool, new_conv, new_h, kf.reshape(bs, ts, N_HEADS, HEAD_DIM),
                       vf.reshape(bs, ts, N_HEADS, HEAD_DIM), kif, v_rows))

    stack = lambda outs, i: jnp.stack([o[i] for o in outs])
    return (xp.reshape(1, t, D_MODEL), xs.reshape(bs, ts, D_MODEL),
            *[stack(outs_p, i) for i in range(6)],
            *[stack(outs_s, i) for i in range(7)])
```

```python
import functools
import math

import numpy as np
import jax
import jax.numpy as jnp
from jax import lax
from jax.experimental import pallas as pl
from jax.experimental.pallas import tpu as pltpu

F32 = jnp.float32
BF16 = jnp.bfloat16
I32 = jnp.int32

LANES = 128
SUBLANES = 8
D_MODEL = 2048
N_HEADS = 8
HEAD_DIM = 64
MIX_DIM = 512
POOL_WINDOWS = (2, 4, 8, 16)
POOL_HIST = 15
CONV_W = 4
SSM_GROUPS = 2
SSM_STATE = 128
XBC_DIM = MIX_DIM + 2 * SSM_GROUPS * SSM_STATE
IDX_HEADS = 4
IDX_DIM = 64
TOPK_MAX = 256
PAGE = 128
REL_BUCKETS = 32
REL_MAX_DIST = 128
ATT_SCALE = HEAD_DIM ** -0.5
IDX_SCALE = (IDX_HEADS * IDX_DIM) ** -0.5
SGU_GROUPS = 4
D_FF = 4 * D_MODEL
EPS = 1e-6
NEG = -1e30
INT_MIN = -(2 ** 31)
INT_MAX = 2 ** 31 - 1
IN_SPLITS = (512, 512, 1024, 8, 512, 512, 512, 256, 64, 4, 512, 512)

C_P, C_Z, C_XBC, C_Q, C_K, C_V, C_U, C_VS, C_QIDX, C_KW, C_DT = (
    0, 512, 1024, 2048, 2560, 3072, 3584, 4096, 4608, 4864, 4992)
H_PACKED = 5120
VMEM_LIMIT = 56 * 1024 * 1024


def _cparams(sem):
    return pltpu.CompilerParams(dimension_semantics=sem, vmem_limit_bytes=VMEM_LIMIT)


def _dot(a, b):
    return jnp.dot(a, b, preferred_element_type=F32)


def _dot_nt(a, b):
    return lax.dot_general(a, b, (((1,), (1,)), ((), ())), preferred_element_type=F32)


def _dot_tn(a, b):
    return lax.dot_general(a, b, (((0,), (0,)), ((), ())), preferred_element_type=F32)


def _silu(x):
    return x * (1.0 / (1.0 + jnp.exp(-x)))


def _gelu_tanh(x):
    return 0.5 * x * (1.0 + jnp.tanh(math.sqrt(2.0 / math.pi) * (x + 0.044715 * (x * x * x))))


def _softplus(x):
    return jnp.maximum(x, 0.0) + jnp.log1p(jnp.exp(-jnp.abs(x)))


def _rms(x, g):
    ms = jnp.mean(x * x, axis=-1, keepdims=True)
    return x * lax.rsqrt(ms + EPS) * g


def _inproj_kernel(x_ref, g_ref, w_ref, o_ref, *rest, attn_operands, tn):
    xn_ref = rest[-1]

    @pl.when(pl.program_id(1) == 0)
    def _():
        xn_ref[...] = _rms(x_ref[...], g_ref[...]).astype(BF16)

    h = _dot_nt(xn_ref[...], w_ref[0])
    o_ref[...] = h
    if attn_operands:
        ob_ref, vt_ref = rest[:2]
        ob_ref[...] = h.astype(BF16)

        @pl.when(pl.program_id(1) == C_V // tn)
        def _():
            vt_ref[...] = jnp.transpose(h[:, C_V % tn:C_V % tn + MIX_DIM]).astype(BF16)


def _inproj(x, g, w_t, layer, tm, tn=512, attn_operands=False):
    m = x.shape[0]
    n = w_t.shape[1]
    out_specs = [pl.BlockSpec((tm, tn), lambda i, j: (i, j))]
    out_shape = [jax.ShapeDtypeStruct((m, n), F32)]
    if attn_operands:
        out_specs += [pl.BlockSpec((tm, tn), lambda i, j: (i, j)), pl.BlockSpec((MIX_DIM, tm), lambda i, j: (0, i))]
        out_shape += [jax.ShapeDtypeStruct((m, n), BF16), jax.ShapeDtypeStruct((MIX_DIM, m), BF16)]
    return pl.pallas_call(
        functools.partial(_inproj_kernel, attn_operands=attn_operands, tn=tn),
        grid=(m // tm, n // tn),
        in_specs=[pl.BlockSpec((tm, D_MODEL), lambda i, j: (i, 0)),
                  pl.BlockSpec((1, D_MODEL), lambda i, j: (0, 0)),
                  pl.BlockSpec((1, tn, D_MODEL), lambda i, j: (layer, j, 0))],
        out_specs=out_specs,
        out_shape=out_shape,
        scratch_shapes=[pltpu.VMEM((tm, D_MODEL), BF16)],
        compiler_params=_cparams(("parallel", "arbitrary")),
        name="inproj",
    )(x, g, w_t)


def _outproj_kernel(x_ref, y0_ref, y1_ref, y2_ref, y3_ref, w_ref, g_ref, x1_ref, xn_ref):
    y = jnp.concatenate([y0_ref[...], y1_ref[...], y2_ref[...], y3_ref[...]], axis=1).astype(BF16)
    x1 = x_ref[...] + _dot(y, w_ref[0])
    x1_ref[...] = x1
    xn_ref[...] = _rms(x1, g_ref[...]).astype(BF16)


def _outproj(x, ys, w, layer, g, tm):
    m = x.shape[0]
    row = lambda i: (i, 0)
    return pl.pallas_call(
        _outproj_kernel,
        grid=(m // tm,),
        in_specs=[pl.BlockSpec((tm, D_MODEL), row)]
                 + [pl.BlockSpec((tm, MIX_DIM), row)] * 4
                 + [pl.BlockSpec((1, D_MODEL, D_MODEL), lambda i: (layer, 0, 0)),
                    pl.BlockSpec((1, D_MODEL), lambda i: (0, 0))],
        out_specs=[pl.BlockSpec((tm, D_MODEL), row), pl.BlockSpec((tm, D_MODEL), row)],
        out_shape=[jax.ShapeDtypeStruct((m, D_MODEL), F32), jax.ShapeDtypeStruct((m, D_MODEL), BF16)],
        compiler_params=_cparams(("parallel",)),
        name="outproj",
    )(x, *ys, w, g)


def _mlp_kernel(xn_ref, w1_ref, w2_ref, x1_ref, g_ref, o_ref, *, final_norm):
    j = pl.program_id(1)

    @pl.when(j == 0)
    def _():
        o_ref[...] = x1_ref[...]

    hid = _dot(xn_ref[...], w1_ref[0])
    hid = jnp.square(jnp.maximum(hid, 0.0)).astype(BF16)
    o_ref[...] += _dot(hid, w2_ref[0])

    if final_norm:
        @pl.when(j == pl.num_programs(1) - 1)
        def _():
            o_ref[...] = _rms(o_ref[...], g_ref[...])


def _mlp(xn, w1, w2, layer, x1, g, tm, final_norm, tf=512):
    m = xn.shape[0]
    return pl.pallas_call(
        functools.partial(_mlp_kernel, final_norm=final_norm),
        grid=(m // tm, D_FF // tf),
        in_specs=[pl.BlockSpec((tm, D_MODEL), lambda i, j: (i, 0)),
                  pl.BlockSpec((1, D_MODEL, tf), lambda i, j: (layer, 0, j)),
                  pl.BlockSpec((1, tf, D_MODEL), lambda i, j: (layer, j, 0)),
                  pl.BlockSpec((tm, D_MODEL), lambda i, j: (i, 0)),
                  pl.BlockSpec((1, D_MODEL), lambda i, j: (0, 0))],
        out_specs=pl.BlockSpec((tm, D_MODEL), lambda i, j: (i, 0)),
        out_shape=jax.ShapeDtypeStruct((m, D_MODEL), F32),
        compiler_params=_cparams(("parallel", "arbitrary")),
        name="mlp",
    )(xn, w1, w2, x1, g)


def _pool_kernel(p_ref, hist_ref, w_ref, scale_ref, y_ref, newhist_ref, ext_ref, *, tc, treal, pos0):
    c = pl.program_id(1)
    hrows = POOL_HIST + 1

    @pl.when(c == 0)
    def _():
        ext_ref[0:1, :] = jnp.zeros((1, MIX_DIM), F32)
        ext_ref[1:hrows, :] = hist_ref[0]

    if treal < tc:
        ext_ref[hrows:hrows + tc, :] = jnp.zeros((tc, MIX_DIM), F32)
    ext_ref[hrows:hrows + treal, :] = p_ref[0]

    pos = pos0 + c * treal + lax.broadcasted_iota(I32, (tc, 1), 0)
    for g, win in enumerate(POOL_WINDOWS):
        cols = slice(g * LANES, (g + 1) * LANES)
        cur = ext_ref[hrows:hrows + tc, cols]
        s = cur
        for j in range(1, win):
            s = s + ext_ref[hrows - j:hrows - j + tc, cols]
        cnt = jnp.minimum(pos + 1, win).astype(F32)
        diff = s / cnt - cur
        y = _dot(diff, w_ref[g]) * scale_ref[:, cols]
        y_ref[0, :, cols] = y[:treal]

    newhist_ref[0] = ext_ref[treal + 1:treal + hrows, :]
    ext_ref[0:hrows, :] = ext_ref[treal:treal + hrows, :]


def _pool(h3, hist, pool_w, pool_scale, *, tc, treal, pos0):
    b, t, _ = h3.shape
    return pl.pallas_call(
        functools.partial(_pool_kernel, tc=tc, treal=treal, pos0=pos0),
        grid=(b, t // treal),
        in_specs=[pl.BlockSpec((1, treal, MIX_DIM), lambda i, c: (i, c, C_P // MIX_DIM)),
                  pl.BlockSpec((1, POOL_HIST, MIX_DIM), lambda i, c: (i, 0, 0)),
                  pl.BlockSpec((4, LANES, LANES), lambda i, c: (0, 0, 0)),
                  pl.BlockSpec((1, MIX_DIM), lambda i, c: (0, 0))],
        out_specs=[pl.BlockSpec((1, treal, MIX_DIM), lambda i, c: (i, c, 0)),
                   pl.BlockSpec((1, POOL_HIST, MIX_DIM), lambda i, c: (i, 0, 0))],
        out_shape=[jax.ShapeDtypeStruct((b, t, MIX_DIM), F32),
                   jax.ShapeDtypeStruct((b, POOL_HIST, MIX_DIM), F32)],
        scratch_shapes=[pltpu.VMEM((POOL_HIST + 1 + tc, MIX_DIM), F32)],
        compiler_params=_cparams(("parallel", "arbitrary")),
        name="pool",
    )(h3, hist, pool_w, pool_scale)


def _sgu_kernel(u_ref, v_ref, w_ref, b_ref, y_ref, *rest, tc, treal, nsub, emit_v):
    ubuf_ref, vbuf_ref = rest[-2:]
    r = lax.broadcasted_iota(I32, (tc, tc), 0)
    s = lax.broadcasted_iota(I32, (tc, tc), 1)
    for sub in range(nsub):
        rows = slice(sub * treal, (sub + 1) * treal)
        if treal < tc:
            ubuf_ref[...] = jnp.zeros((tc, MIX_DIM), F32)
            vbuf_ref[...] = jnp.zeros((tc, MIX_DIM), F32)
        ubuf_ref[0:treal, :] = u_ref[0, rows, :]
        vbuf_ref[0:treal, :] = v_ref[0, rows, :]
        u = _gelu_tanh(ubuf_ref[...])
        v = _gelu_tanh(vbuf_ref[...])
        if emit_v:
            rest[0][0, rows, :] = v[:treal]
        for g in range(SGU_GROUPS):
            cols = slice(g * LANES, (g + 1) * LANES)
            w = jnp.where(s <= r, w_ref[g], 0.0)
            mixed = _dot(w, v[:, cols]) + b_ref[g]
            y_ref[0, rows, cols] = (u[:, cols] * mixed)[:treal]


def _sgu(h3, sgu_w, sgu_b_col, *, tc, treal, nsub=1, emit_v=False):
    b, t, _ = h3.shape
    rows = nsub * treal
    out_specs = [pl.BlockSpec((1, rows, MIX_DIM), lambda i, c: (i, c, 0))]
    out_shape = [jax.ShapeDtypeStruct((b, t, MIX_DIM), F32)]
    return pl.pallas_call(
        functools.partial(_sgu_kernel, tc=tc, treal=treal, nsub=nsub, emit_v=emit_v),
        grid=(b, t // rows),
        in_specs=[pl.BlockSpec((1, rows, MIX_DIM), lambda i, c: (i, c, C_U // MIX_DIM)),
                  pl.BlockSpec((1, rows, MIX_DIM), lambda i, c: (i, c, C_VS // MIX_DIM)),
                  pl.BlockSpec((SGU_GROUPS, tc, tc), lambda i, c: (0, 0, 0)),
                  pl.BlockSpec((SGU_GROUPS, tc, 1), lambda i, c: (0, 0, 0))],
        out_specs=out_specs * (2 if emit_v else 1),
        out_shape=out_shape * (2 if emit_v else 1),
        scratch_shapes=[pltpu.VMEM((tc, MIX_DIM), F32), pltpu.VMEM((tc, MIX_DIM), F32)],
        compiler_params=_cparams(("parallel", "parallel")),
        name="sgu",
    )(h3, h3, sgu_w, sgu_b_col)


def _pool_sgu_kernel(p_ref, hist_ref, pw_ref, ps_ref, u_ref, v_ref, sw_ref, sb_ref,
                     ypool_ref, newhist_ref, ysgu_ref, ext_ref, ubuf_ref, vbuf_ref, *, rows, chunk):
    _pool_kernel(p_ref, hist_ref, pw_ref, ps_ref, ypool_ref, newhist_ref, ext_ref, tc=rows, treal=rows, pos0=0)
    _sgu_kernel(u_ref, v_ref, sw_ref, sb_ref, ysgu_ref, ubuf_ref, vbuf_ref,
                tc=chunk, treal=chunk, nsub=rows // chunk, emit_v=False)


def _pool_sgu(h3, hist, pool_w, pool_scale, sgu_w, sgu_b_col, *, rows, chunk):
    b, t, _ = h3.shape
    blk = lambda col: pl.BlockSpec((1, rows, MIX_DIM), lambda i, c: (i, c, col // MIX_DIM))
    const = lambda shape: pl.BlockSpec(shape, lambda i, c: (0,) * len(shape))
    hist_spec = pl.BlockSpec((1, POOL_HIST, MIX_DIM), lambda i, c: (i, 0, 0))
    return pl.pallas_call(
        functools.partial(_pool_sgu_kernel, rows=rows, chunk=chunk),
        grid=(b, t // rows),
        in_specs=[blk(C_P), hist_spec, const((4, LANES, LANES)), const((1, MIX_DIM)),
                  blk(C_U), blk(C_VS), const((SGU_GROUPS, chunk, chunk)), const((SGU_GROUPS, chunk, 1))],
        out_specs=[blk(0), hist_spec, blk(0)],
        out_shape=[jax.ShapeDtypeStruct((b, t, MIX_DIM), F32),
                   jax.ShapeDtypeStruct((b, POOL_HIST, MIX_DIM), F32),
                   jax.ShapeDtypeStruct((b, t, MIX_DIM), F32)],
        scratch_shapes=[pltpu.VMEM((POOL_HIST + 1 + rows, MIX_DIM), F32),
                        pltpu.VMEM((chunk, MIX_DIM), F32), pltpu.VMEM((chunk, MIX_DIM), F32)],
        compiler_params=_cparams(("parallel", "arbitrary")),
        name="pool_sgu",
    )(h3, hist, pool_w, pool_scale, h3, h3, sgu_w, sgu_b_col)


def _cumsum_rows(a, n):
    row = lax.broadcasted_iota(I32, a.shape, 0)
    sh = 1
    while sh < n:
        a = a + jnp.where(row >= sh, pltpu.roll(a, sh, 0), 0.0)
        sh *= 2
    return a


def _ssd_kernel(z_ref, xbc_ref, dt_ref, hc_ref, h0_ref, convw_ref, convb_ref, dtb_ref, alog_ref, dskip_ref,
                norm_ref, y_ref, newconv_ref, newh_ref, ext_ref, zbuf_ref, dtbuf_ref, state_ref, *, tc, treal):
    c = pl.program_id(1)
    pre = SUBLANES
    hist = CONV_W - 1

    @pl.when(c == 0)
    def _():
        ext_ref[0:pre - hist, :] = jnp.zeros((pre - hist, XBC_DIM), F32)
        ext_ref[pre - hist:pre, :] = hc_ref[0]
        state_ref[...] = h0_ref[...]

    if treal < tc:
        ext_ref[pre:pre + tc, :] = jnp.zeros((tc, XBC_DIM), F32)
        zbuf_ref[...] = jnp.zeros((tc, MIX_DIM), F32)
        dtbuf_ref[...] = jnp.zeros((tc, LANES), F32)
    ext_ref[pre:pre + treal, :] = xbc_ref[0]
    zbuf_ref[0:treal, :] = z_ref[0]
    dtbuf_ref[0:treal, :] = dt_ref[0]

    conv = convb_ref[...]
    for j in range(CONV_W):
        conv = conv + ext_ref[pre - hist + j:pre - hist + j + tc, :] * convw_ref[j:j + 1, :]
    conv = _silu(conv)
    xs = conv[:, :MIX_DIM]
    bmat = conv[:, MIX_DIM:MIX_DIM + SSM_GROUPS * SSM_STATE]
    cmat = conv[:, MIX_DIM + SSM_GROUPS * SSM_STATE:]

    rowi = lax.broadcasted_iota(I32, (tc, LANES), 0)
    dt = _softplus(dtbuf_ref[...] + dtb_ref[...])
    if treal < tc:
        dt = jnp.where(rowi < treal, dt, 0.0)
    a = dt * (-jnp.exp(alog_ref[...]))
    acum = _cumsum_rows(a, tc)
    acum_t = jnp.transpose(acum)
    total = acum[tc - 1:tc, :]
    causal = lax.broadcasted_iota(I32, (tc, tc), 1) <= lax.broadcasted_iota(I32, (tc, tc), 0)

    ys = []
    heads_per_group = N_HEADS // SSM_GROUPS
    for g in range(SSM_GROUPS):
        bg = bmat[:, g * SSM_STATE:(g + 1) * SSM_STATE]
        cg = cmat[:, g * SSM_STATE:(g + 1) * SSM_STATE]
        cb = _dot_nt(cg, bg)
        for hh in range(heads_per_group):
            h = g * heads_per_group + hh
            col = acum[:, h:h + 1]
            row = acum_t[h:h + 1, :]
            decay = jnp.exp(jnp.where(causal, col - row, NEG))
            xh = xs[:, h * HEAD_DIM:(h + 1) * HEAD_DIM]
            xdt = xh * dt[:, h:h + 1]
            hprev = state_ref[0, h]
            y = _dot(cb * decay, xdt)
            y = y + _dot_nt(cg, hprev) * jnp.exp(col)
            y = y + dskip_ref[:, h:h + 1] * xh
            ys.append(y)
            tot = total[:, h:h + 1]
            st = _dot_tn(xdt * jnp.exp(tot - col), bg)
            state_ref[0, h] = jnp.exp(tot) * hprev + st
    y = jnp.concatenate(ys, axis=1) * _silu(zbuf_ref[...])
    gw = MIX_DIM // SSM_GROUPS
    outs = []
    for g in range(SSM_GROUPS):
        yg = y[:, g * gw:(g + 1) * gw]
        ms = jnp.mean(yg * yg, axis=-1, keepdims=True)
        outs.append(yg * lax.rsqrt(ms + EPS) * norm_ref[:, g * gw:(g + 1) * gw])
    y_ref[0] = jnp.concatenate(outs, axis=1)[:treal]

    newconv_ref[0] = ext_ref[pre + treal - hist:pre + treal, :]
    ext_ref[0:pre, :] = ext_ref[treal:treal + pre, :]
    newh_ref[...] = state_ref[...]


def _ssd(h3, hist_conv, h0, conv_w, conv_b, dtb, alog, dskip, norm, *, tc, treal):
    b, t, _ = h3.shape
    full2 = lambda i, c: (0, 0)
    return pl.pallas_call(
        functools.partial(_ssd_kernel, tc=tc, treal=treal),
        grid=(b, t // treal),
        in_specs=[pl.BlockSpec((1, treal, MIX_DIM), lambda i, c: (i, c, C_Z // MIX_DIM)),
                  pl.BlockSpec((1, treal, XBC_DIM), lambda i, c: (i, c, C_XBC // XBC_DIM)),
                  pl.BlockSpec((1, treal, LANES), lambda i, c: (i, c, C_DT // LANES)),
                  pl.BlockSpec((1, CONV_W - 1, XBC_DIM), lambda i, c: (i, 0, 0)),
                  pl.BlockSpec((1, N_HEADS, HEAD_DIM, SSM_STATE), lambda i, c: (i, 0, 0, 0)),
                  pl.BlockSpec((CONV_W, XBC_DIM), full2),
                  pl.BlockSpec((1, XBC_DIM), full2),
                  pl.BlockSpec((1, LANES), full2),
                  pl.BlockSpec((1, LANES), full2),
                  pl.BlockSpec((1, LANES), full2),
                  pl.BlockSpec((1, MIX_DIM), full2)],
        out_specs=[pl.BlockSpec((1, treal, MIX_DIM), lambda i, c: (i, c, 0)),
                   pl.BlockSpec((1, CONV_W - 1, XBC_DIM), lambda i, c: (i, 0, 0)),
                   pl.BlockSpec((1, N_HEADS, HEAD_DIM, SSM_STATE), lambda i, c: (i, 0, 0, 0))],
        out_shape=[jax.ShapeDtypeStruct((b, t, MIX_DIM), F32),
                   jax.ShapeDtypeStruct((b, CONV_W - 1, XBC_DIM), F32),
                   jax.ShapeDtypeStruct((b, N_HEADS, HEAD_DIM, SSM_STATE), F32)],
        scratch_shapes=[pltpu.VMEM((SUBLANES + tc, XBC_DIM), F32),
                        pltpu.VMEM((tc, MIX_DIM), F32),
                        pltpu.VMEM((tc, LANES), F32),
                        pltpu.VMEM((1, N_HEADS, HEAD_DIM, SSM_STATE), F32)],
        compiler_params=_cparams(("parallel", "arbitrary")),
        name="ssd",
    )(h3, h3, h3, hist_conv, h0, conv_w, conv_b, dtb, alog, dskip, norm)


M_INIT = 0.5 * NEG
POS_BITS = 14
assert math.frexp(ATT_SCALE)[0] == 0.5, "q is pre-scaled in bf16, exact only for a power-of-two scale"


def _mono_key(s):
    s = jnp.where(s == 0.0, 0.0, s)
    b = lax.bitcast_convert_type(s, I32)
    return b ^ ((b >> 31) & INT_MAX)


def _select(count, stat_shape, nsel, query_ok, tie_limit=None, threshold=None):
    kf = float(nsel)

    def bit_threshold():
        c0 = count(lambda kt, pos: kt >= 0)
        nonneg = c0 >= kf
        prefix = jnp.where(nonneg, 0, INT_MIN).astype(I32)

        def bit_body(it, carry):
            prefix, cge = carry
            cand = prefix | jnp.left_shift(jnp.int32(1), 30 - it)
            cnt = count(lambda kt, pos: kt >= cand)
            take = cnt >= kf
            return jnp.where(take, cand, prefix), jnp.where(take, cnt, cge)

        return lax.fori_loop(0, 31, bit_body, (prefix, jnp.where(nonneg, c0, kf)))

    v, cge = (threshold or bit_threshold)()
    need = (cge > kf) & (v != INT_MIN) & query_ok

    def bit_tie_limit(v, want):
        def body(it, x):
            cand = x | jnp.left_shift(jnp.int32(1), POS_BITS - 1 - it)
            cnt = count(lambda kt, pos: (kt == v) & (pos < cand))
            return jnp.where(cnt < want, cand, x)
        return lax.fori_loop(0, POS_BITS, body, jnp.zeros(stat_shape, I32))

    def tie_search():
        want = kf - count(lambda kt, pos: kt > v)
        limit = (tie_limit or bit_tie_limit)(v, want)
        return jnp.where(need, limit, INT_MAX)

    any_need = jnp.max(jnp.where(need, 1.0, 0.0)) > 0.0
    jlim = lax.cond(any_need, tie_search, lambda: jnp.full(stat_shape, INT_MAX, I32))
    return v, jlim


WIDE = 4 * LANES
PAIRS = N_HEADS // 2
COUNT_ROWS = 8 * SUBLANES


def _attn_prompt_kernel(far_ref, qidx_ref, kw_ref, q_ref, kidx_ref, k_ref, vt_ref, bd_ref, bs_ref, o_ref,
                        keys_ref, qit_ref, qz_ref, m_ref, l_ref, acc_ref, *, tq, nsel):
    i = pl.program_id(0)
    qpos = i * tq + lax.broadcasted_iota(I32, (1, tq), 1)
    nwide = i // (WIDE // LANES) + 1

    qi_t = jnp.transpose(qidx_ref[...].astype(F32)).astype(BF16)
    for hh in range(IDX_HEADS):
        qit_ref[:, hh * tq:(hh + 1) * tq] = qi_t[hh * IDX_DIM:(hh + 1) * IDX_DIM, :]
    w_t = jnp.transpose(kw_ref[...])[IDX_DIM:IDX_DIM + IDX_HEADS, :] * IDX_SCALE
    q_t = jnp.transpose(q_ref[...].astype(F32) * ATT_SCALE).astype(BF16)
    upper = lax.broadcasted_iota(I32, (LANES, tq), 0) < HEAD_DIM
    zero = jnp.zeros((LANES, tq), BF16)
    for p in range(PAIRS):
        blk = q_t[p * LANES:(p + 1) * LANES, :]
        qz_ref[p, :, 0:tq] = jnp.where(upper, blk, zero)
        qz_ref[p, :, tq:2 * tq] = jnp.where(upper, zero, blk)

    def score_body(c, carry):
        off = pl.multiple_of(c * WIDE, WIDE)
        kt = kidx_ref[pl.ds(off, WIDE), 0:IDX_DIM]
        sc = _dot(kt, qit_ref[...])
        s = jnp.zeros((WIDE, tq), F32)
        for hh in range(IDX_HEADS):
            s = s + jnp.maximum(sc[:, hh * tq:(hh + 1) * tq], 0.0) * w_t[hh:hh + 1, :]
        kpos = off + lax.broadcasted_iota(I32, (WIDE, tq), 0)
        keys_ref[pl.ds(off, WIDE), :] = jnp.where(kpos <= qpos, _mono_key(s), INT_MIN)
        return carry

    lax.fori_loop(0, nwide, score_body, 0)

    def count(pred):
        def body(c, acc):
            off = pl.multiple_of(c * WIDE, WIDE)
            kt = keys_ref[pl.ds(off, WIDE), :]
            kpos = off + lax.broadcasted_iota(I32, (WIDE, tq), 0)
            hit = jnp.where(pred(kt, kpos), 1.0, 0.0)
            return acc + jnp.sum(hit.reshape(WIDE // COUNT_ROWS, COUNT_ROWS, tq), axis=0)
        acc = lax.fori_loop(0, nwide, body, jnp.zeros((COUNT_ROWS, tq), F32))
        return jnp.sum(acc, axis=0, keepdims=True)


    def tie_limit(v, want):
        lower = (lax.broadcasted_iota(I32, (WIDE, WIDE), 1) <= lax.broadcasted_iota(I32, (WIDE, WIDE), 0))
        lower = jnp.where(lower, 1.0, 0.0).astype(BF16)

        def body(c, carry):
            base, limit = carry
            off = pl.multiple_of(c * WIDE, WIDE)
            eq = keys_ref[pl.ds(off, WIDE), :] == v
            rank = base + _dot(lower, jnp.where(eq, 1.0, 0.0).astype(BF16))
            kpos = (off + lax.broadcasted_iota(I32, (WIDE, tq), 0)).astype(F32)
            hit = jnp.max(jnp.where(eq & (rank <= want), kpos, -1.0), axis=0, keepdims=True)
            return rank[WIDE - 1:WIDE, :], jnp.maximum(limit, hit)

        zero = jnp.zeros((1, tq), F32)
        return lax.fori_loop(0, nwide, body, (zero, zero - 1.0))[1].astype(I32)

    thr, jlim = _select(count, (1, tq), nsel, True, tie_limit)

    m_ref[...] = jnp.full(m_ref.shape, M_INIT, F32)
    l_ref[...] = jnp.zeros(l_ref.shape, F32)
    acc_ref[...] = jnp.zeros(acc_ref.shape, F32)

    def tile(off, height, bias_ref):
        kt = keys_ref[pl.ds(off, height), :]
        kpos = off + lax.broadcasted_iota(I32, (height, tq), 0)
        sel = ((kt > thr) | ((kt == thr) & (kpos <= jlim))) & (kpos <= qpos)
        amask = jnp.where(sel, 0.0, NEG)
        def scores(p):
            return _dot(k_ref[pl.ds(off, height), p * LANES:(p + 1) * LANES], qz_ref[p])

        def softmax(p, st):
            out = []
            for e in range(2):
                h = 2 * p + e
                lg = st[:, e * tq:(e + 1) * tq] + amask
                if bias_ref is not None:
                    lg = lg + bias_ref[h]
                m_old = m_ref[h]
                m_new = jnp.maximum(m_old, jnp.max(lg, axis=0, keepdims=True))
                alpha = jnp.exp(m_old - m_new)
                pr = jnp.exp(lg - m_new[0:1, :])
                l_ref[h] = alpha * l_ref[h] + jnp.sum(pr, axis=0, keepdims=True)
                m_ref[h] = m_new
                out.append((h, pr.astype(BF16), alpha[0:1, :]))
            return out

        def values(items):
            for h, pr, alpha in items:
                pv = _dot(vt_ref[h * HEAD_DIM:(h + 1) * HEAD_DIM, pl.ds(off, height)], pr)
                acc_ref[h] = alpha * acc_ref[h] + pv

        sts = {0: scores(0), 1: scores(1)}
        done = None
        for p in range(PAIRS):
            cur = softmax(p, sts.pop(p))
            if p + 2 < PAIRS:
                sts[p + 2] = scores(p + 2)
            if done is not None:
                values(done)
            done = cur
        values(done)

    nfar = jnp.maximum(i - 1, 0)
    nfar_wide = nfar // (WIDE // LANES)

    nfar_xwide = nfar_wide // 2

    def far_xwide(c, carry):
        tile(pl.multiple_of(c * (2 * WIDE), 2 * WIDE), 2 * WIDE, None)
        return carry

    def far_wide(c, carry):
        tile(pl.multiple_of(c * WIDE, WIDE), WIDE, None)
        return carry

    def far_narrow(j, carry):
        tile(pl.multiple_of(j * LANES, LANES), LANES, None)
        return carry

    lax.fori_loop(0, nfar_xwide, far_xwide, 0)
    lax.fori_loop(2 * nfar_xwide, nfar_wide, far_wide, 0)
    lax.fori_loop(nfar_wide * (WIDE // LANES), nfar, far_narrow, 0)
    for h in range(N_HEADS):
        m_ref[h] = m_ref[h] + far_ref[h]

    @pl.when(i >= 1)
    def _():
        tile(pl.multiple_of((i - 1) * LANES, LANES), LANES, bs_ref)

    tile(pl.multiple_of(i * LANES, LANES), LANES, bd_ref)
    out_t = jnp.concatenate([acc_ref[h] / l_ref[h][0:1, :] for h in range(N_HEADS)], axis=0)
    o_ref[...] = jnp.transpose(out_t)


def _attn_prompt(far, h2, hb, vt, bias_diag_t, bias_sub_t, nsel):
    t = h2.shape[0]
    tq = LANES
    nq = IDX_HEADS * IDX_DIM
    return pl.pallas_call(
        functools.partial(_attn_prompt_kernel, tq=tq, nsel=nsel),
        grid=(t // tq,),
        in_specs=[pl.BlockSpec(memory_space=pltpu.SMEM),
                  pl.BlockSpec((tq, nq), lambda i: (i, C_QIDX // nq)),
                  pl.BlockSpec((tq, LANES), lambda i: (i, C_KW // LANES)),
                  pl.BlockSpec((tq, MIX_DIM), lambda i: (i, C_Q // MIX_DIM)),
                  pl.BlockSpec((t, LANES), lambda i: (0, C_KW // LANES)),
                  pl.BlockSpec((t, MIX_DIM), lambda i: (0, C_K // MIX_DIM)),
                  pl.BlockSpec((MIX_DIM, t), lambda i: (0, 0)),
                  pl.BlockSpec((N_HEADS, LANES, tq), lambda i: (0, 0, 0)),
                  pl.BlockSpec((N_HEADS, LANES, tq), lambda i: (0, 0, 0))],
        out_specs=pl.BlockSpec((tq, MIX_DIM), lambda i: (i, 0)),
        out_shape=jax.ShapeDtypeStruct((t, MIX_DIM), F32),
        scratch_shapes=[pltpu.VMEM((t, tq), I32),
                        pltpu.VMEM((IDX_DIM, IDX_HEADS * tq), BF16),
                        pltpu.VMEM((PAIRS, LANES, 2 * tq), BF16),
                        pltpu.VMEM((N_HEADS, SUBLANES, tq), F32),
                        pltpu.VMEM((N_HEADS, SUBLANES, tq), F32),
                        pltpu.VMEM((N_HEADS, HEAD_DIM, tq), F32)],
        compiler_params=_cparams(("parallel",)),
        name="attn_prompt",
    )(far, hb, h2, hb, hb, hb, vt, bias_diag_t, bias_sub_t)


QROWS = 8
SEL_PAGES = 64
ATT_PAGES = 32


SEL_ROWS = 8 * QROWS


def _score_sample_kernel(pt_ref, qst_ref, wcol_ref, *refs, treal):
    pages = refs[:SEL_PAGES]
    knew_ref, keys_ref = refs[SEL_PAGES:]
    j = pl.program_id(1)
    ntiles = keys_ref.shape[2] // LANES

    def score(kt_t):
        r = jnp.maximum(_dot(qst_ref[0], kt_t), 0.0) * (wcol_ref[0] * IDX_SCALE)
        s = r[0:QROWS]
        for hh in range(1, IDX_HEADS):
            s = s + r[hh * QROWS:(hh + 1) * QROWS]
        return _mono_key(s)

    for g in range(SEL_PAGES):
        off = pl.multiple_of((j * SEL_PAGES + g) * LANES, LANES)
        keys_ref[0, :, pl.ds(off, LANES)] = score(pages[g][0, 0].astype(BF16))

    @pl.when(j == pl.num_programs(1) - 1)
    def _():
        rowq = lax.broadcasted_iota(I32, (QROWS, LANES), 0)
        lane = lax.broadcasted_iota(I32, (QROWS, LANES), 1)
        knew = jnp.where((lane < treal) & (lane <= rowq), score(knew_ref[0]), INT_MIN)
        keys_ref[0, :, (ntiles - 1) * LANES:] = knew


def _select_sample_kernel(keys_ref, mask_ref, *, nsel, treal):
    rows, nk = keys_ref.shape
    ntiles = nk // LANES
    lane = lax.broadcasted_iota(I32, (rows, LANES), 1)

    def count(pred):
        acc = jnp.zeros((rows, LANES), F32)
        for jt in range(ntiles):
            kt = keys_ref[:, jt * LANES:(jt + 1) * LANES]
            acc = acc + jnp.where(pred(kt, jt * LANES + lane), 1.0, 0.0)
        return jnp.sum(acc, axis=1, keepdims=True)

    def tie_limit(v, want):
        upper = (lax.broadcasted_iota(I32, (LANES, LANES), 0) <= lax.broadcasted_iota(I32, (LANES, LANES), 1))
        upper = jnp.where(upper, 1.0, 0.0).astype(BF16)
        eqs = [keys_ref[:, jt * LANES:(jt + 1) * LANES] == v for jt in range(ntiles)]
        ranks = [_dot(jnp.where(eq, 1.0, 0.0).astype(BF16), upper) for eq in eqs]
        base = jnp.zeros((rows, 1), F32)
        best = jnp.full((rows, LANES), -1.0, F32)
        for jt in range(ntiles):
            ok = eqs[jt] & (base + ranks[jt] <= want)
            best = jnp.maximum(best, jnp.where(ok, (jt * LANES + lane).astype(F32), -1.0))
            base = base + ranks[jt][:, LANES - 1:LANES]
        return jnp.max(best, axis=1, keepdims=True).astype(I32)

    row_ok = lax.broadcasted_iota(I32, (rows, 1), 0) % QROWS < treal
    thr, jlim = _select(count, (rows, 1), nsel, row_ok, tie_limit)
    for jt in range(ntiles):
        kt = keys_ref[:, jt * LANES:(jt + 1) * LANES]
        sel = (kt > thr) | ((kt == thr) & (jt * LANES + lane <= jlim))
        mask_ref[:, jt * LANES:(jt + 1) * LANES] = jnp.where(sel, 1.0, 0.0)


def _sel_sample(page_table, qst, wcol, cache_kidx, knew, layer, nsel, treal):
    b, npages = page_table.shape
    nk = (npages + 1) * PAGE
    page_spec = lambda g: pl.BlockSpec(
        (1, 1, IDX_DIM, PAGE), lambda i, j, pt: (layer, pt[i, j * SEL_PAGES + g], 0, 0))
    grid_spec = pltpu.PrefetchScalarGridSpec(
        num_scalar_prefetch=1,
        grid=(b, npages // SEL_PAGES),
        in_specs=[pl.BlockSpec((1, IDX_HEADS * QROWS, IDX_DIM), lambda i, j, pt: (i, 0, 0)),
                  pl.BlockSpec((1, IDX_HEADS * QROWS, 1), lambda i, j, pt: (i, 0, 0))]
                 + [page_spec(g) for g in range(SEL_PAGES)]
                 + [pl.BlockSpec((1, IDX_DIM, PAGE), lambda i, j, pt: (i, 0, 0))],
        out_specs=pl.BlockSpec((1, QROWS, nk), lambda i, j, pt: (i, 0, 0)),
    )
    keys = pl.pallas_call(
        functools.partial(_score_sample_kernel, treal=treal),
        grid_spec=grid_spec,
        out_shape=jax.ShapeDtypeStruct((b, QROWS, nk), I32),
        compiler_params=_cparams(("parallel", "arbitrary")),
        name="score_sample",
    )(page_table, qst, wcol, *([cache_kidx] * SEL_PAGES), knew)
    rows = b * QROWS
    blk = math.gcd(rows, SEL_ROWS)
    mask = pl.pallas_call(
        functools.partial(_select_sample_kernel, nsel=nsel, treal=treal),
        grid=(rows // blk,),
        in_specs=[pl.BlockSpec((blk, nk), lambda i: (i, 0))],
        out_specs=pl.BlockSpec((blk, nk), lambda i: (i, 0)),
        out_shape=jax.ShapeDtypeStruct((rows, nk), F32),
        compiler_params=_cparams(("parallel",)),
        name="select_sample",
    )(keys.reshape(rows, nk))
    return mask.reshape(b, QROWS, nk)


def _attn_sample_kernel(pt_ref, far_ref, qh_ref, *refs):
    kp = refs[:ATT_PAGES]
    vp = refs[ATT_PAGES:2 * ATT_PAGES]
    mask_ref, masknew_ref, blast_ref, bnew_ref, knew_ref, vnew_ref, o_ref, m_ref, l_ref, acc_ref = refs[2 * ATT_PAGES:]
    j = pl.program_id(1)
    is_last = j == pl.num_programs(1) - 1

    @pl.when(j == 0)
    def _():
        m_ref[...] = jnp.full(m_ref.shape, M_INIT, F32)
        l_ref[...] = jnp.zeros(l_ref.shape, F32)
        acc_ref[...] = jnp.zeros(acc_ref.shape, F32)

    def update(lgs, vhs):
        prs, alphas = [], []
        for h in range(N_HEADS):
            m_old = m_ref[h]
            m_new = jnp.maximum(m_old, jnp.max(lgs[h], axis=1, keepdims=True))
            alpha = jnp.exp(m_old - m_new)
            pr = jnp.exp(lgs[h] - m_new[:, 0:1])
            l_ref[h] = alpha * l_ref[h] + jnp.sum(pr, axis=1, keepdims=True)
            m_ref[h] = m_new
            prs.append(pr.astype(BF16))
            alphas.append(alpha[:, :HEAD_DIM])
        for h in range(N_HEADS):
            acc_ref[h] = alphas[h] * acc_ref[h] + _dot_nt(prs[h], vhs[h])

    def q_of(h):
        return (qh_ref[0, h].astype(F32) * ATT_SCALE).astype(BF16)

    def head_t(refs_, h):
        return jnp.concatenate([refs_[g][0, 0, h] for g in range(ATT_PAGES)], axis=1).astype(BF16)

    amask = jnp.where(mask_ref[0] > 0.5, 0.0, NEG)
    lgs = []
    for h in range(N_HEADS):
        far = jnp.full((QROWS, LANES), far_ref[h], F32)
        bias = jnp.concatenate([far] * (ATT_PAGES - 1) + [jnp.where(is_last, blast_ref[h], far)], axis=1)
        lgs.append(_dot(q_of(h), head_t(kp, h)) + amask + bias)
    update(lgs, [head_t(vp, h) for h in range(N_HEADS)])

    @pl.when(is_last)
    def _():
        amask_new = jnp.where(masknew_ref[0] > 0.5, 0.0, NEG)
        lgs_new = [_dot(q_of(h), knew_ref[0, h]) + amask_new + bnew_ref[h] for h in range(N_HEADS)]
        update(lgs_new, [vnew_ref[0, h] for h in range(N_HEADS)])
        outs = []
        for h in range(N_HEADS):
            lh = l_ref[h][:, :HEAD_DIM]
            outs.append(acc_ref[h] / jnp.where(lh > 0.0, lh, 1.0))
        o_ref[0] = jnp.concatenate(outs, axis=1)


def _attn_sample(page_table, far, qh, cache_k2, cache_v2, mask, bias_last, bias_new, knew, vnew, layer):
    b, npages = page_table.shape
    nsteps = npages // ATT_PAGES
    page_spec = lambda g: pl.BlockSpec(
        (1, 1, N_HEADS, HEAD_DIM, PAGE), lambda i, j, pt, far: (layer, pt[i, j * ATT_PAGES + g], 0, 0, 0))
    const3 = lambda i, j, pt, far: (0, 0, 0)
    per_b4 = lambda i, j, pt, far: (i, 0, 0, 0)
    grid_spec = pltpu.PrefetchScalarGridSpec(
        num_scalar_prefetch=2,
        grid=(b, nsteps),
        in_specs=[pl.BlockSpec((1, N_HEADS, QROWS, HEAD_DIM), per_b4)]
                 + [page_spec(g) for g in range(ATT_PAGES)] * 2
                 + [pl.BlockSpec((1, QROWS, ATT_PAGES * PAGE), lambda i, j, pt, far: (i, 0, j)),
                    pl.BlockSpec((1, QROWS, PAGE), lambda i, j, pt, far: (i, 0, npages)),
                    pl.BlockSpec((N_HEADS, QROWS, PAGE), const3),
                    pl.BlockSpec((N_HEADS, QROWS, PAGE), const3),
                    pl.BlockSpec((1, N_HEADS, HEAD_DIM, PAGE), per_b4),
                    pl.BlockSpec((1, N_HEADS, HEAD_DIM, PAGE), per_b4)],
        out_specs=pl.BlockSpec((1, QROWS, MIX_DIM), lambda i, j, pt, far: (i, 0, 0)),
        scratch_shapes=[pltpu.VMEM((N_HEADS, QROWS, LANES), F32),
                        pltpu.VMEM((N_HEADS, QROWS, LANES), F32),
                        pltpu.VMEM((N_HEADS, QROWS, HEAD_DIM), F32)],
    )
    return pl.pallas_call(
        _attn_sample_kernel,
        grid_spec=grid_spec,
        out_shape=jax.ShapeDtypeStruct((b, QROWS, MIX_DIM), F32),
        compiler_params=_cparams(("parallel", "arbitrary")),
        name="attn_sample",
    )(page_table, far, qh, *([cache_k2] * ATT_PAGES), *([cache_v2] * ATT_PAGES), mask, mask,
      bias_last, bias_new, knew, vnew)


def _bucket_table(n):
    d = np.arange(n)
    nf = np.maximum(d, 1).astype(np.float32)
    half = REL_BUCKETS // 2
    large = half + (np.log(nf / np.float32(half)) / np.float32(math.log(REL_MAX_DIST / half))
                    * np.float32(REL_BUCKETS - half)).astype(np.int32)
    return np.where(d < half, d, np.minimum(large, REL_BUCKETS - 1)).astype(np.int32)


def _pack_w_in(w):
    w_t = jnp.transpose(w, (0, 2, 1))
    p, z, xbc, dt, q, k, v, qi, ki, wi, u, vs = jnp.split(w_t, np.cumsum(IN_SPLITS)[:-1].tolist(), axis=1)
    zeros = lambda n: jnp.zeros((w.shape[0], n, w.shape[1]), w.dtype)
    packed = jnp.concatenate(
        [p, z, xbc, q, k, v, u, vs, qi, ki, wi, zeros(LANES - IDX_DIM - IDX_HEADS), dt, zeros(LANES - N_HEADS)],
        axis=1)
    return packed.astype(BF16)


def _pad_lanes(x):
    return jnp.pad(x.reshape(1, -1), ((0, 0), (0, LANES - x.shape[-1])))


def _pad_axis(x, axis, size):
    pads = [(0, 0)] * x.ndim
    pads[axis] = (0, size - x.shape[axis])
    return jnp.pad(x, pads)


def kernel(x_prompt, x_sample, state_pool, state_conv, state_ssm, cache_k, cache_v, cache_kidx, page_table,
           rel_bias, norm_mix, w_in, pool_w, pool_scale, conv_w, conv_b, dt_bias, a_log, d_skip, ssm_norm,
           sgu_w, sgu_b, w_out, norm_mlp, mlp_w1, mlp_w2, norm_final):
    depth = w_in.shape[0]
    bp, t, _ = x_prompt.shape
    bs, ts, _ = x_sample.shape
    npages = page_table.shape[1]
    past = npages * PAGE
    n_pool = cache_k.shape[1]
    assert bp == 1 and t % WIDE == 0 and npages % SEL_PAGES == 0 and npages % ATT_PAGES == 0
    assert max(t, past + PAGE) <= 2 ** POS_BITS and ts <= QROWS
    tc = LANES
    nsel_p = min(TOPK_MAX, t // 4)
    nsel_s = min(TOPK_MAX, (past + ts) // 4)

    bucket = _bucket_table(2 * LANES)
    far = rel_bias[REL_BUCKETS - 1]
    kk = np.arange(LANES)[:, None]
    qq = np.arange(LANES)[None, :]
    def bias_of(dist):
        onehot = np.eye(REL_BUCKETS, dtype=np.float32)[bucket[dist].reshape(-1)]
        vals = jnp.dot(onehot, rel_bias, precision=lax.Precision.HIGHEST)
        return jnp.transpose(vals).reshape((N_HEADS,) + dist.shape)

    bias_diag_t = bias_of(np.maximum(qq - kk, 0))
    bias_sub_t = bias_of(np.minimum(LANES + qq - kk, 2 * LANES - 1))
    sq = np.arange(QROWS)[:, None]
    sk = np.arange(PAGE)[None, :]
    bias_last = bias_of(np.minimum(PAGE + sq - sk, 2 * LANES - 1))
    bias_new = bias_of(np.maximum(sq - sk, 0))

    cache_k2 = jnp.transpose(cache_k, (0, 1, 3, 4, 2))
    cache_v2 = jnp.transpose(cache_v, (0, 1, 3, 4, 2))
    cache_kidx2 = jnp.transpose(cache_kidx, (0, 1, 3, 2))
    w_in_t = _pack_w_in(w_in)
    w_out_b, w1_b, w2_b = w_out.astype(BF16), mlp_w1.astype(BF16), mlp_w2.astype(BF16)

    xp = x_prompt.reshape(t, D_MODEL)
    xs = x_sample.reshape(bs * ts, D_MODEL)
    zero_pool = jnp.zeros((1, POOL_HIST, MIX_DIM), F32)
    zero_conv = jnp.zeros((1, CONV_W - 1, XBC_DIM), F32)
    zero_ssm = jnp.zeros((1, N_HEADS, HEAD_DIM, SSM_STATE), F32)
    tm_p = min(1024, t)
    tm_s = bs * ts
    outs_p, outs_s = [], []

    for l in range(depth):
        g_mix = norm_mix[l].reshape(1, D_MODEL)
        g_mlp = norm_mlp[l].reshape(1, D_MODEL)
        g_out = norm_final.reshape(1, D_MODEL) if l == depth - 1 else g_mlp
        pscale = pool_scale[l].reshape(1, MIX_DIM)
        cb = conv_b[l].reshape(1, XBC_DIM)
        dtb, alog, dsk = _pad_lanes(dt_bias[l]), _pad_lanes(a_log[l]), _pad_lanes(d_skip[l])
        snorm = ssm_norm[l].reshape(1, MIX_DIM)
        sgu_b_col = sgu_b[l][:, :, None]
        final = l == depth - 1

        hp, hpb, vt = _inproj(xp, g_mix, w_in_t, l, tm=min(1024, t), tn=1024, attn_operands=True)
        hp3 = hp.reshape(1, t, H_PACKED)
        y_pool, new_pool, y_sgu = _pool_sgu(hp3, zero_pool, pool_w[l], pscale, sgu_w[l], sgu_b_col,
                                            rows=WIDE, chunk=tc)
        y_ssm, new_conv, new_h = _ssd(hp3, zero_conv, zero_ssm, conv_w[l], cb, dtb, alog, dsk, snorm,
                                      tc=tc, treal=tc)
        kf = hp[:, C_K:C_K + MIX_DIM]
        vf = hp[:, C_V:C_V + MIX_DIM]
        kif = hp[:, C_KW:C_KW + IDX_DIM]
        y_att = _attn_prompt(far, hp, hpb, vt, bias_diag_t, bias_sub_t, nsel_p)
        x1, xn = _outproj(xp, [y_pool[0], y_ssm[0], y_att, y_sgu[0]], w_out_b, l, g_mlp, tm=min(512, t))
        xp = _mlp(xn, w1_b, w2_b, l, x1, g_out, tm=tm_p, final_norm=final)
        outs_p.append((new_pool, new_conv, new_h, kf.reshape(1, t, N_HEADS, HEAD_DIM),
                       vf.reshape(1, t, N_HEADS, HEAD_DIM), kif.reshape(1, t, IDX_DIM)))

        hs, = _inproj(xs, g_mix, w_in_t, l, tm=tm_s)
        hs3 = hs.reshape(bs, ts, H_PACKED)
        y_pool, new_pool = _pool(hs3, state_pool[l], pool_w[l], pscale, tc=tc, treal=ts, pos0=past)
        y_ssm, new_conv, new_h = _ssd(hs3, state_conv[l], state_ssm[l], conv_w[l], cb, dtb, alog, dsk, snorm,
                                      tc=tc, treal=ts)
        y_sgu, v_rows = _sgu(hs3, sgu_w[l], sgu_b_col, tc=tc, treal=ts, emit_v=True)
        kf = hs3[:, :, C_K:C_K + MIX_DIM]
        vf = hs3[:, :, C_V:C_V + MIX_DIM]
        kif = hs3[:, :, C_KW:C_KW + IDX_DIM]
        qi = hs3[:, :, C_QIDX:C_QIDX + IDX_HEADS * IDX_DIM].reshape(bs, ts, IDX_HEADS, IDX_DIM)
        qst = _pad_axis(jnp.transpose(qi, (0, 2, 1, 3)), 2, QROWS)
        qst = qst.reshape(bs, IDX_HEADS * QROWS, IDX_DIM).astype(BF16)
        wi = hs3[:, :, C_KW + IDX_DIM:C_KW + IDX_DIM + IDX_HEADS]
        wcol = _pad_axis(jnp.transpose(wi, (0, 2, 1)), 2, QROWS).reshape(bs, IDX_HEADS * QROWS, 1)
        kinew_t = _pad_axis(jnp.transpose(kif, (0, 2, 1)), 2, PAGE).astype(BF16)
        mask = _sel_sample(page_table, qst, wcol, cache_kidx2, kinew_t, l, nsel_s, ts)
        heads = lambda x, perm: jnp.transpose(x.reshape(bs, ts, N_HEADS, HEAD_DIM), perm)
        qh = _pad_axis(heads(hs3[:, :, C_Q:C_Q + MIX_DIM], (0, 2, 1, 3)), 2, QROWS).astype(BF16)
        knew = _pad_axis(heads(kf, (0, 2, 3, 1)), 3, PAGE).astype(BF16)
        vnew = _pad_axis(heads(vf, (0, 2, 3, 1)), 3, PAGE).astype(BF16)
        att = _attn_sample(page_table, far, qh, cache_k2, cache_v2, mask, bias_last, bias_new, knew, vnew, l)
        y_att = att[:, :ts].reshape(bs * ts, MIX_DIM)
        flat = lambda y: y.reshape(bs * ts, MIX_DIM)
        x1, xn = _outproj(xs, [flat(y_pool), flat(y_ssm), y_att, flat(y_sgu)], w_out_b, l, g_mlp, tm=tm_s)
        xs = _mlp(xn, w1_b, w2_b, l, x1, g_out, tm=tm_s, final_norm=final)
        outs_s.append((new_pool, new_conv, new_h, kf.reshape(bs, ts, N_HEADS, HEAD_DIM),
                       vf.reshape(bs, ts, N_HEADS, HEAD_DIM), kif, v_rows))

    stack = lambda outs, i: jnp.stack([o[i] for o in outs])
    return (xp.reshape(1, t, D_MODEL), xs.reshape(bs, ts, D_MODEL),
            *[stack(outs_p, i) for i in range(6)],
            *[stack(outs_s, i) for i in range(7)])
```
